```python
import jax, jax.numpy as jnp
from jax import lax
import numpy as np


D_MODEL = 2048
BATCH = 1
SEQ = 16384
DEPTH = 2

EPS = 1e-6
RET_HEADS = 4
RET_DK = D_MODEL // 8
RET_DV = D_MODEL // 8
RET_CHUNK = 128
ROPE_BASE = 10000.0
GMLP_GROUPS = 4
GMLP_DIM = D_MODEL // 8
GMLP_CHUNK = 128
RET_QK_WIDTH = RET_HEADS * RET_DK
RET_V_WIDTH = RET_HEADS * RET_DV
GMLP_WIDTH = GMLP_GROUPS * GMLP_DIM
AB_IN_WIDTH = 2 * RET_QK_WIDTH + 2 * RET_V_WIDTH + 2 * GMLP_WIDTH
AB_MIX_WIDTH = RET_V_WIDTH + GMLP_WIDTH
AB_SPLITS = [RET_QK_WIDTH, 2 * RET_QK_WIDTH, 2 * RET_QK_WIDTH + RET_V_WIDTH,
             2 * RET_QK_WIDTH + 2 * RET_V_WIDTH, 2 * RET_QK_WIDTH + 2 * RET_V_WIDTH + GMLP_WIDTH]
HGRN_DK = 128
HGRN_DV = 128
HGRN_HEADS = D_MODEL // HGRN_DK
HGRN_WIDTH = HGRN_HEADS * HGRN_DK
HGRN_IN_WIDTH = 4 * HGRN_WIDTH
HGRN_CHUNK = 32
N_GROUPS = 4
EXPERTS_PER_GROUP = 8
N_EXPERTS = N_GROUPS * EXPERTS_PER_GROUP
TOP_K = 2
EXPERT_FF = D_MODEL // 2
MOE_BLOCK = 128

kernel_name = 'hybrid_retention_gmlp_hgrn2_hmoe'


def _rmsnorm(x, g):
    xf = x.astype(jnp.float32)
    y = xf * lax.rsqrt(jnp.mean(xf * xf, axis=-1, keepdims=True) + EPS)
    return (y * g.astype(jnp.float32)).astype(x.dtype)


def _head_norm(x, gain, center):
    if center:
        x = x - jnp.mean(x, axis=-1, keepdims=True)
    y = x * lax.rsqrt(jnp.mean(x * x, axis=-1, keepdims=True) + EPS)
    return y.reshape(x.shape[0], x.shape[1], -1) * gain.astype(jnp.float32)


def _to_chunks(x, c):
    b, s, h, d = x.shape
    return x.reshape(b, s // c, c, h, d).transpose(1, 0, 3, 2, 4)


def _from_chunks(y):
    n, b, h, c, d = y.shape
    return y.transpose(1, 0, 3, 2, 4).reshape(b, n * c, h, d)


def _rope(x):
    s, d = x.shape[1], x.shape[-1]
    inv = ROPE_BASE ** (-jnp.arange(0, d, 2, dtype=jnp.float32) / d)
    ang = jnp.arange(s, dtype=jnp.float32)[:, None] * inv[None, :]
    cos = jnp.cos(ang)[None, :, None, :]
    sin = jnp.sin(ang)[None, :, None, :]
    x1, x2 = x[..., : d // 2], x[..., d // 2:]
    return jnp.concatenate([x1 * cos - x2 * sin, x2 * cos + x1 * sin], axis=-1)


def _retention(q, k, v, gate, gain):
    f32 = jnp.float32
    b, s, _ = q.shape
    q = _rope(q.astype(f32).reshape(b, s, RET_HEADS, RET_DK))
    k = _rope(k.astype(f32).reshape(b, s, RET_HEADS, RET_DK)) * (RET_DK ** -0.5)
    v = v.astype(f32).reshape(b, s, RET_HEADS, RET_DV)
    log_gamma = jnp.log(1.0 - jnp.exp2(-5.0 - jnp.arange(RET_HEADS, dtype=f32)))
    idx = jnp.arange(RET_CHUNK, dtype=f32)
    diff = idx[:, None] - idx[None, :]
    d_intra = jnp.where(diff >= 0, jnp.exp(jnp.maximum(diff, 0.0) * log_gamma[:, None, None]), 0.0)
    q_decay = jnp.exp((idx + 1.0)[None, :] * log_gamma[:, None])[..., None]
    k_decay = jnp.exp((RET_CHUNK - 1.0 - idx)[None, :] * log_gamma[:, None])[..., None]
    chunk_decay = jnp.exp(RET_CHUNK * log_gamma)[:, None, None]

    def step(state, xs):
        qc, kc, vc = xs
        scores = jnp.einsum('bhik,bhjk->bhij', qc, kc) * d_intra
        out = (jnp.einsum('bhij,bhjv->bhiv', scores, vc)
               + jnp.einsum('bhik,bhkv->bhiv', qc * q_decay, state))
        state = chunk_decay * state + jnp.einsum('bhjk,bhjv->bhkv', kc * k_decay, vc)
        return state, out

    init = jnp.zeros((b, RET_HEADS, RET_DK, RET_DV), f32)
    _, o = lax.scan(step, init, (_to_chunks(q, RET_CHUNK), _to_chunks(k, RET_CHUNK), _to_chunks(v, RET_CHUNK)))
    o = _head_norm(_from_chunks(o), gain, center=True)
    return (jax.nn.silu(gate.astype(f32)) * o).astype(gate.dtype)


def _chunked_gmlp(u, v, gain, w_s, b_s):
    f32 = jnp.float32
    b, s, _ = u.shape
    n = s // GMLP_CHUNK
    u = jax.nn.gelu(u.astype(f32))
    v = jax.nn.gelu(v.astype(f32)).reshape(b, n, GMLP_CHUNK, GMLP_GROUPS, GMLP_DIM)
    v = v - jnp.mean(v, axis=-1, keepdims=True)
    v = v * lax.rsqrt(jnp.mean(v * v, axis=-1, keepdims=True) + EPS) * gain.astype(f32).reshape(GMLP_GROUPS, GMLP_DIM)
    causal = jnp.tril(jnp.ones((GMLP_CHUNK, GMLP_CHUNK), f32))
    mixed = (jnp.einsum('gts,bnsgc->bntgc', w_s.astype(f32) * causal, v)
             + b_s.astype(f32).T[:, :, None])
    return (u * mixed.reshape(b, s, GMLP_WIDTH)).astype(u.dtype)


def _hgrn2(zq, zf, zi, zg, lb, gain):
    f32 = jnp.float32
    b, s, _ = zq.shape
    lb = lb.astype(f32)
    zf = zf.astype(f32)
    q = jax.nn.silu(zq.astype(f32)).reshape(b, s, HGRN_HEADS, HGRN_DK)
    f = lb + (1.0 - lb) * jax.nn.sigmoid(zf)
    k = ((1.0 - lb) * jax.nn.sigmoid(-zf)).reshape(b, s, HGRN_HEADS, HGRN_DK)
    log_f = jnp.log(f).reshape(b, s, HGRN_HEADS, HGRN_DK)
    v = zi.astype(f32).reshape(b, s, HGRN_HEADS, HGRN_DV)
    causal = jnp.tril(jnp.ones((HGRN_CHUNK, HGRN_CHUNK), f32))

    def step(state, xs):
        qc, kc, vc, lfc = xs
        cum = jnp.cumsum(lfc, axis=-2)
        q_dec = qc * jnp.exp(cum)
        scores = jnp.einsum('bhik,bhjk->bhij', q_dec, kc * jnp.exp(-cum)) * causal
        out = (jnp.einsum('bhij,bhjv->bhiv', scores, vc)
               + jnp.einsum('bhik,bhkv->bhiv', q_dec, state))
        last = cum[:, :, -1:, :]
        state = (jnp.exp(last)[:, :, 0, :, None] * state
                 + jnp.einsum('bhjk,bhjv->bhkv', kc * jnp.exp(last - cum), vc))
        return state, out

    init = jnp.zeros((b, HGRN_HEADS, HGRN_DK, HGRN_DV), f32)
    _, o = lax.scan(step, init, (_to_chunks(q, HGRN_CHUNK), _to_chunks(k, HGRN_CHUNK),
                                 _to_chunks(v, HGRN_CHUNK), _to_chunks(log_f, HGRN_CHUNK)))
    o = _head_norm(_from_chunks(o), gain, center=False)
    return (o * jax.nn.silu(zg.astype(f32))).astype(zg.dtype)


def _hier_moe(h, w_rg, b_rg, w_re, b_re, w_gate, w_up, w_down):
    f32 = jnp.float32
    b, s, d = h.shape
    t = b * s
    m = t * TOP_K
    xt = h.reshape(t, d)
    p_group = jax.nn.softmax(jnp.einsum('td,dg->tg', xt, w_rg).astype(f32) + b_rg.astype(f32), axis=-1)
    p_sel, g_sel = lax.top_k(p_group, 1)
    logits_e = (jnp.einsum('td,de->te', xt, w_re).astype(f32) + b_re.astype(f32)).reshape(t, N_GROUPS, EXPERTS_PER_GROUP)
    logits_in = jnp.take_along_axis(logits_e, g_sel[:, :, None], axis=1)[:, 0]
    top_logit, e_sel = lax.top_k(logits_in, TOP_K)
    gates = (p_sel * jax.nn.softmax(top_logit, axis=-1)).reshape(m)
    expert_id = (g_sel * EXPERTS_PER_GROUP + e_sel).reshape(m).astype(jnp.int32)
    token_id = jnp.repeat(jnp.arange(t, dtype=jnp.int32), TOP_K)

    order = jnp.argsort(expert_id)
    sorted_e = expert_id[order]
    counts = jnp.zeros((N_EXPERTS,), jnp.int32).at[expert_id].add(1)
    starts = jnp.cumsum(counts) - counts
    padded = (counts + MOE_BLOCK - 1) // MOE_BLOCK * MOE_BLOCK
    padded_ends = jnp.cumsum(padded)
    padded_starts = padded_ends - padded
    dest = padded_starts[sorted_e] + jnp.arange(m, dtype=jnp.int32) - starts[sorted_e]
    n_blocks = -(-(m + N_EXPERTS * (MOE_BLOCK - 1)) // MOE_BLOCK)
    buf_tok = jnp.zeros((n_blocks * MOE_BLOCK,), jnp.int32).at[dest].set(token_id[order])
    buf_gate = jnp.zeros((n_blocks * MOE_BLOCK,), f32).at[dest].set(gates[order])
    block_start = jnp.arange(n_blocks, dtype=jnp.int32) * MOE_BLOCK
    block_expert = jnp.minimum(jnp.searchsorted(padded_ends, block_start, side='right'), N_EXPERTS - 1)

    def run_block(args):
        tok, e = args
        xb = xt[tok]
        hid = jax.nn.silu(xb @ w_gate[e]) * (xb @ w_up[e])
        return hid @ w_down[e]

    out = lax.map(run_block, (buf_tok.reshape(n_blocks, MOE_BLOCK), block_expert))
    y = jnp.zeros((t, d), f32).at[buf_tok].add(out.reshape(-1, d).astype(f32) * buf_gate[:, None])
    return y.astype(h.dtype).reshape(b, s, d)


def setup_inputs(seed: int = 0) -> dict:
    key = jax.random.key(seed)
    ks = jax.random.split(key, 21)
    f32 = jnp.float32
    n_even = (DEPTH + 1) // 2
    n_odd = DEPTH // 2

    def nrm(k, shape, scale):
        return jax.random.normal(k, shape, f32) * scale

    def gain(k, shape):
        return 1.0 + 0.05 * jax.random.normal(k, shape, f32)

    return {
        'x': nrm(ks[0], (BATCH, SEQ, D_MODEL), 1.0),
        'attn_norm': gain(ks[1], (DEPTH, D_MODEL)),
        'ffn_norm': gain(ks[2], (DEPTH, D_MODEL)),
        'final_norm': gain(ks[3], (D_MODEL,)),
        'w_in_ab': nrm(ks[4], (n_even, D_MODEL, AB_IN_WIDTH), D_MODEL ** -0.5),
        'ret_norm': gain(ks[5], (n_even, RET_V_WIDTH)),
        'gmlp_norm': gain(ks[6], (n_even, GMLP_WIDTH)),
        'gmlp_ws': nrm(ks[7], (n_even, GMLP_GROUPS, GMLP_CHUNK, GMLP_CHUNK), GMLP_CHUNK ** -0.5),
        'gmlp_bs': 1.0 + nrm(ks[8], (n_even, GMLP_GROUPS, GMLP_CHUNK), 0.1),
        'w_out_ab': nrm(ks[9], (n_even, AB_MIX_WIDTH, D_MODEL), AB_MIX_WIDTH ** -0.5),
        'w_in_c': nrm(ks[10], (n_odd, D_MODEL, HGRN_IN_WIDTH), D_MODEL ** -0.5),
        'lb_params': nrm(ks[11], (DEPTH, HGRN_WIDTH), 0.1),
        'hgrn_norm': gain(ks[12], (n_odd, HGRN_WIDTH)),
        'w_out_c': nrm(ks[13], (n_odd, HGRN_WIDTH, D_MODEL), HGRN_WIDTH ** -0.5),
        'router_w_group': nrm(ks[14], (DEPTH, D_MODEL, N_GROUPS), D_MODEL ** -0.5),
        'router_b_group': nrm(ks[15], (DEPTH, N_GROUPS), 0.01),
        'router_w_expert': nrm(ks[16], (DEPTH, D_MODEL, N_EXPERTS), D_MODEL ** -0.5),
        'router_b_expert': nrm(ks[17], (DEPTH, N_EXPERTS), 0.01),
        'w_gate': nrm(ks[18], (DEPTH, N_EXPERTS, D_MODEL, EXPERT_FF), D_MODEL ** -0.5),
        'w_up': nrm(ks[19], (DEPTH, N_EXPERTS, D_MODEL, EXPERT_FF), D_MODEL ** -0.5),
        'w_down': nrm(ks[20], (DEPTH, N_EXPERTS, EXPERT_FF, D_MODEL), EXPERT_FF ** -0.5),
    }


def reference(x, attn_norm, ffn_norm, final_norm, w_in_ab, ret_norm, gmlp_norm, gmlp_ws, gmlp_bs,
              w_out_ab, w_in_c, lb_params, hgrn_norm, w_out_c, router_w_group, router_b_group,
              router_w_expert, router_b_expert, w_gate, w_up, w_down):
    lb_soft = jax.nn.softmax(lb_params.astype(jnp.float32), axis=0)
    lower_bounds = jnp.cumsum(lb_soft, axis=0) - lb_soft[0]
    for layer in range(DEPTH):
        i = layer // 2
        h = _rmsnorm(x, attn_norm[layer])
        if layer % 2 == 0:
            proj = jnp.einsum('bsd,de->bse', h, w_in_ab[i])
            q, k, v, g, u, vs = jnp.split(proj, AB_SPLITS, axis=-1)
            mixed = jnp.concatenate([_retention(q, k, v, g, ret_norm[i]),
                                     _chunked_gmlp(u, vs, gmlp_norm[i], gmlp_ws[i], gmlp_bs[i])], axis=-1)
            x = x + jnp.einsum('bse,ed->bsd', mixed, w_out_ab[i])
        else:
            proj = jnp.einsum('bsd,de->bse', h, w_in_c[i])
            zq, zf, zi, zg = jnp.split(proj, 4, axis=-1)
            mixed = _hgrn2(zq, zf, zi, zg, lower_bounds[layer], hgrn_norm[i])
            x = x + jnp.einsum('bse,ed->bsd', mixed, w_out_c[i])
        h = _rmsnorm(x, ffn_norm[layer])
        x = x + _hier_moe(h, router_w_group[layer], router_b_group[layer], router_w_expert[layer],
                          router_b_expert[layer], w_gate[layer], w_up[layer], w_down[layer])
    return _rmsnorm(x, final_norm)
```

```python
import functools

import jax
import jax.numpy as jnp
from jax import lax
from jax.experimental import pallas as pl
from jax.experimental.pallas import tpu as pltpu

F32 = jnp.float32
BF16 = jnp.bfloat16
EPS = 1e-6

RET_HEADS = 4
RET_CHUNK = 128
ROPE_BASE = 10000.0
GMLP_GROUPS = 4
GMLP_CHUNK = 128
HGRN_DK = 128
HGRN_CHUNK = 32
N_GROUPS = 4
EXPERTS_PER_GROUP = 8
N_EXPERTS = N_GROUPS * EXPERTS_PER_GROUP
TOP_K = 2
MOE_BLOCK = 128
ROUTE_LANES = 128

VMEM_LIMIT = 48 * 1024 * 1024


def _params(*sem):
    return pltpu.CompilerParams(dimension_semantics=sem, vmem_limit_bytes=VMEM_LIMIT)


def _rms(x, g):
    return x * lax.rsqrt(jnp.mean(x * x, axis=-1, keepdims=True) + EPS) * g


def _norm_matmul_kernel(*refs, has_y):
    if has_y:
        x_ref, y0_ref, y1_ref, g_ref, w_ref, proj_ref, xnew_ref, xn_ref = refs
    else:
        x_ref, g_ref, w_ref, proj_ref, xn_ref = refs

    @pl.when(pl.program_id(1) == 0)
    def _():
        x = x_ref[...]
        if has_y:
            x = x + (y0_ref[...] + y1_ref[...])
            xnew_ref[...] = x
        xn_ref[...] = _rms(x, g_ref[...]).astype(BF16)

    proj_ref[...] = jnp.dot(xn_ref[...], w_ref[...],
                            preferred_element_type=F32).astype(proj_ref.dtype)


def _norm_matmul(x, y2, g, w, *, tn=1024):
    t, d = x.shape
    n = w.shape[1]
    has_y = y2 is not None
    tm = 256 if has_y else 512
    row = pl.BlockSpec((tm, d), lambda i, j: (i, 0))
    in_specs = [row]
    args = [x]
    if has_y:
        nt = t // tm
        in_specs += [row, pl.BlockSpec((tm, d), lambda i, j: (i + nt, 0))]
        args += [y2, y2]
    in_specs += [pl.BlockSpec((1, d), lambda i, j: (0, 0)),
                 pl.BlockSpec((d, tn), lambda i, j: (0, j))]
    args += [g.reshape(1, d), w]
    out_shape = [jax.ShapeDtypeStruct((t, n), BF16)]
    out_specs = [pl.BlockSpec((tm, tn), lambda i, j: (i, j))]
    if has_y:
        out_shape.append(jax.ShapeDtypeStruct((t, d), F32))
        out_specs.append(row)
    res = pl.pallas_call(
        functools.partial(_norm_matmul_kernel, has_y=has_y),
        grid=(t // tm, n // tn),
        in_specs=in_specs, out_specs=out_specs, out_shape=out_shape,
        scratch_shapes=[pltpu.VMEM((tm, d), BF16)],
        compiler_params=_params("arbitrary", "arbitrary"),
        name="norm_matmul",
    )(*args)
    return (res[0], res[1]) if has_y else (res[0], x)


def _rope(x, cos, sin):
    half = x.shape[-1] // 2
    x1, x2 = x[:, :half], x[:, half:]
    return jnp.concatenate([x1 * cos - x2 * sin, x2 * cos + x1 * sin], axis=-1)


def _retention_kernel(cd_ref, q_ref, k_ref, v_ref, g_ref, cos_ref, sin_ref, dint_ref, qd_ref, kd_ref,
                      gain_ref, o_ref, state_ref, *, chunk, nchunk):
    @pl.when(pl.program_id(1) == 0)
    def _():
        state_ref[...] = jnp.zeros_like(state_ref)

    dk = q_ref.shape[-1]
    dint = dint_ref[0]
    qd = qd_ref[0]
    kd = kd_ref[0]
    cd = cd_ref[pl.program_id(0)]
    gain = gain_ref[...]
    for c in range(nchunk):
        rows = pl.ds(c * chunk, chunk)
        cos = cos_ref[rows, :]
        sin = sin_ref[rows, :]
        q = _rope(q_ref[rows, :].astype(F32), cos, sin)
        k = _rope(k_ref[rows, :].astype(F32), cos, sin) * (dk ** -0.5)
        v = v_ref[rows, :]
        scores = lax.dot_general(q.astype(BF16), k.astype(BF16), (((1,), (1,)), ((), ())),
                                 preferred_element_type=F32) * dint
        state = state_ref[...]
        o = (jnp.dot(scores.astype(BF16), v, preferred_element_type=F32)
             + jnp.dot((q * qd).astype(BF16), state.astype(BF16), preferred_element_type=F32))
        state_ref[...] = cd * state + lax.dot_general(
            (k * kd).astype(BF16), v, (((0,), (0,)), ((), ())), preferred_element_type=F32)
        o = o - jnp.mean(o, axis=-1, keepdims=True)
        o = o * lax.rsqrt(jnp.mean(o * o, axis=-1, keepdims=True) + EPS) * gain
        o_ref[rows, :] = (jax.nn.silu(g_ref[rows, :].astype(F32)) * o).astype(o_ref.dtype)


def _retention(proj, gain, *, rows=512):
    t = proj.shape[0]
    h = RET_HEADS
    dk = proj.shape[1] // 6 // h
    dv = dk
    c = RET_CHUNK
    f32 = F32
    inv = ROPE_BASE ** (-jnp.arange(0, dk, 2, dtype=f32) / dk)
    ang = jnp.arange(t, dtype=f32)[:, None] * inv[None, :]
    cos, sin = jnp.cos(ang), jnp.sin(ang)
    log_gamma = jnp.log(1.0 - jnp.exp2(-5.0 - jnp.arange(h, dtype=f32)))
    idx = jnp.arange(c, dtype=f32)
    diff = idx[:, None] - idx[None, :]
    d_intra = jnp.where(diff >= 0, jnp.exp(jnp.maximum(diff, 0.0) * log_gamma[:, None, None]), 0.0)
    q_decay = jnp.broadcast_to(jnp.exp((idx + 1.0)[None, :] * log_gamma[:, None])[..., None], (h, c, dk))
    k_decay = jnp.broadcast_to(jnp.exp((c - 1.0 - idx)[None, :] * log_gamma[:, None])[..., None], (h, c, dk))
    chunk_decay = jnp.exp(c * log_gamma)

    def col(off):
        return pl.BlockSpec((rows, dk), lambda hh, i: (i, off + hh))

    tab = pl.BlockSpec((rows, dk // 2), lambda hh, i: (i, 0))
    per_head = lambda shp: pl.BlockSpec((1,) + shp, lambda hh, i: (hh, 0, 0))
    return pl.pallas_call(
        functools.partial(_retention_kernel, chunk=c, nchunk=rows // c),
        grid=(h, t // rows),
        in_specs=[pl.BlockSpec(memory_space=pltpu.SMEM),
                  col(0), col(h), col(2 * h), col(3 * h), tab, tab,
                  per_head((c, c)), per_head((c, dk)), per_head((c, dk)),
                  pl.BlockSpec((1, dv), lambda hh, i: (0, hh))],
        out_specs=pl.BlockSpec((rows, dv), lambda hh, i: (i, hh)),
        out_shape=jax.ShapeDtypeStruct((t, h * dv), BF16),
        scratch_shapes=[pltpu.VMEM((dk, dv), F32)],
        compiler_params=_params("arbitrary", "arbitrary"),
        name="retention",
    )(chunk_decay, proj, proj, proj, proj, cos, sin, d_intra, q_decay, k_decay, gain.reshape(1, h * dv))


def _gmlp_kernel(u_ref, vs_ref, gain_ref, ws_ref, bs_ref, o_ref, *, chunk, nchunk):
    r = lax.broadcasted_iota(jnp.int32, (chunk, chunk), 0)
    s = lax.broadcasted_iota(jnp.int32, (chunk, chunk), 1)
    w = jnp.where(r >= s, ws_ref[0], 0.0).astype(BF16)
    gain = gain_ref[...]
    bs = bs_ref[0]
    for c in range(nchunk):
        rows = pl.ds(c * chunk, chunk)
        v = jax.nn.gelu(vs_ref[rows, :].astype(F32))
        v = v - jnp.mean(v, axis=-1, keepdims=True)
        v = v * lax.rsqrt(jnp.mean(v * v, axis=-1, keepdims=True) + EPS) * gain
        mixed = jnp.dot(w, v.astype(BF16), preferred_element_type=F32) + bs
        o_ref[rows, :] = (jax.nn.gelu(u_ref[rows, :].astype(F32)) * mixed).astype(o_ref.dtype)


def _gmlp(proj, gain, ws, bs, *, rows=512):
    t = proj.shape[0]
    g = GMLP_GROUPS
    dim = proj.shape[1] // 6 // g
    c = GMLP_CHUNK
    return pl.pallas_call(
        functools.partial(_gmlp_kernel, chunk=c, nchunk=rows // c),
        grid=(g, t // rows),
        in_specs=[pl.BlockSpec((rows, dim), lambda gg, i: (i, 4 * g + gg)),
                  pl.BlockSpec((rows, dim), lambda gg, i: (i, 5 * g + gg)),
                  pl.BlockSpec((1, dim), lambda gg, i: (0, gg)),
                  pl.BlockSpec((1, c, c), lambda gg, i: (gg, 0, 0)),
                  pl.BlockSpec((1, c, 1), lambda gg, i: (gg, 0, 0))],
        out_specs=pl.BlockSpec((rows, dim), lambda gg, i: (i, gg)),
        out_shape=jax.ShapeDtypeStruct((t, g * dim), BF16),
        compiler_params=_params("arbitrary", "arbitrary"),
        name="gmlp",
    )(proj, proj, gain.reshape(1, g * dim), ws, bs.reshape(g, c, 1))


def _hgrn_kernel(zq_ref, zf_ref, zi_ref, zg_ref, lb_ref, gain_ref, o_ref, state_ref, *, chunk, nchunk):
    @pl.when(pl.program_id(1) == 0)
    def _():
        state_ref[...] = jnp.zeros_like(state_ref)

    rows_total = chunk * nchunk
    lb = lb_ref[...]
    gain = gain_ref[...]
    zf = zf_ref[...].astype(F32)
    f = lb + (1.0 - lb) * jax.nn.sigmoid(zf)
    kk = (1.0 - lb) * jax.nn.sigmoid(-zf)
    log_f = jnp.log(f)
    r = lax.broadcasted_iota(jnp.int32, (rows_total, rows_total), 0)
    s = lax.broadcasted_iota(jnp.int32, (rows_total, rows_total), 1)
    tri = jnp.where((r >= s) & ((r // chunk) == (s // chunk)), 1.0, 0.0).astype(BF16)
    p0 = log_f.astype(BF16)
    r0 = log_f - p0.astype(F32)
    p1 = r0.astype(BF16)
    p2 = (r0 - p1.astype(F32)).astype(BF16)
    cum_all = (jnp.dot(tri, p0, preferred_element_type=F32)
               + jnp.dot(tri, p1, preferred_element_type=F32)
               + jnp.dot(tri, p2, preferred_element_type=F32))
    q_all = jax.nn.silu(zq_ref[...].astype(F32))
    ci = lax.broadcasted_iota(jnp.int32, (chunk, chunk), 0)
    cj = lax.broadcasted_iota(jnp.int32, (chunk, chunk), 1)
    causal = ci >= cj
    for c in range(nchunk):
        sl = slice(c * chunk, (c + 1) * chunk)
        cum = cum_all[sl]
        last = cum[chunk - 1:chunk, :]
        q_dec = (q_all[sl] * jnp.exp(cum)).astype(BF16)
        k_dec = (kk[sl] * jnp.exp(-cum)).astype(BF16)
        v = zi_ref[sl, :]
        scores = lax.dot_general(q_dec, k_dec, (((1,), (1,)), ((), ())), preferred_element_type=F32)
        scores = jnp.where(causal, scores, 0.0)
        state_t = state_ref[...]
        o = (jnp.dot(scores.astype(BF16), v, preferred_element_type=F32)
             + lax.dot_general(q_dec, state_t.astype(BF16), (((1,), (1,)), ((), ())),
                               preferred_element_type=F32))
        k_out = (kk[sl] * jnp.exp(last - cum)).astype(BF16)
        state_ref[...] = jnp.exp(last) * state_t + lax.dot_general(
            v, k_out, (((0,), (0,)), ((), ())), preferred_element_type=F32)
        o = o * lax.rsqrt(jnp.mean(o * o, axis=-1, keepdims=True) + EPS) * gain
        o_ref[sl, :] = (o * jax.nn.silu(zg_ref[sl, :].astype(F32))).astype(o_ref.dtype)


def _hgrn2(proj, lb, gain, *, rows=256):
    t = proj.shape[0]
    width = proj.shape[1] // 4
    dk = HGRN_DK
    h = width // dk
    c = HGRN_CHUNK

    def col(off):
        return pl.BlockSpec((rows, dk), lambda hh, i: (i, off + hh))

    vec = pl.BlockSpec((1, dk), lambda hh, i: (0, hh))
    return pl.pallas_call(
        functools.partial(_hgrn_kernel, chunk=c, nchunk=rows // c),
        grid=(h, t // rows),
        in_specs=[col(0), col(h), col(2 * h), col(3 * h), vec, vec],
        out_specs=pl.BlockSpec((rows, dk), lambda hh, i: (i, hh)),
        out_shape=jax.ShapeDtypeStruct((t, width), BF16),
        scratch_shapes=[pltpu.VMEM((dk, dk), F32)],
        compiler_params=_params("arbitrary", "arbitrary"),
        name="hgrn2",
    )(proj, proj, proj, proj, lb.reshape(1, width), gain.reshape(1, width))


def _matmul_residual_kernel(*refs, n_act):
    x_ref = refs[0]
    a_refs = refs[1:1 + n_act]
    w_refs = refs[1 + n_act:1 + 2 * n_act]
    o_ref = refs[1 + 2 * n_act]
    acc = x_ref[...]
    for a_ref, w_ref in zip(a_refs, w_refs):
        acc = acc + jnp.dot(a_ref[...], w_ref[...], preferred_element_type=F32)
    o_ref[...] = acc


def _matmul_residual(x, acts, ws, *, tm=256):
    t, d = x.shape
    row = pl.BlockSpec((tm, d), lambda i: (i, 0))
    in_specs = [row]
    in_specs += [pl.BlockSpec((tm, a.shape[1]), lambda i: (i, 0)) for a in acts]
    in_specs += [pl.BlockSpec(w.shape, lambda i: (0, 0)) for w in ws]
    return pl.pallas_call(
        functools.partial(_matmul_residual_kernel, n_act=len(acts)),
        grid=(t // tm,),
        in_specs=in_specs, out_specs=row,
        out_shape=jax.ShapeDtypeStruct((t, d), F32),
        compiler_params=_params("arbitrary"),
        name="matmul_residual",
    )(x, *acts, *ws)


def _norm_router_kernel(x_ref, g_ref, wr_ref, br_ref, h_ref, route_ref):
    h = _rms(x_ref[...], g_ref[...])
    h_ref[...] = h
    logits = jnp.dot(h, wr_ref[...], preferred_element_type=F32,
                     precision=lax.Precision.HIGHEST) + br_ref[...]
    lane = lax.broadcasted_iota(jnp.int32, logits.shape, 1)
    lane_f = lane.astype(F32)
    neg = -jnp.inf
    big = float(ROUTE_LANES)
    lg = jnp.where(lane < N_GROUPS, logits, neg)
    mg = jnp.max(lg, axis=-1, keepdims=True)
    eg = jnp.exp(lg - mg)
    pg = eg / jnp.sum(eg, axis=-1, keepdims=True)
    p_sel = jnp.max(pg, axis=-1, keepdims=True)
    g_sel = jnp.min(jnp.where(lg == mg, lane_f, big), axis=-1, keepdims=True)
    e_grp = ((lane - N_GROUPS) // EXPERTS_PER_GROUP).astype(F32)
    in_grp = jnp.where(lane >= N_GROUPS, e_grp, -1.0) == g_sel
    le = jnp.where(in_grp, logits, neg)
    t1 = jnp.max(le, axis=-1, keepdims=True)
    i1 = jnp.min(jnp.where(le == t1, lane_f, big), axis=-1, keepdims=True)
    le2 = jnp.where(lane_f == i1, neg, le)
    t2 = jnp.max(le2, axis=-1, keepdims=True)
    i2 = jnp.min(jnp.where(le2 == t2, lane_f, big), axis=-1, keepdims=True)
    e2 = jnp.exp(t2 - t1)
    den = 1.0 + e2
    gate1 = p_sel * (1.0 / den)
    gate2 = p_sel * (e2 / den)
    route_ref[...] = jnp.where(lane == 0, i1 - N_GROUPS,
                               jnp.where(lane == 1, i2 - N_GROUPS,
                                         jnp.where(lane == 2, gate1,
                                                   jnp.where(lane == 3, gate2, 0.0))))


def _norm_router(x, g, w_rg, b_rg, w_re, b_re, *, tm=256):
    t, d = x.shape
    used = N_GROUPS + N_EXPERTS
    wr = jnp.zeros((d, ROUTE_LANES), F32).at[:, :N_GROUPS].set(w_rg).at[:, N_GROUPS:used].set(w_re)
    br = jnp.zeros((1, ROUTE_LANES), F32).at[0, :N_GROUPS].set(b_rg).at[0, N_GROUPS:used].set(b_re)
    row = pl.BlockSpec((tm, d), lambda i: (i, 0))
    return pl.pallas_call(
        _norm_router_kernel,
        grid=(t // tm,),
        in_specs=[row, pl.BlockSpec((1, d), lambda i: (0, 0)),
                  pl.BlockSpec((d, ROUTE_LANES), lambda i: (0, 0)),
                  pl.BlockSpec((1, ROUTE_LANES), lambda i: (0, 0))],
        out_specs=[row, pl.BlockSpec((tm, ROUTE_LANES), lambda i: (i, 0))],
        out_shape=[jax.ShapeDtypeStruct((t, d), F32), jax.ShapeDtypeStruct((t, ROUTE_LANES), F32)],
        compiler_params=_params("arbitrary"),
        name="norm_router",
    )(x, g.reshape(1, d), wr, br)


def _moe_kernel(be_ref, nused_ref, src_ref, dst_ref, gate_ref, h_hbm, wg_ref, wu_ref, wd_ref, out_hbm,
                xbuf, obuf, gsem, ssem):
    del be_ref
    blk = xbuf.shape[0]

    def gather(r):
        return pltpu.make_async_copy(h_hbm.at[pl.ds(src_ref[0, 0, r], 1)], xbuf.at[pl.ds(r, 1)], gsem)

    def scatter(r):
        return pltpu.make_async_copy(obuf.at[pl.ds(r, 1)], out_hbm.at[pl.ds(dst_ref[0, 0, r], 1)], ssem)

    def each_row(fn):
        def body(r, carry):
            fn(r)
            return carry
        lax.fori_loop(0, blk, body, 0)

    def each_valid_row(fn):
        def body(r, carry):
            pl.when(dst_ref[0, 0, r] >= 0)(lambda: fn(r))
            return carry
        lax.fori_loop(0, blk, body, 0)

    @pl.when(pl.program_id(0) < nused_ref[0])
    def _():
        each_row(lambda r: gather(r).start())
        each_row(lambda r: gather(r).wait())
        x = xbuf[...].astype(BF16)
        hg = jnp.dot(x, wg_ref[0], preferred_element_type=F32)
        hu = jnp.dot(x, wu_ref[0], preferred_element_type=F32)
        hid = (jax.nn.silu(hg) * hu).astype(BF16)
        obuf[...] = jnp.dot(hid, wd_ref[0], preferred_element_type=F32) * gate_ref[0]
        each_valid_row(lambda r: scatter(r).start())
        each_valid_row(lambda r: scatter(r).wait())


def _moe_dispatch(route, t):
    m = t * TOP_K
    expert_id = route[:, :TOP_K].astype(jnp.int32).reshape(m)
    gates = route[:, TOP_K:2 * TOP_K].reshape(m)
    onehot = (expert_id[:, None] == jnp.arange(N_EXPERTS, dtype=jnp.int32)[None, :]).astype(jnp.int32)
    incl = jnp.cumsum(onehot, axis=0)
    counts = incl[-1]
    rank = jnp.sum(incl * onehot, axis=1) - 1
    padded = (counts + MOE_BLOCK - 1) // MOE_BLOCK * MOE_BLOCK
    padded_ends = jnp.cumsum(padded)
    padded_starts = padded_ends - padded
    dest = padded_starts[expert_id] + rank
    n_blocks = -(-(m + N_EXPERTS * (MOE_BLOCK - 1)) // MOE_BLOCK)
    cap = n_blocks * MOE_BLOCK
    flat = jnp.arange(m, dtype=jnp.int32)
    token = flat // TOP_K
    out_row = (flat % TOP_K) * t + token
    buf_src = jnp.zeros((cap,), jnp.int32).at[dest].set(token)
    buf_dst = jnp.full((cap,), -1, jnp.int32).at[dest].set(out_row)
    buf_gate = jnp.zeros((cap,), F32).at[dest].set(gates)
    block_start = jnp.arange(n_blocks, dtype=jnp.int32) * MOE_BLOCK
    block_expert = jnp.minimum(jnp.searchsorted(padded_ends, block_start, side='right'),
                               N_EXPERTS - 1).astype(jnp.int32)
    n_used = (padded_ends[-1] // MOE_BLOCK).astype(jnp.int32).reshape(1)
    return (block_expert, n_used, buf_src.reshape(n_blocks, 1, MOE_BLOCK),
            buf_dst.reshape(n_blocks, 1, MOE_BLOCK), buf_gate.reshape(n_blocks, MOE_BLOCK, 1))


def _moe(h, route, w_gate, w_up, w_down):
    t, d = h.shape
    ff = w_gate.shape[-1]
    block_expert, n_used, buf_src, buf_dst, buf_gate = _moe_dispatch(route, t)
    n_blocks = buf_src.shape[0]
    m = t * TOP_K
    idx_spec = pl.BlockSpec((1, 1, MOE_BLOCK), lambda b, be, nu: (b, 0, 0), memory_space=pltpu.SMEM)
    grid_spec = pltpu.PrefetchScalarGridSpec(
        num_scalar_prefetch=2,
        grid=(n_blocks,),
        in_specs=[idx_spec, idx_spec,
                  pl.BlockSpec((1, MOE_BLOCK, 1), lambda b, be, nu: (b, 0, 0)),
                  pl.BlockSpec(memory_space=pl.ANY),
                  pl.BlockSpec((1, d, ff), lambda b, be, nu: (be[b], 0, 0)),
                  pl.BlockSpec((1, d, ff), lambda b, be, nu: (be[b], 0, 0)),
                  pl.BlockSpec((1, ff, d), lambda b, be, nu: (be[b], 0, 0))],
        out_specs=pl.BlockSpec(memory_space=pl.ANY),
        scratch_shapes=[pltpu.VMEM((MOE_BLOCK, d), F32), pltpu.VMEM((MOE_BLOCK, d), F32),
                        pltpu.SemaphoreType.DMA(()), pltpu.SemaphoreType.DMA(())],
    )
    return pl.pallas_call(
        _moe_kernel,
        grid_spec=grid_spec,
        out_shape=jax.ShapeDtypeStruct((m, d), F32),
        compiler_params=_params("arbitrary"),
        name="moe_experts",
    )(block_expert, n_used, buf_src, buf_dst, buf_gate, h, w_gate, w_up, w_down)


def _final_norm_kernel(x_ref, y0_ref, y1_ref, g_ref, o_ref):
    x = x_ref[...] + (y0_ref[...] + y1_ref[...])
    o_ref[...] = _rms(x, g_ref[...])


def _final_norm(x, y2, g, *, tm=256):
    t, d = x.shape
    nt = t // tm
    row = pl.BlockSpec((tm, d), lambda i: (i, 0))
    return pl.pallas_call(
        _final_norm_kernel,
        grid=(nt,),
        in_specs=[row, row, pl.BlockSpec((tm, d), lambda i: (i + nt, 0)),
                  pl.BlockSpec((1, d), lambda i: (0, 0))],
        out_specs=row,
        out_shape=jax.ShapeDtypeStruct((t, d), F32),
        compiler_params=_params("arbitrary"),
        name="final_norm",
    )(x, y2, y2, g.reshape(1, d))


def kernel(x, attn_norm, ffn_norm, final_norm, w_in_ab, ret_norm, gmlp_norm, gmlp_ws, gmlp_bs, w_out_ab, w_in_c, lb_params, hgrn_norm, w_out_c, router_w_group, router_b_group, router_w_expert, router_b_expert, w_gate, w_up, w_down):
    b, s, d = x.shape
    depth = attn_norm.shape[0]
    lb_soft = jax.nn.softmax(lb_params.astype(F32), axis=0)
    lower_bounds = jnp.cumsum(lb_soft, axis=0) - lb_soft[0]
    xt = x.reshape(b * s, d)
    y2 = None
    for layer in range(depth):
        i = layer // 2
        if layer % 2 == 0:
            proj, xt = _norm_matmul(xt, y2, attn_norm[layer], w_in_ab[i].astype(BF16))
            ret = _retention(proj, ret_norm[i])
            gm = _gmlp(proj, gmlp_norm[i], gmlp_ws[i], gmlp_bs[i])
            w_out = w_out_ab[i].astype(BF16)
            nr = ret.shape[1]
            xt = _matmul_residual(xt, [ret, gm], [w_out[:nr], w_out[nr:]])
        else:
            proj, xt = _norm_matmul(xt, y2, attn_norm[layer], w_in_c[i].astype(BF16))
            mixed = _hgrn2(proj, lower_bounds[layer], hgrn_norm[i])
            xt = _matmul_residual(xt, [mixed], [w_out_c[i].astype(BF16)])
        h, route = _norm_router(xt, ffn_norm[layer], router_w_group[layer], router_b_group[layer],
                                router_w_expert[layer], router_b_expert[layer])
        y2 = _moe(h, route, w_gate[layer].astype(BF16), w_up[layer].astype(BF16),
                  w_down[layer].astype(BF16))
    out = _final_norm(xt, y2, final_norm)
    return out.reshape(b, s, d)
```

```python
import functools

import jax
import jax.numpy as jnp
from jax import lax
from jax.experimental import pallas as pl
from jax.experimental.pallas import tpu as pltpu

F32 = jnp.float32
BF16 = jnp.bfloat16
EPS = 1e-6

RET_HEADS = 4
RET_CHUNK = 128
ROPE_BASE = 10000.0
GMLP_GROUPS = 4
GMLP_CHUNK = 128
HGRN_DK = 128
HGRN_CHUNK = 32
N_GROUPS = 4
EXPERTS_PER_GROUP = 8
N_EXPERTS = N_GROUPS * EXPERTS_PER_GROUP
TOP_K = 2
MOE_BLOCK = 128
ROUTE_LANES = 128

VMEM_LIMIT = 48 * 1024 * 1024
MOE_UP_VMEM_LIMIT = 56 * 1024 * 1024


def _params(*sem):
    return pltpu.CompilerParams(dimension_semantics=sem, vmem_limit_bytes=VMEM_LIMIT)


def _rms(x, g):
    return x * lax.rsqrt(jnp.mean(x * x, axis=-1, keepdims=True) + EPS) * g


def _norm_matmul_kernel(x_ref, g_ref, w_ref, proj_ref, xn_ref):
    @pl.when(pl.program_id(1) == 0)
    def _():
        xn_ref[...] = _rms(x_ref[...], g_ref[...]).astype(BF16)

    proj_ref[...] = jnp.dot(xn_ref[...], w_ref[...],
                            preferred_element_type=F32).astype(proj_ref.dtype)


def _norm_matmul(x, g, w, *, tm=512, tn=1024):
    t, d = x.shape
    n = w.shape[1]
    return pl.pallas_call(
        _norm_matmul_kernel,
        grid=(t // tm, n // tn),
        in_specs=[pl.BlockSpec((tm, d), lambda i, j: (i, 0)),
                  pl.BlockSpec((1, d), lambda i, j: (0, 0)),
                  pl.BlockSpec((d, tn), lambda i, j: (0, j))],
        out_specs=pl.BlockSpec((tm, tn), lambda i, j: (i, j)),
        out_shape=jax.ShapeDtypeStruct((t, n), BF16),
        scratch_shapes=[pltpu.VMEM((tm, d), BF16)],
        compiler_params=_params("arbitrary", "arbitrary"),
        name="norm_matmul",
    )(x, g.reshape(1, d), w)


def _rope(x, cos, sin):
    half = x.shape[-1] // 2
    x1, x2 = x[:, :half], x[:, half:]
    return jnp.concatenate([x1 * cos - x2 * sin, x2 * cos + x1 * sin], axis=-1)


def _retention_kernel(cd_ref, q_ref, k_ref, v_ref, g_ref, cos_ref, sin_ref, dint_ref, qd_ref, kd_ref,
                      gain_ref, o_ref, state_ref, *, chunk, nchunk):
    @pl.when(pl.program_id(1) == 0)
    def _():
        state_ref[...] = jnp.zeros_like(state_ref)

    dk = q_ref.shape[-1]
    dint = dint_ref[0]
    qd = qd_ref[0]
    kd = kd_ref[0]
    cd = cd_ref[pl.program_id(0)]
    gain = gain_ref[...]
    for c in range(nchunk):
        rows = pl.ds(c * chunk, chunk)
        cos = cos_ref[rows, :]
        sin = sin_ref[rows, :]
        q = _rope(q_ref[rows, :].astype(F32), cos, sin)
        k = _rope(k_ref[rows, :].astype(F32), cos, sin) * (dk ** -0.5)
        v = v_ref[rows, :]
        scores = lax.dot_general(q.astype(BF16), k.astype(BF16), (((1,), (1,)), ((), ())),
                                 preferred_element_type=F32) * dint
        state = state_ref[...]
        o = (jnp.dot(scores.astype(BF16), v, preferred_element_type=F32)
             + jnp.dot((q * qd).astype(BF16), state.astype(BF16), preferred_element_type=F32))
        state_ref[...] = cd * state + lax.dot_general(
            (k * kd).astype(BF16), v, (((0,), (0,)), ((), ())), preferred_element_type=F32)
        o = o - jnp.mean(o, axis=-1, keepdims=True)
        o = o * lax.rsqrt(jnp.mean(o * o, axis=-1, keepdims=True) + EPS) * gain
        o_ref[rows, :] = (jax.nn.silu(g_ref[rows, :].astype(F32)) * o).astype(o_ref.dtype)


def _retention(proj, gain, *, rows=512):
    t = proj.shape[0]
    h = RET_HEADS
    dk = proj.shape[1] // 6 // h
    dv = dk
    c = RET_CHUNK
    f32 = F32
    inv = ROPE_BASE ** (-jnp.arange(0, dk, 2, dtype=f32) / dk)
    ang = jnp.arange(t, dtype=f32)[:, None] * inv[None, :]
    cos, sin = jnp.cos(ang), jnp.sin(ang)
    log_gamma = jnp.log(1.0 - jnp.exp2(-5.0 - jnp.arange(h, dtype=f32)))
    idx = jnp.arange(c, dtype=f32)
    diff = idx[:, None] - idx[None, :]
    d_intra = jnp.where(diff >= 0, jnp.exp(jnp.maximum(diff, 0.0) * log_gamma[:, None, None]), 0.0)
    q_decay = jnp.broadcast_to(jnp.exp((idx + 1.0)[None, :] * log_gamma[:, None])[..., None], (h, c, dk))
    k_decay = jnp.broadcast_to(jnp.exp((c - 1.0 - idx)[None, :] * log_gamma[:, None])[..., None], (h, c, dk))
    chunk_decay = jnp.exp(c * log_gamma)

    def col(off):
        return pl.BlockSpec((rows, dk), lambda hh, i: (i, off + hh))

    tab = pl.BlockSpec((rows, dk // 2), lambda hh, i: (i, 0))
    per_head = lambda shp: pl.BlockSpec((1,) + shp, lambda hh, i: (hh, 0, 0))
    return pl.pallas_call(
        functools.partial(_retention_kernel, chunk=c, nchunk=rows // c),
        grid=(h, t // rows),
        in_specs=[pl.BlockSpec(memory_space=pltpu.SMEM),
                  col(0), col(h), col(2 * h), col(3 * h), tab, tab,
                  per_head((c, c)), per_head((c, dk)), per_head((c, dk)),
                  pl.BlockSpec((1, dv), lambda hh, i: (0, hh))],
        out_specs=pl.BlockSpec((rows, dv), lambda hh, i: (i, hh)),
        out_shape=jax.ShapeDtypeStruct((t, h * dv), BF16),
        scratch_shapes=[pltpu.VMEM((dk, dv), F32)],
        compiler_params=_params("arbitrary", "arbitrary"),
        name="retention",
    )(chunk_decay, proj, proj, proj, proj, cos, sin, d_intra, q_decay, k_decay, gain.reshape(1, h * dv))


def _gmlp_kernel(u_ref, vs_ref, gain_ref, ws_ref, bs_ref, o_ref, *, chunk, nchunk):
    r = lax.broadcasted_iota(jnp.int32, (chunk, chunk), 0)
    s = lax.broadcasted_iota(jnp.int32, (chunk, chunk), 1)
    w = jnp.where(r >= s, ws_ref[0], 0.0).astype(BF16)
    gain = gain_ref[...]
    bs = bs_ref[0]
    for c in range(nchunk):
        rows = pl.ds(c * chunk, chunk)
        v = jax.nn.gelu(vs_ref[rows, :].astype(F32))
        v = v - jnp.mean(v, axis=-1, keepdims=True)
        v = v * lax.rsqrt(jnp.mean(v * v, axis=-1, keepdims=True) + EPS) * gain
        mixed = jnp.dot(w, v.astype(BF16), preferred_element_type=F32) + bs
        o_ref[rows, :] = (jax.nn.gelu(u_ref[rows, :].astype(F32)) * mixed).astype(o_ref.dtype)


def _gmlp(proj, gain, ws, bs, *, rows=512):
    t = proj.shape[0]
    g = GMLP_GROUPS
    dim = proj.shape[1] // 6 // g
    c = GMLP_CHUNK
    return pl.pallas_call(
        functools.partial(_gmlp_kernel, chunk=c, nchunk=rows // c),
        grid=(g, t // rows),
        in_specs=[pl.BlockSpec((rows, dim), lambda gg, i: (i, 4 * g + gg)),
                  pl.BlockSpec((rows, dim), lambda gg, i: (i, 5 * g + gg)),
                  pl.BlockSpec((1, dim), lambda gg, i: (0, gg)),
                  pl.BlockSpec((1, c, c), lambda gg, i: (gg, 0, 0)),
                  pl.BlockSpec((1, c, 1), lambda gg, i: (gg, 0, 0))],
        out_specs=pl.BlockSpec((rows, dim), lambda gg, i: (i, gg)),
        out_shape=jax.ShapeDtypeStruct((t, g * dim), BF16),
        compiler_params=_params("arbitrary", "arbitrary"),
        name="gmlp",
    )(proj, proj, gain.reshape(1, g * dim), ws, bs.reshape(g, c, 1))


def _hgrn_kernel(zq_ref, zf_ref, zi_ref, zg_ref, lb_ref, gain_ref, o_ref, state_ref, *, chunk, nchunk):
    @pl.when(pl.program_id(1) == 0)
    def _():
        state_ref[...] = jnp.zeros_like(state_ref)

    rows_total = chunk * nchunk
    lb = lb_ref[...]
    gain = gain_ref[...]
    zf = zf_ref[...].astype(F32)
    f = lb + (1.0 - lb) * jax.nn.sigmoid(zf)
    kk = (1.0 - lb) * jax.nn.sigmoid(-zf)
    log_f = jnp.log(f)
    r = lax.broadcasted_iota(jnp.int32, (rows_total, rows_total), 0)
    s = lax.broadcasted_iota(jnp.int32, (rows_total, rows_total), 1)
    tri = jnp.where((r >= s) & ((r // chunk) == (s // chunk)), 1.0, 0.0).astype(BF16)
    p0 = log_f.astype(BF16)
    r0 = log_f - p0.astype(F32)
    p1 = r0.astype(BF16)
    p2 = (r0 - p1.astype(F32)).astype(BF16)
    cum_all = (jnp.dot(tri, p0, preferred_element_type=F32)
               + jnp.dot(tri, p1, preferred_element_type=F32)
               + jnp.dot(tri, p2, preferred_element_type=F32))
    q_all = jax.nn.silu(zq_ref[...].astype(F32))
    ci = lax.broadcasted_iota(jnp.int32, (chunk, chunk), 0)
    cj = lax.broadcasted_iota(jnp.int32, (chunk, chunk), 1)
    causal = ci >= cj
    for c in range(nchunk):
        sl = slice(c * chunk, (c + 1) * chunk)
        cum = cum_all[sl]
        last = cum[chunk - 1:chunk, :]
        q_dec = (q_all[sl] * jnp.exp(cum)).astype(BF16)
        k_dec = (kk[sl] * jnp.exp(-cum)).astype(BF16)
        v = zi_ref[sl, :]
        scores = lax.dot_general(q_dec, k_dec, (((1,), (1,)), ((), ())), preferred_element_type=F32)
        scores = jnp.where(causal, scores, 0.0)
        state_t = state_ref[...]
        o = (jnp.dot(scores.astype(BF16), v, preferred_element_type=F32)
             + lax.dot_general(q_dec, state_t.astype(BF16), (((1,), (1,)), ((), ())),
                               preferred_element_type=F32))
        k_out = (kk[sl] * jnp.exp(last - cum)).astype(BF16)
        state_ref[...] = jnp.exp(last) * state_t + lax.dot_general(
            v, k_out, (((0,), (0,)), ((), ())), preferred_element_type=F32)
        o = o * lax.rsqrt(jnp.mean(o * o, axis=-1, keepdims=True) + EPS) * gain
        o_ref[sl, :] = (o * jax.nn.silu(zg_ref[sl, :].astype(F32))).astype(o_ref.dtype)


def _hgrn2(proj, lb, gain, *, rows=256):
    t = proj.shape[0]
    width = proj.shape[1] // 4
    dk = HGRN_DK
    h = width // dk
    c = HGRN_CHUNK

    def col(off):
        return pl.BlockSpec((rows, dk), lambda hh, i: (i, off + hh))

    vec = pl.BlockSpec((1, dk), lambda hh, i: (0, hh))
    return pl.pallas_call(
        functools.partial(_hgrn_kernel, chunk=c, nchunk=rows // c),
        grid=(h, t // rows),
        in_specs=[col(0), col(h), col(2 * h), col(3 * h), vec, vec],
        out_specs=pl.BlockSpec((rows, dk), lambda hh, i: (i, hh)),
        out_shape=jax.ShapeDtypeStruct((t, width), BF16),
        scratch_shapes=[pltpu.VMEM((dk, dk), F32)],
        compiler_params=_params("arbitrary", "arbitrary"),
        name="hgrn2",
    )(proj, proj, proj, proj, lb.reshape(1, width), gain.reshape(1, width))


def _matmul_residual_kernel(*refs, n_act):
    x_ref = refs[0]
    a_refs = refs[1:1 + n_act]
    w_refs = refs[1 + n_act:1 + 2 * n_act]
    o_ref = refs[1 + 2 * n_act]
    acc = x_ref[...]
    for a_ref, w_ref in zip(a_refs, w_refs):
        acc = acc + jnp.dot(a_ref[...], w_ref[...], preferred_element_type=F32)
    o_ref[...] = acc


def _matmul_residual(x, acts, ws, *, tm=256):
    t, d = x.shape
    row = pl.BlockSpec((tm, d), lambda i: (i, 0))
    in_specs = [row]
    in_specs += [pl.BlockSpec((tm, a.shape[1]), lambda i: (i, 0)) for a in acts]
    in_specs += [pl.BlockSpec(w.shape, lambda i: (0, 0)) for w in ws]
    return pl.pallas_call(
        functools.partial(_matmul_residual_kernel, n_act=len(acts)),
        grid=(t // tm,),
        in_specs=in_specs, out_specs=row,
        out_shape=jax.ShapeDtypeStruct((t, d), F32),
        compiler_params=_params("arbitrary"),
        name="matmul_residual",
    )(x, *acts, *ws)


def _norm_router_kernel(x_ref, g_ref, wr_ref, br_ref, h_ref, route_ref, count_ref, run_ref):
    @pl.when(pl.program_id(0) == 0)
    def _():
        run_ref[...] = jnp.zeros_like(run_ref)

    h = _rms(x_ref[...], g_ref[...])
    h_ref[...] = h
    logits = jnp.dot(h, wr_ref[...], preferred_element_type=F32,
                     precision=lax.Precision.HIGHEST) + br_ref[...]
    lane = lax.broadcasted_iota(jnp.int32, logits.shape, 1)
    lane_f = lane.astype(F32)
    neg = -jnp.inf
    big = float(ROUTE_LANES)
    lg = jnp.where(lane < N_GROUPS, logits, neg)
    mg = jnp.max(lg, axis=-1, keepdims=True)
    eg = jnp.exp(lg - mg)
    pg = eg / jnp.sum(eg, axis=-1, keepdims=True)
    p_sel = jnp.max(pg, axis=-1, keepdims=True)
    g_sel = jnp.min(jnp.where(lg == mg, lane_f, big), axis=-1, keepdims=True)
    e_grp = ((lane - N_GROUPS) // EXPERTS_PER_GROUP).astype(F32)
    in_grp = jnp.where(lane >= N_GROUPS, e_grp, -1.0) == g_sel
    le = jnp.where(in_grp, logits, neg)
    t1 = jnp.max(le, axis=-1, keepdims=True)
    i1 = jnp.min(jnp.where(le == t1, lane_f, big), axis=-1, keepdims=True)
    le2 = jnp.where(lane_f == i1, neg, le)
    t2 = jnp.max(le2, axis=-1, keepdims=True)
    i2 = jnp.min(jnp.where(le2 == t2, lane_f, big), axis=-1, keepdims=True)
    e2 = jnp.exp(t2 - t1)
    den = 1.0 + e2
    gate1 = p_sel * (1.0 / den)
    gate2 = p_sel * (e2 / den)
    tm = logits.shape[0]
    hit1 = jnp.where(lane_f == i1, 1.0, 0.0)
    hit2 = jnp.where(lane_f == i2, 1.0, 0.0)
    hits = hit1 + hit2
    r = lax.broadcasted_iota(jnp.int32, (tm, tm), 0)
    s = lax.broadcasted_iota(jnp.int32, (tm, tm), 1)
    before = jnp.where(r > s, 1.0, 0.0).astype(BF16)
    prefix = jnp.dot(before, hits.astype(BF16), preferred_element_type=F32) + run_ref[...]
    rank1 = jnp.sum(prefix * hit1, axis=-1, keepdims=True)
    rank2 = jnp.sum(prefix * hit2, axis=-1, keepdims=True)
    total = run_ref[...] + jnp.sum(hits, axis=0, keepdims=True)
    run_ref[...] = total
    count_ref[...] = total
    vals = (i1 - N_GROUPS, i2 - N_GROUPS, gate1, gate2, rank1, rank2)
    slab = jnp.zeros_like(logits)
    for pos, val in enumerate(vals):
        slab = jnp.where(lane == pos, val, slab)
    route_ref[...] = slab


def _norm_router(x, g, w_rg, b_rg, w_re, b_re, *, tm=256):
    t, d = x.shape
    used = N_GROUPS + N_EXPERTS
    wr = jnp.zeros((d, ROUTE_LANES), F32).at[:, :N_GROUPS].set(w_rg).at[:, N_GROUPS:used].set(w_re)
    br = jnp.zeros((1, ROUTE_LANES), F32).at[0, :N_GROUPS].set(b_rg).at[0, N_GROUPS:used].set(b_re)
    row = pl.BlockSpec((tm, d), lambda i: (i, 0))
    return pl.pallas_call(
        _norm_router_kernel,
        grid=(t // tm,),
        in_specs=[row, pl.BlockSpec((1, d), lambda i: (0, 0)),
                  pl.BlockSpec((d, ROUTE_LANES), lambda i: (0, 0)),
                  pl.BlockSpec((1, ROUTE_LANES), lambda i: (0, 0))],
        out_specs=[row, pl.BlockSpec((tm, ROUTE_LANES), lambda i: (i, 0)),
                   pl.BlockSpec((1, ROUTE_LANES), lambda i: (0, 0))],
        out_shape=[jax.ShapeDtypeStruct((t, d), F32), jax.ShapeDtypeStruct((t, ROUTE_LANES), F32),
                   jax.ShapeDtypeStruct((1, ROUTE_LANES), F32)],
        scratch_shapes=[pltpu.VMEM((1, ROUTE_LANES), F32)],
        compiler_params=_params("arbitrary"),
        name="norm_router",
    )(x, g.reshape(1, d), wr, br)


def _expert_changed(be_ref, b):
    return (b == 0) | (be_ref[b] != be_ref[jnp.maximum(b - 1, 0)])


def _moe_up_kernel(be_ref, nu_ref, src_ref, nxt_ref, h_hbm, wg_ref, wu_ref, hid_ref,
                   xbuf, wg_bf, wu_bf, sem):
    b = pl.program_id(0)
    nu = nu_ref[0]
    slot = b % 2
    blk = xbuf.shape[1]

    def start_rows(idx_ref, dst_slot):
        def body(r, carry):
            pltpu.make_async_copy(h_hbm.at[pl.ds(idx_ref[0, 0, r], 1)],
                                  xbuf.at[dst_slot, pl.ds(r, 1)], sem.at[dst_slot]).start()
            return carry
        lax.fori_loop(0, blk, body, 0, unroll=8)

    @pl.when(b == 0)
    def _():
        start_rows(src_ref, 0)

    @pl.when(b + 1 < nu)
    def _():
        start_rows(nxt_ref, 1 - slot)

    @pl.when(b < nu)
    def _():
        pltpu.make_async_copy(h_hbm.at[pl.ds(0, blk)], xbuf.at[slot], sem.at[slot]).wait()

        @pl.when(_expert_changed(be_ref, b))
        def _():
            wg_bf[...] = wg_ref[0, 0].astype(BF16)
            wu_bf[...] = wu_ref[0, 0].astype(BF16)

        x = xbuf[slot].astype(BF16)
        hg = jnp.dot(x, wg_bf[...], preferred_element_type=F32)
        hu = jnp.dot(x, wu_bf[...], preferred_element_type=F32)
        hid_ref[...] = (jax.nn.silu(hg) * hu).astype(hid_ref.dtype)

    @pl.when(b >= nu)
    def _():
        hid_ref[...] = jnp.zeros_like(hid_ref)


def _moe_down_kernel(be_ref, nu_ref, hid_ref, wd_ref, out_ref, wd_bf):
    b = pl.program_id(0)
    nu = nu_ref[0]

    @pl.when(b < nu)
    def _():
        @pl.when(_expert_changed(be_ref, b))
        def _():
            wd_bf[...] = wd_ref[0, 0].astype(BF16)

        out_ref[...] = jnp.dot(hid_ref[...], wd_bf[...], preferred_element_type=F32)

    @pl.when(b >= nu)
    def _():
        out_ref[...] = jnp.zeros_like(out_ref)


def _combine_kernel(*refs, final):
    if final:
        pos_ref, nxt_ref, x_ref, route_ref, eo_hbm, g_ref, o_ref, ybuf, sem = refs
    else:
        pos_ref, nxt_ref, x_ref, route_ref, eo_hbm, o_ref, ybuf, sem = refs
    i = pl.program_id(0)
    slot = i % 2
    tm = x_ref.shape[0]

    def start_rows(idx_ref, dst_slot):
        def body(r, carry):
            for k in range(TOP_K):
                pltpu.make_async_copy(eo_hbm.at[pl.ds(idx_ref[0, 0, TOP_K * r + k], 1)],
                                      ybuf.at[dst_slot, k, pl.ds(r, 1)], sem.at[dst_slot]).start()
            return carry
        lax.fori_loop(0, tm, body, 0, unroll=4)

    @pl.when(i == 0)
    def _():
        start_rows(pos_ref, 0)

    @pl.when(i + 1 < pl.num_programs(0))
    def _():
        start_rows(nxt_ref, 1 - slot)

    for k in range(TOP_K):
        pltpu.make_async_copy(eo_hbm.at[pl.ds(0, tm)], ybuf.at[slot, k], sem.at[slot]).wait()
    route = route_ref[...]
    y = ybuf[slot, 0] * route[:, TOP_K:TOP_K + 1] + ybuf[slot, 1] * route[:, TOP_K + 1:TOP_K + 2]
    x = x_ref[...] + y
    o_ref[...] = _rms(x, g_ref[...]) if final else x


def _moe_dispatch(route, counts_slab, t):
    m = t * TOP_K
    expert = route[:, :TOP_K].astype(jnp.int32)
    rank = route[:, 2 * TOP_K:3 * TOP_K].astype(jnp.int32)
    counts = counts_slab[0, N_GROUPS:N_GROUPS + N_EXPERTS].astype(jnp.int32)
    padded = (counts + MOE_BLOCK - 1) // MOE_BLOCK * MOE_BLOCK
    padded_ends = jnp.cumsum(padded)
    padded_starts = padded_ends - padded
    dest = padded_starts[expert] + rank
    n_blocks = -(-(m + N_EXPERTS * (MOE_BLOCK - 1)) // MOE_BLOCK)
    cap = n_blocks * MOE_BLOCK
    token = jnp.broadcast_to(jnp.arange(t, dtype=jnp.int32)[:, None], (t, TOP_K))
    buf_src = jnp.zeros((cap,), jnp.int32).at[dest.reshape(m)].set(token.reshape(m))
    block_start = jnp.arange(n_blocks, dtype=jnp.int32) * MOE_BLOCK
    block_expert = jnp.minimum(jnp.searchsorted(padded_ends, block_start, side='right'),
                               N_EXPERTS - 1).astype(jnp.int32)
    n_used = (padded_ends[-1] // MOE_BLOCK).astype(jnp.int32).reshape(1)
    return block_expert, n_used, buf_src.reshape(n_blocks, 1, MOE_BLOCK), dest


def _moe_experts(h, block_expert, n_used, buf_src, w_gate, w_up, w_down, layer):
    t, d = h.shape
    ff = w_gate.shape[-1]
    n_blocks = buf_src.shape[0]
    cap = n_blocks * MOE_BLOCK
    idx_block = (1, 1, MOE_BLOCK)
    weight = lambda shp: pl.BlockSpec((1, 1) + shp, lambda b, be, nu: (layer, be[b], 0, 0))
    hid = pl.pallas_call(
        _moe_up_kernel,
        grid_spec=pltpu.PrefetchScalarGridSpec(
            num_scalar_prefetch=2,
            grid=(n_blocks,),
            in_specs=[pl.BlockSpec(idx_block, lambda b, be, nu: (b, 0, 0), memory_space=pltpu.SMEM),
                      pl.BlockSpec(idx_block, lambda b, be, nu: (jnp.minimum(b + 1, n_blocks - 1), 0, 0),
                                   memory_space=pltpu.SMEM),
                      pl.BlockSpec(memory_space=pl.ANY),
                      weight((d, ff)), weight((d, ff))],
            out_specs=pl.BlockSpec((MOE_BLOCK, ff), lambda b, be, nu: (b, 0)),
            scratch_shapes=[pltpu.VMEM((2, MOE_BLOCK, d), F32),
                            pltpu.VMEM((d, ff), BF16), pltpu.VMEM((d, ff), BF16),
                            pltpu.SemaphoreType.DMA((2,))],
        ),
        out_shape=jax.ShapeDtypeStruct((cap, ff), BF16),
        compiler_params=pltpu.CompilerParams(dimension_semantics=("arbitrary",),
                                             vmem_limit_bytes=MOE_UP_VMEM_LIMIT),
        name="moe_up",
    )(block_expert, n_used, buf_src, buf_src, h, w_gate, w_up)
    return pl.pallas_call(
        _moe_down_kernel,
        grid_spec=pltpu.PrefetchScalarGridSpec(
            num_scalar_prefetch=2,
            grid=(n_blocks,),
            in_specs=[pl.BlockSpec((MOE_BLOCK, ff), lambda b, be, nu: (b, 0)), weight((ff, d))],
            out_specs=pl.BlockSpec((MOE_BLOCK, d), lambda b, be, nu: (b, 0)),
            scratch_shapes=[pltpu.VMEM((ff, d), BF16)],
        ),
        out_shape=jax.ShapeDtypeStruct((cap, d), F32),
        compiler_params=_params("arbitrary"),
        name="moe_down",
    )(block_expert, n_used, hid, w_down)


def _combine(x, route, dest, expert_out, final_gain, *, tm=256):
    t, d = x.shape
    nt = t // tm
    final = final_gain is not None
    pos = dest.reshape(nt, 1, TOP_K * tm)
    idx_block = (1, 1, TOP_K * tm)
    row = pl.BlockSpec((tm, d), lambda i: (i, 0))
    in_specs = [pl.BlockSpec(idx_block, lambda i: (i, 0, 0), memory_space=pltpu.SMEM),
                pl.BlockSpec(idx_block, lambda i: (jnp.minimum(i + 1, nt - 1), 0, 0),
                             memory_space=pltpu.SMEM),
                row, pl.BlockSpec((tm, ROUTE_LANES), lambda i: (i, 0)),
                pl.BlockSpec(memory_space=pl.ANY)]
    args = [pos, pos, x, route, expert_out]
    if final:
        in_specs.append(pl.BlockSpec((1, d), lambda i: (0, 0)))
        args.append(final_gain.reshape(1, d))
    return pl.pallas_call(
        functools.partial(_combine_kernel, final=final),
        grid=(nt,),
        in_specs=in_specs, out_specs=row,
        out_shape=jax.ShapeDtypeStruct((t, d), F32),
        scratch_shapes=[pltpu.VMEM((2, TOP_K, tm, d), F32), pltpu.SemaphoreType.DMA((2,))],
        compiler_params=_params("arbitrary"),
        name="moe_combine",
    )(*args)


def kernel(x, attn_norm, ffn_norm, final_norm, w_in_ab, ret_norm, gmlp_norm, gmlp_ws, gmlp_bs, w_out_ab, w_in_c, lb_params, hgrn_norm, w_out_c, router_w_group, router_b_group, router_w_expert, router_b_expert, w_gate, w_up, w_down):
    b, s, d = x.shape
    assert b == 1, "the sequence mixers carry state along the flattened token axis"
    depth = attn_norm.shape[0]
    lb_soft = jax.nn.softmax(lb_params.astype(F32), axis=0)
    lower_bounds = jnp.cumsum(lb_soft, axis=0) - lb_soft[0]
    xt = x.reshape(b * s, d)
    t = b * s
    for layer in range(depth):
        i = layer // 2
        if layer % 2 == 0:
            proj = _norm_matmul(xt, attn_norm[layer], w_in_ab[i].astype(BF16))
            ret = _retention(proj, ret_norm[i])
            gm = _gmlp(proj, gmlp_norm[i], gmlp_ws[i], gmlp_bs[i])
            w_out = w_out_ab[i].astype(BF16)
            nr = ret.shape[1]
            xt = _matmul_residual(xt, [ret, gm], [w_out[:nr], w_out[nr:]])
        else:
            proj = _norm_matmul(xt, attn_norm[layer], w_in_c[i].astype(BF16))
            mixed = _hgrn2(proj, lower_bounds[layer], hgrn_norm[i])
            xt = _matmul_residual(xt, [mixed], [w_out_c[i].astype(BF16)])
        h, route, counts = _norm_router(xt, ffn_norm[layer], router_w_group[layer], router_b_group[layer],
                                        router_w_expert[layer], router_b_expert[layer])
        block_expert, n_used, buf_src, dest = _moe_dispatch(route, counts, t)
        expert_out = _moe_experts(h, block_expert, n_used, buf_src, w_gate, w_up, w_down, layer)
        xt = _combine(xt, route, dest, expert_out, final_norm if layer == depth - 1 else None)
    return xt.reshape(b, s, d)
```

```python
import functools

import jax
import jax.numpy as jnp
from jax import lax
from jax.experimental import pallas as pl
from jax.experimental.pallas import tpu as pltpu

F32 = jnp.float32
BF16 = jnp.bfloat16
EPS = 1e-6

RET_HEADS = 4
RET_CHUNK = 128
ROPE_BASE = 10000.0
GMLP_GROUPS = 4
GMLP_CHUNK = 128
HGRN_DK = 128
HGRN_CHUNK = 32
N_GROUPS = 4
EXPERTS_PER_GROUP = 8
N_EXPERTS = N_GROUPS * EXPERTS_PER_GROUP
TOP_K = 2
MOE_BLOCK = 128
ROUTE_LANES = 128

VMEM_LIMIT = 48 * 1024 * 1024
MOE_UP_VMEM_LIMIT = 56 * 1024 * 1024


def _params(*sem):
    return pltpu.CompilerParams(dimension_semantics=sem, vmem_limit_bytes=VMEM_LIMIT)


def _rms(x, g):
    return x * lax.rsqrt(jnp.mean(x * x, axis=-1, keepdims=True) + EPS) * g


def _norm_matmul_kernel(x_ref, g_ref, w_ref, proj_ref, xn_ref):
    @pl.when(pl.program_id(1) == 0)
    def _():
        xn_ref[...] = _rms(x_ref[...], g_ref[...]).astype(BF16)

    proj_ref[...] = jnp.dot(xn_ref[...], w_ref[...],
                            preferred_element_type=F32).astype(proj_ref.dtype)


def _norm_matmul(x, g, w, *, tm=512, tn=1024):
    t, d = x.shape
    n = w.shape[1]
    return pl.pallas_call(
        _norm_matmul_kernel,
        grid=(t // tm, n // tn),
        in_specs=[pl.BlockSpec((tm, d), lambda i, j: (i, 0)),
                  pl.BlockSpec((1, d), lambda i, j: (0, 0)),
                  pl.BlockSpec((d, tn), lambda i, j: (0, j))],
        out_specs=pl.BlockSpec((tm, tn), lambda i, j: (i, j)),
        out_shape=jax.ShapeDtypeStruct((t, n), BF16),
        scratch_shapes=[pltpu.VMEM((tm, d), BF16)],
        compiler_params=_params("arbitrary", "arbitrary"),
        name="norm_matmul",
    )(x, g.reshape(1, d), w)


def _rope(x, cos, sin):
    half = x.shape[-1] // 2
    x1, x2 = x[:, :half], x[:, half:]
    return jnp.concatenate([x1 * cos - x2 * sin, x2 * cos + x1 * sin], axis=-1)


def _retention_kernel(cd_ref, q_ref, k_ref, v_ref, g_ref, cos_ref, sin_ref, dint_ref, qd_ref, kd_ref,
                      gain_ref, o_ref, state_ref, *, chunk, nchunk):
    @pl.when(pl.program_id(1) == 0)
    def _():
        state_ref[...] = jnp.zeros_like(state_ref)

    dk = q_ref.shape[-1]
    dint = dint_ref[0]
    qd = qd_ref[0]
    kd = kd_ref[0]
    cd = cd_ref[pl.program_id(0)]
    gain = gain_ref[...]
    for c in range(nchunk):
        rows = pl.ds(c * chunk, chunk)
        cos = cos_ref[rows, :]
        sin = sin_ref[rows, :]
        q = _rope(q_ref[rows, :].astype(F32), cos, sin)
        k = _rope(k_ref[rows, :].astype(F32), cos, sin) * (dk ** -0.5)
        v = v_ref[rows, :]
        scores = lax.dot_general(q.astype(BF16), k.astype(BF16), (((1,), (1,)), ((), ())),
                                 preferred_element_type=F32) * dint
        state = state_ref[...]
        o = (jnp.dot(scores.astype(BF16), v, preferred_element_type=F32)
             + jnp.dot((q * qd).astype(BF16), state.astype(BF16), preferred_element_type=F32))
        state_ref[...] = cd * state + lax.dot_general(
            (k * kd).astype(BF16), v, (((0,), (0,)), ((), ())), preferred_element_type=F32)
        o = o - jnp.mean(o, axis=-1, keepdims=True)
        o = o * lax.rsqrt(jnp.mean(o * o, axis=-1, keepdims=True) + EPS) * gain
        o_ref[rows, :] = (jax.nn.silu(g_ref[rows, :].astype(F32)) * o).astype(o_ref.dtype)


def _retention(proj, gain, *, rows=512):
    t = proj.shape[0]
    h = RET_HEADS
    dk = proj.shape[1] // 6 // h
    dv = dk
    c = RET_CHUNK
    f32 = F32
    inv = ROPE_BASE ** (-jnp.arange(0, dk, 2, dtype=f32) / dk)
    ang = jnp.arange(t, dtype=f32)[:, None] * inv[None, :]
    cos, sin = jnp.cos(ang), jnp.sin(ang)
    log_gamma = jnp.log(1.0 - jnp.exp2(-5.0 - jnp.arange(h, dtype=f32)))
    idx = jnp.arange(c, dtype=f32)
    diff = idx[:, None] - idx[None, :]
    d_intra = jnp.where(diff >= 0, jnp.exp(jnp.maximum(diff, 0.0) * log_gamma[:, None, None]), 0.0)
    q_decay = jnp.broadcast_to(jnp.exp((idx + 1.0)[None, :] * log_gamma[:, None])[..., None], (h, c, dk))
    k_decay = jnp.broadcast_to(jnp.exp((c - 1.0 - idx)[None, :] * log_gamma[:, None])[..., None], (h, c, dk))
    chunk_decay = jnp.exp(c * log_gamma)

    def col(off):
        return pl.BlockSpec((rows, dk), lambda hh, i: (i, off + hh))

    tab = pl.BlockSpec((rows, dk // 2), lambda hh, i: (i, 0))
    per_head = lambda shp: pl.BlockSpec((1,) + shp, lambda hh, i: (hh, 0, 0))
    return pl.pallas_call(
        functools.partial(_retention_kernel, chunk=c, nchunk=rows // c),
        grid=(h, t // rows),
        in_specs=[pl.BlockSpec(memory_space=pltpu.SMEM),
                  col(0), col(h), col(2 * h), col(3 * h), tab, tab,
                  per_head((c, c)), per_head((c, dk)), per_head((c, dk)),
                  pl.BlockSpec((1, dv), lambda hh, i: (0, hh))],
        out_specs=pl.BlockSpec((rows, dv), lambda hh, i: (i, hh)),
        out_shape=jax.ShapeDtypeStruct((t, h * dv), BF16),
        scratch_shapes=[pltpu.VMEM((dk, dv), F32)],
        compiler_params=_params("arbitrary", "arbitrary"),
        name="retention",
    )(chunk_decay, proj, proj, proj, proj, cos, sin, d_intra, q_decay, k_decay, gain.reshape(1, h * dv))


def _gmlp_kernel(u_ref, vs_ref, gain_ref, ws_ref, bs_ref, o_ref, *, chunk, nchunk):
    r = lax.broadcasted_iota(jnp.int32, (chunk, chunk), 0)
    s = lax.broadcasted_iota(jnp.int32, (chunk, chunk), 1)
    w = jnp.where(r >= s, ws_ref[0], 0.0).astype(BF16)
    gain = gain_ref[...]
    bs = bs_ref[0]
    for c in range(nchunk):
        rows = pl.ds(c * chunk, chunk)
        v = jax.nn.gelu(vs_ref[rows, :].astype(F32))
        v = v - jnp.mean(v, axis=-1, keepdims=True)
        v = v * lax.rsqrt(jnp.mean(v * v, axis=-1, keepdims=True) + EPS) * gain
        mixed = jnp.dot(w, v.astype(BF16), preferred_element_type=F32) + bs
        o_ref[rows, :] = (jax.nn.gelu(u_ref[rows, :].astype(F32)) * mixed).astype(o_ref.dtype)


def _gmlp(proj, gain, ws, bs, *, rows=512):
    t = proj.shape[0]
    g = GMLP_GROUPS
    dim = proj.shape[1] // 6 // g
    c = GMLP_CHUNK
    return pl.pallas_call(
        functools.partial(_gmlp_kernel, chunk=c, nchunk=rows // c),
        grid=(g, t // rows),
        in_specs=[pl.BlockSpec((rows, dim), lambda gg, i: (i, 4 * g + gg)),
                  pl.BlockSpec((rows, dim), lambda gg, i: (i, 5 * g + gg)),
                  pl.BlockSpec((1, dim), lambda gg, i: (0, gg)),
                  pl.BlockSpec((1, c, c), lambda gg, i: (gg, 0, 0)),
                  pl.BlockSpec((1, c, 1), lambda gg, i: (gg, 0, 0))],
        out_specs=pl.BlockSpec((rows, dim), lambda gg, i: (i, gg)),
        out_shape=jax.ShapeDtypeStruct((t, g * dim), BF16),
        compiler_params=_params("arbitrary", "arbitrary"),
        name="gmlp",
    )(proj, proj, gain.reshape(1, g * dim), ws, bs.reshape(g, c, 1))


def _hgrn_kernel(zq_ref, zf_ref, zi_ref, zg_ref, lb_ref, gain_ref, tri_ref, keep_ref, o_ref, state_ref,
                 *, chunk, nchunk, heads, dk):
    @pl.when(pl.program_id(1) == 0)
    def _():
        state_ref[...] = jnp.zeros_like(state_ref)

    rows = chunk * nchunk
    nt = (((1,), (1,)), ((), ()))
    tri = tri_ref[...]
    keep = keep_ref[...] > 0.0
    chunk_of_row = lax.broadcasted_iota(jnp.int32, (rows, dk), 0) // chunk
    for hd in range(heads):
        cols = slice(hd * dk, (hd + 1) * dk)
        lb = lb_ref[:, cols]
        zf = zf_ref[:, cols].astype(F32)
        f = lb + (1.0 - lb) * jax.nn.sigmoid(zf)
        kk = (1.0 - lb) * jax.nn.sigmoid(-zf)
        log_f = jnp.log(f)
        p0 = log_f.astype(BF16)
        r0 = log_f - p0.astype(F32)
        p1 = r0.astype(BF16)
        p2 = (r0 - p1.astype(F32)).astype(BF16)
        cum = (jnp.dot(tri, p0, preferred_element_type=F32)
               + jnp.dot(tri, p1, preferred_element_type=F32)
               + jnp.dot(tri, p2, preferred_element_type=F32))
        lasts = [cum[(c + 1) * chunk - 1:(c + 1) * chunk, :] for c in range(nchunk)]
        last_rows = jnp.concatenate([jnp.broadcast_to(l, (chunk, dk)) for l in lasts], axis=0)
        q_dec = (jax.nn.silu(zq_ref[:, cols].astype(F32)) * jnp.exp(cum)).astype(BF16)
        k_dec = (kk * jnp.exp(-cum)).astype(BF16)
        k_out = (kk * jnp.exp(last_rows - cum)).astype(BF16)
        v = zi_ref[:, cols]
        v_t = v.astype(F32).T.astype(BF16)
        scores = lax.dot_general(q_dec, k_dec, nt, preferred_element_type=F32)
        scores = jnp.where(keep, scores, 0.0).astype(BF16)
        o = jnp.dot(scores, v, preferred_element_type=F32)
        state = state_ref[hd]
        inter = []
        for c in range(nchunk):
            sl = slice(c * chunk, (c + 1) * chunk)
            inter.append(lax.dot_general(q_dec[sl], state.astype(BF16), nt, preferred_element_type=F32))
            k_c = jnp.where(chunk_of_row == c, k_out, jnp.zeros_like(k_out))
            state = jnp.exp(lasts[c]) * state + jnp.dot(v_t, k_c, preferred_element_type=F32)
        state_ref[hd] = state
        o = o + jnp.concatenate(inter, axis=0)
        o = o * lax.rsqrt(jnp.mean(o * o, axis=-1, keepdims=True) + EPS) * gain_ref[:, cols]
        o_ref[:, cols] = (o * jax.nn.silu(zg_ref[:, cols].astype(F32))).astype(o_ref.dtype)


def _hgrn2(proj, lb, gain, *, rows=256, heads=4):
    t = proj.shape[0]
    width = proj.shape[1] // 4
    dk = HGRN_DK
    h = width // dk
    c = HGRN_CHUNK
    hw = heads * dk
    groups = h // heads
    r = jnp.arange(rows, dtype=jnp.int32)
    tri = ((r[:, None] >= r[None, :]) & ((r[:, None] // c) == (r[None, :] // c))).astype(F32)

    def col(off):
        return pl.BlockSpec((rows, hw), lambda hh, i: (i, off + hh))

    vec = pl.BlockSpec((1, hw), lambda hh, i: (0, hh))
    mask = pl.BlockSpec((rows, rows), lambda hh, i: (0, 0))
    return pl.pallas_call(
        functools.partial(_hgrn_kernel, chunk=c, nchunk=rows // c, heads=heads, dk=dk),
        grid=(groups, t // rows),
        in_specs=[col(0), col(groups), col(2 * groups), col(3 * groups), vec, vec, mask, mask],
        out_specs=pl.BlockSpec((rows, hw), lambda hh, i: (i, hh)),
        out_shape=jax.ShapeDtypeStruct((t, width), BF16),
        scratch_shapes=[pltpu.VMEM((heads, dk, dk), F32)],
        compiler_params=_params("arbitrary", "arbitrary"),
        name="hgrn2",
    )(proj, proj, proj, proj, lb.reshape(1, width), gain.reshape(1, width), tri.astype(BF16), tri)


def _matmul_residual_kernel(*refs, n_act):
    x_ref = refs[0]
    a_refs = refs[1:1 + n_act]
    w_refs = refs[1 + n_act:1 + 2 * n_act]
    o_ref = refs[1 + 2 * n_act]
    acc = x_ref[...]
    for a_ref, w_ref in zip(a_refs, w_refs):
        acc = acc + jnp.dot(a_ref[...], w_ref[...], preferred_element_type=F32)
    o_ref[...] = acc


def _matmul_residual(x, acts, ws, *, tm=256):
    t, d = x.shape
    row = pl.BlockSpec((tm, d), lambda i: (i, 0))
    in_specs = [row]
    in_specs += [pl.BlockSpec((tm, a.shape[1]), lambda i: (i, 0)) for a in acts]
    in_specs += [pl.BlockSpec(w.shape, lambda i: (0, 0)) for w in ws]
    return pl.pallas_call(
        functools.partial(_matmul_residual_kernel, n_act=len(acts)),
        grid=(t // tm,),
        in_specs=in_specs, out_specs=row,
        out_shape=jax.ShapeDtypeStruct((t, d), F32),
        compiler_params=_params("arbitrary"),
        name="matmul_residual",
    )(x, *acts, *ws)


def _norm_router_kernel(x_ref, g_ref, wr_ref, br_ref, h_ref, route_ref, count_ref, run_ref):
    @pl.when(pl.program_id(0) == 0)
    def _():
        run_ref[...] = jnp.zeros_like(run_ref)

    h = _rms(x_ref[...], g_ref[...])
    h_ref[...] = h
    logits = jnp.dot(h, wr_ref[...], preferred_element_type=F32,
                     precision=lax.Precision.HIGHEST) + br_ref[...]
    lane = lax.broadcasted_iota(jnp.int32, logits.shape, 1)
    lane_f = lane.astype(F32)
    neg = -jnp.inf
    big = float(ROUTE_LANES)
    lg = jnp.where(lane < N_GROUPS, logits, neg)
    mg = jnp.max(lg, axis=-1, keepdims=True)
    eg = jnp.exp(lg - mg)
    pg = eg / jnp.sum(eg, axis=-1, keepdims=True)
    p_sel = jnp.max(pg, axis=-1, keepdims=True)
    g_sel = jnp.min(jnp.where(lg == mg, lane_f, big), axis=-1, keepdims=True)
    e_grp = ((lane - N_GROUPS) // EXPERTS_PER_GROUP).astype(F32)
    in_grp = jnp.where(lane >= N_GROUPS, e_grp, -1.0) == g_sel
    le = jnp.where(in_grp, logits, neg)
    t1 = jnp.max(le, axis=-1, keepdims=True)
    i1 = jnp.min(jnp.where(le == t1, lane_f, big), axis=-1, keepdims=True)
    le2 = jnp.where(lane_f == i1, neg, le)
    t2 = jnp.max(le2, axis=-1, keepdims=True)
    i2 = jnp.min(jnp.where(le2 == t2, lane_f, big), axis=-1, keepdims=True)
    e2 = jnp.exp(t2 - t1)
    den = 1.0 + e2
    gate1 = p_sel * (1.0 / den)
    gate2 = p_sel * (e2 / den)
    tm = logits.shape[0]
    hit1 = jnp.where(lane_f == i1, 1.0, 0.0)
    hit2 = jnp.where(lane_f == i2, 1.0, 0.0)
    hits = hit1 + hit2
    r = lax.broadcasted_iota(jnp.int32, (tm, tm), 0)
    s = lax.broadcasted_iota(jnp.int32, (tm, tm), 1)
    before = jnp.where(r > s, 1.0, 0.0).astype(BF16)
    prefix = jnp.dot(before, hits.astype(BF16), preferred_element_type=F32) + run_ref[...]
    rank1 = jnp.sum(prefix * hit1, axis=-1, keepdims=True)
    rank2 = jnp.sum(prefix * hit2, axis=-1, keepdims=True)
    total = run_ref[...] + jnp.sum(hits, axis=0, keepdims=True)
    run_ref[...] = total
    count_ref[...] = total
    vals = (i1 - N_GROUPS, i2 - N_GROUPS, gate1, gate2, rank1, rank2)
    slab = jnp.zeros_like(logits)
    for pos, val in enumerate(vals):
        slab = jnp.where(lane == pos, val, slab)
    route_ref[...] = slab


def _norm_router(x, g, w_rg, b_rg, w_re, b_re, *, tm=256):
    t, d = x.shape
    used = N_GROUPS + N_EXPERTS
    wr = jnp.zeros((d, ROUTE_LANES), F32).at[:, :N_GROUPS].set(w_rg).at[:, N_GROUPS:used].set(w_re)
    br = jnp.zeros((1, ROUTE_LANES), F32).at[0, :N_GROUPS].set(b_rg).at[0, N_GROUPS:used].set(b_re)
    row = pl.BlockSpec((tm, d), lambda i: (i, 0))
    return pl.pallas_call(
        _norm_router_kernel,
        grid=(t // tm,),
        in_specs=[row, pl.BlockSpec((1, d), lambda i: (0, 0)),
                  pl.BlockSpec((d, ROUTE_LANES), lambda i: (0, 0)),
                  pl.BlockSpec((1, ROUTE_LANES), lambda i: (0, 0))],
        out_specs=[row, pl.BlockSpec((tm, ROUTE_LANES), lambda i: (i, 0)),
                   pl.BlockSpec((1, ROUTE_LANES), lambda i: (0, 0))],
        out_shape=[jax.ShapeDtypeStruct((t, d), F32), jax.ShapeDtypeStruct((t, ROUTE_LANES), F32),
                   jax.ShapeDtypeStruct((1, ROUTE_LANES), F32)],
        scratch_shapes=[pltpu.VMEM((1, ROUTE_LANES), F32)],
        compiler_params=_params("arbitrary"),
        name="norm_router",
    )(x, g.reshape(1, d), wr, br)


def _expert_changed(be_ref, b):
    return (b == 0) | (be_ref[b] != be_ref[jnp.maximum(b - 1, 0)])


def _moe_up_kernel(be_ref, nu_ref, src_ref, nxt_ref, h_hbm, wg_ref, wu_ref, hid_ref,
                   xbuf, wg_bf, wu_bf, sem):
    b = pl.program_id(0)
    nu = nu_ref[0]
    slot = b % 2
    blk = xbuf.shape[1]

    def start_rows(idx_ref, dst_slot):
        def body(r, carry):
            pltpu.make_async_copy(h_hbm.at[pl.ds(idx_ref[0, 0, r], 1)],
                                  xbuf.at[dst_slot, pl.ds(r, 1)], sem.at[dst_slot]).start()
            return carry
        lax.fori_loop(0, blk, body, 0, unroll=8)

    @pl.when(b == 0)
    def _():
        start_rows(src_ref, 0)

    @pl.when(b + 1 < nu)
    def _():
        start_rows(nxt_ref, 1 - slot)

    @pl.when(b < nu)
    def _():
        pltpu.make_async_copy(h_hbm.at[pl.ds(0, blk)], xbuf.at[slot], sem.at[slot]).wait()

        @pl.when(_expert_changed(be_ref, b))
        def _():
            wg_bf[...] = wg_ref[0, 0].astype(BF16)
            wu_bf[...] = wu_ref[0, 0].astype(BF16)

        x = xbuf[slot].astype(BF16)
        hg = jnp.dot(x, wg_bf[...], preferred_element_type=F32)
        hu = jnp.dot(x, wu_bf[...], preferred_element_type=F32)
        hid_ref[...] = (jax.nn.silu(hg) * hu).astype(hid_ref.dtype)

    @pl.when(b >= nu)
    def _():
        hid_ref[...] = jnp.zeros_like(hid_ref)


def _moe_down_kernel(be_ref, nu_ref, hid_ref, wd_ref, out_ref, wd_bf):
    b = pl.program_id(0)
    nu = nu_ref[0]

    @pl.when(b < nu)
    def _():
        @pl.when(_expert_changed(be_ref, b))
        def _():
            wd_bf[...] = wd_ref[0, 0].astype(BF16)

        out_ref[...] = jnp.dot(hid_ref[...], wd_bf[...], preferred_element_type=F32)

    @pl.when(b >= nu)
    def _():
        out_ref[...] = jnp.zeros_like(out_ref)


def _combine_kernel(*refs, final):
    if final:
        pos_ref, nxt_ref, x_ref, route_ref, eo_hbm, g_ref, o_ref, ybuf, sem = refs
    else:
        pos_ref, nxt_ref, x_ref, route_ref, eo_hbm, o_ref, ybuf, sem = refs
    i = pl.program_id(0)
    slot = i % 2
    tm = x_ref.shape[0]

    def start_rows(idx_ref, dst_slot):
        def body(r, carry):
            for k in range(TOP_K):
                pltpu.make_async_copy(eo_hbm.at[pl.ds(idx_ref[0, 0, TOP_K * r + k], 1)],
                                      ybuf.at[dst_slot, k, pl.ds(r, 1)], sem.at[dst_slot]).start()
            return carry
        lax.fori_loop(0, tm, body, 0, unroll=4)

    @pl.when(i == 0)
    def _():
        start_rows(pos_ref, 0)

    @pl.when(i + 1 < pl.num_programs(0))
    def _():
        start_rows(nxt_ref, 1 - slot)

    for k in range(TOP_K):
        pltpu.make_async_copy(eo_hbm.at[pl.ds(0, tm)], ybuf.at[slot, k], sem.at[slot]).wait()
    route = route_ref[...]
    y = ybuf[slot, 0] * route[:, TOP_K:TOP_K + 1] + ybuf[slot, 1] * route[:, TOP_K + 1:TOP_K + 2]
    x = x_ref[...] + y
    o_ref[...] = _rms(x, g_ref[...]) if final else x


def _moe_dispatch(route, counts_slab, t):
    m = t * TOP_K
    expert = route[:, :TOP_K].astype(jnp.int32)
    rank = route[:, 2 * TOP_K:3 * TOP_K].astype(jnp.int32)
    counts = counts_slab[0, N_GROUPS:N_GROUPS + N_EXPERTS].astype(jnp.int32)
    padded = (counts + MOE_BLOCK - 1) // MOE_BLOCK * MOE_BLOCK
    padded_ends = jnp.cumsum(padded)
    padded_starts = padded_ends - padded
    dest = padded_starts[expert] + rank
    n_blocks = -(-(m + N_EXPERTS * (MOE_BLOCK - 1)) // MOE_BLOCK)
    cap = n_blocks * MOE_BLOCK
    token = jnp.broadcast_to(jnp.arange(t, dtype=jnp.int32)[:, None], (t, TOP_K))
    buf_src = jnp.zeros((cap,), jnp.int32).at[dest.reshape(m)].set(token.reshape(m))
    block_start = jnp.arange(n_blocks, dtype=jnp.int32) * MOE_BLOCK
    block_expert = jnp.minimum(
        jnp.sum((padded_ends[None, :] <= block_start[:, None]).astype(jnp.int32), axis=1), N_EXPERTS - 1)
    n_used = (padded_ends[-1] // MOE_BLOCK).astype(jnp.int32).reshape(1)
    return block_expert, n_used, buf_src.reshape(n_blocks, 1, MOE_BLOCK), dest


def _moe_experts(h, block_expert, n_used, buf_src, w_gate, w_up, w_down, layer):
    t, d = h.shape
    ff = w_gate.shape[-1]
    n_blocks = buf_src.shape[0]
    cap = n_blocks * MOE_BLOCK
    idx_block = (1, 1, MOE_BLOCK)
    weight = lambda shp: pl.BlockSpec((1, 1) + shp, lambda b, be, nu: (layer, be[b], 0, 0))
    hid = pl.pallas_call(
        _moe_up_kernel,
        grid_spec=pltpu.PrefetchScalarGridSpec(
            num_scalar_prefetch=2,
            grid=(n_blocks,),
            in_specs=[pl.BlockSpec(idx_block, lambda b, be, nu: (b, 0, 0), memory_space=pltpu.SMEM),
                      pl.BlockSpec(idx_block, lambda b, be, nu: (jnp.minimum(b + 1, n_blocks - 1), 0, 0),
                                   memory_space=pltpu.SMEM),
                      pl.BlockSpec(memory_space=pl.ANY),
                      weight((d, ff)), weight((d, ff))],
            out_specs=pl.BlockSpec((MOE_BLOCK, ff), lambda b, be, nu: (b, 0)),
            scratch_shapes=[pltpu.VMEM((2, MOE_BLOCK, d), F32),
                            pltpu.VMEM((d, ff), BF16), pltpu.VMEM((d, ff), BF16),
                            pltpu.SemaphoreType.DMA((2,))],
        ),
        out_shape=jax.ShapeDtypeStruct((cap, ff), BF16),
        compiler_params=pltpu.CompilerParams(dimension_semantics=("arbitrary",),
                                             vmem_limit_bytes=MOE_UP_VMEM_LIMIT),
        name="moe_up",
    )(block_expert, n_used, buf_src, buf_src, h, w_gate, w_up)
    return pl.pallas_call(
        _moe_down_kernel,
        grid_spec=pltpu.PrefetchScalarGridSpec(
            num_scalar_prefetch=2,
            grid=(n_blocks,),
            in_specs=[pl.BlockSpec((MOE_BLOCK, ff), lambda b, be, nu: (b, 0)), weight((ff, d))],
            out_specs=pl.BlockSpec((MOE_BLOCK, d), lambda b, be, nu: (b, 0)),
            scratch_shapes=[pltpu.VMEM((ff, d), BF16)],
        ),
        out_shape=jax.ShapeDtypeStruct((cap, d), F32),
        compiler_params=_params("arbitrary"),
        name="moe_down",
    )(block_expert, n_used, hid, w_down)


def _combine(x, route, dest, expert_out, final_gain, *, tm=256):
    t, d = x.shape
    nt = t // tm
    final = final_gain is not None
    pos = dest.reshape(nt, 1, TOP_K * tm)
    idx_block = (1, 1, TOP_K * tm)
    row = pl.BlockSpec((tm, d), lambda i: (i, 0))
    in_specs = [pl.BlockSpec(idx_block, lambda i: (i, 0, 0), memory_space=pltpu.SMEM),
                pl.BlockSpec(idx_block, lambda i: (jnp.minimum(i + 1, nt - 1), 0, 0),
                             memory_space=pltpu.SMEM),
                row, pl.BlockSpec((tm, ROUTE_LANES), lambda i: (i, 0)),
                pl.BlockSpec(memory_space=pl.ANY)]
    args = [pos, pos, x, route, expert_out]
    if final:
        in_specs.append(pl.BlockSpec((1, d), lambda i: (0, 0)))
        args.append(final_gain.reshape(1, d))
    return pl.pallas_call(
        functools.partial(_combine_kernel, final=final),
        grid=(nt,),
        in_specs=in_specs, out_specs=row,
        out_shape=jax.ShapeDtypeStruct((t, d), F32),
        scratch_shapes=[pltpu.VMEM((2, TOP_K, tm, d), F32), pltpu.SemaphoreType.DMA((2,))],
        compiler_params=_params("arbitrary"),
        name="moe_combine",
    )(*args)


def kernel(x, attn_norm, ffn_norm, final_norm, w_in_ab, ret_norm, gmlp_norm, gmlp_ws, gmlp_bs, w_out_ab, w_in_c, lb_params, hgrn_norm, w_out_c, router_w_group, router_b_group, router_w_expert, router_b_expert, w_gate, w_up, w_down):
    b, s, d = x.shape
    assert b == 1, "the sequence mixers carry state along the flattened token axis"
    depth = attn_norm.shape[0]
    lb_soft = jax.nn.softmax(lb_params.astype(F32), axis=0)
    lower_bounds = jnp.cumsum(lb_soft, axis=0) - lb_soft[0]
    xt = x.reshape(b * s, d)
    t = b * s
    for layer in range(depth):
        i = layer // 2
        if layer % 2 == 0:
            proj = _norm_matmul(xt, attn_norm[layer], w_in_ab[i].astype(BF16))
            ret = _retention(proj, ret_norm[i])
            gm = _gmlp(proj, gmlp_norm[i], gmlp_ws[i], gmlp_bs[i])
            w_out = w_out_ab[i].astype(BF16)
            nr = ret.shape[1]
            xt = _matmul_residual(xt, [ret, gm], [w_out[:nr], w_out[nr:]])
        else:
            proj = _norm_matmul(xt, attn_norm[layer], w_in_c[i].astype(BF16))
            mixed = _hgrn2(proj, lower_bounds[layer], hgrn_norm[i])
            xt = _matmul_residual(xt, [mixed], [w_out_c[i].astype(BF16)])
        h, route, counts = _norm_router(xt, ffn_norm[layer], router_w_group[layer], router_b_group[layer],
                                        router_w_expert[layer], router_b_expert[layer])
        block_expert, n_used, buf_src, dest = _moe_dispatch(route, counts, t)
        expert_out = _moe_experts(h, block_expert, n_used, buf_src, w_gate, w_up, w_down, layer)
        xt = _combine(xt, route, dest, expert_out, final_norm if layer == depth - 1 else None)
    return xt.reshape(b, s, d)
```

```python
import functools

import jax
import jax.numpy as jnp
from jax import lax
from jax.experimental import pallas as pl
from jax.experimental.pallas import tpu as pltpu

F32 = jnp.float32
BF16 = jnp.bfloat16
EPS = 1e-6

RET_HEADS = 4
RET_CHUNK = 128
ROPE_BASE = 10000.0
GMLP_GROUPS = 4
GMLP_CHUNK = 128
HGRN_DK = 128
HGRN_CHUNK = 32
N_GROUPS = 4
EXPERTS_PER_GROUP = 8
N_EXPERTS = N_GROUPS * EXPERTS_PER_GROUP
TOP_K = 2
MOE_BLOCK = 128
ROUTE_LANES = 128

VMEM_LIMIT = 48 * 1024 * 1024
MOE_VMEM_LIMIT = 56 * 1024 * 1024


def _params(*sem):
    return pltpu.CompilerParams(dimension_semantics=sem, vmem_limit_bytes=VMEM_LIMIT)


def _rms(x, g):
    return x * lax.rsqrt(jnp.mean(x * x, axis=-1, keepdims=True) + EPS) * g


def _norm_matmul_kernel(x_ref, g_ref, w_ref, proj_ref, xn_ref):
    @pl.when(pl.program_id(1) == 0)
    def _():
        xn_ref[...] = _rms(x_ref[...], g_ref[...]).astype(BF16)

    proj_ref[...] = jnp.dot(xn_ref[...], w_ref[...],
                            preferred_element_type=F32).astype(proj_ref.dtype)


def _norm_matmul(x, g, w, *, tm=512, tn=1024):
    t, d = x.shape
    n = w.shape[1]
    return pl.pallas_call(
        _norm_matmul_kernel,
        grid=(t // tm, n // tn),
        in_specs=[pl.BlockSpec((tm, d), lambda i, j: (i, 0)),
                  pl.BlockSpec((1, d), lambda i, j: (0, 0)),
                  pl.BlockSpec((d, tn), lambda i, j: (0, j))],
        out_specs=pl.BlockSpec((tm, tn), lambda i, j: (i, j)),
        out_shape=jax.ShapeDtypeStruct((t, n), BF16),
        scratch_shapes=[pltpu.VMEM((tm, d), BF16)],
        compiler_params=_params("arbitrary", "arbitrary"),
        name="norm_matmul",
    )(x, g.reshape(1, d), w)


def _rope(x, cos, sin):
    half = x.shape[-1] // 2
    x1, x2 = x[:, :half], x[:, half:]
    return jnp.concatenate([x1 * cos - x2 * sin, x2 * cos + x1 * sin], axis=-1)


def _retention_kernel(cd_ref, q_ref, k_ref, v_ref, g_ref, cos_ref, sin_ref, dint_ref, qd_ref, kd_ref,
                      gain_ref, o_ref, state_ref, *, chunk, nchunk):
    @pl.when(pl.program_id(1) == 0)
    def _():
        state_ref[...] = jnp.zeros_like(state_ref)

    dk = q_ref.shape[-1]
    dint = dint_ref[0]
    qd = qd_ref[0]
    kd = kd_ref[0]
    cd = cd_ref[pl.program_id(0)]
    gain = gain_ref[...]
    for c in range(nchunk):
        rows = pl.ds(c * chunk, chunk)
        cos = cos_ref[rows, :]
        sin = sin_ref[rows, :]
        q = _rope(q_ref[rows, :].astype(F32), cos, sin)
        k = _rope(k_ref[rows, :].astype(F32), cos, sin) * (dk ** -0.5)
        v = v_ref[rows, :]
        scores = lax.dot_general(q.astype(BF16), k.astype(BF16), (((1,), (1,)), ((), ())),
                                 preferred_element_type=F32) * dint
        state = state_ref[...]
        o = (jnp.dot(scores.astype(BF16), v, preferred_element_type=F32)
             + jnp.dot((q * qd).astype(BF16), state.astype(BF16), preferred_element_type=F32))
        state_ref[...] = cd * state + lax.dot_general(
            (k * kd).astype(BF16), v, (((0,), (0,)), ((), ())), preferred_element_type=F32)
        o = o - jnp.mean(o, axis=-1, keepdims=True)
        o = o * lax.rsqrt(jnp.mean(o * o, axis=-1, keepdims=True) + EPS) * gain
        o_ref[rows, :] = (jax.nn.silu(g_ref[rows, :].astype(F32)) * o).astype(o_ref.dtype)


def _retention(proj, gain, *, rows=512):
    t = proj.shape[0]
    h = RET_HEADS
    dk = proj.shape[1] // 6 // h
    dv = dk
    c = RET_CHUNK
    f32 = F32
    inv = ROPE_BASE ** (-jnp.arange(0, dk, 2, dtype=f32) / dk)
    ang = jnp.arange(t, dtype=f32)[:, None] * inv[None, :]
    cos, sin = jnp.cos(ang), jnp.sin(ang)
    log_gamma = jnp.log(1.0 - jnp.exp2(-5.0 - jnp.arange(h, dtype=f32)))
    idx = jnp.arange(c, dtype=f32)
    diff = idx[:, None] - idx[None, :]
    d_intra = jnp.where(diff >= 0, jnp.exp(jnp.maximum(diff, 0.0) * log_gamma[:, None, None]), 0.0)
    q_decay = jnp.broadcast_to(jnp.exp((idx + 1.0)[None, :] * log_gamma[:, None])[..., None], (h, c, dk))
    k_decay = jnp.broadcast_to(jnp.exp((c - 1.0 - idx)[None, :] * log_gamma[:, None])[..., None], (h, c, dk))
    chunk_decay = jnp.exp(c * log_gamma)

    def col(off):
        return pl.BlockSpec((rows, dk), lambda hh, i: (i, off + hh))

    tab = pl.BlockSpec((rows, dk // 2), lambda hh, i: (i, 0))
    per_head = lambda shp: pl.BlockSpec((1,) + shp, lambda hh, i: (hh, 0, 0))
    return pl.pallas_call(
        functools.partial(_retention_kernel, chunk=c, nchunk=rows // c),
        grid=(h, t // rows),
        in_specs=[pl.BlockSpec(memory_space=pltpu.SMEM),
                  col(0), col(h), col(2 * h), col(3 * h), tab, tab,
                  per_head((c, c)), per_head((c, dk)), per_head((c, dk)),
                  pl.BlockSpec((1, dv), lambda hh, i: (0, hh))],
        out_specs=pl.BlockSpec((rows, dv), lambda hh, i: (i, hh)),
        out_shape=jax.ShapeDtypeStruct((t, h * dv), BF16),
        scratch_shapes=[pltpu.VMEM((dk, dv), F32)],
        compiler_params=_params("arbitrary", "arbitrary"),
        name="retention",
    )(chunk_decay, proj, proj, proj, proj, cos, sin, d_intra, q_decay, k_decay, gain.reshape(1, h * dv))


def _gmlp_kernel(u_ref, vs_ref, gain_ref, ws_ref, bs_ref, o_ref, *, chunk, nchunk):
    r = lax.broadcasted_iota(jnp.int32, (chunk, chunk), 0)
    s = lax.broadcasted_iota(jnp.int32, (chunk, chunk), 1)
    w = jnp.where(r >= s, ws_ref[0], 0.0).astype(BF16)
    gain = gain_ref[...]
    bs = bs_ref[0]
    for c in range(nchunk):
        rows = pl.ds(c * chunk, chunk)
        v = jax.nn.gelu(vs_ref[rows, :].astype(F32))
        v = v - jnp.mean(v, axis=-1, keepdims=True)
        v = v * lax.rsqrt(jnp.mean(v * v, axis=-1, keepdims=True) + EPS) * gain
        mixed = jnp.dot(w, v.astype(BF16), preferred_element_type=F32) + bs
        o_ref[rows, :] = (jax.nn.gelu(u_ref[rows, :].astype(F32)) * mixed).astype(o_ref.dtype)


def _gmlp(proj, gain, ws, bs, *, rows=512):
    t = proj.shape[0]
    g = GMLP_GROUPS
    dim = proj.shape[1] // 6 // g
    c = GMLP_CHUNK
    return pl.pallas_call(
        functools.partial(_gmlp_kernel, chunk=c, nchunk=rows // c),
        grid=(g, t // rows),
        in_specs=[pl.BlockSpec((rows, dim), lambda gg, i: (i, 4 * g + gg)),
                  pl.BlockSpec((rows, dim), lambda gg, i: (i, 5 * g + gg)),
                  pl.BlockSpec((1, dim), lambda gg, i: (0, gg)),
                  pl.BlockSpec((1, c, c), lambda gg, i: (gg, 0, 0)),
                  pl.BlockSpec((1, c, 1), lambda gg, i: (gg, 0, 0))],
        out_specs=pl.BlockSpec((rows, dim), lambda gg, i: (i, gg)),
        out_shape=jax.ShapeDtypeStruct((t, g * dim), BF16),
        compiler_params=_params("arbitrary", "arbitrary"),
        name="gmlp",
    )(proj, proj, gain.reshape(1, g * dim), ws, bs.reshape(g, c, 1))


def _hgrn_kernel(zq_ref, zf_ref, zi_ref, zg_ref, lb_ref, gain_ref, tri_ref, keep_ref, o_ref, state_ref,
                 *, chunk, nchunk, heads, dk):
    @pl.when(pl.program_id(1) == 0)
    def _():
        state_ref[...] = jnp.zeros_like(state_ref)

    rows = chunk * nchunk
    nt = (((1,), (1,)), ((), ()))
    tri = tri_ref[...]
    keep = keep_ref[...] > 0.0
    chunk_of_row = lax.broadcasted_iota(jnp.int32, (rows, dk), 0) // chunk
    for hd in range(heads):
        cols = slice(hd * dk, (hd + 1) * dk)
        lb = lb_ref[:, cols]
        zf = zf_ref[:, cols].astype(F32)
        f = lb + (1.0 - lb) * jax.nn.sigmoid(zf)
        kk = (1.0 - lb) * jax.nn.sigmoid(-zf)
        log_f = jnp.log(f)
        p0 = log_f.astype(BF16)
        r0 = log_f - p0.astype(F32)
        p1 = r0.astype(BF16)
        p2 = (r0 - p1.astype(F32)).astype(BF16)
        cum = (jnp.dot(tri, p0, preferred_element_type=F32)
               + jnp.dot(tri, p1, preferred_element_type=F32)
               + jnp.dot(tri, p2, preferred_element_type=F32))
        lasts = [cum[(c + 1) * chunk - 1:(c + 1) * chunk, :] for c in range(nchunk)]
        last_rows = jnp.concatenate([jnp.broadcast_to(l, (chunk, dk)) for l in lasts], axis=0)
        q_dec = (jax.nn.silu(zq_ref[:, cols].astype(F32)) * jnp.exp(cum)).astype(BF16)
        k_dec = (kk * jnp.exp(-cum)).astype(BF16)
        k_out = (kk * jnp.exp(last_rows - cum)).astype(BF16)
        v = zi_ref[:, cols]
        v_t = v.astype(F32).T.astype(BF16)
        scores = lax.dot_general(q_dec, k_dec, nt, preferred_element_type=F32)
        scores = jnp.where(keep, scores, 0.0).astype(BF16)
        o = jnp.dot(scores, v, preferred_element_type=F32)
        state = state_ref[hd]
        inter = []
        for c in range(nchunk):
            sl = slice(c * chunk, (c + 1) * chunk)
            inter.append(lax.dot_general(q_dec[sl], state.astype(BF16), nt, preferred_element_type=F32))
            k_c = jnp.where(chunk_of_row == c, k_out, jnp.zeros_like(k_out))
            state = jnp.exp(lasts[c]) * state + jnp.dot(v_t, k_c, preferred_element_type=F32)
        state_ref[hd] = state
        o = o + jnp.concatenate(inter, axis=0)
        o = o * lax.rsqrt(jnp.mean(o * o, axis=-1, keepdims=True) + EPS) * gain_ref[:, cols]
        o_ref[:, cols] = (o * jax.nn.silu(zg_ref[:, cols].astype(F32))).astype(o_ref.dtype)


def _hgrn2(proj, lb, gain, *, rows=256, heads=4):
    t = proj.shape[0]
    width = proj.shape[1] // 4
    dk = HGRN_DK
    h = width // dk
    c = HGRN_CHUNK
    hw = heads * dk
    groups = h // heads
    r = jnp.arange(rows, dtype=jnp.int32)
    tri = ((r[:, None] >= r[None, :]) & ((r[:, None] // c) == (r[None, :] // c))).astype(F32)

    def col(off):
        return pl.BlockSpec((rows, hw), lambda hh, i: (i, off + hh))

    vec = pl.BlockSpec((1, hw), lambda hh, i: (0, hh))
    mask = pl.BlockSpec((rows, rows), lambda hh, i: (0, 0))
    return pl.pallas_call(
        functools.partial(_hgrn_kernel, chunk=c, nchunk=rows // c, heads=heads, dk=dk),
        grid=(groups, t // rows),
        in_specs=[col(0), col(groups), col(2 * groups), col(3 * groups), vec, vec, mask, mask],
        out_specs=pl.BlockSpec((rows, hw), lambda hh, i: (i, hh)),
        out_shape=jax.ShapeDtypeStruct((t, width), BF16),
        scratch_shapes=[pltpu.VMEM((heads, dk, dk), F32)],
        compiler_params=_params("arbitrary", "arbitrary"),
        name="hgrn2",
    )(proj, proj, proj, proj, lb.reshape(1, width), gain.reshape(1, width), tri.astype(BF16), tri)


def _matmul_residual_kernel(*refs, n_act):
    x_ref = refs[0]
    a_refs = refs[1:1 + n_act]
    w_refs = refs[1 + n_act:1 + 2 * n_act]
    o_ref = refs[1 + 2 * n_act]
    acc = x_ref[...]
    for a_ref, w_ref in zip(a_refs, w_refs):
        acc = acc + jnp.dot(a_ref[...], w_ref[...], preferred_element_type=F32)
    o_ref[...] = acc


def _matmul_residual(x, acts, ws, *, tm=256):
    t, d = x.shape
    row = pl.BlockSpec((tm, d), lambda i: (i, 0))
    in_specs = [row]
    in_specs += [pl.BlockSpec((tm, a.shape[1]), lambda i: (i, 0)) for a in acts]
    in_specs += [pl.BlockSpec(w.shape, lambda i: (0, 0)) for w in ws]
    return pl.pallas_call(
        functools.partial(_matmul_residual_kernel, n_act=len(acts)),
        grid=(t // tm,),
        in_specs=in_specs, out_specs=row,
        out_shape=jax.ShapeDtypeStruct((t, d), F32),
        compiler_params=_params("arbitrary"),
        name="matmul_residual",
    )(x, *acts, *ws)


def _norm_router_kernel(x_ref, g_ref, wr_ref, br_ref, h_ref, route_ref, count_ref, run_ref):
    @pl.when(pl.program_id(0) == 0)
    def _():
        run_ref[...] = jnp.zeros_like(run_ref)

    h = _rms(x_ref[...], g_ref[...])
    h_ref[...] = h
    logits = jnp.dot(h, wr_ref[...], preferred_element_type=F32,
                     precision=lax.Precision.HIGHEST) + br_ref[...]
    lane = lax.broadcasted_iota(jnp.int32, logits.shape, 1)
    lane_f = lane.astype(F32)
    neg = -jnp.inf
    big = float(ROUTE_LANES)
    lg = jnp.where(lane < N_GROUPS, logits, neg)
    mg = jnp.max(lg, axis=-1, keepdims=True)
    eg = jnp.exp(lg - mg)
    pg = eg / jnp.sum(eg, axis=-1, keepdims=True)
    p_sel = jnp.max(pg, axis=-1, keepdims=True)
    g_sel = jnp.min(jnp.where(lg == mg, lane_f, big), axis=-1, keepdims=True)
    e_grp = ((lane - N_GROUPS) // EXPERTS_PER_GROUP).astype(F32)
    in_grp = jnp.where(lane >= N_GROUPS, e_grp, -1.0) == g_sel
    le = jnp.where(in_grp, logits, neg)
    t1 = jnp.max(le, axis=-1, keepdims=True)
    i1 = jnp.min(jnp.where(le == t1, lane_f, big), axis=-1, keepdims=True)
    le2 = jnp.where(lane_f == i1, neg, le)
    t2 = jnp.max(le2, axis=-1, keepdims=True)
    i2 = jnp.min(jnp.where(le2 == t2, lane_f, big), axis=-1, keepdims=True)
    e2 = jnp.exp(t2 - t1)
    den = 1.0 + e2
    gate1 = p_sel * (1.0 / den)
    gate2 = p_sel * (e2 / den)
    tm = logits.shape[0]
    hit1 = jnp.where(lane_f == i1, 1.0, 0.0)
    hit2 = jnp.where(lane_f == i2, 1.0, 0.0)
    hits = hit1 + hit2
    r = lax.broadcasted_iota(jnp.int32, (tm, tm), 0)
    s = lax.broadcasted_iota(jnp.int32, (tm, tm), 1)
    before = jnp.where(r > s, 1.0, 0.0).astype(BF16)
    prefix = jnp.dot(before, hits.astype(BF16), preferred_element_type=F32) + run_ref[...]
    rank1 = jnp.sum(prefix * hit1, axis=-1, keepdims=True)
    rank2 = jnp.sum(prefix * hit2, axis=-1, keepdims=True)
    total = run_ref[...] + jnp.sum(hits, axis=0, keepdims=True)
    run_ref[...] = total
    count_ref[...] = total
    vals = (i1 - N_GROUPS, i2 - N_GROUPS, gate1, gate2, rank1, rank2)
    slab = jnp.zeros_like(logits)
    for pos, val in enumerate(vals):
        slab = jnp.where(lane == pos, val, slab)
    route_ref[...] = slab


def _norm_router(x, g, w_rg, b_rg, w_re, b_re, *, tm=256):
    t, d = x.shape
    used = N_GROUPS + N_EXPERTS
    wr = jnp.zeros((d, ROUTE_LANES), F32).at[:, :N_GROUPS].set(w_rg).at[:, N_GROUPS:used].set(w_re)
    br = jnp.zeros((1, ROUTE_LANES), F32).at[0, :N_GROUPS].set(b_rg).at[0, N_GROUPS:used].set(b_re)
    row = pl.BlockSpec((tm, d), lambda i: (i, 0))
    return pl.pallas_call(
        _norm_router_kernel,
        grid=(t // tm,),
        in_specs=[row, pl.BlockSpec((1, d), lambda i: (0, 0)),
                  pl.BlockSpec((d, ROUTE_LANES), lambda i: (0, 0)),
                  pl.BlockSpec((1, ROUTE_LANES), lambda i: (0, 0))],
        out_specs=[row, pl.BlockSpec((tm, ROUTE_LANES), lambda i: (i, 0)),
                   pl.BlockSpec((1, ROUTE_LANES), lambda i: (0, 0))],
        out_shape=[jax.ShapeDtypeStruct((t, d), F32), jax.ShapeDtypeStruct((t, ROUTE_LANES), F32),
                   jax.ShapeDtypeStruct((1, ROUTE_LANES), F32)],
        scratch_shapes=[pltpu.VMEM((1, ROUTE_LANES), F32)],
        compiler_params=_params("arbitrary"),
        name="norm_router",
    )(x, g.reshape(1, d), wr, br)


def _expert_changed(be_ref, b):
    return (b == 0) | (be_ref[b] != be_ref[jnp.maximum(b - 1, 0)])


def _moe_ffn_kernel(be_ref, nu_ref, ne_ref, src_ref, nxt_ref, h_hbm, wg_hbm, wu_hbm, wd_hbm, out_ref,
                    xbuf, stage_g, stage_u, stage_d, wg_bf, wu_bf, wd_bf, xsem, wsem, *, layer):
    b = pl.program_id(0)
    nu = nu_ref[0]
    slot = b % 2
    blk = xbuf.shape[1]

    def start_rows(idx_ref, dst_slot):
        def body(r, carry):
            pltpu.make_async_copy(h_hbm.at[pl.ds(idx_ref[0, 0, r], 1)],
                                  xbuf.at[dst_slot, pl.ds(r, 1)], xsem.at[dst_slot]).start()
            return carry
        lax.fori_loop(0, blk, body, 0, unroll=8)

    def weight_copies(e):
        return (pltpu.make_async_copy(wg_hbm.at[layer, e], stage_g, wsem.at[0]),
                pltpu.make_async_copy(wu_hbm.at[layer, e], stage_u, wsem.at[1]),
                pltpu.make_async_copy(wd_hbm.at[layer, e], stage_d, wsem.at[2]))

    @pl.when(b == 0)
    def _():
        for cp in weight_copies(be_ref[0]):
            cp.start()
        start_rows(src_ref, 0)

    @pl.when(b + 1 < nu)
    def _():
        start_rows(nxt_ref, 1 - slot)

    @pl.when(b < nu)
    def _():
        @pl.when(_expert_changed(be_ref, b))
        def _():
            for cp in weight_copies(be_ref[b]):
                cp.wait()
            wg_bf[...] = stage_g[...].astype(BF16)
            wu_bf[...] = stage_u[...].astype(BF16)
            wd_bf[...] = stage_d[...].astype(BF16)

            @pl.when(ne_ref[b] >= 0)
            def _():
                for cp in weight_copies(ne_ref[b]):
                    cp.start()

        pltpu.make_async_copy(h_hbm.at[pl.ds(0, blk)], xbuf.at[slot], xsem.at[slot]).wait()
        x = xbuf[slot].astype(BF16)
        hg = jnp.dot(x, wg_bf[...], preferred_element_type=F32)
        hu = jnp.dot(x, wu_bf[...], preferred_element_type=F32)
        hid = (jax.nn.silu(hg) * hu).astype(BF16)
        out_ref[...] = jnp.dot(hid, wd_bf[...], preferred_element_type=F32)

    @pl.when(b >= nu)
    def _():
        out_ref[...] = jnp.zeros_like(out_ref)


def _combine_kernel(*refs, final):
    if final:
        pos_ref, nxt_ref, x_ref, route_ref, eo_hbm, g_ref, o_ref, ybuf, sem = refs
    else:
        pos_ref, nxt_ref, x_ref, route_ref, eo_hbm, o_ref, ybuf, sem = refs
    i = pl.program_id(0)
    slot = i % 2
    tm = x_ref.shape[0]

    def start_rows(idx_ref, dst_slot):
        def body(r, carry):
            for k in range(TOP_K):
                pltpu.make_async_copy(eo_hbm.at[pl.ds(idx_ref[0, 0, TOP_K * r + k], 1)],
                                      ybuf.at[dst_slot, k, pl.ds(r, 1)], sem.at[dst_slot]).start()
            return carry
        lax.fori_loop(0, tm, body, 0, unroll=4)

    @pl.when(i == 0)
    def _():
        start_rows(pos_ref, 0)

    @pl.when(i + 1 < pl.num_programs(0))
    def _():
        start_rows(nxt_ref, 1 - slot)

    for k in range(TOP_K):
        pltpu.make_async_copy(eo_hbm.at[pl.ds(0, tm)], ybuf.at[slot, k], sem.at[slot]).wait()
    route = route_ref[...]
    y = ybuf[slot, 0] * route[:, TOP_K:TOP_K + 1] + ybuf[slot, 1] * route[:, TOP_K + 1:TOP_K + 2]
    x = x_ref[...] + y
    o_ref[...] = _rms(x, g_ref[...]) if final else x


def _moe_dispatch(route, counts_slab, t):
    m = t * TOP_K
    expert = route[:, :TOP_K].astype(jnp.int32)
    rank = route[:, 2 * TOP_K:3 * TOP_K].astype(jnp.int32)
    counts = counts_slab[0, N_GROUPS:N_GROUPS + N_EXPERTS].astype(jnp.int32)
    padded = (counts + MOE_BLOCK - 1) // MOE_BLOCK * MOE_BLOCK
    padded_ends = jnp.cumsum(padded)
    padded_starts = padded_ends - padded
    dest = padded_starts[expert] + rank
    n_blocks = -(-(m + N_EXPERTS * (MOE_BLOCK - 1)) // MOE_BLOCK)
    cap = n_blocks * MOE_BLOCK
    token = jnp.broadcast_to(jnp.arange(t, dtype=jnp.int32)[:, None], (t, TOP_K))
    buf_src = jnp.zeros((cap,), jnp.int32).at[dest.reshape(m)].set(token.reshape(m))
    block_start = jnp.arange(n_blocks, dtype=jnp.int32) * MOE_BLOCK
    block_expert = jnp.minimum(
        jnp.sum((padded_ends[None, :] <= block_start[:, None]).astype(jnp.int32), axis=1), N_EXPERTS - 1)
    n_used = (padded_ends[-1] // MOE_BLOCK).astype(jnp.int32).reshape(1)
    ids = jnp.arange(N_EXPERTS, dtype=jnp.int32)
    later = (ids[None, :] > ids[:, None]) & (counts[None, :] > 0)
    next_active = jnp.min(jnp.where(later, ids[None, :], N_EXPERTS), axis=1)
    next_active = jnp.where(next_active < N_EXPERTS, next_active, -1)
    block_next = next_active[block_expert]
    return block_expert, n_used, block_next, buf_src.reshape(n_blocks, 1, MOE_BLOCK), dest


def _moe_experts(h, block_expert, n_used, block_next, buf_src, w_gate, w_up, w_down, layer):
    t, d = h.shape
    ff = w_gate.shape[-1]
    n_blocks = buf_src.shape[0]
    cap = n_blocks * MOE_BLOCK
    idx_block = (1, 1, MOE_BLOCK)
    hbm = pl.BlockSpec(memory_space=pl.ANY)
    return pl.pallas_call(
        functools.partial(_moe_ffn_kernel, layer=layer),
        grid_spec=pltpu.PrefetchScalarGridSpec(
            num_scalar_prefetch=3,
            grid=(n_blocks,),
            in_specs=[pl.BlockSpec(idx_block, lambda b, *_: (b, 0, 0), memory_space=pltpu.SMEM),
                      pl.BlockSpec(idx_block, lambda b, *_: (jnp.minimum(b + 1, n_blocks - 1), 0, 0),
                                   memory_space=pltpu.SMEM),
                      hbm, hbm, hbm, hbm],
            out_specs=pl.BlockSpec((MOE_BLOCK, d), lambda b, *_: (b, 0)),
            scratch_shapes=[pltpu.VMEM((2, MOE_BLOCK, d), F32),
                            pltpu.VMEM((d, ff), F32), pltpu.VMEM((d, ff), F32), pltpu.VMEM((ff, d), F32),
                            pltpu.VMEM((d, ff), BF16), pltpu.VMEM((d, ff), BF16), pltpu.VMEM((ff, d), BF16),
                            pltpu.SemaphoreType.DMA((2,)), pltpu.SemaphoreType.DMA((3,))],
        ),
        out_shape=jax.ShapeDtypeStruct((cap, d), F32),
        compiler_params=pltpu.CompilerParams(dimension_semantics=("arbitrary",),
                                             vmem_limit_bytes=MOE_VMEM_LIMIT),
        name="moe_ffn",
    )(block_expert, n_used, block_next, buf_src, buf_src, h, w_gate, w_up, w_down)


def _combine(x, route, dest, expert_out, final_gain, *, tm=256):
    t, d = x.shape
    nt = t // tm
    final = final_gain is not None
    pos = dest.reshape(nt, 1, TOP_K * tm)
    idx_block = (1, 1, TOP_K * tm)
    row = pl.BlockSpec((tm, d), lambda i: (i, 0))
    in_specs = [pl.BlockSpec(idx_block, lambda i: (i, 0, 0), memory_space=pltpu.SMEM),
                pl.BlockSpec(idx_block, lambda i: (jnp.minimum(i + 1, nt - 1), 0, 0),
                             memory_space=pltpu.SMEM),
                row, pl.BlockSpec((tm, ROUTE_LANES), lambda i: (i, 0)),
                pl.BlockSpec(memory_space=pl.ANY)]
    args = [pos, pos, x, route, expert_out]
    if final:
        in_specs.append(pl.BlockSpec((1, d), lambda i: (0, 0)))
        args.append(final_gain.reshape(1, d))
    return pl.pallas_call(
        functools.partial(_combine_kernel, final=final),
        grid=(nt,),
        in_specs=in_specs, out_specs=row,
        out_shape=jax.ShapeDtypeStruct((t, d), F32),
        scratch_shapes=[pltpu.VMEM((2, TOP_K, tm, d), F32), pltpu.SemaphoreType.DMA((2,))],
        compiler_params=_params("arbitrary"),
        name="moe_combine",
    )(*args)


def kernel(x, attn_norm, ffn_norm, final_norm, w_in_ab, ret_norm, gmlp_norm, gmlp_ws, gmlp_bs, w_out_ab, w_in_c, lb_params, hgrn_norm, w_out_c, router_w_group, router_b_group, router_w_expert, router_b_expert, w_gate, w_up, w_down):
    b, s, d = x.shape
    assert b == 1, "the sequence mixers carry state along the flattened token axis"
    depth = attn_norm.shape[0]
    lb_soft = jax.nn.softmax(lb_params.astype(F32), axis=0)
    lower_bounds = jnp.cumsum(lb_soft, axis=0) - lb_soft[0]
    xt = x.reshape(b * s, d)
    t = b * s
    for layer in range(depth):
        i = layer // 2
        if layer % 2 == 0:
            proj = _norm_matmul(xt, attn_norm[layer], w_in_ab[i].astype(BF16))
            ret = _retention(proj, ret_norm[i])
            gm = _gmlp(proj, gmlp_norm[i], gmlp_ws[i], gmlp_bs[i])
            w_out = w_out_ab[i].astype(BF16)
            nr = ret.shape[1]
            xt = _matmul_residual(xt, [ret, gm], [w_out[:nr], w_out[nr:]])
        else:
            proj = _norm_matmul(xt, attn_norm[layer], w_in_c[i].astype(BF16))
            mixed = _hgrn2(proj, lower_bounds[layer], hgrn_norm[i])
            xt = _matmul_residual(xt, [mixed], [w_out_c[i].astype(BF16)])
        h, route, counts = _norm_router(xt, ffn_norm[layer], router_w_group[layer], router_b_group[layer],
                                        router_w_expert[layer], router_b_expert[layer])
        block_expert, n_used, block_next, buf_src, dest = _moe_dispatch(route, counts, t)
        expert_out = _moe_experts(h, block_expert, n_used, block_next, buf_src, w_gate, w_up, w_down, layer)
        xt = _combine(xt, route, dest, expert_out, final_norm if layer == depth - 1 else None)
    return xt.reshape(b, s, d)
```

```python
import functools

import jax
import jax.numpy as jnp
from jax import lax
from jax.experimental import pallas as pl
from jax.experimental.pallas import tpu as pltpu

F32 = jnp.float32
BF16 = jnp.bfloat16
EPS = 1e-6

RET_HEADS = 4
RET_CHUNK = 128
ROPE_BASE = 10000.0
GMLP_GROUPS = 4
GMLP_CHUNK = 128
HGRN_DK = 128
HGRN_CHUNK = 32
N_GROUPS = 4
EXPERTS_PER_GROUP = 8
N_EXPERTS = N_GROUPS * EXPERTS_PER_GROUP
TOP_K = 2
MOE_BLOCK = 128
ROUTE_LANES = 128

VMEM_LIMIT = 48 * 1024 * 1024
MOE_VMEM_LIMIT = 56 * 1024 * 1024


def _params(*sem):
    return pltpu.CompilerParams(dimension_semantics=sem, vmem_limit_bytes=VMEM_LIMIT)


def _rms(x, g):
    return x * lax.rsqrt(jnp.mean(x * x, axis=-1, keepdims=True) + EPS) * g


def _norm_matmul_kernel(x_ref, g_ref, w_ref, proj_ref, xn_ref):
    @pl.when(pl.program_id(1) == 0)
    def _():
        xn_ref[...] = _rms(x_ref[...], g_ref[...]).astype(BF16)

    proj_ref[...] = jnp.dot(xn_ref[...], w_ref[...],
                            preferred_element_type=F32).astype(proj_ref.dtype)


def _norm_matmul(x, g, w, *, tm=512, tn=1024):
    t, d = x.shape
    n = w.shape[1]
    return pl.pallas_call(
        _norm_matmul_kernel,
        grid=(t // tm, n // tn),
        in_specs=[pl.BlockSpec((tm, d), lambda i, j: (i, 0)),
                  pl.BlockSpec((1, d), lambda i, j: (0, 0)),
                  pl.BlockSpec((d, tn), lambda i, j: (0, j))],
        out_specs=pl.BlockSpec((tm, tn), lambda i, j: (i, j)),
        out_shape=jax.ShapeDtypeStruct((t, n), BF16),
        scratch_shapes=[pltpu.VMEM((tm, d), BF16)],
        compiler_params=_params("arbitrary", "arbitrary"),
        name="norm_matmul",
    )(x, g.reshape(1, d), w)


def _rope(x, cos, sin):
    half = x.shape[-1] // 2
    x1, x2 = x[:, :half], x[:, half:]
    return jnp.concatenate([x1 * cos - x2 * sin, x2 * cos + x1 * sin], axis=-1)


def _retention_kernel(cd_ref, q_ref, k_ref, v_ref, g_ref, cos_ref, sin_ref, dint_ref, qd_ref, kd_ref,
                      gain_ref, o_ref, state_ref, *, chunk, nchunk):
    @pl.when(pl.program_id(1) == 0)
    def _():
        state_ref[...] = jnp.zeros_like(state_ref)

    dk = q_ref.shape[-1]
    dint = dint_ref[0]
    qd = qd_ref[0]
    kd = kd_ref[0]
    cd = cd_ref[pl.program_id(0)]
    gain = gain_ref[...]
    for c in range(nchunk):
        rows = pl.ds(c * chunk, chunk)
        cos = cos_ref[rows, :]
        sin = sin_ref[rows, :]
        q = _rope(q_ref[rows, :].astype(F32), cos, sin)
        k = _rope(k_ref[rows, :].astype(F32), cos, sin) * (dk ** -0.5)
        v = v_ref[rows, :]
        scores = lax.dot_general(q.astype(BF16), k.astype(BF16), (((1,), (1,)), ((), ())),
                                 preferred_element_type=F32) * dint
        state = state_ref[...]
        o = (jnp.dot(scores.astype(BF16), v, preferred_element_type=F32)
             + jnp.dot((q * qd).astype(BF16), state.astype(BF16), preferred_element_type=F32))
        state_ref[...] = cd * state + lax.dot_general(
            (k * kd).astype(BF16), v, (((0,), (0,)), ((), ())), preferred_element_type=F32)
        o = o - jnp.mean(o, axis=-1, keepdims=True)
        o = o * lax.rsqrt(jnp.mean(o * o, axis=-1, keepdims=True) + EPS) * gain
        o_ref[rows, :] = (jax.nn.silu(g_ref[rows, :].astype(F32)) * o).astype(o_ref.dtype)


def _retention(proj, gain, *, rows=512):
    t = proj.shape[0]
    h = RET_HEADS
    dk = proj.shape[1] // 6 // h
    dv = dk
    c = RET_CHUNK
    f32 = F32
    inv = ROPE_BASE ** (-jnp.arange(0, dk, 2, dtype=f32) / dk)
    ang = jnp.arange(t, dtype=f32)[:, None] * inv[None, :]
    cos, sin = jnp.cos(ang), jnp.sin(ang)
    log_gamma = jnp.log(1.0 - jnp.exp2(-5.0 - jnp.arange(h, dtype=f32)))
    idx = jnp.arange(c, dtype=f32)
    diff = idx[:, None] - idx[None, :]
    d_intra = jnp.where(diff >= 0, jnp.exp(jnp.maximum(diff, 0.0) * log_gamma[:, None, None]), 0.0)
    q_decay = jnp.broadcast_to(jnp.exp((idx + 1.0)[None, :] * log_gamma[:, None])[..., None], (h, c, dk))
    k_decay = jnp.broadcast_to(jnp.exp((c - 1.0 - idx)[None, :] * log_gamma[:, None])[..., None], (h, c, dk))
    chunk_decay = jnp.exp(c * log_gamma)

    def col(off):
        return pl.BlockSpec((rows, dk), lambda hh, i: (i, off + hh))

    tab = pl.BlockSpec((rows, dk // 2), lambda hh, i: (i, 0))
    per_head = lambda shp: pl.BlockSpec((1,) + shp, lambda hh, i: (hh, 0, 0))
    return pl.pallas_call(
        functools.partial(_retention_kernel, chunk=c, nchunk=rows // c),
        grid=(h, t // rows),
        in_specs=[pl.BlockSpec(memory_space=pltpu.SMEM),
                  col(0), col(h), col(2 * h), col(3 * h), tab, tab,
                  per_head((c, c)), per_head((c, dk)), per_head((c, dk)),
                  pl.BlockSpec((1, dv), lambda hh, i: (0, hh))],
        out_specs=pl.BlockSpec((rows, dv), lambda hh, i: (i, hh)),
        out_shape=jax.ShapeDtypeStruct((t, h * dv), BF16),
        scratch_shapes=[pltpu.VMEM((dk, dv), F32)],
        compiler_params=_params("arbitrary", "arbitrary"),
        name="retention",
    )(chunk_decay, proj, proj, proj, proj, cos, sin, d_intra, q_decay, k_decay, gain.reshape(1, h * dv))


def _gmlp_kernel(u_ref, vs_ref, gain_ref, ws_ref, bs_ref, o_ref, *, chunk, nchunk):
    r = lax.broadcasted_iota(jnp.int32, (chunk, chunk), 0)
    s = lax.broadcasted_iota(jnp.int32, (chunk, chunk), 1)
    w = jnp.where(r >= s, ws_ref[0], 0.0).astype(BF16)
    gain = gain_ref[...]
    bs = bs_ref[0]
    for c in range(nchunk):
        rows = pl.ds(c * chunk, chunk)
        v = jax.nn.gelu(vs_ref[rows, :].astype(F32))
        v = v - jnp.mean(v, axis=-1, keepdims=True)
        v = v * lax.rsqrt(jnp.mean(v * v, axis=-1, keepdims=True) + EPS) * gain
        mixed = jnp.dot(w, v.astype(BF16), preferred_element_type=F32) + bs
        o_ref[rows, :] = (jax.nn.gelu(u_ref[rows, :].astype(F32)) * mixed).astype(o_ref.dtype)


def _gmlp(proj, gain, ws, bs, *, rows=512):
    t = proj.shape[0]
    g = GMLP_GROUPS
    dim = proj.shape[1] // 6 // g
    c = GMLP_CHUNK
    return pl.pallas_call(
        functools.partial(_gmlp_kernel, chunk=c, nchunk=rows // c),
        grid=(g, t // rows),
        in_specs=[pl.BlockSpec((rows, dim), lambda gg, i: (i, 4 * g + gg)),
                  pl.BlockSpec((rows, dim), lambda gg, i: (i, 5 * g + gg)),
                  pl.BlockSpec((1, dim), lambda gg, i: (0, gg)),
                  pl.BlockSpec((1, c, c), lambda gg, i: (gg, 0, 0)),
                  pl.BlockSpec((1, c, 1), lambda gg, i: (gg, 0, 0))],
        out_specs=pl.BlockSpec((rows, dim), lambda gg, i: (i, gg)),
        out_shape=jax.ShapeDtypeStruct((t, g * dim), BF16),
        compiler_params=_params("arbitrary", "arbitrary"),
        name="gmlp",
    )(proj, proj, gain.reshape(1, g * dim), ws, bs.reshape(g, c, 1))


def _hgrn_kernel(zq_ref, zf_ref, zi_ref, zg_ref, lb_ref, gain_ref, tri_ref, keep_ref, o_ref, state_ref,
                 *, chunk, nchunk, heads, dk):
    @pl.when(pl.program_id(1) == 0)
    def _():
        state_ref[...] = jnp.zeros_like(state_ref)

    rows = chunk * nchunk
    nt = (((1,), (1,)), ((), ()))
    tri = tri_ref[...]
    keep = keep_ref[...] > 0.0
    chunk_of_row = lax.broadcasted_iota(jnp.int32, (rows, dk), 0) // chunk
    for hd in range(heads):
        cols = slice(hd * dk, (hd + 1) * dk)
        lb = lb_ref[:, cols]
        zf = zf_ref[:, cols].astype(F32)
        f = lb + (1.0 - lb) * jax.nn.sigmoid(zf)
        kk = (1.0 - lb) * jax.nn.sigmoid(-zf)
        log_f = jnp.log(f)
        p0 = log_f.astype(BF16)
        r0 = log_f - p0.astype(F32)
        p1 = r0.astype(BF16)
        p2 = (r0 - p1.astype(F32)).astype(BF16)
        cum = (jnp.dot(tri, p0, preferred_element_type=F32)
               + jnp.dot(tri, p1, preferred_element_type=F32)
               + jnp.dot(tri, p2, preferred_element_type=F32))
        lasts = [cum[(c + 1) * chunk - 1:(c + 1) * chunk, :] for c in range(nchunk)]
        last_rows = jnp.concatenate([jnp.broadcast_to(l, (chunk, dk)) for l in lasts], axis=0)
        q_dec = (jax.nn.silu(zq_ref[:, cols].astype(F32)) * jnp.exp(cum)).astype(BF16)
        k_dec = (kk * jnp.exp(-cum)).astype(BF16)
        k_out = (kk * jnp.exp(last_rows - cum)).astype(BF16)
        v = zi_ref[:, cols]
        v_t = v.astype(F32).T.astype(BF16)
        scores = lax.dot_general(q_dec, k_dec, nt, preferred_element_type=F32)
        scores = jnp.where(keep, scores, 0.0).astype(BF16)
        o = jnp.dot(scores, v, preferred_element_type=F32)
        state = state_ref[hd]
        inter = []
        for c in range(nchunk):
            sl = slice(c * chunk, (c + 1) * chunk)
            inter.append(lax.dot_general(q_dec[sl], state.astype(BF16), nt, preferred_element_type=F32))
            k_c = jnp.where(chunk_of_row == c, k_out, jnp.zeros_like(k_out))
            state = jnp.exp(lasts[c]) * state + jnp.dot(v_t, k_c, preferred_element_type=F32)
        state_ref[hd] = state
        o = o + jnp.concatenate(inter, axis=0)
        o = o * lax.rsqrt(jnp.mean(o * o, axis=-1, keepdims=True) + EPS) * gain_ref[:, cols]
        o_ref[:, cols] = (o * jax.nn.silu(zg_ref[:, cols].astype(F32))).astype(o_ref.dtype)


def _hgrn2(proj, lb, gain, *, rows=256, heads=4):
    t = proj.shape[0]
    width = proj.shape[1] // 4
    dk = HGRN_DK
    h = width // dk
    c = HGRN_CHUNK
    hw = heads * dk
    groups = h // heads
    r = jnp.arange(rows, dtype=jnp.int32)
    tri = ((r[:, None] >= r[None, :]) & ((r[:, None] // c) == (r[None, :] // c))).astype(F32)

    def col(off):
        return pl.BlockSpec((rows, hw), lambda hh, i: (i, off + hh))

    vec = pl.BlockSpec((1, hw), lambda hh, i: (0, hh))
    mask = pl.BlockSpec((rows, rows), lambda hh, i: (0, 0))
    return pl.pallas_call(
        functools.partial(_hgrn_kernel, chunk=c, nchunk=rows // c, heads=heads, dk=dk),
        grid=(groups, t // rows),
        in_specs=[col(0), col(groups), col(2 * groups), col(3 * groups), vec, vec, mask, mask],
        out_specs=pl.BlockSpec((rows, hw), lambda hh, i: (i, hh)),
        out_shape=jax.ShapeDtypeStruct((t, width), BF16),
        scratch_shapes=[pltpu.VMEM((heads, dk, dk), F32)],
        compiler_params=_params("arbitrary", "arbitrary"),
        name="hgrn2",
    )(proj, proj, proj, proj, lb.reshape(1, width), gain.reshape(1, width), tri.astype(BF16), tri)


def _matmul_residual_kernel(*refs, n_act):
    x_ref = refs[0]
    a_refs = refs[1:1 + n_act]
    w_refs = refs[1 + n_act:1 + 2 * n_act]
    o_ref = refs[1 + 2 * n_act]
    acc = x_ref[...]
    for a_ref, w_ref in zip(a_refs, w_refs):
        acc = acc + jnp.dot(a_ref[...], w_ref[...], preferred_element_type=F32)
    o_ref[...] = acc


def _matmul_residual(x, acts, ws, *, tm=256):
    t, d = x.shape
    row = pl.BlockSpec((tm, d), lambda i: (i, 0))
    in_specs = [row]
    in_specs += [pl.BlockSpec((tm, a.shape[1]), lambda i: (i, 0)) for a in acts]
    in_specs += [pl.BlockSpec(w.shape, lambda i: (0, 0)) for w in ws]
    return pl.pallas_call(
        functools.partial(_matmul_residual_kernel, n_act=len(acts)),
        grid=(t // tm,),
        in_specs=in_specs, out_specs=row,
        out_shape=jax.ShapeDtypeStruct((t, d), F32),
        compiler_params=_params("arbitrary"),
        name="matmul_residual",
    )(x, *acts, *ws)


def _norm_router_kernel(x_ref, g_ref, wr_ref, br_ref, h_ref, route_ref, count_ref, run_ref):
    @pl.when(pl.program_id(0) == 0)
    def _():
        run_ref[...] = jnp.zeros_like(run_ref)

    h = _rms(x_ref[...], g_ref[...])
    h_ref[...] = h
    logits = jnp.dot(h, wr_ref[...], preferred_element_type=F32,
                     precision=lax.Precision.HIGHEST) + br_ref[...]
    lane = lax.broadcasted_iota(jnp.int32, logits.shape, 1)
    lane_f = lane.astype(F32)
    neg = -jnp.inf
    big = float(ROUTE_LANES)
    lg = jnp.where(lane < N_GROUPS, logits, neg)
    mg = jnp.max(lg, axis=-1, keepdims=True)
    eg = jnp.exp(lg - mg)
    pg = eg / jnp.sum(eg, axis=-1, keepdims=True)
    p_sel = jnp.max(pg, axis=-1, keepdims=True)
    g_sel = jnp.min(jnp.where(lg == mg, lane_f, big), axis=-1, keepdims=True)
    e_grp = ((lane - N_GROUPS) // EXPERTS_PER_GROUP).astype(F32)
    in_grp = jnp.where(lane >= N_GROUPS, e_grp, -1.0) == g_sel
    le = jnp.where(in_grp, logits, neg)
    t1 = jnp.max(le, axis=-1, keepdims=True)
    i1 = jnp.min(jnp.where(le == t1, lane_f, big), axis=-1, keepdims=True)
    le2 = jnp.where(lane_f == i1, neg, le)
    t2 = jnp.max(le2, axis=-1, keepdims=True)
    i2 = jnp.min(jnp.where(le2 == t2, lane_f, big), axis=-1, keepdims=True)
    e2 = jnp.exp(t2 - t1)
    den = 1.0 + e2
    gate1 = p_sel * (1.0 / den)
    gate2 = p_sel * (e2 / den)
    tm = logits.shape[0]
    hit1 = jnp.where(lane_f == i1, 1.0, 0.0)
    hit2 = jnp.where(lane_f == i2, 1.0, 0.0)
    hits = hit1 + hit2
    r = lax.broadcasted_iota(jnp.int32, (tm, tm), 0)
    s = lax.broadcasted_iota(jnp.int32, (tm, tm), 1)
    before = jnp.where(r > s, 1.0, 0.0).astype(BF16)
    prefix = jnp.dot(before, hits.astype(BF16), preferred_element_type=F32) + run_ref[...]
    rank1 = jnp.sum(prefix * hit1, axis=-1, keepdims=True)
    rank2 = jnp.sum(prefix * hit2, axis=-1, keepdims=True)
    total = run_ref[...] + jnp.sum(hits, axis=0, keepdims=True)
    run_ref[...] = total
    count_ref[...] = total
    vals = (i1 - N_GROUPS, i2 - N_GROUPS, gate1, gate2, rank1, rank2)
    slab = jnp.zeros_like(logits)
    for pos, val in enumerate(vals):
        slab = jnp.where(lane == pos, val, slab)
    route_ref[...] = slab


def _norm_router(x, g, w_rg, b_rg, w_re, b_re, *, tm=256):
    t, d = x.shape
    used = N_GROUPS + N_EXPERTS
    wr = jnp.zeros((d, ROUTE_LANES), F32).at[:, :N_GROUPS].set(w_rg).at[:, N_GROUPS:used].set(w_re)
    br = jnp.zeros((1, ROUTE_LANES), F32).at[0, :N_GROUPS].set(b_rg).at[0, N_GROUPS:used].set(b_re)
    row = pl.BlockSpec((tm, d), lambda i: (i, 0))
    return pl.pallas_call(
        _norm_router_kernel,
        grid=(t // tm,),
        in_specs=[row, pl.BlockSpec((1, d), lambda i: (0, 0)),
                  pl.BlockSpec((d, ROUTE_LANES), lambda i: (0, 0)),
                  pl.BlockSpec((1, ROUTE_LANES), lambda i: (0, 0))],
        out_specs=[row, pl.BlockSpec((tm, ROUTE_LANES), lambda i: (i, 0)),
                   pl.BlockSpec((1, ROUTE_LANES), lambda i: (0, 0))],
        out_shape=[jax.ShapeDtypeStruct((t, d), F32), jax.ShapeDtypeStruct((t, ROUTE_LANES), F32),
                   jax.ShapeDtypeStruct((1, ROUTE_LANES), F32)],
        scratch_shapes=[pltpu.VMEM((1, ROUTE_LANES), F32)],
        compiler_params=_params("arbitrary"),
        name="norm_router",
    )(x, g.reshape(1, d), wr, br)


def _expert_changed(be_ref, b):
    return (b == 0) | (be_ref[b] != be_ref[jnp.maximum(b - 1, 0)])


def _moe_ffn_kernel(be_ref, nu_ref, ne_ref, src_ref, nxt_ref, h_hbm, wg_hbm, wu_hbm, wd_hbm, out_ref,
                    xbuf, stage_g, stage_u, stage_d, wg_bf, wu_bf, wd_bf, xsem, wsem, *, layer):
    b = pl.program_id(0)
    nu = nu_ref[0]
    blk = xbuf.shape[1]

    def start_rows(idx_ref, dst_slot):
        for r in range(blk):
            pltpu.make_async_copy(h_hbm.at[pl.ds(idx_ref[0, 0, r], 1)],
                                  xbuf.at[dst_slot, pl.ds(r, 1)], xsem.at[dst_slot]).start()

    def wait_rows(dst_slot):
        pltpu.make_async_copy(h_hbm.at[pl.ds(0, blk)], xbuf.at[dst_slot], xsem.at[dst_slot]).wait()

    def weight_copies(e):
        return (pltpu.make_async_copy(wg_hbm.at[layer, e], stage_g, wsem.at[0]),
                pltpu.make_async_copy(wu_hbm.at[layer, e], stage_u, wsem.at[1]),
                pltpu.make_async_copy(wd_hbm.at[layer, e], stage_d, wsem.at[2]))

    @pl.when(b == 0)
    def _():
        for cp in weight_copies(be_ref[0]):
            cp.start()
        start_rows(src_ref, 0)

    @pl.when((b < nu) & _expert_changed(be_ref, b))
    def _():
        for cp in weight_copies(be_ref[b]):
            cp.wait()
        wg_bf[...] = stage_g[...].astype(BF16)
        wu_bf[...] = stage_u[...].astype(BF16)
        wd_bf[...] = stage_d[...].astype(BF16)

        @pl.when(ne_ref[b] >= 0)
        def _():
            for cp in weight_copies(ne_ref[b]):
                cp.start()

    def compute(slot):
        wait_rows(slot)
        x = xbuf[slot].astype(BF16)
        start_rows(nxt_ref, 1 - slot)
        hg = jnp.dot(x, wg_bf[...], preferred_element_type=F32)
        hu = jnp.dot(x, wu_bf[...], preferred_element_type=F32)
        hid = (jax.nn.silu(hg) * hu).astype(BF16)
        out_ref[...] = jnp.dot(hid, wd_bf[...], preferred_element_type=F32)

        @pl.when(b == nu - 1)
        def _():
            wait_rows(1 - slot)

    for slot in range(2):
        pl.when((b < nu) & (b % 2 == slot))(functools.partial(compute, slot))

    @pl.when(b >= nu)
    def _():
        out_ref[...] = jnp.zeros_like(out_ref)


def _combine_kernel(*refs, final):
    if final:
        pos_ref, nxt_ref, x_ref, route_ref, eo_hbm, g_ref, o_ref, ybuf, sem = refs
    else:
        pos_ref, nxt_ref, x_ref, route_ref, eo_hbm, o_ref, ybuf, sem = refs
    i = pl.program_id(0)
    last = pl.num_programs(0) - 1
    tm = x_ref.shape[0]

    def start_rows(idx_ref, dst_slot):
        for r in range(tm):
            for k in range(TOP_K):
                pltpu.make_async_copy(eo_hbm.at[pl.ds(idx_ref[0, 0, TOP_K * r + k], 1)],
                                      ybuf.at[dst_slot, k, pl.ds(r, 1)], sem.at[dst_slot]).start()

    def wait_rows(dst_slot):
        for k in range(TOP_K):
            pltpu.make_async_copy(eo_hbm.at[pl.ds(0, tm)], ybuf.at[dst_slot, k], sem.at[dst_slot]).wait()

    @pl.when(i == 0)
    def _():
        start_rows(pos_ref, 0)

    def step(slot):
        start_rows(nxt_ref, 1 - slot)
        wait_rows(slot)
        route = route_ref[...]
        y = ybuf[slot, 0] * route[:, TOP_K:TOP_K + 1] + ybuf[slot, 1] * route[:, TOP_K + 1:TOP_K + 2]
        x = x_ref[...] + y
        o_ref[...] = _rms(x, g_ref[...]) if final else x

        @pl.when(i == last)
        def _():
            wait_rows(1 - slot)

    for slot in range(2):
        pl.when(i % 2 == slot)(functools.partial(step, slot))


def _moe_dispatch(route, counts_slab, t):
    m = t * TOP_K
    expert = route[:, :TOP_K].astype(jnp.int32)
    rank = route[:, 2 * TOP_K:3 * TOP_K].astype(jnp.int32)
    counts = counts_slab[0, N_GROUPS:N_GROUPS + N_EXPERTS].astype(jnp.int32)
    padded = (counts + MOE_BLOCK - 1) // MOE_BLOCK * MOE_BLOCK
    padded_ends = jnp.cumsum(padded)
    padded_starts = padded_ends - padded
    dest = padded_starts[expert] + rank
    n_blocks = -(-(m + N_EXPERTS * (MOE_BLOCK - 1)) // MOE_BLOCK)
    cap = n_blocks * MOE_BLOCK
    token = jnp.broadcast_to(jnp.arange(t, dtype=jnp.int32)[:, None], (t, TOP_K))
    buf_src = jnp.zeros((cap,), jnp.int32).at[dest.reshape(m)].set(token.reshape(m))
    block_start = jnp.arange(n_blocks, dtype=jnp.int32) * MOE_BLOCK
    block_expert = jnp.minimum(
        jnp.sum((padded_ends[None, :] <= block_start[:, None]).astype(jnp.int32), axis=1), N_EXPERTS - 1)
    n_used = (padded_ends[-1] // MOE_BLOCK).astype(jnp.int32).reshape(1)
    ids = jnp.arange(N_EXPERTS, dtype=jnp.int32)
    later = (ids[None, :] > ids[:, None]) & (counts[None, :] > 0)
    next_active = jnp.min(jnp.where(later, ids[None, :], N_EXPERTS), axis=1)
    next_active = jnp.where(next_active < N_EXPERTS, next_active, -1)
    block_next = next_active[block_expert]
    return block_expert, n_used, block_next, buf_src.reshape(n_blocks, 1, MOE_BLOCK), dest


def _moe_experts(h, block_expert, n_used, block_next, buf_src, w_gate, w_up, w_down, layer):
    t, d = h.shape
    ff = w_gate.shape[-1]
    n_blocks = buf_src.shape[0]
    cap = n_blocks * MOE_BLOCK
    idx_block = (1, 1, MOE_BLOCK)
    hbm = pl.BlockSpec(memory_space=pl.ANY)
    return pl.pallas_call(
        functools.partial(_moe_ffn_kernel, layer=layer),
        grid_spec=pltpu.PrefetchScalarGridSpec(
            num_scalar_prefetch=3,
            grid=(n_blocks,),
            in_specs=[pl.BlockSpec(idx_block, lambda b, *_: (b, 0, 0), memory_space=pltpu.SMEM),
                      pl.BlockSpec(idx_block, lambda b, *_: (jnp.minimum(b + 1, n_blocks - 1), 0, 0),
                                   memory_space=pltpu.SMEM),
                      hbm, hbm, hbm, hbm],
            out_specs=pl.BlockSpec((MOE_BLOCK, d), lambda b, *_: (b, 0)),
            scratch_shapes=[pltpu.VMEM((2, MOE_BLOCK, d), F32),
                            pltpu.VMEM((d, ff), F32), pltpu.VMEM((d, ff), F32), pltpu.VMEM((ff, d), F32),
                            pltpu.VMEM((d, ff), BF16), pltpu.VMEM((d, ff), BF16), pltpu.VMEM((ff, d), BF16),
                            pltpu.SemaphoreType.DMA((2,)), pltpu.SemaphoreType.DMA((3,))],
        ),
        out_shape=jax.ShapeDtypeStruct((cap, d), F32),
        compiler_params=pltpu.CompilerParams(dimension_semantics=("arbitrary",),
                                             vmem_limit_bytes=MOE_VMEM_LIMIT),
        name="moe_ffn",
    )(block_expert, n_used, block_next, buf_src, buf_src, h, w_gate, w_up, w_down)


def _combine(x, route, dest, expert_out, final_gain, *, tm=256):
    t, d = x.shape
    nt = t // tm
    final = final_gain is not None
    pos = dest.reshape(nt, 1, TOP_K * tm)
    idx_block = (1, 1, TOP_K * tm)
    row = pl.BlockSpec((tm, d), lambda i: (i, 0))
    in_specs = [pl.BlockSpec(idx_block, lambda i: (i, 0, 0), memory_space=pltpu.SMEM),
                pl.BlockSpec(idx_block, lambda i: (jnp.minimum(i + 1, nt - 1), 0, 0),
                             memory_space=pltpu.SMEM),
                row, pl.BlockSpec((tm, ROUTE_LANES), lambda i: (i, 0)),
                pl.BlockSpec(memory_space=pl.ANY)]
    args = [pos, pos, x, route, expert_out]
    if final:
        in_specs.append(pl.BlockSpec((1, d), lambda i: (0, 0)))
        args.append(final_gain.reshape(1, d))
    return pl.pallas_call(
        functools.partial(_combine_kernel, final=final),
        grid=(nt,),
        in_specs=in_specs, out_specs=row,
        out_shape=jax.ShapeDtypeStruct((t, d), F32),
        scratch_shapes=[pltpu.VMEM((2, TOP_K, tm, d), F32), pltpu.SemaphoreType.DMA((2,))],
        compiler_params=_params("arbitrary"),
        name="moe_combine",
    )(*args)


def kernel(x, attn_norm, ffn_norm, final_norm, w_in_ab, ret_norm, gmlp_norm, gmlp_ws, gmlp_bs, w_out_ab, w_in_c, lb_params, hgrn_norm, w_out_c, router_w_group, router_b_group, router_w_expert, router_b_expert, w_gate, w_up, w_down):
    b, s, d = x.shape
    assert b == 1, "the sequence mixers carry state along the flattened token axis"
    depth = attn_norm.shape[0]
    lb_soft = jax.nn.softmax(lb_params.astype(F32), axis=0)
    lower_bounds = jnp.cumsum(lb_soft, axis=0) - lb_soft[0]
    xt = x.reshape(b * s, d)
    t = b * s
    for layer in range(depth):
        i = layer // 2
        if layer % 2 == 0:
            proj = _norm_matmul(xt, attn_norm[layer], w_in_ab[i].astype(BF16))
            ret = _retention(proj, ret_norm[i])
            gm = _gmlp(proj, gmlp_norm[i], gmlp_ws[i], gmlp_bs[i])
            w_out = w_out_ab[i].astype(BF16)
            nr = ret.shape[1]
            xt = _matmul_residual(xt, [ret, gm], [w_out[:nr], w_out[nr:]])
        else:
            proj = _norm_matmul(xt, attn_norm[layer], w_in_c[i].astype(BF16))
            mixed = _hgrn2(proj, lower_bounds[layer], hgrn_norm[i])
            xt = _matmul_residual(xt, [mixed], [w_out_c[i].astype(BF16)])
        h, route, counts = _norm_router(xt, ffn_norm[layer], router_w_group[layer], router_b_group[layer],
                                        router_w_expert[layer], router_b_expert[layer])
        block_expert, n_used, block_next, buf_src, dest = _moe_dispatch(route, counts, t)
        expert_out = _moe_experts(h, block_expert, n_used, block_next, buf_src, w_gate, w_up, w_down, layer)
        xt = _combine(xt, route, dest, expert_out, final_norm if layer == depth - 1 else None)
    return xt.reshape(b, s, d)
```

```python
import functools

import jax
import jax.numpy as jnp
from jax import lax
from jax.experimental import pallas as pl
from jax.experimental.pallas import tpu as pltpu

F32 = jnp.float32
BF16 = jnp.bfloat16
EPS = 1e-6

RET_HEADS = 4
RET_CHUNK = 128
ROPE_BASE = 10000.0
GMLP_GROUPS = 4
GMLP_CHUNK = 128
HGRN_DK = 128
HGRN_CHUNK = 32
N_GROUPS = 4
EXPERTS_PER_GROUP = 8
N_EXPERTS = N_GROUPS * EXPERTS_PER_GROUP
TOP_K = 2
MOE_BLOCK = 128
MOE_ROW_SLOTS = 3
ROUTE_LANES = 128

VMEM_LIMIT = 48 * 1024 * 1024
MOE_VMEM_LIMIT = 56 * 1024 * 1024


def _params(*sem):
    return pltpu.CompilerParams(dimension_semantics=sem, vmem_limit_bytes=VMEM_LIMIT)


def _rms(x, g):
    return x * lax.rsqrt(jnp.mean(x * x, axis=-1, keepdims=True) + EPS) * g


def _norm_matmul_kernel(x_ref, g_ref, w_ref, proj_ref, xn_ref):
    @pl.when(pl.program_id(1) == 0)
    def _():
        xn_ref[...] = _rms(x_ref[...], g_ref[...]).astype(BF16)

    proj_ref[...] = jnp.dot(xn_ref[...], w_ref[...],
                            preferred_element_type=F32).astype(proj_ref.dtype)


def _norm_matmul(x, g, w, *, tm=512, tn=1024):
    t, d = x.shape
    n = w.shape[1]
    return pl.pallas_call(
        _norm_matmul_kernel,
        grid=(t // tm, n // tn),
        in_specs=[pl.BlockSpec((tm, d), lambda i, j: (i, 0)),
                  pl.BlockSpec((1, d), lambda i, j: (0, 0)),
                  pl.BlockSpec((d, tn), lambda i, j: (0, j))],
        out_specs=pl.BlockSpec((tm, tn), lambda i, j: (i, j)),
        out_shape=jax.ShapeDtypeStruct((t, n), BF16),
        scratch_shapes=[pltpu.VMEM((tm, d), BF16)],
        compiler_params=_params("arbitrary", "arbitrary"),
        name="norm_matmul",
    )(x, g.reshape(1, d), w)


def _rope(x, cos, sin):
    half = x.shape[-1] // 2
    x1, x2 = x[:, :half], x[:, half:]
    return jnp.concatenate([x1 * cos - x2 * sin, x2 * cos + x1 * sin], axis=-1)


def _retention_kernel(cd_ref, q_ref, k_ref, v_ref, g_ref, cos_ref, sin_ref, dint_ref, qd_ref, kd_ref,
                      gain_ref, o_ref, state_ref, *, chunk, nchunk):
    @pl.when(pl.program_id(1) == 0)
    def _():
        state_ref[...] = jnp.zeros_like(state_ref)

    dk = q_ref.shape[-1]
    dint = dint_ref[0]
    qd = qd_ref[0]
    kd = kd_ref[0]
    cd = cd_ref[pl.program_id(0)]
    gain = gain_ref[...]
    for c in range(nchunk):
        rows = pl.ds(c * chunk, chunk)
        cos = cos_ref[rows, :]
        sin = sin_ref[rows, :]
        q = _rope(q_ref[rows, :].astype(F32), cos, sin)
        k = _rope(k_ref[rows, :].astype(F32), cos, sin) * (dk ** -0.5)
        v = v_ref[rows, :]
        scores = lax.dot_general(q.astype(BF16), k.astype(BF16), (((1,), (1,)), ((), ())),
                                 preferred_element_type=F32) * dint
        state = state_ref[...]
        o = (jnp.dot(scores.astype(BF16), v, preferred_element_type=F32)
             + jnp.dot((q * qd).astype(BF16), state.astype(BF16), preferred_element_type=F32))
        state_ref[...] = cd * state + lax.dot_general(
            (k * kd).astype(BF16), v, (((0,), (0,)), ((), ())), preferred_element_type=F32)
        o = o - jnp.mean(o, axis=-1, keepdims=True)
        o = o * lax.rsqrt(jnp.mean(o * o, axis=-1, keepdims=True) + EPS) * gain
        o_ref[rows, :] = (jax.nn.silu(g_ref[rows, :].astype(F32)) * o).astype(o_ref.dtype)


def _retention(proj, gain, *, rows=512):
    t = proj.shape[0]
    h = RET_HEADS
    dk = proj.shape[1] // 6 // h
    dv = dk
    c = RET_CHUNK
    f32 = F32
    inv = ROPE_BASE ** (-jnp.arange(0, dk, 2, dtype=f32) / dk)
    ang = jnp.arange(t, dtype=f32)[:, None] * inv[None, :]
    cos, sin = jnp.cos(ang), jnp.sin(ang)
    log_gamma = jnp.log(1.0 - jnp.exp2(-5.0 - jnp.arange(h, dtype=f32)))
    idx = jnp.arange(c, dtype=f32)
    diff = idx[:, None] - idx[None, :]
    d_intra = jnp.where(diff >= 0, jnp.exp(jnp.maximum(diff, 0.0) * log_gamma[:, None, None]), 0.0)
    q_decay = jnp.broadcast_to(jnp.exp((idx + 1.0)[None, :] * log_gamma[:, None])[..., None], (h, c, dk))
    k_decay = jnp.broadcast_to(jnp.exp((c - 1.0 - idx)[None, :] * log_gamma[:, None])[..., None], (h, c, dk))
    chunk_decay = jnp.exp(c * log_gamma)

    def col(off):
        return pl.BlockSpec((rows, dk), lambda hh, i: (i, off + hh))

    tab = pl.BlockSpec((rows, dk // 2), lambda hh, i: (i, 0))
    per_head = lambda shp: pl.BlockSpec((1,) + shp, lambda hh, i: (hh, 0, 0))
    return pl.pallas_call(
        functools.partial(_retention_kernel, chunk=c, nchunk=rows // c),
        grid=(h, t // rows),
        in_specs=[pl.BlockSpec(memory_space=pltpu.SMEM),
                  col(0), col(h), col(2 * h), col(3 * h), tab, tab,
                  per_head((c, c)), per_head((c, dk)), per_head((c, dk)),
                  pl.BlockSpec((1, dv), lambda hh, i: (0, hh))],
        out_specs=pl.BlockSpec((rows, dv), lambda hh, i: (i, hh)),
        out_shape=jax.ShapeDtypeStruct((t, h * dv), BF16),
        scratch_shapes=[pltpu.VMEM((dk, dv), F32)],
        compiler_params=_params("arbitrary", "arbitrary"),
        name="retention",
    )(chunk_decay, proj, proj, proj, proj, cos, sin, d_intra, q_decay, k_decay, gain.reshape(1, h * dv))


def _gmlp_kernel(u_ref, vs_ref, gain_ref, ws_ref, bs_ref, o_ref, *, chunk, nchunk):
    r = lax.broadcasted_iota(jnp.int32, (chunk, chunk), 0)
    s = lax.broadcasted_iota(jnp.int32, (chunk, chunk), 1)
    w = jnp.where(r >= s, ws_ref[0], 0.0).astype(BF16)
    gain = gain_ref[...]
    bs = bs_ref[0]
    for c in range(nchunk):
        rows = pl.ds(c * chunk, chunk)
        v = jax.nn.gelu(vs_ref[rows, :].astype(F32))
        v = v - jnp.mean(v, axis=-1, keepdims=True)
        v = v * lax.rsqrt(jnp.mean(v * v, axis=-1, keepdims=True) + EPS) * gain
        mixed = jnp.dot(w, v.astype(BF16), preferred_element_type=F32) + bs
        o_ref[rows, :] = (jax.nn.gelu(u_ref[rows, :].astype(F32)) * mixed).astype(o_ref.dtype)


def _gmlp(proj, gain, ws, bs, *, rows=512):
    t = proj.shape[0]
    g = GMLP_GROUPS
    dim = proj.shape[1] // 6 // g
    c = GMLP_CHUNK
    return pl.pallas_call(
        functools.partial(_gmlp_kernel, chunk=c, nchunk=rows // c),
        grid=(g, t // rows),
        in_specs=[pl.BlockSpec((rows, dim), lambda gg, i: (i, 4 * g + gg)),
                  pl.BlockSpec((rows, dim), lambda gg, i: (i, 5 * g + gg)),
                  pl.BlockSpec((1, dim), lambda gg, i: (0, gg)),
                  pl.BlockSpec((1, c, c), lambda gg, i: (gg, 0, 0)),
                  pl.BlockSpec((1, c, 1), lambda gg, i: (gg, 0, 0))],
        out_specs=pl.BlockSpec((rows, dim), lambda gg, i: (i, gg)),
        out_shape=jax.ShapeDtypeStruct((t, g * dim), BF16),
        compiler_params=_params("arbitrary", "arbitrary"),
        name="gmlp",
    )(proj, proj, gain.reshape(1, g * dim), ws, bs.reshape(g, c, 1))


def _hgrn_kernel(zq_ref, zf_ref, zi_ref, zg_ref, lb_ref, gain_ref, tri_ref, keep_ref, o_ref, state_ref,
                 *, chunk, nchunk, heads, dk):
    @pl.when(pl.program_id(1) == 0)
    def _():
        state_ref[...] = jnp.zeros_like(state_ref)

    rows = chunk * nchunk
    nt = (((1,), (1,)), ((), ()))
    tri = tri_ref[...]
    keep = keep_ref[...] > 0.0
    chunk_of_row = lax.broadcasted_iota(jnp.int32, (rows, dk), 0) // chunk
    for hd in range(heads):
        cols = slice(hd * dk, (hd + 1) * dk)
        lb = lb_ref[:, cols]
        zf = zf_ref[:, cols].astype(F32)
        f = lb + (1.0 - lb) * jax.nn.sigmoid(zf)
        kk = (1.0 - lb) * jax.nn.sigmoid(-zf)
        log_f = jnp.log(f)
        p0 = log_f.astype(BF16)
        r0 = log_f - p0.astype(F32)
        p1 = r0.astype(BF16)
        p2 = (r0 - p1.astype(F32)).astype(BF16)
        cum = (jnp.dot(tri, p0, preferred_element_type=F32)
               + jnp.dot(tri, p1, preferred_element_type=F32)
               + jnp.dot(tri, p2, preferred_element_type=F32))
        lasts = [cum[(c + 1) * chunk - 1:(c + 1) * chunk, :] for c in range(nchunk)]
        last_rows = jnp.concatenate([jnp.broadcast_to(l, (chunk, dk)) for l in lasts], axis=0)
        q_dec = (jax.nn.silu(zq_ref[:, cols].astype(F32)) * jnp.exp(cum)).astype(BF16)
        k_dec = (kk * jnp.exp(-cum)).astype(BF16)
        k_out = (kk * jnp.exp(last_rows - cum)).astype(BF16)
        v = zi_ref[:, cols]
        v_t = v.astype(F32).T.astype(BF16)
        scores = lax.dot_general(q_dec, k_dec, nt, preferred_element_type=F32)
        scores = jnp.where(keep, scores, 0.0).astype(BF16)
        o = jnp.dot(scores, v, preferred_element_type=F32)
        state = state_ref[hd]
        inter = []
        for c in range(nchunk):
            sl = slice(c * chunk, (c + 1) * chunk)
            inter.append(lax.dot_general(q_dec[sl], state.astype(BF16), nt, preferred_element_type=F32))
            k_c = jnp.where(chunk_of_row == c, k_out, jnp.zeros_like(k_out))
            state = jnp.exp(lasts[c]) * state + jnp.dot(v_t, k_c, preferred_element_type=F32)
        state_ref[hd] = state
        o = o + jnp.concatenate(inter, axis=0)
        o = o * lax.rsqrt(jnp.mean(o * o, axis=-1, keepdims=True) + EPS) * gain_ref[:, cols]
        o_ref[:, cols] = (o * jax.nn.silu(zg_ref[:, cols].astype(F32))).astype(o_ref.dtype)


def _hgrn2(proj, lb, gain, *, rows=256, heads=4):
    t = proj.shape[0]
    width = proj.shape[1] // 4
    dk = HGRN_DK
    h = width // dk
    c = HGRN_CHUNK
    hw = heads * dk
    groups = h // heads
    r = jnp.arange(rows, dtype=jnp.int32)
    tri = ((r[:, None] >= r[None, :]) & ((r[:, None] // c) == (r[None, :] // c))).astype(F32)

    def col(off):
        return pl.BlockSpec((rows, hw), lambda hh, i: (i, off + hh))

    vec = pl.BlockSpec((1, hw), lambda hh, i: (0, hh))
    mask = pl.BlockSpec((rows, rows), lambda hh, i: (0, 0))
    return pl.pallas_call(
        functools.partial(_hgrn_kernel, chunk=c, nchunk=rows // c, heads=heads, dk=dk),
        grid=(groups, t // rows),
        in_specs=[col(0), col(groups), col(2 * groups), col(3 * groups), vec, vec, mask, mask],
        out_specs=pl.BlockSpec((rows, hw), lambda hh, i: (i, hh)),
        out_shape=jax.ShapeDtypeStruct((t, width), BF16),
        scratch_shapes=[pltpu.VMEM((heads, dk, dk), F32)],
        compiler_params=_params("arbitrary", "arbitrary"),
        name="hgrn2",
    )(proj, proj, proj, proj, lb.reshape(1, width), gain.reshape(1, width), tri.astype(BF16), tri)


def _matmul_residual_kernel(*refs, n_act):
    x_ref = refs[0]
    a_refs = refs[1:1 + n_act]
    w_refs = refs[1 + n_act:1 + 2 * n_act]
    o_ref = refs[1 + 2 * n_act]
    acc = x_ref[...]
    for a_ref, w_ref in zip(a_refs, w_refs):
        acc = acc + jnp.dot(a_ref[...], w_ref[...], preferred_element_type=F32)
    o_ref[...] = acc


def _matmul_residual(x, acts, ws, *, tm=256):
    t, d = x.shape
    row = pl.BlockSpec((tm, d), lambda i: (i, 0))
    in_specs = [row]
    in_specs += [pl.BlockSpec((tm, a.shape[1]), lambda i: (i, 0)) for a in acts]
    in_specs += [pl.BlockSpec(w.shape, lambda i: (0, 0)) for w in ws]
    return pl.pallas_call(
        functools.partial(_matmul_residual_kernel, n_act=len(acts)),
        grid=(t // tm,),
        in_specs=in_specs, out_specs=row,
        out_shape=jax.ShapeDtypeStruct((t, d), F32),
        compiler_params=_params("arbitrary"),
        name="matmul_residual",
    )(x, *acts, *ws)


def _norm_router_kernel(x_ref, g_ref, wr_ref, br_ref, h_ref, route_ref, count_ref, run_ref):
    @pl.when(pl.program_id(0) == 0)
    def _():
        run_ref[...] = jnp.zeros_like(run_ref)

    h = _rms(x_ref[...], g_ref[...])
    h_ref[...] = h
    logits = jnp.dot(h, wr_ref[...], preferred_element_type=F32,
                     precision=lax.Precision.HIGHEST) + br_ref[...]
    lane = lax.broadcasted_iota(jnp.int32, logits.shape, 1)
    lane_f = lane.astype(F32)
    neg = -jnp.inf
    big = float(ROUTE_LANES)
    lg = jnp.where(lane < N_GROUPS, logits, neg)
    mg = jnp.max(lg, axis=-1, keepdims=True)
    eg = jnp.exp(lg - mg)
    pg = eg / jnp.sum(eg, axis=-1, keepdims=True)
    p_sel = jnp.max(pg, axis=-1, keepdims=True)
    g_sel = jnp.min(jnp.where(lg == mg, lane_f, big), axis=-1, keepdims=True)
    e_grp = ((lane - N_GROUPS) // EXPERTS_PER_GROUP).astype(F32)
    in_grp = jnp.where(lane >= N_GROUPS, e_grp, -1.0) == g_sel
    le = jnp.where(in_grp, logits, neg)
    t1 = jnp.max(le, axis=-1, keepdims=True)
    i1 = jnp.min(jnp.where(le == t1, lane_f, big), axis=-1, keepdims=True)
    le2 = jnp.where(lane_f == i1, neg, le)
    t2 = jnp.max(le2, axis=-1, keepdims=True)
    i2 = jnp.min(jnp.where(le2 == t2, lane_f, big), axis=-1, keepdims=True)
    e2 = jnp.exp(t2 - t1)
    den = 1.0 + e2
    gate1 = p_sel * (1.0 / den)
    gate2 = p_sel * (e2 / den)
    tm = logits.shape[0]
    hit1 = jnp.where(lane_f == i1, 1.0, 0.0)
    hit2 = jnp.where(lane_f == i2, 1.0, 0.0)
    hits = hit1 + hit2
    r = lax.broadcasted_iota(jnp.int32, (tm, tm), 0)
    s = lax.broadcasted_iota(jnp.int32, (tm, tm), 1)
    before = jnp.where(r > s, 1.0, 0.0).astype(BF16)
    prefix = jnp.dot(before, hits.astype(BF16), preferred_element_type=F32) + run_ref[...]
    rank1 = jnp.sum(prefix * hit1, axis=-1, keepdims=True)
    rank2 = jnp.sum(prefix * hit2, axis=-1, keepdims=True)
    total = run_ref[...] + jnp.sum(hits, axis=0, keepdims=True)
    run_ref[...] = total
    count_ref[...] = total
    vals = (i1 - N_GROUPS, i2 - N_GROUPS, gate1, gate2, rank1, rank2)
    slab = jnp.zeros_like(logits)
    for pos, val in enumerate(vals):
        slab = jnp.where(lane == pos, val, slab)
    route_ref[...] = slab


def _norm_router(x, g, w_rg, b_rg, w_re, b_re, *, tm=256):
    t, d = x.shape
    used = N_GROUPS + N_EXPERTS
    wr = jnp.zeros((d, ROUTE_LANES), F32).at[:, :N_GROUPS].set(w_rg).at[:, N_GROUPS:used].set(w_re)
    br = jnp.zeros((1, ROUTE_LANES), F32).at[0, :N_GROUPS].set(b_rg).at[0, N_GROUPS:used].set(b_re)
    row = pl.BlockSpec((tm, d), lambda i: (i, 0))
    return pl.pallas_call(
        _norm_router_kernel,
        grid=(t // tm,),
        in_specs=[row, pl.BlockSpec((1, d), lambda i: (0, 0)),
                  pl.BlockSpec((d, ROUTE_LANES), lambda i: (0, 0)),
                  pl.BlockSpec((1, ROUTE_LANES), lambda i: (0, 0))],
        out_specs=[row, pl.BlockSpec((tm, ROUTE_LANES), lambda i: (i, 0)),
                   pl.BlockSpec((1, ROUTE_LANES), lambda i: (0, 0))],
        out_shape=[jax.ShapeDtypeStruct((t, d), F32), jax.ShapeDtypeStruct((t, ROUTE_LANES), F32),
                   jax.ShapeDtypeStruct((1, ROUTE_LANES), F32)],
        scratch_shapes=[pltpu.VMEM((1, ROUTE_LANES), F32)],
        compiler_params=_params("arbitrary"),
        name="norm_router",
    )(x, g.reshape(1, d), wr, br)


def _expert_changed(be_ref, b):
    return (b == 0) | (be_ref[b] != be_ref[jnp.maximum(b - 1, 0)])


def _moe_ffn_kernel(be_ref, nu_ref, ne_ref, src0_ref, src1_ref, src2_ref, h_hbm, wg_hbm, wu_hbm, wd_hbm,
                    out_ref, xbuf, stage_g, stage_u, stage_d, wg_bf, wu_bf, wd_bf, xsem, wsem, *, layer):
    b = pl.program_id(0)
    nu = nu_ref[0]
    nslot, blk = xbuf.shape[0], xbuf.shape[1]

    def start_rows(idx_ref, dst_slot):
        for r in range(blk):
            pltpu.make_async_copy(h_hbm.at[pl.ds(idx_ref[0, 0, r], 1)],
                                  xbuf.at[dst_slot, pl.ds(r, 1)], xsem.at[dst_slot]).start(priority=0)

    def wait_rows(dst_slot):
        pltpu.make_async_copy(h_hbm.at[pl.ds(0, blk)], xbuf.at[dst_slot], xsem.at[dst_slot]).wait()

    def weight_copies(e):
        return (pltpu.make_async_copy(wg_hbm.at[layer, e], stage_g, wsem.at[0]),
                pltpu.make_async_copy(wu_hbm.at[layer, e], stage_u, wsem.at[1]),
                pltpu.make_async_copy(wd_hbm.at[layer, e], stage_d, wsem.at[2]))

    @pl.when(b == 0)
    def _():
        for cp in weight_copies(be_ref[0]):
            cp.start(priority=1)
        start_rows(src0_ref, 0)
        start_rows(src1_ref, 1)

    @pl.when((b < nu) & _expert_changed(be_ref, b))
    def _():
        for cp in weight_copies(be_ref[b]):
            cp.wait()
        wg_bf[...] = stage_g[...].astype(BF16)
        wu_bf[...] = stage_u[...].astype(BF16)
        wd_bf[...] = stage_d[...].astype(BF16)

        @pl.when(ne_ref[b] >= 0)
        def _():
            for cp in weight_copies(ne_ref[b]):
                cp.start(priority=1)

    def compute(slot):
        wait_rows(slot)
        x = xbuf[slot].astype(BF16)
        start_rows(src2_ref, (slot + 2) % nslot)
        hg = jnp.dot(x, wg_bf[...], preferred_element_type=F32)
        hu = jnp.dot(x, wu_bf[...], preferred_element_type=F32)
        hid = (jax.nn.silu(hg) * hu).astype(BF16)
        out_ref[...] = jnp.dot(hid, wd_bf[...], preferred_element_type=F32)

        @pl.when(b == nu - 1)
        def _():
            wait_rows((slot + 1) % nslot)
            wait_rows((slot + 2) % nslot)

    for slot in range(nslot):
        pl.when((b < nu) & (b % nslot == slot))(functools.partial(compute, slot))

    @pl.when(b >= nu)
    def _():
        out_ref[...] = jnp.zeros_like(out_ref)


def _combine_kernel(*refs, final):
    if final:
        pos_ref, nxt_ref, x_ref, route_ref, eo_hbm, g_ref, o_ref, ybuf, sem = refs
    else:
        pos_ref, nxt_ref, x_ref, route_ref, eo_hbm, o_ref, ybuf, sem = refs
    i = pl.program_id(0)
    last = pl.num_programs(0) - 1
    tm = x_ref.shape[0]

    def start_rows(idx_ref, dst_slot):
        for r in range(tm):
            for k in range(TOP_K):
                pltpu.make_async_copy(eo_hbm.at[pl.ds(idx_ref[0, 0, TOP_K * r + k], 1)],
                                      ybuf.at[dst_slot, k, pl.ds(r, 1)], sem.at[dst_slot]).start()

    def wait_rows(dst_slot):
        for k in range(TOP_K):
            pltpu.make_async_copy(eo_hbm.at[pl.ds(0, tm)], ybuf.at[dst_slot, k], sem.at[dst_slot]).wait()

    @pl.when(i == 0)
    def _():
        start_rows(pos_ref, 0)

    def step(slot):
        start_rows(nxt_ref, 1 - slot)
        wait_rows(slot)
        route = route_ref[...]
        y = ybuf[slot, 0] * route[:, TOP_K:TOP_K + 1] + ybuf[slot, 1] * route[:, TOP_K + 1:TOP_K + 2]
        x = x_ref[...] + y
        o_ref[...] = _rms(x, g_ref[...]) if final else x

        @pl.when(i == last)
        def _():
            wait_rows(1 - slot)

    for slot in range(2):
        pl.when(i % 2 == slot)(functools.partial(step, slot))


def _moe_dispatch(route, counts_slab, t):
    m = t * TOP_K
    expert = route[:, :TOP_K].astype(jnp.int32)
    rank = route[:, 2 * TOP_K:3 * TOP_K].astype(jnp.int32)
    counts = counts_slab[0, N_GROUPS:N_GROUPS + N_EXPERTS].astype(jnp.int32)
    padded = (counts + MOE_BLOCK - 1) // MOE_BLOCK * MOE_BLOCK
    padded_ends = jnp.cumsum(padded)
    padded_starts = padded_ends - padded
    dest = padded_starts[expert] + rank
    n_blocks = -(-(m + N_EXPERTS * (MOE_BLOCK - 1)) // MOE_BLOCK)
    cap = n_blocks * MOE_BLOCK
    token = jnp.broadcast_to(jnp.arange(t, dtype=jnp.int32)[:, None], (t, TOP_K))
    buf_src = jnp.zeros((cap,), jnp.int32).at[dest.reshape(m)].set(token.reshape(m))
    block_start = jnp.arange(n_blocks, dtype=jnp.int32) * MOE_BLOCK
    block_expert = jnp.minimum(
        jnp.sum((padded_ends[None, :] <= block_start[:, None]).astype(jnp.int32), axis=1), N_EXPERTS - 1)
    n_used = (padded_ends[-1] // MOE_BLOCK).astype(jnp.int32).reshape(1)
    ids = jnp.arange(N_EXPERTS, dtype=jnp.int32)
    later = (ids[None, :] > ids[:, None]) & (counts[None, :] > 0)
    next_active = jnp.min(jnp.where(later, ids[None, :], N_EXPERTS), axis=1)
    next_active = jnp.where(next_active < N_EXPERTS, next_active, -1)
    block_next = next_active[block_expert]
    return block_expert, n_used, block_next, buf_src.reshape(n_blocks, 1, MOE_BLOCK), dest


def _moe_experts(h, block_expert, n_used, block_next, buf_src, w_gate, w_up, w_down, layer):
    t, d = h.shape
    ff = w_gate.shape[-1]
    n_blocks = buf_src.shape[0]
    cap = n_blocks * MOE_BLOCK
    idx_block = (1, 1, MOE_BLOCK)
    hbm = pl.BlockSpec(memory_space=pl.ANY)

    def ahead(k, b, *_):
        return (jnp.minimum(b + k, n_blocks - 1), 0, 0)

    return pl.pallas_call(
        functools.partial(_moe_ffn_kernel, layer=layer),
        grid_spec=pltpu.PrefetchScalarGridSpec(
            num_scalar_prefetch=3,
            grid=(n_blocks,),
            in_specs=[pl.BlockSpec(idx_block, functools.partial(ahead, 0), memory_space=pltpu.SMEM),
                      pl.BlockSpec(idx_block, functools.partial(ahead, 1), memory_space=pltpu.SMEM),
                      pl.BlockSpec(idx_block, functools.partial(ahead, 2), memory_space=pltpu.SMEM),
                      hbm, hbm, hbm, hbm],
            out_specs=pl.BlockSpec((MOE_BLOCK, d), lambda b, *_: (b, 0)),
            scratch_shapes=[pltpu.VMEM((MOE_ROW_SLOTS, MOE_BLOCK, d), F32),
                            pltpu.VMEM((d, ff), F32), pltpu.VMEM((d, ff), F32), pltpu.VMEM((ff, d), F32),
                            pltpu.VMEM((d, ff), BF16), pltpu.VMEM((d, ff), BF16), pltpu.VMEM((ff, d), BF16),
                            pltpu.SemaphoreType.DMA((MOE_ROW_SLOTS,)), pltpu.SemaphoreType.DMA((3,))],
        ),
        out_shape=jax.ShapeDtypeStruct((cap, d), F32),
        compiler_params=pltpu.CompilerParams(dimension_semantics=("arbitrary",),
                                             vmem_limit_bytes=MOE_VMEM_LIMIT),
        name="moe_ffn",
    )(block_expert, n_used, block_next, buf_src, buf_src, buf_src, h, w_gate, w_up, w_down)


def _combine(x, route, dest, expert_out, final_gain, *, tm=256):
    t, d = x.shape
    nt = t // tm
    final = final_gain is not None
    pos = dest.reshape(nt, 1, TOP_K * tm)
    idx_block = (1, 1, TOP_K * tm)
    row = pl.BlockSpec((tm, d), lambda i: (i, 0))
    in_specs = [pl.BlockSpec(idx_block, lambda i: (i, 0, 0), memory_space=pltpu.SMEM),
                pl.BlockSpec(idx_block, lambda i: (jnp.minimum(i + 1, nt - 1), 0, 0),
                             memory_space=pltpu.SMEM),
                row, pl.BlockSpec((tm, ROUTE_LANES), lambda i: (i, 0)),
                pl.BlockSpec(memory_space=pl.ANY)]
    args = [pos, pos, x, route, expert_out]
    if final:
        in_specs.append(pl.BlockSpec((1, d), lambda i: (0, 0)))
        args.append(final_gain.reshape(1, d))
    return pl.pallas_call(
        functools.partial(_combine_kernel, final=final),
        grid=(nt,),
        in_specs=in_specs, out_specs=row,
        out_shape=jax.ShapeDtypeStruct((t, d), F32),
        scratch_shapes=[pltpu.VMEM((2, TOP_K, tm, d), F32), pltpu.SemaphoreType.DMA((2,))],
        compiler_params=_params("arbitrary"),
        name="moe_combine",
    )(*args)


def kernel(x, attn_norm, ffn_norm, final_norm, w_in_ab, ret_norm, gmlp_norm, gmlp_ws, gmlp_bs, w_out_ab, w_in_c, lb_params, hgrn_norm, w_out_c, router_w_group, router_b_group, router_w_expert, router_b_expert, w_gate, w_up, w_down):
    b, s, d = x.shape
    assert b == 1, "the sequence mixers carry state along the flattened token axis"
    depth = attn_norm.shape[0]
    lb_soft = jax.nn.softmax(lb_params.astype(F32), axis=0)
    lower_bounds = jnp.cumsum(lb_soft, axis=0) - lb_soft[0]
    xt = x.reshape(b * s, d)
    t = b * s
    for layer in range(depth):
        i = layer // 2
        if layer % 2 == 0:
            proj = _norm_matmul(xt, attn_norm[layer], w_in_ab[i].astype(BF16))
            ret = _retention(proj, ret_norm[i])
            gm = _gmlp(proj, gmlp_norm[i], gmlp_ws[i], gmlp_bs[i])
            w_out = w_out_ab[i].astype(BF16)
            nr = ret.shape[1]
            xt = _matmul_residual(xt, [ret, gm], [w_out[:nr], w_out[nr:]])
        else:
            proj = _norm_matmul(xt, attn_norm[layer], w_in_c[i].astype(BF16))
            mixed = _hgrn2(proj, lower_bounds[layer], hgrn_norm[i])
            xt = _matmul_residual(xt, [mixed], [w_out_c[i].astype(BF16)])
        h, route, counts = _norm_router(xt, ffn_norm[layer], router_w_group[layer], router_b_group[layer],
                                        router_w_expert[layer], router_b_expert[layer])
        block_expert, n_used, block_next, buf_src, dest = _moe_dispatch(route, counts, t)
        expert_out = _moe_experts(h, block_expert, n_used, block_next, buf_src, w_gate, w_up, w_down, layer)
        xt = _combine(xt, route, dest, expert_out, final_norm if layer == depth - 1 else None)
    return xt.reshape(b, s, d)
```

```python
import functools

import jax
import jax.numpy as jnp
from jax import lax
from jax.experimental import pallas as pl
from jax.experimental.pallas import tpu as pltpu

F32 = jnp.float32
BF16 = jnp.bfloat16
EPS = 1e-6

RET_HEADS = 4
RET_CHUNK = 128
ROPE_BASE = 10000.0
GMLP_GROUPS = 4
GMLP_CHUNK = 128
HGRN_DK = 128
HGRN_CHUNK = 32
N_GROUPS = 4
EXPERTS_PER_GROUP = 8
N_EXPERTS = N_GROUPS * EXPERTS_PER_GROUP
TOP_K = 2
MOE_BLOCK = 128
MOE_ROW_SLOTS = 3
ROUTE_LANES = 128
NORM_SLAB = 256

VMEM_LIMIT = 48 * 1024 * 1024
MOE_VMEM_LIMIT = 56 * 1024 * 1024


def _params(*sem):
    return pltpu.CompilerParams(dimension_semantics=sem, vmem_limit_bytes=VMEM_LIMIT)


def _rms(x, g):
    return x * lax.rsqrt(jnp.mean(x * x, axis=-1, keepdims=True) + EPS) * g


def _norm_matmul_kernel(x_ref, g_ref, w_ref, proj_ref, xn_ref):
    @pl.when(pl.program_id(1) == 0)
    def _():
        for r0 in range(0, x_ref.shape[0], NORM_SLAB):
            rows = pl.ds(r0, NORM_SLAB)
            xn_ref[rows, :] = _rms(x_ref[rows, :], g_ref[...]).astype(BF16)

    proj_ref[...] = jnp.dot(xn_ref[...], w_ref[...],
                            preferred_element_type=F32).astype(proj_ref.dtype)


def _norm_matmul(x, g, w, *, tm=1024, tn=1024):
    t, d = x.shape
    n = w.shape[1]
    return pl.pallas_call(
        _norm_matmul_kernel,
        grid=(t // tm, n // tn),
        in_specs=[pl.BlockSpec((tm, d), lambda i, j: (i, 0)),
                  pl.BlockSpec((1, d), lambda i, j: (0, 0)),
                  pl.BlockSpec((d, tn), lambda i, j: (0, j))],
        out_specs=pl.BlockSpec((tm, tn), lambda i, j: (i, j)),
        out_shape=jax.ShapeDtypeStruct((t, n), BF16),
        scratch_shapes=[pltpu.VMEM((tm, d), BF16)],
        compiler_params=_params("arbitrary", "arbitrary"),
        name="norm_matmul",
    )(x, g.reshape(1, d), w)


def _rope(x, cos, sin):
    half = x.shape[-1] // 2
    x1, x2 = x[:, :half], x[:, half:]
    return jnp.concatenate([x1 * cos - x2 * sin, x2 * cos + x1 * sin], axis=-1)


def _retention_kernel(cd_ref, q_ref, k_ref, v_ref, g_ref, cos_ref, sin_ref, dint_ref, qd_ref, kd_ref,
                      gain_ref, o_ref, state_ref, *, chunk, nchunk):
    @pl.when(pl.program_id(1) == 0)
    def _():
        state_ref[...] = jnp.zeros_like(state_ref)

    dk = q_ref.shape[-1]
    dint = dint_ref[0]
    qd = qd_ref[0]
    kd = kd_ref[0]
    cd = cd_ref[pl.program_id(0)]
    gain = gain_ref[...]
    for c in range(nchunk):
        rows = pl.ds(c * chunk, chunk)
        cos = cos_ref[rows, :]
        sin = sin_ref[rows, :]
        q = _rope(q_ref[rows, :].astype(F32), cos, sin)
        k = _rope(k_ref[rows, :].astype(F32), cos, sin) * (dk ** -0.5)
        v = v_ref[rows, :]
        scores = lax.dot_general(q.astype(BF16), k.astype(BF16), (((1,), (1,)), ((), ())),
                                 preferred_element_type=F32) * dint
        state = state_ref[...]
        o = (jnp.dot(scores.astype(BF16), v, preferred_element_type=F32)
             + jnp.dot((q * qd).astype(BF16), state.astype(BF16), preferred_element_type=F32))
        state_ref[...] = cd * state + lax.dot_general(
            (k * kd).astype(BF16), v, (((0,), (0,)), ((), ())), preferred_element_type=F32)
        o = o - jnp.mean(o, axis=-1, keepdims=True)
        o = o * lax.rsqrt(jnp.mean(o * o, axis=-1, keepdims=True) + EPS) * gain
        o_ref[rows, :] = (jax.nn.silu(g_ref[rows, :].astype(F32)) * o).astype(o_ref.dtype)


def _retention(proj, gain, *, rows=512):
    t = proj.shape[0]
    h = RET_HEADS
    dk = proj.shape[1] // 6 // h
    dv = dk
    c = RET_CHUNK
    f32 = F32
    inv = ROPE_BASE ** (-jnp.arange(0, dk, 2, dtype=f32) / dk)
    ang = jnp.arange(t, dtype=f32)[:, None] * inv[None, :]
    cos, sin = jnp.cos(ang), jnp.sin(ang)
    log_gamma = jnp.log(1.0 - jnp.exp2(-5.0 - jnp.arange(h, dtype=f32)))
    idx = jnp.arange(c, dtype=f32)
    diff = idx[:, None] - idx[None, :]
    d_intra = jnp.where(diff >= 0, jnp.exp(jnp.maximum(diff, 0.0) * log_gamma[:, None, None]), 0.0)
    q_decay = jnp.broadcast_to(jnp.exp((idx + 1.0)[None, :] * log_gamma[:, None])[..., None], (h, c, dk))
    k_decay = jnp.broadcast_to(jnp.exp((c - 1.0 - idx)[None, :] * log_gamma[:, None])[..., None], (h, c, dk))
    chunk_decay = jnp.exp(c * log_gamma)

    def col(off):
        return pl.BlockSpec((rows, dk), lambda hh, i: (i, off + hh))

    tab = pl.BlockSpec((rows, dk // 2), lambda hh, i: (i, 0))
    per_head = lambda shp: pl.BlockSpec((1,) + shp, lambda hh, i: (hh, 0, 0))
    return pl.pallas_call(
        functools.partial(_retention_kernel, chunk=c, nchunk=rows // c),
        grid=(h, t // rows),
        in_specs=[pl.BlockSpec(memory_space=pltpu.SMEM),
                  col(0), col(h), col(2 * h), col(3 * h), tab, tab,
                  per_head((c, c)), per_head((c, dk)), per_head((c, dk)),
                  pl.BlockSpec((1, dv), lambda hh, i: (0, hh))],
        out_specs=pl.BlockSpec((rows, dv), lambda hh, i: (i, hh)),
        out_shape=jax.ShapeDtypeStruct((t, h * dv), BF16),
        scratch_shapes=[pltpu.VMEM((dk, dv), F32)],
        compiler_params=_params("arbitrary", "arbitrary"),
        name="retention",
    )(chunk_decay, proj, proj, proj, proj, cos, sin, d_intra, q_decay, k_decay, gain.reshape(1, h * dv))


def _gmlp_kernel(u_ref, vs_ref, gain_ref, ws_ref, bs_ref, o_ref, *, chunk, nchunk):
    r = lax.broadcasted_iota(jnp.int32, (chunk, chunk), 0)
    s = lax.broadcasted_iota(jnp.int32, (chunk, chunk), 1)
    w = jnp.where(r >= s, ws_ref[0], 0.0).astype(BF16)
    gain = gain_ref[...]
    bs = bs_ref[0]
    for c in range(nchunk):
        rows = pl.ds(c * chunk, chunk)
        v = jax.nn.gelu(vs_ref[rows, :].astype(F32))
        v = v - jnp.mean(v, axis=-1, keepdims=True)
        v = v * lax.rsqrt(jnp.mean(v * v, axis=-1, keepdims=True) + EPS) * gain
        mixed = jnp.dot(w, v.astype(BF16), preferred_element_type=F32) + bs
        o_ref[rows, :] = (jax.nn.gelu(u_ref[rows, :].astype(F32)) * mixed).astype(o_ref.dtype)


def _gmlp(proj, gain, ws, bs, *, rows=512):
    t = proj.shape[0]
    g = GMLP_GROUPS
    dim = proj.shape[1] // 6 // g
    c = GMLP_CHUNK
    return pl.pallas_call(
        functools.partial(_gmlp_kernel, chunk=c, nchunk=rows // c),
        grid=(g, t // rows),
        in_specs=[pl.BlockSpec((rows, dim), lambda gg, i: (i, 4 * g + gg)),
                  pl.BlockSpec((rows, dim), lambda gg, i: (i, 5 * g + gg)),
                  pl.BlockSpec((1, dim), lambda gg, i: (0, gg)),
                  pl.BlockSpec((1, c, c), lambda gg, i: (gg, 0, 0)),
                  pl.BlockSpec((1, c, 1), lambda gg, i: (gg, 0, 0))],
        out_specs=pl.BlockSpec((rows, dim), lambda gg, i: (i, gg)),
        out_shape=jax.ShapeDtypeStruct((t, g * dim), BF16),
        compiler_params=_params("arbitrary", "arbitrary"),
        name="gmlp",
    )(proj, proj, gain.reshape(1, g * dim), ws, bs.reshape(g, c, 1))


def _hgrn_kernel(zq_ref, zf_ref, zi_ref, zg_ref, lb_ref, gain_ref, tri_ref, keep_ref, o_ref, state_ref,
                 *, chunk, nchunk, heads, dk):
    @pl.when(pl.program_id(1) == 0)
    def _():
        state_ref[...] = jnp.zeros_like(state_ref)

    rows = chunk * nchunk
    nt = (((1,), (1,)), ((), ()))
    tri = tri_ref[...]
    keep = keep_ref[...] > 0.0
    chunk_of_row = lax.broadcasted_iota(jnp.int32, (rows, dk), 0) // chunk
    hw = heads * dk
    head_cols = [slice(hd * dk, (hd + 1) * dk) for hd in range(heads)]
    lb = lb_ref[...]
    zf = zf_ref[...].astype(F32)
    f = lb + (1.0 - lb) * jax.nn.sigmoid(zf)
    kk = (1.0 - lb) * jax.nn.sigmoid(-zf)
    log_f = jnp.log(f)
    p0 = log_f.astype(BF16)
    r0 = log_f - p0.astype(F32)
    p1 = r0.astype(BF16)
    p2 = (r0 - p1.astype(F32)).astype(BF16)
    cum = (jnp.dot(tri, p0, preferred_element_type=F32)
           + jnp.dot(tri, p1, preferred_element_type=F32)
           + jnp.dot(tri, p2, preferred_element_type=F32))
    lasts = [cum[(c + 1) * chunk - 1:(c + 1) * chunk, :] for c in range(nchunk)]
    last_rows = jnp.concatenate([jnp.broadcast_to(l, (chunk, hw)) for l in lasts], axis=0)
    decays = [jnp.exp(l) for l in lasts]
    q_dec = (jax.nn.silu(zq_ref[...].astype(F32)) * jnp.exp(cum)).astype(BF16)
    k_dec = (kk * jnp.exp(-cum)).astype(BF16)
    k_out = (kk * jnp.exp(last_rows - cum)).astype(BF16)
    v = zi_ref[...]
    gate = jax.nn.silu(zg_ref[...].astype(F32))
    scores = [lax.dot_general(q_dec[:, cs], k_dec[:, cs], nt, preferred_element_type=F32)
              for cs in head_cols]
    scores = [jnp.where(keep, s, 0.0).astype(BF16) for s in scores]
    o_intra = [jnp.dot(s, v[:, cs], preferred_element_type=F32) for s, cs in zip(scores, head_cols)]
    v_t = [v[:, cs].astype(F32).T.astype(BF16) for cs in head_cols]
    zero = jnp.zeros((rows, dk), BF16)
    k_blocks = [jnp.concatenate([jnp.where(chunk_of_row == c, k_out[:, cs], zero) for c in range(nchunk)],
                                axis=1) for cs in head_cols]
    incr = [jnp.dot(v_t[hd], k_blocks[hd], preferred_element_type=F32) for hd in range(heads)]
    states = [state_ref[hd] for hd in range(heads)]
    inter = [[] for _ in range(heads)]
    for c in range(nchunk):
        sl = slice(c * chunk, (c + 1) * chunk)
        for hd, cs in enumerate(head_cols):
            inter[hd].append(lax.dot_general(q_dec[sl, cs], states[hd].astype(BF16), nt,
                                             preferred_element_type=F32))
            states[hd] = decays[c][:, cs] * states[hd] + incr[hd][:, c * dk:(c + 1) * dk]
    for hd, cs in enumerate(head_cols):
        state_ref[hd] = states[hd]
        o = o_intra[hd] + jnp.concatenate(inter[hd], axis=0)
        o = o * lax.rsqrt(jnp.mean(o * o, axis=-1, keepdims=True) + EPS) * gain_ref[:, cs]
        o_ref[:, cs] = (o * gate[:, cs]).astype(o_ref.dtype)


def _hgrn2(proj, lb, gain, *, rows=256, heads=4):
    t = proj.shape[0]
    width = proj.shape[1] // 4
    dk = HGRN_DK
    h = width // dk
    c = HGRN_CHUNK
    hw = heads * dk
    groups = h // heads
    r = jnp.arange(rows, dtype=jnp.int32)
    tri = ((r[:, None] >= r[None, :]) & ((r[:, None] // c) == (r[None, :] // c))).astype(F32)

    def col(off):
        return pl.BlockSpec((rows, hw), lambda hh, i: (i, off + hh))

    vec = pl.BlockSpec((1, hw), lambda hh, i: (0, hh))
    mask = pl.BlockSpec((rows, rows), lambda hh, i: (0, 0))
    return pl.pallas_call(
        functools.partial(_hgrn_kernel, chunk=c, nchunk=rows // c, heads=heads, dk=dk),
        grid=(groups, t // rows),
        in_specs=[col(0), col(groups), col(2 * groups), col(3 * groups), vec, vec, mask, mask],
        out_specs=pl.BlockSpec((rows, hw), lambda hh, i: (i, hh)),
        out_shape=jax.ShapeDtypeStruct((t, width), BF16),
        scratch_shapes=[pltpu.VMEM((heads, dk, dk), F32)],
        compiler_params=_params("arbitrary", "arbitrary"),
        name="hgrn2",
    )(proj, proj, proj, proj, lb.reshape(1, width), gain.reshape(1, width), tri.astype(BF16), tri)


def _matmul_residual_kernel(*refs, n_act):
    x_ref = refs[0]
    a_refs = refs[1:1 + n_act]
    w_refs = refs[1 + n_act:1 + 2 * n_act]
    o_ref = refs[1 + 2 * n_act]
    acc = x_ref[...]
    for a_ref, w_ref in zip(a_refs, w_refs):
        acc = acc + jnp.dot(a_ref[...], w_ref[...], preferred_element_type=F32)
    o_ref[...] = acc


def _matmul_residual(x, acts, ws, *, tm=256):
    t, d = x.shape
    row = pl.BlockSpec((tm, d), lambda i: (i, 0))
    in_specs = [row]
    in_specs += [pl.BlockSpec((tm, a.shape[1]), lambda i: (i, 0)) for a in acts]
    in_specs += [pl.BlockSpec(w.shape, lambda i: (0, 0)) for w in ws]
    return pl.pallas_call(
        functools.partial(_matmul_residual_kernel, n_act=len(acts)),
        grid=(t // tm,),
        in_specs=in_specs, out_specs=row,
        out_shape=jax.ShapeDtypeStruct((t, d), F32),
        compiler_params=_params("arbitrary"),
        name="matmul_residual",
    )(x, *acts, *ws)


def _norm_router_kernel(x_ref, g_ref, wr_ref, br_ref, h_ref, route_ref, count_ref, run_ref):
    @pl.when(pl.program_id(0) == 0)
    def _():
        run_ref[...] = jnp.zeros_like(run_ref)

    h = _rms(x_ref[...], g_ref[...])
    h_ref[...] = h
    logits = jnp.dot(h, wr_ref[...], preferred_element_type=F32,
                     precision=lax.Precision.HIGHEST) + br_ref[...]
    lane = lax.broadcasted_iota(jnp.int32, logits.shape, 1)
    lane_f = lane.astype(F32)
    neg = -jnp.inf
    big = float(ROUTE_LANES)
    lg = jnp.where(lane < N_GROUPS, logits, neg)
    mg = jnp.max(lg, axis=-1, keepdims=True)
    eg = jnp.exp(lg - mg)
    pg = eg / jnp.sum(eg, axis=-1, keepdims=True)
    p_sel = jnp.max(pg, axis=-1, keepdims=True)
    g_sel = jnp.min(jnp.where(lg == mg, lane_f, big), axis=-1, keepdims=True)
    e_grp = ((lane - N_GROUPS) // EXPERTS_PER_GROUP).astype(F32)
    in_grp = jnp.where(lane >= N_GROUPS, e_grp, -1.0) == g_sel
    le = jnp.where(in_grp, logits, neg)
    t1 = jnp.max(le, axis=-1, keepdims=True)
    i1 = jnp.min(jnp.where(le == t1, lane_f, big), axis=-1, keepdims=True)
    le2 = jnp.where(lane_f == i1, neg, le)
    t2 = jnp.max(le2, axis=-1, keepdims=True)
    i2 = jnp.min(jnp.where(le2 == t2, lane_f, big), axis=-1, keepdims=True)
    e2 = jnp.exp(t2 - t1)
    den = 1.0 + e2
    gate1 = p_sel * (1.0 / den)
    gate2 = p_sel * (e2 / den)
    tm = logits.shape[0]
    hit1 = jnp.where(lane_f == i1, 1.0, 0.0)
    hit2 = jnp.where(lane_f == i2, 1.0, 0.0)
    hits = hit1 + hit2
    r = lax.broadcasted_iota(jnp.int32, (tm, tm), 0)
    s = lax.broadcasted_iota(jnp.int32, (tm, tm), 1)
    before = jnp.where(r > s, 1.0, 0.0).astype(BF16)
    prefix = jnp.dot(before, hits.astype(BF16), preferred_element_type=F32) + run_ref[...]
    rank1 = jnp.sum(prefix * hit1, axis=-1, keepdims=True)
    rank2 = jnp.sum(prefix * hit2, axis=-1, keepdims=True)
    total = run_ref[...] + jnp.sum(hits, axis=0, keepdims=True)
    run_ref[...] = total
    count_ref[...] = total
    vals = (i1 - N_GROUPS, i2 - N_GROUPS, gate1, gate2, rank1, rank2)
    slab = jnp.zeros_like(logits)
    for pos, val in enumerate(vals):
        slab = jnp.where(lane == pos, val, slab)
    route_ref[...] = slab


def _norm_router(x, g, w_rg, b_rg, w_re, b_re, *, tm=256):
    t, d = x.shape
    used = N_GROUPS + N_EXPERTS
    wr = jnp.zeros((d, ROUTE_LANES), F32).at[:, :N_GROUPS].set(w_rg).at[:, N_GROUPS:used].set(w_re)
    br = jnp.zeros((1, ROUTE_LANES), F32).at[0, :N_GROUPS].set(b_rg).at[0, N_GROUPS:used].set(b_re)
    row = pl.BlockSpec((tm, d), lambda i: (i, 0))
    return pl.pallas_call(
        _norm_router_kernel,
        grid=(t // tm,),
        in_specs=[row, pl.BlockSpec((1, d), lambda i: (0, 0)),
                  pl.BlockSpec((d, ROUTE_LANES), lambda i: (0, 0)),
                  pl.BlockSpec((1, ROUTE_LANES), lambda i: (0, 0))],
        out_specs=[row, pl.BlockSpec((tm, ROUTE_LANES), lambda i: (i, 0)),
                   pl.BlockSpec((1, ROUTE_LANES), lambda i: (0, 0))],
        out_shape=[jax.ShapeDtypeStruct((t, d), F32), jax.ShapeDtypeStruct((t, ROUTE_LANES), F32),
                   jax.ShapeDtypeStruct((1, ROUTE_LANES), F32)],
        scratch_shapes=[pltpu.VMEM((1, ROUTE_LANES), F32)],
        compiler_params=_params("arbitrary"),
        name="norm_router",
    )(x, g.reshape(1, d), wr, br)


def _expert_changed(be_ref, b):
    return (b == 0) | (be_ref[b] != be_ref[jnp.maximum(b - 1, 0)])


def _moe_ffn_kernel(be_ref, nu_ref, ne_ref, src0_ref, src1_ref, src2_ref, h_hbm, wg_hbm, wu_hbm, wd_hbm,
                    out_ref, xbuf, stage_g, stage_u, stage_d, wg_bf, wu_bf, wd_bf, xsem, wsem, *, layer):
    b = pl.program_id(0)
    nu = nu_ref[0]
    nslot, blk = xbuf.shape[0], xbuf.shape[1]

    def start_rows(idx_ref, dst_slot):
        for r in range(blk):
            pltpu.make_async_copy(h_hbm.at[pl.ds(idx_ref[0, 0, r], 1)],
                                  xbuf.at[dst_slot, pl.ds(r, 1)], xsem.at[dst_slot]).start(priority=0)

    def wait_rows(dst_slot):
        pltpu.make_async_copy(h_hbm.at[pl.ds(0, blk)], xbuf.at[dst_slot], xsem.at[dst_slot]).wait()

    def weight_copies(e):
        return (pltpu.make_async_copy(wg_hbm.at[layer, e], stage_g, wsem.at[0]),
                pltpu.make_async_copy(wu_hbm.at[layer, e], stage_u, wsem.at[1]),
                pltpu.make_async_copy(wd_hbm.at[layer, e], stage_d, wsem.at[2]))

    @pl.when(b == 0)
    def _():
        for cp in weight_copies(be_ref[0]):
            cp.start(priority=1)
        start_rows(src0_ref, 0)
        start_rows(src1_ref, 1)

    @pl.when((b < nu) & _expert_changed(be_ref, b))
    def _():
        for cp in weight_copies(be_ref[b]):
            cp.wait()
        wg_bf[...] = stage_g[...].astype(BF16)
        wu_bf[...] = stage_u[...].astype(BF16)
        wd_bf[...] = stage_d[...].astype(BF16)

        @pl.when(ne_ref[b] >= 0)
        def _():
            for cp in weight_copies(ne_ref[b]):
                cp.start(priority=1)

    def compute(slot):
        wait_rows(slot)
        x = xbuf[slot].astype(BF16)
        start_rows(src2_ref, (slot + 2) % nslot)
        hg = jnp.dot(x, wg_bf[...], preferred_element_type=F32)
        hu = jnp.dot(x, wu_bf[...], preferred_element_type=F32)
        hid = (jax.nn.silu(hg) * hu).astype(BF16)
        out_ref[...] = jnp.dot(hid, wd_bf[...], preferred_element_type=F32)

        @pl.when(b == nu - 1)
        def _():
            wait_rows((slot + 1) % nslot)
            wait_rows((slot + 2) % nslot)

    for slot in range(nslot):
        pl.when((b < nu) & (b % nslot == slot))(functools.partial(compute, slot))

    @pl.when(b >= nu)
    def _():
        out_ref[...] = jnp.zeros_like(out_ref)


def _combine_kernel(*refs, final):
    if final:
        pos_ref, nxt_ref, x_ref, route_ref, eo_hbm, g_ref, o_ref, ybuf, sem = refs
    else:
        pos_ref, nxt_ref, x_ref, route_ref, eo_hbm, o_ref, ybuf, sem = refs
    i = pl.program_id(0)
    last = pl.num_programs(0) - 1
    tm = x_ref.shape[0]

    def start_rows(idx_ref, dst_slot):
        for r in range(tm):
            for k in range(TOP_K):
                pltpu.make_async_copy(eo_hbm.at[pl.ds(idx_ref[0, 0, TOP_K * r + k], 1)],
                                      ybuf.at[dst_slot, k, pl.ds(r, 1)], sem.at[dst_slot]).start()

    def wait_rows(dst_slot):
        for k in range(TOP_K):
            pltpu.make_async_copy(eo_hbm.at[pl.ds(0, tm)], ybuf.at[dst_slot, k], sem.at[dst_slot]).wait()

    @pl.when(i == 0)
    def _():
        start_rows(pos_ref, 0)

    def step(slot):
        start_rows(nxt_ref, 1 - slot)
        wait_rows(slot)
        route = route_ref[...]
        y = ybuf[slot, 0] * route[:, TOP_K:TOP_K + 1] + ybuf[slot, 1] * route[:, TOP_K + 1:TOP_K + 2]
        x = x_ref[...] + y
        o_ref[...] = _rms(x, g_ref[...]) if final else x

        @pl.when(i == last)
        def _():
            wait_rows(1 - slot)

    for slot in range(2):
        pl.when(i % 2 == slot)(functools.partial(step, slot))


def _moe_dispatch(route, counts_slab, t):
    m = t * TOP_K
    expert = route[:, :TOP_K].astype(jnp.int32)
    rank = route[:, 2 * TOP_K:3 * TOP_K].astype(jnp.int32)
    counts = counts_slab[0, N_GROUPS:N_GROUPS + N_EXPERTS].astype(jnp.int32)
    padded = (counts + MOE_BLOCK - 1) // MOE_BLOCK * MOE_BLOCK
    padded_ends = jnp.cumsum(padded)
    padded_starts = padded_ends - padded
    dest = padded_starts[expert] + rank
    n_blocks = -(-(m + N_EXPERTS * (MOE_BLOCK - 1)) // MOE_BLOCK)
    cap = n_blocks * MOE_BLOCK
    token = jnp.broadcast_to(jnp.arange(t, dtype=jnp.int32)[:, None], (t, TOP_K))
    buf_src = jnp.zeros((cap,), jnp.int32).at[dest.reshape(m)].set(token.reshape(m))
    block_start = jnp.arange(n_blocks, dtype=jnp.int32) * MOE_BLOCK
    block_expert = jnp.minimum(
        jnp.sum((padded_ends[None, :] <= block_start[:, None]).astype(jnp.int32), axis=1), N_EXPERTS - 1)
    n_used = (padded_ends[-1] // MOE_BLOCK).astype(jnp.int32).reshape(1)
    ids = jnp.arange(N_EXPERTS, dtype=jnp.int32)
    later = (ids[None, :] > ids[:, None]) & (counts[None, :] > 0)
    next_active = jnp.min(jnp.where(later, ids[None, :], N_EXPERTS), axis=1)
    next_active = jnp.where(next_active < N_EXPERTS, next_active, -1)
    block_next = next_active[block_expert]
    return block_expert, n_used, block_next, buf_src.reshape(n_blocks, 1, MOE_BLOCK), dest


def _moe_experts(h, block_expert, n_used, block_next, buf_src, w_gate, w_up, w_down, layer):
    t, d = h.shape
    ff = w_gate.shape[-1]
    n_blocks = buf_src.shape[0]
    cap = n_blocks * MOE_BLOCK
    idx_block = (1, 1, MOE_BLOCK)
    hbm = pl.BlockSpec(memory_space=pl.ANY)

    def ahead(k, b, *_):
        return (jnp.minimum(b + k, n_blocks - 1), 0, 0)

    return pl.pallas_call(
        functools.partial(_moe_ffn_kernel, layer=layer),
        grid_spec=pltpu.PrefetchScalarGridSpec(
            num_scalar_prefetch=3,
            grid=(n_blocks,),
            in_specs=[pl.BlockSpec(idx_block, functools.partial(ahead, 0), memory_space=pltpu.SMEM),
                      pl.BlockSpec(idx_block, functools.partial(ahead, 1), memory_space=pltpu.SMEM),
                      pl.BlockSpec(idx_block, functools.partial(ahead, 2), memory_space=pltpu.SMEM),
                      hbm, hbm, hbm, hbm],
            out_specs=pl.BlockSpec((MOE_BLOCK, d), lambda b, *_: (b, 0)),
            scratch_shapes=[pltpu.VMEM((MOE_ROW_SLOTS, MOE_BLOCK, d), F32),
                            pltpu.VMEM((d, ff), F32), pltpu.VMEM((d, ff), F32), pltpu.VMEM((ff, d), F32),
                            pltpu.VMEM((d, ff), BF16), pltpu.VMEM((d, ff), BF16), pltpu.VMEM((ff, d), BF16),
                            pltpu.SemaphoreType.DMA((MOE_ROW_SLOTS,)), pltpu.SemaphoreType.DMA((3,))],
        ),
        out_shape=jax.ShapeDtypeStruct((cap, d), F32),
        compiler_params=pltpu.CompilerParams(dimension_semantics=("arbitrary",),
                                             vmem_limit_bytes=MOE_VMEM_LIMIT),
        name="moe_ffn",
    )(block_expert, n_used, block_next, buf_src, buf_src, buf_src, h, w_gate, w_up, w_down)


def _combine(x, route, dest, expert_out, final_gain, *, tm=256):
    t, d = x.shape
    nt = t // tm
    final = final_gain is not None
    pos = dest.reshape(nt, 1, TOP_K * tm)
    idx_block = (1, 1, TOP_K * tm)
    row = pl.BlockSpec((tm, d), lambda i: (i, 0))
    in_specs = [pl.BlockSpec(idx_block, lambda i: (i, 0, 0), memory_space=pltpu.SMEM),
                pl.BlockSpec(idx_block, lambda i: (jnp.minimum(i + 1, nt - 1), 0, 0),
                             memory_space=pltpu.SMEM),
                row, pl.BlockSpec((tm, ROUTE_LANES), lambda i: (i, 0)),
                pl.BlockSpec(memory_space=pl.ANY)]
    args = [pos, pos, x, route, expert_out]
    if final:
        in_specs.append(pl.BlockSpec((1, d), lambda i: (0, 0)))
        args.append(final_gain.reshape(1, d))
    return pl.pallas_call(
        functools.partial(_combine_kernel, final=final),
        grid=(nt,),
        in_specs=in_specs, out_specs=row,
        out_shape=jax.ShapeDtypeStruct((t, d), F32),
        scratch_shapes=[pltpu.VMEM((2, TOP_K, tm, d), F32), pltpu.SemaphoreType.DMA((2,))],
        compiler_params=_params("arbitrary"),
        name="moe_combine",
    )(*args)


def kernel(x, attn_norm, ffn_norm, final_norm, w_in_ab, ret_norm, gmlp_norm, gmlp_ws, gmlp_bs, w_out_ab, w_in_c, lb_params, hgrn_norm, w_out_c, router_w_group, router_b_group, router_w_expert, router_b_expert, w_gate, w_up, w_down):
    b, s, d = x.shape
    assert b == 1, "the sequence mixers carry state along the flattened token axis"
    depth = attn_norm.shape[0]
    lb_soft = jax.nn.softmax(lb_params.astype(F32), axis=0)
    lower_bounds = jnp.cumsum(lb_soft, axis=0) - lb_soft[0]
    xt = x.reshape(b * s, d)
    t = b * s
    for layer in range(depth):
        i = layer // 2
        if layer % 2 == 0:
            proj = _norm_matmul(xt, attn_norm[layer], w_in_ab[i].astype(BF16))
            ret = _retention(proj, ret_norm[i])
            gm = _gmlp(proj, gmlp_norm[i], gmlp_ws[i], gmlp_bs[i])
            w_out = w_out_ab[i].astype(BF16)
            nr = ret.shape[1]
            xt = _matmul_residual(xt, [ret, gm], [w_out[:nr], w_out[nr:]])
        else:
            proj = _norm_matmul(xt, attn_norm[layer], w_in_c[i].astype(BF16))
            mixed = _hgrn2(proj, lower_bounds[layer], hgrn_norm[i])
            xt = _matmul_residual(xt, [mixed], [w_out_c[i].astype(BF16)])
        h, route, counts = _norm_router(xt, ffn_norm[layer], router_w_group[layer], router_b_group[layer],
                                        router_w_expert[layer], router_b_expert[layer])
        block_expert, n_used, block_next, buf_src, dest = _moe_dispatch(route, counts, t)
        expert_out = _moe_experts(h, block_expert, n_used, block_next, buf_src, w_gate, w_up, w_down, layer)
        xt = _combine(xt, route, dest, expert_out, final_norm if layer == depth - 1 else None)
    return xt.reshape(b, s, d)
```

```python
import functools

import jax
import jax.numpy as jnp
from jax import lax
from jax.experimental import pallas as pl
from jax.experimental.pallas import tpu as pltpu

F32 = jnp.float32
BF16 = jnp.bfloat16
EPS = 1e-6

RET_HEADS = 4
RET_CHUNK = 128
ROPE_BASE = 10000.0
GMLP_GROUPS = 4
GMLP_CHUNK = 128
HGRN_DK = 128
HGRN_CHUNK = 32
N_GROUPS = 4
EXPERTS_PER_GROUP = 8
N_EXPERTS = N_GROUPS * EXPERTS_PER_GROUP
TOP_K = 2
MOE_BLOCK = 128
MOE_ROW_SLOTS = 3
ROUTE_LANES = 128
NORM_SLAB = 256
ROUTER_SLAB = 256

VMEM_LIMIT = 48 * 1024 * 1024
MOE_VMEM_LIMIT = 56 * 1024 * 1024


def _params(*sem):
    return pltpu.CompilerParams(dimension_semantics=sem, vmem_limit_bytes=VMEM_LIMIT)


def _rms(x, g):
    return x * lax.rsqrt(jnp.mean(x * x, axis=-1, keepdims=True) + EPS) * g


def _norm_matmul_kernel(x_ref, g_ref, w_ref, proj_ref, xn_ref):
    @pl.when(pl.program_id(1) == 0)
    def _():
        for r0 in range(0, x_ref.shape[0], NORM_SLAB):
            rows = pl.ds(r0, NORM_SLAB)
            xn_ref[rows, :] = _rms(x_ref[rows, :], g_ref[...]).astype(BF16)

    proj_ref[...] = jnp.dot(xn_ref[...], w_ref[...],
                            preferred_element_type=F32).astype(proj_ref.dtype)


def _norm_matmul(x, g, w, *, tm=1024, tn=1024):
    t, d = x.shape
    n = w.shape[1]
    return pl.pallas_call(
        _norm_matmul_kernel,
        grid=(t // tm, n // tn),
        in_specs=[pl.BlockSpec((tm, d), lambda i, j: (i, 0)),
                  pl.BlockSpec((1, d), lambda i, j: (0, 0)),
                  pl.BlockSpec((d, tn), lambda i, j: (0, j))],
        out_specs=pl.BlockSpec((tm, tn), lambda i, j: (i, j)),
        out_shape=jax.ShapeDtypeStruct((t, n), BF16),
        scratch_shapes=[pltpu.VMEM((tm, d), BF16)],
        compiler_params=_params("arbitrary", "arbitrary"),
        name="norm_matmul",
    )(x, g.reshape(1, d), w)


def _rope(x, cos, sin):
    half = x.shape[-1] // 2
    x1, x2 = x[:, :half], x[:, half:]
    return jnp.concatenate([x1 * cos - x2 * sin, x2 * cos + x1 * sin], axis=-1)


def _retention_kernel(cd_ref, q_ref, k_ref, v_ref, g_ref, cos_ref, sin_ref, dint_ref, qd_ref, kd_ref,
                      gain_ref, o_ref, state_ref, *, chunk, nchunk):
    @pl.when(pl.program_id(1) == 0)
    def _():
        state_ref[...] = jnp.zeros_like(state_ref)

    dk = q_ref.shape[-1]
    dint = dint_ref[0]
    qd = qd_ref[0]
    kd = kd_ref[0]
    cd = cd_ref[pl.program_id(0)]
    gain = gain_ref[...]
    for c in range(nchunk):
        rows = pl.ds(c * chunk, chunk)
        cos = cos_ref[rows, :]
        sin = sin_ref[rows, :]
        q = _rope(q_ref[rows, :].astype(F32), cos, sin)
        k = _rope(k_ref[rows, :].astype(F32), cos, sin) * (dk ** -0.5)
        v = v_ref[rows, :]
        scores = lax.dot_general(q.astype(BF16), k.astype(BF16), (((1,), (1,)), ((), ())),
                                 preferred_element_type=F32) * dint
        state = state_ref[...]
        o = (jnp.dot(scores.astype(BF16), v, preferred_element_type=F32)
             + jnp.dot((q * qd).astype(BF16), state.astype(BF16), preferred_element_type=F32))
        state_ref[...] = cd * state + lax.dot_general(
            (k * kd).astype(BF16), v, (((0,), (0,)), ((), ())), preferred_element_type=F32)
        o = o - jnp.mean(o, axis=-1, keepdims=True)
        o = o * lax.rsqrt(jnp.mean(o * o, axis=-1, keepdims=True) + EPS) * gain
        o_ref[rows, :] = (jax.nn.silu(g_ref[rows, :].astype(F32)) * o).astype(o_ref.dtype)


def _retention(proj, gain, *, rows=512):
    t = proj.shape[0]
    h = RET_HEADS
    dk = proj.shape[1] // 6 // h
    dv = dk
    c = RET_CHUNK
    f32 = F32
    inv = ROPE_BASE ** (-jnp.arange(0, dk, 2, dtype=f32) / dk)
    ang = jnp.arange(t, dtype=f32)[:, None] * inv[None, :]
    cos, sin = jnp.cos(ang), jnp.sin(ang)
    log_gamma = jnp.log(1.0 - jnp.exp2(-5.0 - jnp.arange(h, dtype=f32)))
    idx = jnp.arange(c, dtype=f32)
    diff = idx[:, None] - idx[None, :]
    d_intra = jnp.where(diff >= 0, jnp.exp(jnp.maximum(diff, 0.0) * log_gamma[:, None, None]), 0.0)
    q_decay = jnp.broadcast_to(jnp.exp((idx + 1.0)[None, :] * log_gamma[:, None])[..., None], (h, c, dk))
    k_decay = jnp.broadcast_to(jnp.exp((c - 1.0 - idx)[None, :] * log_gamma[:, None])[..., None], (h, c, dk))
    chunk_decay = jnp.exp(c * log_gamma)

    def col(off):
        return pl.BlockSpec((rows, dk), lambda hh, i: (i, off + hh))

    tab = pl.BlockSpec((rows, dk // 2), lambda hh, i: (i, 0))
    per_head = lambda shp: pl.BlockSpec((1,) + shp, lambda hh, i: (hh, 0, 0))
    return pl.pallas_call(
        functools.partial(_retention_kernel, chunk=c, nchunk=rows // c),
        grid=(h, t // rows),
        in_specs=[pl.BlockSpec(memory_space=pltpu.SMEM),
                  col(0), col(h), col(2 * h), col(3 * h), tab, tab,
                  per_head((c, c)), per_head((c, dk)), per_head((c, dk)),
                  pl.BlockSpec((1, dv), lambda hh, i: (0, hh))],
        out_specs=pl.BlockSpec((rows, dv), lambda hh, i: (i, hh)),
        out_shape=jax.ShapeDtypeStruct((t, h * dv), BF16),
        scratch_shapes=[pltpu.VMEM((dk, dv), F32)],
        compiler_params=_params("arbitrary", "arbitrary"),
        name="retention",
    )(chunk_decay, proj, proj, proj, proj, cos, sin, d_intra, q_decay, k_decay, gain.reshape(1, h * dv))


def _gmlp_kernel(u_ref, vs_ref, gain_ref, ws_ref, bs_ref, o_ref, *, chunk, nchunk):
    r = lax.broadcasted_iota(jnp.int32, (chunk, chunk), 0)
    s = lax.broadcasted_iota(jnp.int32, (chunk, chunk), 1)
    w = jnp.where(r >= s, ws_ref[0], 0.0).astype(BF16)
    gain = gain_ref[...]
    bs = bs_ref[0]
    for c in range(nchunk):
        rows = pl.ds(c * chunk, chunk)
        v = jax.nn.gelu(vs_ref[rows, :].astype(F32))
        v = v - jnp.mean(v, axis=-1, keepdims=True)
        v = v * lax.rsqrt(jnp.mean(v * v, axis=-1, keepdims=True) + EPS) * gain
        mixed = jnp.dot(w, v.astype(BF16), preferred_element_type=F32) + bs
        o_ref[rows, :] = (jax.nn.gelu(u_ref[rows, :].astype(F32)) * mixed).astype(o_ref.dtype)


def _gmlp(proj, gain, ws, bs, *, rows=512):
    t = proj.shape[0]
    g = GMLP_GROUPS
    dim = proj.shape[1] // 6 // g
    c = GMLP_CHUNK
    return pl.pallas_call(
        functools.partial(_gmlp_kernel, chunk=c, nchunk=rows // c),
        grid=(g, t // rows),
        in_specs=[pl.BlockSpec((rows, dim), lambda gg, i: (i, 4 * g + gg)),
                  pl.BlockSpec((rows, dim), lambda gg, i: (i, 5 * g + gg)),
                  pl.BlockSpec((1, dim), lambda gg, i: (0, gg)),
                  pl.BlockSpec((1, c, c), lambda gg, i: (gg, 0, 0)),
                  pl.BlockSpec((1, c, 1), lambda gg, i: (gg, 0, 0))],
        out_specs=pl.BlockSpec((rows, dim), lambda gg, i: (i, gg)),
        out_shape=jax.ShapeDtypeStruct((t, g * dim), BF16),
        compiler_params=_params("arbitrary", "arbitrary"),
        name="gmlp",
    )(proj, proj, gain.reshape(1, g * dim), ws, bs.reshape(g, c, 1))


def _hgrn_kernel(zq_ref, zf_ref, zi_ref, zg_ref, lb_ref, gain_ref, tri_ref, keep_ref, o_ref, state_ref,
                 *, chunk, nchunk, heads, dk):
    @pl.when(pl.program_id(1) == 0)
    def _():
        state_ref[...] = jnp.zeros_like(state_ref)

    rows = chunk * nchunk
    nt = (((1,), (1,)), ((), ()))
    tri = tri_ref[...]
    keep = keep_ref[...] > 0.0
    chunk_of_row = lax.broadcasted_iota(jnp.int32, (rows, dk), 0) // chunk
    hw = heads * dk
    head_cols = [slice(hd * dk, (hd + 1) * dk) for hd in range(heads)]
    lb = lb_ref[...]
    zf = zf_ref[...].astype(F32)
    f = lb + (1.0 - lb) * jax.nn.sigmoid(zf)
    kk = (1.0 - lb) * jax.nn.sigmoid(-zf)
    log_f = jnp.log(f)
    p0 = log_f.astype(BF16)
    r0 = log_f - p0.astype(F32)
    p1 = r0.astype(BF16)
    p2 = (r0 - p1.astype(F32)).astype(BF16)
    cum = (jnp.dot(tri, p0, preferred_element_type=F32)
           + jnp.dot(tri, p1, preferred_element_type=F32)
           + jnp.dot(tri, p2, preferred_element_type=F32))
    lasts = [cum[(c + 1) * chunk - 1:(c + 1) * chunk, :] for c in range(nchunk)]
    last_rows = jnp.concatenate([jnp.broadcast_to(l, (chunk, hw)) for l in lasts], axis=0)
    decays = [jnp.exp(l) for l in lasts]
    q_dec = (jax.nn.silu(zq_ref[...].astype(F32)) * jnp.exp(cum)).astype(BF16)
    k_dec = (kk * jnp.exp(-cum)).astype(BF16)
    k_out = (kk * jnp.exp(last_rows - cum)).astype(BF16)
    v = zi_ref[...]
    gate = jax.nn.silu(zg_ref[...].astype(F32))
    scores = [lax.dot_general(q_dec[:, cs], k_dec[:, cs], nt, preferred_element_type=F32)
              for cs in head_cols]
    scores = [jnp.where(keep, s, 0.0).astype(BF16) for s in scores]
    o_intra = [jnp.dot(s, v[:, cs], preferred_element_type=F32) for s, cs in zip(scores, head_cols)]
    v_t = [v[:, cs].astype(F32).T.astype(BF16) for cs in head_cols]
    zero = jnp.zeros((rows, dk), BF16)
    k_blocks = [jnp.concatenate([jnp.where(chunk_of_row == c, k_out[:, cs], zero) for c in range(nchunk)],
                                axis=1) for cs in head_cols]
    incr = [jnp.dot(v_t[hd], k_blocks[hd], preferred_element_type=F32) for hd in range(heads)]
    states = [state_ref[hd] for hd in range(heads)]
    inter = [[] for _ in range(heads)]
    for c in range(nchunk):
        sl = slice(c * chunk, (c + 1) * chunk)
        for hd, cs in enumerate(head_cols):
            inter[hd].append(lax.dot_general(q_dec[sl, cs], states[hd].astype(BF16), nt,
                                             preferred_element_type=F32))
            states[hd] = decays[c][:, cs] * states[hd] + incr[hd][:, c * dk:(c + 1) * dk]
    for hd, cs in enumerate(head_cols):
        state_ref[hd] = states[hd]
        o = o_intra[hd] + jnp.concatenate(inter[hd], axis=0)
        o = o * lax.rsqrt(jnp.mean(o * o, axis=-1, keepdims=True) + EPS) * gain_ref[:, cs]
        o_ref[:, cs] = (o * gate[:, cs]).astype(o_ref.dtype)


def _hgrn2(proj, lb, gain, *, rows=256, heads=4):
    t = proj.shape[0]
    width = proj.shape[1] // 4
    dk = HGRN_DK
    h = width // dk
    c = HGRN_CHUNK
    hw = heads * dk
    groups = h // heads
    r = jnp.arange(rows, dtype=jnp.int32)
    tri = ((r[:, None] >= r[None, :]) & ((r[:, None] // c) == (r[None, :] // c))).astype(F32)

    def col(off):
        return pl.BlockSpec((rows, hw), lambda hh, i: (i, off + hh))

    vec = pl.BlockSpec((1, hw), lambda hh, i: (0, hh))
    mask = pl.BlockSpec((rows, rows), lambda hh, i: (0, 0))
    return pl.pallas_call(
        functools.partial(_hgrn_kernel, chunk=c, nchunk=rows // c, heads=heads, dk=dk),
        grid=(groups, t // rows),
        in_specs=[col(0), col(groups), col(2 * groups), col(3 * groups), vec, vec, mask, mask],
        out_specs=pl.BlockSpec((rows, hw), lambda hh, i: (i, hh)),
        out_shape=jax.ShapeDtypeStruct((t, width), BF16),
        scratch_shapes=[pltpu.VMEM((heads, dk, dk), F32)],
        compiler_params=_params("arbitrary", "arbitrary"),
        name="hgrn2",
    )(proj, proj, proj, proj, lb.reshape(1, width), gain.reshape(1, width), tri.astype(BF16), tri)


def _pack_bf16_pairs(h):
    half = h.shape[1] // 2
    bits = pltpu.bitcast(h.astype(BF16).astype(F32), jnp.uint32)
    return (bits[:, :half] >> 16) | bits[:, half:]


def _unpack_bf16_pairs(words):
    lo = pltpu.bitcast(words << 16, F32)
    hi = pltpu.bitcast(words & jnp.uint32(0xFFFF0000), F32)
    return jnp.concatenate([lo, hi], axis=1).astype(BF16)


def _route(logits, run):
    lane = lax.broadcasted_iota(jnp.int32, logits.shape, 1)
    lane_f = lane.astype(F32)
    neg = -jnp.inf
    big = float(ROUTE_LANES)
    lg = jnp.where(lane < N_GROUPS, logits, neg)
    mg = jnp.max(lg, axis=-1, keepdims=True)
    eg = jnp.exp(lg - mg)
    pg = eg / jnp.sum(eg, axis=-1, keepdims=True)
    p_sel = jnp.max(pg, axis=-1, keepdims=True)
    g_sel = jnp.min(jnp.where(lg == mg, lane_f, big), axis=-1, keepdims=True)
    e_grp = ((lane - N_GROUPS) // EXPERTS_PER_GROUP).astype(F32)
    in_grp = jnp.where(lane >= N_GROUPS, e_grp, -1.0) == g_sel
    le = jnp.where(in_grp, logits, neg)
    t1 = jnp.max(le, axis=-1, keepdims=True)
    i1 = jnp.min(jnp.where(le == t1, lane_f, big), axis=-1, keepdims=True)
    le2 = jnp.where(lane_f == i1, neg, le)
    t2 = jnp.max(le2, axis=-1, keepdims=True)
    i2 = jnp.min(jnp.where(le2 == t2, lane_f, big), axis=-1, keepdims=True)
    e2 = jnp.exp(t2 - t1)
    den = 1.0 + e2
    gate1 = p_sel * (1.0 / den)
    gate2 = p_sel * (e2 / den)
    tm = logits.shape[0]
    hit1 = jnp.where(lane_f == i1, 1.0, 0.0)
    hit2 = jnp.where(lane_f == i2, 1.0, 0.0)
    hits = hit1 + hit2
    r = lax.broadcasted_iota(jnp.int32, (tm, tm), 0)
    s = lax.broadcasted_iota(jnp.int32, (tm, tm), 1)
    before = jnp.where(r > s, 1.0, 0.0).astype(BF16)
    prefix = jnp.dot(before, hits.astype(BF16), preferred_element_type=F32) + run
    rank1 = jnp.sum(prefix * hit1, axis=-1, keepdims=True)
    rank2 = jnp.sum(prefix * hit2, axis=-1, keepdims=True)
    vals = (i1 - N_GROUPS, i2 - N_GROUPS, gate1, gate2, rank1, rank2)
    slab = jnp.zeros_like(logits)
    for pos, val in enumerate(vals):
        slab = jnp.where(lane == pos, val, slab)
    return slab, run + jnp.sum(hits, axis=0, keepdims=True)


def _outproj_router_kernel(*refs, n_act):
    x_ref = refs[0]
    a_refs = refs[1:1 + n_act]
    w_refs = refs[1 + n_act:1 + 2 * n_act]
    g_ref, wr_ref, br_ref, xo_ref, hp_ref, route_ref, count_ref, run_ref = refs[1 + 2 * n_act:]

    @pl.when(pl.program_id(0) == 0)
    def _():
        run_ref[...] = jnp.zeros_like(run_ref)

    run = run_ref[...]
    for r0 in range(0, x_ref.shape[0], ROUTER_SLAB):
        rows = pl.ds(r0, ROUTER_SLAB)
        x = x_ref[rows, :]
        for a_ref, w_ref in zip(a_refs, w_refs):
            x = x + jnp.dot(a_ref[rows, :], w_ref[...], preferred_element_type=F32)
        xo_ref[rows, :] = x
        h = _rms(x, g_ref[...])
        hp_ref[rows, :] = _pack_bf16_pairs(h)
        h_hi = h.astype(BF16)
        h_lo = (h - h_hi.astype(F32)).astype(BF16)
        p = (jnp.dot(h_hi, wr_ref[...], preferred_element_type=F32)
             + jnp.dot(h_lo, wr_ref[...], preferred_element_type=F32))
        logits = p[:, :ROUTE_LANES] + p[:, ROUTE_LANES:] + br_ref[...]
        route_ref[rows, :], run = _route(logits, run)
    run_ref[...] = run
    count_ref[...] = run


def _outproj_router(x, acts, ws, g, w_rg, b_rg, w_re, b_re, *, tm=256):
    t, d = x.shape
    used = N_GROUPS + N_EXPERTS
    wr = jnp.zeros((d, ROUTE_LANES), F32).at[:, :N_GROUPS].set(w_rg).at[:, N_GROUPS:used].set(w_re)
    wr_hi = wr.astype(BF16)
    wr_lo = (wr - wr_hi.astype(F32)).astype(BF16)
    wr2 = jnp.concatenate([wr_hi, wr_lo], axis=1)
    br = jnp.zeros((1, ROUTE_LANES), F32).at[0, :N_GROUPS].set(b_rg).at[0, N_GROUPS:used].set(b_re)
    row = pl.BlockSpec((tm, d), lambda i: (i, 0))
    const = lambda a: pl.BlockSpec(a.shape, lambda i: (0, 0))
    in_specs = [row]
    in_specs += [pl.BlockSpec((tm, a.shape[1]), lambda i: (i, 0)) for a in acts]
    in_specs += [const(w) for w in ws]
    in_specs += [pl.BlockSpec((1, d), lambda i: (0, 0)), const(wr2), const(br)]
    return pl.pallas_call(
        functools.partial(_outproj_router_kernel, n_act=len(acts)),
        grid=(t // tm,),
        in_specs=in_specs,
        out_specs=[row, pl.BlockSpec((tm, d // 2), lambda i: (i, 0)),
                   pl.BlockSpec((tm, ROUTE_LANES), lambda i: (i, 0)),
                   pl.BlockSpec((1, ROUTE_LANES), lambda i: (0, 0))],
        out_shape=[jax.ShapeDtypeStruct((t, d), F32), jax.ShapeDtypeStruct((t, d // 2), jnp.uint32),
                   jax.ShapeDtypeStruct((t, ROUTE_LANES), F32),
                   jax.ShapeDtypeStruct((1, ROUTE_LANES), F32)],
        scratch_shapes=[pltpu.VMEM((1, ROUTE_LANES), F32)],
        compiler_params=_params("arbitrary"),
        name="outproj_router",
    )(x, *acts, *ws, g.reshape(1, d), wr2, br)


def _expert_changed(be_ref, b):
    return (b == 0) | (be_ref[b] != be_ref[jnp.maximum(b - 1, 0)])


def _moe_ffn_kernel(be_ref, nu_ref, ne_ref, src0_ref, src1_ref, src2_ref, h_hbm, wg_hbm, wu_hbm, wd_hbm,
                    out_ref, xbuf, stage_g, stage_u, stage_d, wg_bf, wu_bf, wd_bf, xsem, wsem, *, layer):
    b = pl.program_id(0)
    nu = nu_ref[0]
    nslot, blk = xbuf.shape[0], xbuf.shape[1]

    def start_rows(idx_ref, dst_slot):
        for r in range(blk):
            pltpu.make_async_copy(h_hbm.at[pl.ds(idx_ref[0, 0, r], 1)],
                                  xbuf.at[dst_slot, pl.ds(r, 1)], xsem.at[dst_slot]).start(priority=0)

    def wait_rows(dst_slot):
        pltpu.make_async_copy(h_hbm.at[pl.ds(0, blk)], xbuf.at[dst_slot], xsem.at[dst_slot]).wait()

    def weight_copies(e):
        return (pltpu.make_async_copy(wg_hbm.at[layer, e], stage_g, wsem.at[0]),
                pltpu.make_async_copy(wu_hbm.at[layer, e], stage_u, wsem.at[1]),
                pltpu.make_async_copy(wd_hbm.at[layer, e], stage_d, wsem.at[2]))

    @pl.when(b == 0)
    def _():
        for cp in weight_copies(be_ref[0]):
            cp.start(priority=1)
        start_rows(src0_ref, 0)
        start_rows(src1_ref, 1)

    @pl.when((b < nu) & _expert_changed(be_ref, b))
    def _():
        for cp in weight_copies(be_ref[b]):
            cp.wait()
        wg_bf[...] = stage_g[...].astype(BF16)
        wu_bf[...] = stage_u[...].astype(BF16)
        wd_bf[...] = stage_d[...].astype(BF16)

        @pl.when(ne_ref[b] >= 0)
        def _():
            for cp in weight_copies(ne_ref[b]):
                cp.start(priority=1)

    def compute(slot):
        wait_rows(slot)
        x = _unpack_bf16_pairs(xbuf[slot])
        start_rows(src2_ref, (slot + 2) % nslot)
        hg = jnp.dot(x, wg_bf[...], preferred_element_type=F32)
        hu = jnp.dot(x, wu_bf[...], preferred_element_type=F32)
        hid = (jax.nn.silu(hg) * hu).astype(BF16)
        out_ref[...] = jnp.dot(hid, wd_bf[...], preferred_element_type=F32)

        @pl.when(b == nu - 1)
        def _():
            wait_rows((slot + 1) % nslot)
            wait_rows((slot + 2) % nslot)

    for slot in range(nslot):
        pl.when((b < nu) & (b % nslot == slot))(functools.partial(compute, slot))

    @pl.when(b >= nu)
    def _():
        out_ref[...] = jnp.zeros_like(out_ref)


def _combine_kernel(*refs, final):
    if final:
        pos_ref, nxt_ref, x_ref, route_ref, eo_hbm, g_ref, o_ref, ybuf, sem = refs
    else:
        pos_ref, nxt_ref, x_ref, route_ref, eo_hbm, o_ref, ybuf, sem = refs
    i = pl.program_id(0)
    last = pl.num_programs(0) - 1
    tm = x_ref.shape[0]

    def start_rows(idx_ref, dst_slot):
        for r in range(tm):
            for k in range(TOP_K):
                pltpu.make_async_copy(eo_hbm.at[pl.ds(idx_ref[0, 0, TOP_K * r + k], 1)],
                                      ybuf.at[dst_slot, k, pl.ds(r, 1)], sem.at[dst_slot]).start()

    def wait_rows(dst_slot):
        for k in range(TOP_K):
            pltpu.make_async_copy(eo_hbm.at[pl.ds(0, tm)], ybuf.at[dst_slot, k], sem.at[dst_slot]).wait()

    @pl.when(i == 0)
    def _():
        start_rows(pos_ref, 0)

    def step(slot):
        start_rows(nxt_ref, 1 - slot)
        wait_rows(slot)
        route = route_ref[...]
        y = ybuf[slot, 0] * route[:, TOP_K:TOP_K + 1] + ybuf[slot, 1] * route[:, TOP_K + 1:TOP_K + 2]
        x = x_ref[...] + y
        o_ref[...] = _rms(x, g_ref[...]) if final else x

        @pl.when(i == last)
        def _():
            wait_rows(1 - slot)

    for slot in range(2):
        pl.when(i % 2 == slot)(functools.partial(step, slot))


def _moe_dispatch(route, counts_slab, t):
    m = t * TOP_K
    expert = route[:, :TOP_K].astype(jnp.int32)
    rank = route[:, 2 * TOP_K:3 * TOP_K].astype(jnp.int32)
    counts = counts_slab[0, N_GROUPS:N_GROUPS + N_EXPERTS].astype(jnp.int32)
    padded = (counts + MOE_BLOCK - 1) // MOE_BLOCK * MOE_BLOCK
    padded_ends = jnp.cumsum(padded)
    padded_starts = padded_ends - padded
    dest = padded_starts[expert] + rank
    n_blocks = -(-(m + N_EXPERTS * (MOE_BLOCK - 1)) // MOE_BLOCK)
    cap = n_blocks * MOE_BLOCK
    token = jnp.broadcast_to(jnp.arange(t, dtype=jnp.int32)[:, None], (t, TOP_K))
    buf_src = jnp.zeros((cap,), jnp.int32).at[dest.reshape(m)].set(token.reshape(m))
    block_start = jnp.arange(n_blocks, dtype=jnp.int32) * MOE_BLOCK
    block_expert = jnp.minimum(
        jnp.sum((padded_ends[None, :] <= block_start[:, None]).astype(jnp.int32), axis=1), N_EXPERTS - 1)
    n_used = (padded_ends[-1] // MOE_BLOCK).astype(jnp.int32).reshape(1)
    ids = jnp.arange(N_EXPERTS, dtype=jnp.int32)
    later = (ids[None, :] > ids[:, None]) & (counts[None, :] > 0)
    next_active = jnp.min(jnp.where(later, ids[None, :], N_EXPERTS), axis=1)
    next_active = jnp.where(next_active < N_EXPERTS, next_active, -1)
    block_next = next_active[block_expert]
    return block_expert, n_used, block_next, buf_src.reshape(n_blocks, 1, MOE_BLOCK), dest


def _moe_experts(h_packed, block_expert, n_used, block_next, buf_src, w_gate, w_up, w_down, layer):
    d, ff = w_gate.shape[-2:]
    n_blocks = buf_src.shape[0]
    cap = n_blocks * MOE_BLOCK
    idx_block = (1, 1, MOE_BLOCK)
    hbm = pl.BlockSpec(memory_space=pl.ANY)

    def ahead(k, b, *_):
        return (jnp.minimum(b + k, n_blocks - 1), 0, 0)

    return pl.pallas_call(
        functools.partial(_moe_ffn_kernel, layer=layer),
        grid_spec=pltpu.PrefetchScalarGridSpec(
            num_scalar_prefetch=3,
            grid=(n_blocks,),
            in_specs=[pl.BlockSpec(idx_block, functools.partial(ahead, 0), memory_space=pltpu.SMEM),
                      pl.BlockSpec(idx_block, functools.partial(ahead, 1), memory_space=pltpu.SMEM),
                      pl.BlockSpec(idx_block, functools.partial(ahead, 2), memory_space=pltpu.SMEM),
                      hbm, hbm, hbm, hbm],
            out_specs=pl.BlockSpec((MOE_BLOCK, d), lambda b, *_: (b, 0)),
            scratch_shapes=[pltpu.VMEM((MOE_ROW_SLOTS, MOE_BLOCK, d // 2), jnp.uint32),
                            pltpu.VMEM((d, ff), F32), pltpu.VMEM((d, ff), F32), pltpu.VMEM((ff, d), F32),
                            pltpu.VMEM((d, ff), BF16), pltpu.VMEM((d, ff), BF16), pltpu.VMEM((ff, d), BF16),
                            pltpu.SemaphoreType.DMA((MOE_ROW_SLOTS,)), pltpu.SemaphoreType.DMA((3,))],
        ),
        out_shape=jax.ShapeDtypeStruct((cap, d), F32),
        compiler_params=pltpu.CompilerParams(dimension_semantics=("arbitrary",),
                                             vmem_limit_bytes=MOE_VMEM_LIMIT),
        name="moe_ffn",
    )(block_expert, n_used, block_next, buf_src, buf_src, buf_src, h_packed, w_gate, w_up, w_down)


def _combine(x, route, dest, expert_out, final_gain, *, tm=256):
    t, d = x.shape
    nt = t // tm
    final = final_gain is not None
    pos = dest.reshape(nt, 1, TOP_K * tm)
    idx_block = (1, 1, TOP_K * tm)
    row = pl.BlockSpec((tm, d), lambda i: (i, 0))
    in_specs = [pl.BlockSpec(idx_block, lambda i: (i, 0, 0), memory_space=pltpu.SMEM),
                pl.BlockSpec(idx_block, lambda i: (jnp.minimum(i + 1, nt - 1), 0, 0),
                             memory_space=pltpu.SMEM),
                row, pl.BlockSpec((tm, ROUTE_LANES), lambda i: (i, 0)),
                pl.BlockSpec(memory_space=pl.ANY)]
    args = [pos, pos, x, route, expert_out]
    if final:
        in_specs.append(pl.BlockSpec((1, d), lambda i: (0, 0)))
        args.append(final_gain.reshape(1, d))
    return pl.pallas_call(
        functools.partial(_combine_kernel, final=final),
        grid=(nt,),
        in_specs=in_specs, out_specs=row,
        out_shape=jax.ShapeDtypeStruct((t, d), F32),
        scratch_shapes=[pltpu.VMEM((2, TOP_K, tm, d), F32), pltpu.SemaphoreType.DMA((2,))],
        compiler_params=_params("arbitrary"),
        name="moe_combine",
    )(*args)


def kernel(x, attn_norm, ffn_norm, final_norm, w_in_ab, ret_norm, gmlp_norm, gmlp_ws, gmlp_bs, w_out_ab, w_in_c, lb_params, hgrn_norm, w_out_c, router_w_group, router_b_group, router_w_expert, router_b_expert, w_gate, w_up, w_down):
    b, s, d = x.shape
    assert b == 1, "the sequence mixers carry state along the flattened token axis"
    depth = attn_norm.shape[0]
    lb_soft = jax.nn.softmax(lb_params.astype(F32), axis=0)
    lower_bounds = jnp.cumsum(lb_soft, axis=0) - lb_soft[0]
    xt = x.reshape(b * s, d)
    t = b * s
    for layer in range(depth):
        i = layer // 2
        if layer % 2 == 0:
            proj = _norm_matmul(xt, attn_norm[layer], w_in_ab[i].astype(BF16))
            ret = _retention(proj, ret_norm[i])
            gm = _gmlp(proj, gmlp_norm[i], gmlp_ws[i], gmlp_bs[i])
            w_out = w_out_ab[i].astype(BF16)
            nr = ret.shape[1]
            acts, w_outs = [ret, gm], [w_out[:nr], w_out[nr:]]
        else:
            proj = _norm_matmul(xt, attn_norm[layer], w_in_c[i].astype(BF16))
            acts = [_hgrn2(proj, lower_bounds[layer], hgrn_norm[i])]
            w_outs = [w_out_c[i].astype(BF16)]
        xt, h_packed, route, counts = _outproj_router(
            xt, acts, w_outs, ffn_norm[layer], router_w_group[layer], router_b_group[layer],
            router_w_expert[layer], router_b_expert[layer])
        block_expert, n_used, block_next, buf_src, dest = _moe_dispatch(route, counts, t)
        expert_out = _moe_experts(h_packed, block_expert, n_used, block_next, buf_src,
                                  w_gate, w_up, w_down, layer)
        xt = _combine(xt, route, dest, expert_out, final_norm if layer == depth - 1 else None)
    return xt.reshape(b, s, d)
```

```python
import functools

import jax
import jax.numpy as jnp
from jax import lax
from jax.experimental import pallas as pl
from jax.experimental.pallas import tpu as pltpu

F32 = jnp.float32
BF16 = jnp.bfloat16
EPS = 1e-6

RET_HEADS = 4
RET_CHUNK = 128
ROPE_BASE = 10000.0
GMLP_GROUPS = 4
GMLP_CHUNK = 128
HGRN_DK = 128
HGRN_CHUNK = 32
N_GROUPS = 4
EXPERTS_PER_GROUP = 8
N_EXPERTS = N_GROUPS * EXPERTS_PER_GROUP
TOP_K = 2
MOE_BLOCK = 128
MOE_ROW_SLOTS = 3
ROUTE_LANES = 128
NORM_SLAB = 256
ROUTER_SLAB = 256
ROUTE_FIELDS = 8
TOKEN_TILE = 256

VMEM_LIMIT = 48 * 1024 * 1024
MOE_VMEM_LIMIT = 56 * 1024 * 1024


def _params(*sem):
    return pltpu.CompilerParams(dimension_semantics=sem, vmem_limit_bytes=VMEM_LIMIT)


def _rms(x, g):
    return x * lax.rsqrt(jnp.mean(x * x, axis=-1, keepdims=True) + EPS) * g


def _norm_matmul_kernel(x_ref, g_ref, w_ref, proj_ref, xn_ref):
    @pl.when(pl.program_id(1) == 0)
    def _():
        for r0 in range(0, x_ref.shape[0], NORM_SLAB):
            rows = pl.ds(r0, NORM_SLAB)
            xn_ref[rows, :] = _rms(x_ref[rows, :], g_ref[...]).astype(BF16)

    proj_ref[...] = jnp.dot(xn_ref[...], w_ref[...],
                            preferred_element_type=F32).astype(proj_ref.dtype)


def _norm_matmul(x, g, w, *, tm=1024, tn=1024):
    t, d = x.shape
    n = w.shape[1]
    return pl.pallas_call(
        _norm_matmul_kernel,
        grid=(t // tm, n // tn),
        in_specs=[pl.BlockSpec((tm, d), lambda i, j: (i, 0)),
                  pl.BlockSpec((1, d), lambda i, j: (0, 0)),
                  pl.BlockSpec((d, tn), lambda i, j: (0, j))],
        out_specs=pl.BlockSpec((tm, tn), lambda i, j: (i, j)),
        out_shape=jax.ShapeDtypeStruct((t, n), BF16),
        scratch_shapes=[pltpu.VMEM((tm, d), BF16)],
        compiler_params=_params("arbitrary", "arbitrary"),
        name="norm_matmul",
    )(x, g.reshape(1, d), w)


def _rope(x, cos, sin):
    half = x.shape[-1] // 2
    x1, x2 = x[:, :half], x[:, half:]
    return jnp.concatenate([x1 * cos - x2 * sin, x2 * cos + x1 * sin], axis=-1)


def _retention_kernel(cd_ref, q_ref, k_ref, v_ref, g_ref, cos_ref, sin_ref, dint_ref, qd_ref, kd_ref,
                      gain_ref, o_ref, state_ref, *, chunk, nchunk):
    @pl.when(pl.program_id(1) == 0)
    def _():
        state_ref[...] = jnp.zeros_like(state_ref)

    dk = q_ref.shape[-1]
    dint = dint_ref[0]
    qd = qd_ref[0]
    kd = kd_ref[0]
    cd = cd_ref[pl.program_id(0)]
    gain = gain_ref[...]
    for c in range(nchunk):
        rows = pl.ds(c * chunk, chunk)
        cos = cos_ref[rows, :]
        sin = sin_ref[rows, :]
        q = _rope(q_ref[rows, :].astype(F32), cos, sin)
        k = _rope(k_ref[rows, :].astype(F32), cos, sin) * (dk ** -0.5)
        v = v_ref[rows, :]
        scores = lax.dot_general(q.astype(BF16), k.astype(BF16), (((1,), (1,)), ((), ())),
                                 preferred_element_type=F32) * dint
        state = state_ref[...]
        o = (jnp.dot(scores.astype(BF16), v, preferred_element_type=F32)
             + jnp.dot((q * qd).astype(BF16), state.astype(BF16), preferred_element_type=F32))
        state_ref[...] = cd * state + lax.dot_general(
            (k * kd).astype(BF16), v, (((0,), (0,)), ((), ())), preferred_element_type=F32)
        o = o - jnp.mean(o, axis=-1, keepdims=True)
        o = o * lax.rsqrt(jnp.mean(o * o, axis=-1, keepdims=True) + EPS) * gain
        o_ref[rows, :] = (jax.nn.silu(g_ref[rows, :].astype(F32)) * o).astype(o_ref.dtype)


def _retention(proj, gain, *, rows=1024):
    t = proj.shape[0]
    h = RET_HEADS
    dk = proj.shape[1] // 6 // h
    dv = dk
    c = RET_CHUNK
    f32 = F32
    inv = ROPE_BASE ** (-jnp.arange(0, dk, 2, dtype=f32) / dk)
    ang = jnp.arange(t, dtype=f32)[:, None] * inv[None, :]
    cos, sin = jnp.cos(ang), jnp.sin(ang)
    log_gamma = jnp.log(1.0 - jnp.exp2(-5.0 - jnp.arange(h, dtype=f32)))
    idx = jnp.arange(c, dtype=f32)
    diff = idx[:, None] - idx[None, :]
    d_intra = jnp.where(diff >= 0, jnp.exp(jnp.maximum(diff, 0.0) * log_gamma[:, None, None]), 0.0)
    q_decay = jnp.broadcast_to(jnp.exp((idx + 1.0)[None, :] * log_gamma[:, None])[..., None], (h, c, dk))
    k_decay = jnp.broadcast_to(jnp.exp((c - 1.0 - idx)[None, :] * log_gamma[:, None])[..., None], (h, c, dk))
    chunk_decay = jnp.exp(c * log_gamma)

    def col(off):
        return pl.BlockSpec((rows, dk), lambda hh, i: (i, off + hh))

    tab = pl.BlockSpec((rows, dk // 2), lambda hh, i: (i, 0))
    per_head = lambda shp: pl.BlockSpec((1,) + shp, lambda hh, i: (hh, 0, 0))
    return pl.pallas_call(
        functools.partial(_retention_kernel, chunk=c, nchunk=rows // c),
        grid=(h, t // rows),
        in_specs=[pl.BlockSpec(memory_space=pltpu.SMEM),
                  col(0), col(h), col(2 * h), col(3 * h), tab, tab,
                  per_head((c, c)), per_head((c, dk)), per_head((c, dk)),
                  pl.BlockSpec((1, dv), lambda hh, i: (0, hh))],
        out_specs=pl.BlockSpec((rows, dv), lambda hh, i: (i, hh)),
        out_shape=jax.ShapeDtypeStruct((t, h * dv), BF16),
        scratch_shapes=[pltpu.VMEM((dk, dv), F32)],
        compiler_params=_params("arbitrary", "arbitrary"),
        name="retention",
    )(chunk_decay, proj, proj, proj, proj, cos, sin, d_intra, q_decay, k_decay, gain.reshape(1, h * dv))


def _gmlp_kernel(u_ref, vs_ref, gain_ref, ws_ref, bs_ref, o_ref, *, chunk, nchunk):
    r = lax.broadcasted_iota(jnp.int32, (chunk, chunk), 0)
    s = lax.broadcasted_iota(jnp.int32, (chunk, chunk), 1)
    w = jnp.where(r >= s, ws_ref[0], 0.0).astype(BF16)
    gain = gain_ref[...]
    bs = bs_ref[0]
    for c in range(nchunk):
        rows = pl.ds(c * chunk, chunk)
        v = jax.nn.gelu(vs_ref[rows, :].astype(F32))
        v = v - jnp.mean(v, axis=-1, keepdims=True)
        v = v * lax.rsqrt(jnp.mean(v * v, axis=-1, keepdims=True) + EPS) * gain
        mixed = jnp.dot(w, v.astype(BF16), preferred_element_type=F32) + bs
        o_ref[rows, :] = (jax.nn.gelu(u_ref[rows, :].astype(F32)) * mixed).astype(o_ref.dtype)


def _gmlp(proj, gain, ws, bs, *, rows=1024):
    t = proj.shape[0]
    g = GMLP_GROUPS
    dim = proj.shape[1] // 6 // g
    c = GMLP_CHUNK
    return pl.pallas_call(
        functools.partial(_gmlp_kernel, chunk=c, nchunk=rows // c),
        grid=(g, t // rows),
        in_specs=[pl.BlockSpec((rows, dim), lambda gg, i: (i, 4 * g + gg)),
                  pl.BlockSpec((rows, dim), lambda gg, i: (i, 5 * g + gg)),
                  pl.BlockSpec((1, dim), lambda gg, i: (0, gg)),
                  pl.BlockSpec((1, c, c), lambda gg, i: (gg, 0, 0)),
                  pl.BlockSpec((1, c, 1), lambda gg, i: (gg, 0, 0))],
        out_specs=pl.BlockSpec((rows, dim), lambda gg, i: (i, gg)),
        out_shape=jax.ShapeDtypeStruct((t, g * dim), BF16),
        compiler_params=_params("arbitrary", "arbitrary"),
        name="gmlp",
    )(proj, proj, gain.reshape(1, g * dim), ws, bs.reshape(g, c, 1))


def _hgrn_kernel(zq_ref, zf_ref, zi_ref, zg_ref, lb_ref, gain_ref, tri_ref, keep_ref, o_ref, state_ref,
                 *, chunk, nchunk, heads, dk):
    @pl.when(pl.program_id(1) == 0)
    def _():
        state_ref[...] = jnp.zeros_like(state_ref)

    rows = chunk * nchunk
    nt = (((1,), (1,)), ((), ()))
    tri = tri_ref[...]
    keep = keep_ref[...] > 0.0
    chunk_of_row = lax.broadcasted_iota(jnp.int32, (rows, dk), 0) // chunk
    hw = heads * dk
    head_cols = [slice(hd * dk, (hd + 1) * dk) for hd in range(heads)]
    lb = lb_ref[...]
    zf = zf_ref[...].astype(F32)
    f = lb + (1.0 - lb) * jax.nn.sigmoid(zf)
    kk = (1.0 - lb) * jax.nn.sigmoid(-zf)
    log_f = jnp.log(f)
    p0 = log_f.astype(BF16)
    r0 = log_f - p0.astype(F32)
    p1 = r0.astype(BF16)
    p2 = (r0 - p1.astype(F32)).astype(BF16)
    cum = (jnp.dot(tri, p0, preferred_element_type=F32)
           + jnp.dot(tri, p1, preferred_element_type=F32)
           + jnp.dot(tri, p2, preferred_element_type=F32))
    lasts = [cum[(c + 1) * chunk - 1:(c + 1) * chunk, :] for c in range(nchunk)]
    last_rows = jnp.concatenate([jnp.broadcast_to(l, (chunk, hw)) for l in lasts], axis=0)
    decays = [jnp.exp(l) for l in lasts]
    q_dec = (jax.nn.silu(zq_ref[...].astype(F32)) * jnp.exp(cum)).astype(BF16)
    k_dec = (kk * jnp.exp(-cum)).astype(BF16)
    k_out = (kk * jnp.exp(last_rows - cum)).astype(BF16)
    v = zi_ref[...]
    gate = jax.nn.silu(zg_ref[...].astype(F32))
    scores = [lax.dot_general(q_dec[:, cs], k_dec[:, cs], nt, preferred_element_type=F32)
              for cs in head_cols]
    scores = [jnp.where(keep, s, 0.0).astype(BF16) for s in scores]
    o_intra = [jnp.dot(s, v[:, cs], preferred_element_type=F32) for s, cs in zip(scores, head_cols)]
    v_t = [v[:, cs].astype(F32).T.astype(BF16) for cs in head_cols]
    zero = jnp.zeros((rows, dk), BF16)
    k_blocks = [jnp.concatenate([jnp.where(chunk_of_row == c, k_out[:, cs], zero) for c in range(nchunk)],
                                axis=1) for cs in head_cols]
    incr = [jnp.dot(v_t[hd], k_blocks[hd], preferred_element_type=F32) for hd in range(heads)]
    states = [state_ref[hd] for hd in range(heads)]
    inter = [[] for _ in range(heads)]
    for c in range(nchunk):
        sl = slice(c * chunk, (c + 1) * chunk)
        for hd, cs in enumerate(head_cols):
            inter[hd].append(lax.dot_general(q_dec[sl, cs], states[hd].astype(BF16), nt,
                                             preferred_element_type=F32))
            states[hd] = decays[c][:, cs] * states[hd] + incr[hd][:, c * dk:(c + 1) * dk]
    for hd, cs in enumerate(head_cols):
        state_ref[hd] = states[hd]
        o = o_intra[hd] + jnp.concatenate(inter[hd], axis=0)
        o = o * lax.rsqrt(jnp.mean(o * o, axis=-1, keepdims=True) + EPS) * gain_ref[:, cs]
        o_ref[:, cs] = (o * gate[:, cs]).astype(o_ref.dtype)


def _hgrn2(proj, lb, gain, *, rows=256, heads=4):
    t = proj.shape[0]
    width = proj.shape[1] // 4
    dk = HGRN_DK
    h = width // dk
    c = HGRN_CHUNK
    hw = heads * dk
    groups = h // heads
    r = jnp.arange(rows, dtype=jnp.int32)
    tri = ((r[:, None] >= r[None, :]) & ((r[:, None] // c) == (r[None, :] // c))).astype(F32)

    def col(off):
        return pl.BlockSpec((rows, hw), lambda hh, i: (i, off + hh))

    vec = pl.BlockSpec((1, hw), lambda hh, i: (0, hh))
    mask = pl.BlockSpec((rows, rows), lambda hh, i: (0, 0))
    return pl.pallas_call(
        functools.partial(_hgrn_kernel, chunk=c, nchunk=rows // c, heads=heads, dk=dk),
        grid=(groups, t // rows),
        in_specs=[col(0), col(groups), col(2 * groups), col(3 * groups), vec, vec, mask, mask],
        out_specs=pl.BlockSpec((rows, hw), lambda hh, i: (i, hh)),
        out_shape=jax.ShapeDtypeStruct((t, width), BF16),
        scratch_shapes=[pltpu.VMEM((heads, dk, dk), F32)],
        compiler_params=_params("arbitrary", "arbitrary"),
        name="hgrn2",
    )(proj, proj, proj, proj, lb.reshape(1, width), gain.reshape(1, width), tri.astype(BF16), tri)


def _pack_bf16_pairs(h):
    half = h.shape[1] // 2
    bits = pltpu.bitcast(h.astype(BF16).astype(F32), jnp.uint32)
    return (bits[:, :half] >> 16) | bits[:, half:]


def _unpack_bf16_pairs(words):
    lo = pltpu.bitcast(words << 16, F32)
    hi = pltpu.bitcast(words & jnp.uint32(0xFFFF0000), F32)
    return jnp.concatenate([lo, hi], axis=1).astype(BF16)


def _route(logits, run):
    lane = lax.broadcasted_iota(jnp.int32, logits.shape, 1)
    lane_f = lane.astype(F32)
    neg = -jnp.inf
    big = float(ROUTE_LANES)
    lg = jnp.where(lane < N_GROUPS, logits, neg)
    mg = jnp.max(lg, axis=-1, keepdims=True)
    eg = jnp.exp(lg - mg)
    pg = eg / jnp.sum(eg, axis=-1, keepdims=True)
    p_sel = jnp.max(pg, axis=-1, keepdims=True)
    g_sel = jnp.min(jnp.where(lg == mg, lane_f, big), axis=-1, keepdims=True)
    e_grp = ((lane - N_GROUPS) // EXPERTS_PER_GROUP).astype(F32)
    in_grp = jnp.where(lane >= N_GROUPS, e_grp, -1.0) == g_sel
    le = jnp.where(in_grp, logits, neg)
    t1 = jnp.max(le, axis=-1, keepdims=True)
    i1 = jnp.min(jnp.where(le == t1, lane_f, big), axis=-1, keepdims=True)
    le2 = jnp.where(lane_f == i1, neg, le)
    t2 = jnp.max(le2, axis=-1, keepdims=True)
    i2 = jnp.min(jnp.where(le2 == t2, lane_f, big), axis=-1, keepdims=True)
    e2 = jnp.exp(t2 - t1)
    den = 1.0 + e2
    gate1 = p_sel * (1.0 / den)
    gate2 = p_sel * (e2 / den)
    tm = logits.shape[0]
    hit1 = jnp.where(lane_f == i1, 1.0, 0.0)
    hit2 = jnp.where(lane_f == i2, 1.0, 0.0)
    hits = hit1 + hit2
    r = lax.broadcasted_iota(jnp.int32, (tm, tm), 0)
    s = lax.broadcasted_iota(jnp.int32, (tm, tm), 1)
    before = jnp.where(r > s, 1.0, 0.0).astype(BF16)
    prefix = jnp.dot(before, hits.astype(BF16), preferred_element_type=F32) + run
    rank1 = jnp.sum(prefix * hit1, axis=-1, keepdims=True)
    rank2 = jnp.sum(prefix * hit2, axis=-1, keepdims=True)
    vals = (i1 - N_GROUPS, i2 - N_GROUPS, gate1, gate2, rank1, rank2)
    slab = jnp.zeros_like(logits)
    for pos, val in enumerate(vals):
        slab = jnp.where(lane == pos, val, slab)
    return slab, run + jnp.sum(hits, axis=0, keepdims=True)


def _outproj_router_kernel(*refs, n_act):
    x_ref = refs[0]
    a_refs = refs[1:1 + n_act]
    w_refs = refs[1 + n_act:1 + 2 * n_act]
    g_ref, wr_ref, br_ref, xo_ref, hp_ref, route_ref, count_ref, run_ref = refs[1 + 2 * n_act:]

    @pl.when(pl.program_id(0) == 0)
    def _():
        run_ref[...] = jnp.zeros_like(run_ref)

    run = run_ref[...]
    for r0 in range(0, x_ref.shape[0], ROUTER_SLAB):
        rows = pl.ds(r0, ROUTER_SLAB)
        x = x_ref[rows, :]
        for a_ref, w_ref in zip(a_refs, w_refs):
            x = x + jnp.dot(a_ref[rows, :], w_ref[...], preferred_element_type=F32)
        xo_ref[rows, :] = x
        h = _rms(x, g_ref[...])
        hp_ref[rows, :] = _pack_bf16_pairs(h)
        h_hi = h.astype(BF16)
        h_lo = (h - h_hi.astype(F32)).astype(BF16)
        p = (jnp.dot(h_hi, wr_ref[...], preferred_element_type=F32)
             + jnp.dot(h_lo, wr_ref[...], preferred_element_type=F32))
        logits = p[:, :ROUTE_LANES] + p[:, ROUTE_LANES:] + br_ref[...]
        slab, run = _route(logits, run)
        route_ref[0, :, r0:r0 + ROUTER_SLAB] = slab.T[:ROUTE_FIELDS, :]
    run_ref[...] = run
    count_ref[...] = run


def _outproj_router(x, acts, ws, g, w_rg, b_rg, w_re, b_re, *, tm=TOKEN_TILE):
    t, d = x.shape
    used = N_GROUPS + N_EXPERTS
    wr = jnp.zeros((d, ROUTE_LANES), F32).at[:, :N_GROUPS].set(w_rg).at[:, N_GROUPS:used].set(w_re)
    wr_hi = wr.astype(BF16)
    wr_lo = (wr - wr_hi.astype(F32)).astype(BF16)
    wr2 = jnp.concatenate([wr_hi, wr_lo], axis=1)
    br = jnp.zeros((1, ROUTE_LANES), F32).at[0, :N_GROUPS].set(b_rg).at[0, N_GROUPS:used].set(b_re)
    row = pl.BlockSpec((tm, d), lambda i: (i, 0))
    const = lambda a: pl.BlockSpec(a.shape, lambda i: (0, 0))
    in_specs = [row]
    in_specs += [pl.BlockSpec((tm, a.shape[1]), lambda i: (i, 0)) for a in acts]
    in_specs += [const(w) for w in ws]
    in_specs += [pl.BlockSpec((1, d), lambda i: (0, 0)), const(wr2), const(br)]
    return pl.pallas_call(
        functools.partial(_outproj_router_kernel, n_act=len(acts)),
        grid=(t // tm,),
        in_specs=in_specs,
        out_specs=[row, pl.BlockSpec((tm, d // 2), lambda i: (i, 0)),
                   pl.BlockSpec((1, ROUTE_FIELDS, tm), lambda i: (i, 0, 0)),
                   pl.BlockSpec((1, ROUTE_LANES), lambda i: (0, 0))],
        out_shape=[jax.ShapeDtypeStruct((t, d), F32), jax.ShapeDtypeStruct((t, d // 2), jnp.uint32),
                   jax.ShapeDtypeStruct((t // tm, ROUTE_FIELDS, tm), F32),
                   jax.ShapeDtypeStruct((1, ROUTE_LANES), F32)],
        scratch_shapes=[pltpu.VMEM((1, ROUTE_LANES), F32)],
        compiler_params=_params("arbitrary"),
        name="outproj_router",
    )(x, *acts, *ws, g.reshape(1, d), wr2, br)


def _expert_changed(be_ref, b):
    return (b == 0) | (be_ref[b] != be_ref[jnp.maximum(b - 1, 0)])


def _moe_ffn_kernel(be_ref, nu_ref, ne_ref, src0_ref, src1_ref, src2_ref, h_hbm, wg_hbm, wu_hbm, wd_hbm,
                    out_ref, xbuf, stage_g, stage_u, stage_d, wg_bf, wu_bf, wd_bf, xsem, wsem, *, layer):
    b = pl.program_id(0)
    nu = nu_ref[0]
    nslot, blk = xbuf.shape[0], xbuf.shape[1]

    def start_rows(idx_ref, dst_slot):
        for r in range(blk):
            pltpu.make_async_copy(h_hbm.at[pl.ds(idx_ref[0, 0, r], 1)],
                                  xbuf.at[dst_slot, pl.ds(r, 1)], xsem.at[dst_slot]).start(priority=0)

    def wait_rows(dst_slot):
        pltpu.make_async_copy(h_hbm.at[pl.ds(0, blk)], xbuf.at[dst_slot], xsem.at[dst_slot]).wait()

    def weight_copies(e):
        return (pltpu.make_async_copy(wg_hbm.at[layer, e], stage_g, wsem.at[0]),
                pltpu.make_async_copy(wu_hbm.at[layer, e], stage_u, wsem.at[1]),
                pltpu.make_async_copy(wd_hbm.at[layer, e], stage_d, wsem.at[2]))

    @pl.when(b == 0)
    def _():
        for cp in weight_copies(be_ref[0]):
            cp.start(priority=1)
        start_rows(src0_ref, 0)
        start_rows(src1_ref, 1)

    @pl.when((b < nu) & _expert_changed(be_ref, b))
    def _():
        for cp in weight_copies(be_ref[b]):
            cp.wait()
        wg_bf[...] = stage_g[...].astype(BF16)
        wu_bf[...] = stage_u[...].astype(BF16)
        wd_bf[...] = stage_d[...].astype(BF16)

        @pl.when(ne_ref[b] >= 0)
        def _():
            for cp in weight_copies(ne_ref[b]):
                cp.start(priority=1)

    def compute(slot):
        wait_rows(slot)
        x = _unpack_bf16_pairs(xbuf[slot])
        start_rows(src2_ref, (slot + 2) % nslot)
        hg = jnp.dot(x, wg_bf[...], preferred_element_type=F32)
        hu = jnp.dot(x, wu_bf[...], preferred_element_type=F32)
        hid = (jax.nn.silu(hg) * hu).astype(BF16)
        out_ref[...] = jnp.dot(hid, wd_bf[...], preferred_element_type=F32)

        @pl.when(b == nu - 1)
        def _():
            wait_rows((slot + 1) % nslot)
            wait_rows((slot + 2) % nslot)

    for slot in range(nslot):
        pl.when((b < nu) & (b % nslot == slot))(functools.partial(compute, slot))

    @pl.when(b >= nu)
    def _():
        out_ref[...] = jnp.zeros_like(out_ref)


def _combine_kernel(*refs, final):
    if final:
        pos_ref, nxt_ref, x_ref, route_ref, eo_hbm, g_ref, o_ref, ybuf, sem = refs
    else:
        pos_ref, nxt_ref, x_ref, route_ref, eo_hbm, o_ref, ybuf, sem = refs
    i = pl.program_id(0)
    last = pl.num_programs(0) - 1
    tm = x_ref.shape[0]

    def start_rows(idx_ref, dst_slot):
        for r in range(tm):
            for k in range(TOP_K):
                pltpu.make_async_copy(eo_hbm.at[pl.ds(idx_ref[0, 0, k * tm + r], 1)],
                                      ybuf.at[dst_slot, k, pl.ds(r, 1)], sem.at[dst_slot]).start()

    def wait_rows(dst_slot):
        for k in range(TOP_K):
            pltpu.make_async_copy(eo_hbm.at[pl.ds(0, tm)], ybuf.at[dst_slot, k], sem.at[dst_slot]).wait()

    @pl.when(i == 0)
    def _():
        start_rows(pos_ref, 0)

    def step(slot):
        start_rows(nxt_ref, 1 - slot)
        wait_rows(slot)
        route = route_ref[0].T
        y = ybuf[slot, 0] * route[:, TOP_K:TOP_K + 1] + ybuf[slot, 1] * route[:, TOP_K + 1:TOP_K + 2]
        x = x_ref[...] + y
        o_ref[...] = _rms(x, g_ref[...]) if final else x

        @pl.when(i == last)
        def _():
            wait_rows(1 - slot)

    for slot in range(2):
        pl.when(i % 2 == slot)(functools.partial(step, slot))


def _moe_dispatch(route, counts_slab, t):
    m = t * TOP_K
    nt, _, tm = route.shape
    expert = route[:, :TOP_K, :].astype(jnp.int32)
    rank = route[:, 2 * TOP_K:3 * TOP_K, :].astype(jnp.int32)
    counts = counts_slab[0, N_GROUPS:N_GROUPS + N_EXPERTS].astype(jnp.int32)
    padded = (counts + MOE_BLOCK - 1) // MOE_BLOCK * MOE_BLOCK
    padded_ends = jnp.cumsum(padded)
    padded_starts = padded_ends - padded
    ids = jnp.arange(N_EXPERTS, dtype=jnp.int32)
    start_of = jnp.sum(jnp.where(expert[..., None] == ids, padded_starts, 0), axis=-1)
    dest = start_of + rank
    n_blocks = -(-(m + N_EXPERTS * (MOE_BLOCK - 1)) // MOE_BLOCK)
    cap = n_blocks * MOE_BLOCK
    token = (jnp.arange(nt, dtype=jnp.int32)[:, None, None] * tm
             + jnp.arange(tm, dtype=jnp.int32)[None, None, :])
    token = jnp.broadcast_to(token, dest.shape)
    buf_src = jnp.zeros((cap,), jnp.int32).at[dest.reshape(m)].set(
        token.reshape(m), unique_indices=True, mode='promise_in_bounds')
    block_start = jnp.arange(n_blocks, dtype=jnp.int32) * MOE_BLOCK
    block_expert = jnp.minimum(
        jnp.sum((padded_ends[None, :] <= block_start[:, None]).astype(jnp.int32), axis=1), N_EXPERTS - 1)
    n_used = (padded_ends[-1] // MOE_BLOCK).astype(jnp.int32).reshape(1)
    ids = jnp.arange(N_EXPERTS, dtype=jnp.int32)
    later = (ids[None, :] > ids[:, None]) & (counts[None, :] > 0)
    next_active = jnp.min(jnp.where(later, ids[None, :], N_EXPERTS), axis=1)
    next_active = jnp.where(next_active < N_EXPERTS, next_active, -1)
    block_next = next_active[block_expert]
    return block_expert, n_used, block_next, buf_src.reshape(n_blocks, 1, MOE_BLOCK), dest


def _moe_experts(h_packed, block_expert, n_used, block_next, buf_src, w_gate, w_up, w_down, layer):
    d, ff = w_gate.shape[-2:]
    n_blocks = buf_src.shape[0]
    cap = n_blocks * MOE_BLOCK
    idx_block = (1, 1, MOE_BLOCK)
    hbm = pl.BlockSpec(memory_space=pl.ANY)

    def ahead(k, b, *_):
        return (jnp.minimum(b + k, n_blocks - 1), 0, 0)

    return pl.pallas_call(
        functools.partial(_moe_ffn_kernel, layer=layer),
        grid_spec=pltpu.PrefetchScalarGridSpec(
            num_scalar_prefetch=3,
            grid=(n_blocks,),
            in_specs=[pl.BlockSpec(idx_block, lambda b, *_: (0, 0, 0), memory_space=pltpu.SMEM),
                      pl.BlockSpec(idx_block, lambda b, *_: (min(1, n_blocks - 1), 0, 0),
                                   memory_space=pltpu.SMEM),
                      pl.BlockSpec(idx_block, functools.partial(ahead, 2), memory_space=pltpu.SMEM),
                      hbm, hbm, hbm, hbm],
            out_specs=pl.BlockSpec((MOE_BLOCK, d), lambda b, *_: (b, 0)),
            scratch_shapes=[pltpu.VMEM((MOE_ROW_SLOTS, MOE_BLOCK, d // 2), jnp.uint32),
                            pltpu.VMEM((d, ff), F32), pltpu.VMEM((d, ff), F32), pltpu.VMEM((ff, d), F32),
                            pltpu.VMEM((d, ff), BF16), pltpu.VMEM((d, ff), BF16), pltpu.VMEM((ff, d), BF16),
                            pltpu.SemaphoreType.DMA((MOE_ROW_SLOTS,)), pltpu.SemaphoreType.DMA((3,))],
        ),
        out_shape=jax.ShapeDtypeStruct((cap, d), F32),
        compiler_params=pltpu.CompilerParams(dimension_semantics=("arbitrary",),
                                             vmem_limit_bytes=MOE_VMEM_LIMIT),
        name="moe_ffn",
    )(block_expert, n_used, block_next, buf_src, buf_src, buf_src, h_packed, w_gate, w_up, w_down)


def _combine(x, route, dest, expert_out, final_gain):
    t, d = x.shape
    nt, _, tm = route.shape
    final = final_gain is not None
    pos = dest.reshape(nt, 1, TOP_K * tm)
    idx_block = (1, 1, TOP_K * tm)
    row = pl.BlockSpec((tm, d), lambda i: (i, 0))
    in_specs = [pl.BlockSpec(idx_block, lambda i: (i, 0, 0), memory_space=pltpu.SMEM),
                pl.BlockSpec(idx_block, lambda i: (jnp.minimum(i + 1, nt - 1), 0, 0),
                             memory_space=pltpu.SMEM),
                row, pl.BlockSpec((1, ROUTE_FIELDS, tm), lambda i: (i, 0, 0)),
                pl.BlockSpec(memory_space=pl.ANY)]
    args = [pos, pos, x, route, expert_out]
    if final:
        in_specs.append(pl.BlockSpec((1, d), lambda i: (0, 0)))
        args.append(final_gain.reshape(1, d))
    return pl.pallas_call(
        functools.partial(_combine_kernel, final=final),
        grid=(nt,),
        in_specs=in_specs, out_specs=row,
        out_shape=jax.ShapeDtypeStruct((t, d), F32),
        scratch_shapes=[pltpu.VMEM((2, TOP_K, tm, d), F32), pltpu.SemaphoreType.DMA((2,))],
        compiler_params=_params("arbitrary"),
        name="moe_combine",
    )(*args)


def kernel(x, attn_norm, ffn_norm, final_norm, w_in_ab, ret_norm, gmlp_norm, gmlp_ws, gmlp_bs, w_out_ab, w_in_c, lb_params, hgrn_norm, w_out_c, router_w_group, router_b_group, router_w_expert, router_b_expert, w_gate, w_up, w_down):
    b, s, d = x.shape
    assert b == 1, "the sequence mixers carry state along the flattened token axis"
    depth = attn_norm.shape[0]
    lb_soft = jax.nn.softmax(lb_params.astype(F32), axis=0)
    lower_bounds = jnp.cumsum(lb_soft, axis=0) - lb_soft[0]
    xt = x.reshape(b * s, d)
    t = b * s
    for layer in range(depth):
        i = layer // 2
        if layer % 2 == 0:
            proj = _norm_matmul(xt, attn_norm[layer], w_in_ab[i].astype(BF16))
            ret = _retention(proj, ret_norm[i])
            gm = _gmlp(proj, gmlp_norm[i], gmlp_ws[i], gmlp_bs[i])
            w_out = w_out_ab[i].astype(BF16)
            nr = ret.shape[1]
            acts, w_outs = [ret, gm], [w_out[:nr], w_out[nr:]]
        else:
            proj = _norm_matmul(xt, attn_norm[layer], w_in_c[i].astype(BF16))
            acts = [_hgrn2(proj, lower_bounds[layer], hgrn_norm[i])]
            w_outs = [w_out_c[i].astype(BF16)]
        xt, h_packed, route, counts = _outproj_router(
            xt, acts, w_outs, ffn_norm[layer], router_w_group[layer], router_b_group[layer],
            router_w_expert[layer], router_b_expert[layer])
        block_expert, n_used, block_next, buf_src, dest = _moe_dispatch(route, counts, t)
        expert_out = _moe_experts(h_packed, block_expert, n_used, block_next, buf_src,
                                  w_gate, w_up, w_down, layer)
        xt = _combine(xt, route, dest, expert_out, final_norm if layer == depth - 1 else None)
    return xt.reshape(b, s, d)
```

```python
import functools

import jax
import jax.numpy as jnp
from jax import lax
from jax.experimental import pallas as pl
from jax.experimental.pallas import tpu as pltpu

F32 = jnp.float32
BF16 = jnp.bfloat16
EPS = 1e-6

RET_HEADS = 4
RET_CHUNK = 128
ROPE_BASE = 10000.0
GMLP_GROUPS = 4
GMLP_CHUNK = 128
HGRN_DK = 128
HGRN_CHUNK = 32
N_GROUPS = 4
EXPERTS_PER_GROUP = 8
N_EXPERTS = N_GROUPS * EXPERTS_PER_GROUP
TOP_K = 2
MOE_BLOCK = 128
MOE_ROW_SLOTS = 3
ROUTE_LANES = 128
NORM_SLAB = 256
ROUTE_FIELDS = 8
TOKEN_TILE = 256

VMEM_LIMIT = 48 * 1024 * 1024
MOE_VMEM_LIMIT = 56 * 1024 * 1024


def _params(*sem):
    return pltpu.CompilerParams(dimension_semantics=sem, vmem_limit_bytes=VMEM_LIMIT)


def _rms(x, g):
    return x * lax.rsqrt(jnp.mean(x * x, axis=-1, keepdims=True) + EPS) * g


def _norm_matmul_kernel(x_ref, g_ref, w_ref, proj_ref, xn_ref):
    @pl.when(pl.program_id(1) == 0)
    def _():
        for r0 in range(0, x_ref.shape[0], NORM_SLAB):
            rows = pl.ds(r0, NORM_SLAB)
            xn_ref[rows, :] = _rms(x_ref[rows, :], g_ref[...]).astype(BF16)

    proj_ref[...] = jnp.dot(xn_ref[...], w_ref[...],
                            preferred_element_type=F32).astype(proj_ref.dtype)


def _norm_matmul(x, g, w, *, tm=1024, tn=1024):
    t, d = x.shape
    n = w.shape[1]
    return pl.pallas_call(
        _norm_matmul_kernel,
        grid=(t // tm, n // tn),
        in_specs=[pl.BlockSpec((tm, d), lambda i, j: (i, 0)),
                  pl.BlockSpec((1, d), lambda i, j: (0, 0)),
                  pl.BlockSpec((d, tn), lambda i, j: (0, j))],
        out_specs=pl.BlockSpec((tm, tn), lambda i, j: (i, j)),
        out_shape=jax.ShapeDtypeStruct((t, n), BF16),
        scratch_shapes=[pltpu.VMEM((tm, d), BF16)],
        compiler_params=_params("arbitrary", "arbitrary"),
        name="norm_matmul",
    )(x, g.reshape(1, d), w)


def _rope(x, cos, sin):
    half = x.shape[-1] // 2
    x1, x2 = x[:, :half], x[:, half:]
    return jnp.concatenate([x1 * cos - x2 * sin, x2 * cos + x1 * sin], axis=-1)


def _retention_kernel(cd_ref, q_ref, k_ref, v_ref, g_ref, cos_ref, sin_ref, dint_ref, qd_ref, kd_ref,
                      gain_ref, o_ref, state_ref, *, chunk, nchunk):
    @pl.when(pl.program_id(1) == 0)
    def _():
        state_ref[...] = jnp.zeros_like(state_ref)

    dk = q_ref.shape[-1]
    dint = dint_ref[0]
    qd = qd_ref[0]
    kd = kd_ref[0]
    cd = cd_ref[pl.program_id(0)]
    gain = gain_ref[...]
    for c in range(nchunk):
        rows = pl.ds(c * chunk, chunk)
        cos = cos_ref[rows, :]
        sin = sin_ref[rows, :]
        q = _rope(q_ref[rows, :].astype(F32), cos, sin)
        k = _rope(k_ref[rows, :].astype(F32), cos, sin) * (dk ** -0.5)
        v = v_ref[rows, :]
        scores = lax.dot_general(q.astype(BF16), k.astype(BF16), (((1,), (1,)), ((), ())),
                                 preferred_element_type=F32) * dint
        state = state_ref[...]
        o = (jnp.dot(scores.astype(BF16), v, preferred_element_type=F32)
             + jnp.dot((q * qd).astype(BF16), state.astype(BF16), preferred_element_type=F32))
        state_ref[...] = cd * state + lax.dot_general(
            (k * kd).astype(BF16), v, (((0,), (0,)), ((), ())), preferred_element_type=F32)
        o = o - jnp.mean(o, axis=-1, keepdims=True)
        o = o * lax.rsqrt(jnp.mean(o * o, axis=-1, keepdims=True) + EPS) * gain
        o_ref[rows, :] = (jax.nn.silu(g_ref[rows, :].astype(F32)) * o).astype(o_ref.dtype)


def _retention(proj, gain, *, rows=1024):
    t = proj.shape[0]
    h = RET_HEADS
    dk = proj.shape[1] // 6 // h
    dv = dk
    c = RET_CHUNK
    f32 = F32
    inv = ROPE_BASE ** (-jnp.arange(0, dk, 2, dtype=f32) / dk)
    ang = jnp.arange(t, dtype=f32)[:, None] * inv[None, :]
    cos, sin = jnp.cos(ang), jnp.sin(ang)
    log_gamma = jnp.log(1.0 - jnp.exp2(-5.0 - jnp.arange(h, dtype=f32)))
    idx = jnp.arange(c, dtype=f32)
    diff = idx[:, None] - idx[None, :]
    d_intra = jnp.where(diff >= 0, jnp.exp(jnp.maximum(diff, 0.0) * log_gamma[:, None, None]), 0.0)
    q_decay = jnp.broadcast_to(jnp.exp((idx + 1.0)[None, :] * log_gamma[:, None])[..., None], (h, c, dk))
    k_decay = jnp.broadcast_to(jnp.exp((c - 1.0 - idx)[None, :] * log_gamma[:, None])[..., None], (h, c, dk))
    chunk_decay = jnp.exp(c * log_gamma)

    def col(off):
        return pl.BlockSpec((rows, dk), lambda hh, i: (i, off + hh))

    tab = pl.BlockSpec((rows, dk // 2), lambda hh, i: (i, 0))
    per_head = lambda shp: pl.BlockSpec((1,) + shp, lambda hh, i: (hh, 0, 0))
    return pl.pallas_call(
        functools.partial(_retention_kernel, chunk=c, nchunk=rows // c),
        grid=(h, t // rows),
        in_specs=[pl.BlockSpec(memory_space=pltpu.SMEM),
                  col(0), col(h), col(2 * h), col(3 * h), tab, tab,
                  per_head((c, c)), per_head((c, dk)), per_head((c, dk)),
                  pl.BlockSpec((1, dv), lambda hh, i: (0, hh))],
        out_specs=pl.BlockSpec((rows, dv), lambda hh, i: (i, hh)),
        out_shape=jax.ShapeDtypeStruct((t, h * dv), BF16),
        scratch_shapes=[pltpu.VMEM((dk, dv), F32)],
        compiler_params=_params("arbitrary", "arbitrary"),
        name="retention",
    )(chunk_decay, proj, proj, proj, proj, cos, sin, d_intra, q_decay, k_decay, gain.reshape(1, h * dv))


def _gmlp_kernel(u_ref, vs_ref, gain_ref, ws_ref, bs_ref, o_ref, *, chunk, nchunk):
    r = lax.broadcasted_iota(jnp.int32, (chunk, chunk), 0)
    s = lax.broadcasted_iota(jnp.int32, (chunk, chunk), 1)
    w = jnp.where(r >= s, ws_ref[0], 0.0).astype(BF16)
    gain = gain_ref[...]
    bs = bs_ref[0]
    for c in range(nchunk):
        rows = pl.ds(c * chunk, chunk)
        v = jax.nn.gelu(vs_ref[rows, :].astype(F32))
        v = v - jnp.mean(v, axis=-1, keepdims=True)
        v = v * lax.rsqrt(jnp.mean(v * v, axis=-1, keepdims=True) + EPS) * gain
        mixed = jnp.dot(w, v.astype(BF16), preferred_element_type=F32) + bs
        o_ref[rows, :] = (jax.nn.gelu(u_ref[rows, :].astype(F32)) * mixed).astype(o_ref.dtype)


def _gmlp(proj, gain, ws, bs, *, rows=1024):
    t = proj.shape[0]
    g = GMLP_GROUPS
    dim = proj.shape[1] // 6 // g
    c = GMLP_CHUNK
    return pl.pallas_call(
        functools.partial(_gmlp_kernel, chunk=c, nchunk=rows // c),
        grid=(g, t // rows),
        in_specs=[pl.BlockSpec((rows, dim), lambda gg, i: (i, 4 * g + gg)),
                  pl.BlockSpec((rows, dim), lambda gg, i: (i, 5 * g + gg)),
                  pl.BlockSpec((1, dim), lambda gg, i: (0, gg)),
                  pl.BlockSpec((1, c, c), lambda gg, i: (gg, 0, 0)),
                  pl.BlockSpec((1, c, 1), lambda gg, i: (gg, 0, 0))],
        out_specs=pl.BlockSpec((rows, dim), lambda gg, i: (i, gg)),
        out_shape=jax.ShapeDtypeStruct((t, g * dim), BF16),
        compiler_params=_params("arbitrary", "arbitrary"),
        name="gmlp",
    )(proj, proj, gain.reshape(1, g * dim), ws, bs.reshape(g, c, 1))


def _hgrn_kernel(zq_ref, zf_ref, zi_ref, zg_ref, lb_ref, gain_ref, tri_ref, keep_ref, o_ref, state_ref,
                 *, chunk, nchunk, heads, dk):
    @pl.when(pl.program_id(1) == 0)
    def _():
        state_ref[...] = jnp.zeros_like(state_ref)

    rows = chunk * nchunk
    nt = (((1,), (1,)), ((), ()))
    tri = tri_ref[...]
    keep = keep_ref[...] > 0.0
    chunk_of_row = lax.broadcasted_iota(jnp.int32, (rows, dk), 0) // chunk
    hw = heads * dk
    head_cols = [slice(hd * dk, (hd + 1) * dk) for hd in range(heads)]
    lb = lb_ref[...]
    zf = zf_ref[...].astype(F32)
    f = lb + (1.0 - lb) * jax.nn.sigmoid(zf)
    kk = (1.0 - lb) * jax.nn.sigmoid(-zf)
    log_f = jnp.log(f)
    p0 = log_f.astype(BF16)
    r0 = log_f - p0.astype(F32)
    p1 = r0.astype(BF16)
    p2 = (r0 - p1.astype(F32)).astype(BF16)
    cum = (jnp.dot(tri, p0, preferred_element_type=F32)
           + jnp.dot(tri, p1, preferred_element_type=F32)
           + jnp.dot(tri, p2, preferred_element_type=F32))
    lasts = [cum[(c + 1) * chunk - 1:(c + 1) * chunk, :] for c in range(nchunk)]
    last_rows = jnp.concatenate([jnp.broadcast_to(l, (chunk, hw)) for l in lasts], axis=0)
    decays = [jnp.exp(l) for l in lasts]
    q_dec = (jax.nn.silu(zq_ref[...].astype(F32)) * jnp.exp(cum)).astype(BF16)
    k_dec = (kk * jnp.exp(-cum)).astype(BF16)
    k_out = (kk * jnp.exp(last_rows - cum)).astype(BF16)
    v = zi_ref[...]
    gate = jax.nn.silu(zg_ref[...].astype(F32))
    scores = [lax.dot_general(q_dec[:, cs], k_dec[:, cs], nt, preferred_element_type=F32)
              for cs in head_cols]
    scores = [jnp.where(keep, s, 0.0).astype(BF16) for s in scores]
    o_intra = [jnp.dot(s, v[:, cs], preferred_element_type=F32) for s, cs in zip(scores, head_cols)]
    v_t = [v[:, cs].astype(F32).T.astype(BF16) for cs in head_cols]
    zero = jnp.zeros((rows, dk), BF16)
    k_blocks = [jnp.concatenate([jnp.where(chunk_of_row == c, k_out[:, cs], zero) for c in range(nchunk)],
                                axis=1) for cs in head_cols]
    incr = [jnp.dot(v_t[hd], k_blocks[hd], preferred_element_type=F32) for hd in range(heads)]
    states = [state_ref[hd] for hd in range(heads)]
    inter = [[] for _ in range(heads)]
    for c in range(nchunk):
        sl = slice(c * chunk, (c + 1) * chunk)
        for hd, cs in enumerate(head_cols):
            inter[hd].append(lax.dot_general(q_dec[sl, cs], states[hd].astype(BF16), nt,
                                             preferred_element_type=F32))
            states[hd] = decays[c][:, cs] * states[hd] + incr[hd][:, c * dk:(c + 1) * dk]
    for hd, cs in enumerate(head_cols):
        state_ref[hd] = states[hd]
        o = o_intra[hd] + jnp.concatenate(inter[hd], axis=0)
        o = o * lax.rsqrt(jnp.mean(o * o, axis=-1, keepdims=True) + EPS) * gain_ref[:, cs]
        o_ref[:, cs] = (o * gate[:, cs]).astype(o_ref.dtype)


def _hgrn2(proj, lb, gain, *, rows=256, heads=4):
    t = proj.shape[0]
    width = proj.shape[1] // 4
    dk = HGRN_DK
    h = width // dk
    c = HGRN_CHUNK
    hw = heads * dk
    groups = h // heads
    r = jnp.arange(rows, dtype=jnp.int32)
    tri = ((r[:, None] >= r[None, :]) & ((r[:, None] // c) == (r[None, :] // c))).astype(F32)

    def col(off):
        return pl.BlockSpec((rows, hw), lambda hh, i: (i, off + hh))

    vec = pl.BlockSpec((1, hw), lambda hh, i: (0, hh))
    mask = pl.BlockSpec((rows, rows), lambda hh, i: (0, 0))
    return pl.pallas_call(
        functools.partial(_hgrn_kernel, chunk=c, nchunk=rows // c, heads=heads, dk=dk),
        grid=(groups, t // rows),
        in_specs=[col(0), col(groups), col(2 * groups), col(3 * groups), vec, vec, mask, mask],
        out_specs=pl.BlockSpec((rows, hw), lambda hh, i: (i, hh)),
        out_shape=jax.ShapeDtypeStruct((t, width), BF16),
        scratch_shapes=[pltpu.VMEM((heads, dk, dk), F32)],
        compiler_params=_params("arbitrary", "arbitrary"),
        name="hgrn2",
    )(proj, proj, proj, proj, lb.reshape(1, width), gain.reshape(1, width), tri.astype(BF16), tri)


def _pack_bf16_pairs(h):
    half = h.shape[1] // 2
    bits = pltpu.bitcast(h.astype(BF16).astype(F32), jnp.uint32)
    return (bits[:, :half] >> 16) | bits[:, half:]


def _unpack_bf16_pairs(words, dtype=BF16):
    lo = pltpu.bitcast(words << 16, F32)
    hi = pltpu.bitcast(words & jnp.uint32(0xFFFF0000), F32)
    return jnp.concatenate([lo, hi], axis=1).astype(dtype)


def _route(logits, run):
    lane = lax.broadcasted_iota(jnp.int32, logits.shape, 1)
    lane_f = lane.astype(F32)
    neg = -jnp.inf
    big = float(ROUTE_LANES)
    lg = jnp.where(lane < N_GROUPS, logits, neg)
    mg = jnp.max(lg, axis=-1, keepdims=True)
    eg = jnp.exp(lg - mg)
    pg = eg / jnp.sum(eg, axis=-1, keepdims=True)
    p_sel = jnp.max(pg, axis=-1, keepdims=True)
    g_sel = jnp.min(jnp.where(lg == mg, lane_f, big), axis=-1, keepdims=True)
    e_grp = ((lane - N_GROUPS) // EXPERTS_PER_GROUP).astype(F32)
    in_grp = jnp.where(lane >= N_GROUPS, e_grp, -1.0) == g_sel
    le = jnp.where(in_grp, logits, neg)
    t1 = jnp.max(le, axis=-1, keepdims=True)
    i1 = jnp.min(jnp.where(le == t1, lane_f, big), axis=-1, keepdims=True)
    le2 = jnp.where(lane_f == i1, neg, le)
    t2 = jnp.max(le2, axis=-1, keepdims=True)
    i2 = jnp.min(jnp.where(le2 == t2, lane_f, big), axis=-1, keepdims=True)
    e2 = jnp.exp(t2 - t1)
    den = 1.0 + e2
    gate1 = p_sel * (1.0 / den)
    gate2 = p_sel * (e2 / den)
    tm = logits.shape[0]
    hit1 = jnp.where(lane_f == i1, 1.0, 0.0)
    hit2 = jnp.where(lane_f == i2, 1.0, 0.0)
    hits = hit1 + hit2
    r = lax.broadcasted_iota(jnp.int32, (tm, tm), 0)
    s = lax.broadcasted_iota(jnp.int32, (tm, tm), 1)
    before = jnp.where(r > s, 1.0, 0.0).astype(BF16)
    prefix = jnp.dot(before, hits.astype(BF16), preferred_element_type=F32) + run
    rank1 = jnp.sum(prefix * hit1, axis=-1, keepdims=True)
    rank2 = jnp.sum(prefix * hit2, axis=-1, keepdims=True)
    vals = (i1 - N_GROUPS, i2 - N_GROUPS, gate1, gate2, rank1, rank2)
    slab = jnp.zeros_like(logits)
    for pos, val in enumerate(vals):
        slab = jnp.where(lane == pos, val, slab)
    return slab, run + jnp.sum(hits, axis=0, keepdims=True)


def _outproj_router_kernel(*refs, n_act):
    x_ref = refs[0]
    a_refs = refs[1:1 + n_act]
    w_refs = refs[1 + n_act:1 + 2 * n_act]
    g_ref, wr_ref, br_ref, xo_ref, hp_ref, route_ref, count_ref, run_ref, xs_ref = refs[1 + 2 * n_act:]
    i = pl.program_id(0)

    @pl.when(i == 0)
    def _():
        run_ref[...] = jnp.zeros_like(run_ref)
        xs_ref[1] = jnp.zeros(xs_ref.shape[1:], xs_ref.dtype)

    def step(slot):
        x = x_ref[...]
        for a_ref, w_ref in zip(a_refs, w_refs):
            x = x + jnp.dot(a_ref[...], w_ref[...], preferred_element_type=F32)
        xo_ref[...] = x
        xs_ref[slot] = x

        x_prev = xs_ref[1 - slot]
        h = _rms(x_prev, g_ref[...])
        hp_ref[...] = _pack_bf16_pairs(h)
        h_hi = h.astype(BF16)
        h_lo = (h - h_hi.astype(F32)).astype(BF16)
        p = (jnp.dot(h_hi, wr_ref[...], preferred_element_type=F32)
             + jnp.dot(h_lo, wr_ref[...], preferred_element_type=F32))
        logits = p[:, :ROUTE_LANES] + p[:, ROUTE_LANES:] + br_ref[...]
        run = run_ref[...]
        slab, run_next = _route(logits, run)
        route_ref[0] = slab.T[:ROUTE_FIELDS, :]
        run = jnp.where(i >= 1, run_next, run)
        run_ref[...] = run
        count_ref[...] = run

    for slot in range(2):
        pl.when(i % 2 == slot)(functools.partial(step, slot))


def _outproj_router(x, acts, ws, g, w_rg, b_rg, w_re, b_re, *, tm=TOKEN_TILE):
    t, d = x.shape
    used = N_GROUPS + N_EXPERTS
    wr = jnp.zeros((d, ROUTE_LANES), F32).at[:, :N_GROUPS].set(w_rg).at[:, N_GROUPS:used].set(w_re)
    wr_hi = wr.astype(BF16)
    wr_lo = (wr - wr_hi.astype(F32)).astype(BF16)
    wr2 = jnp.concatenate([wr_hi, wr_lo], axis=1)
    br = jnp.zeros((1, ROUTE_LANES), F32).at[0, :N_GROUPS].set(b_rg).at[0, N_GROUPS:used].set(b_re)
    nt = t // tm
    cur = lambda i: (jnp.minimum(i, nt - 1), 0)
    prev = lambda i: (jnp.maximum(i - 1, 0), 0)
    row = pl.BlockSpec((tm, d), cur)
    const = lambda a: pl.BlockSpec(a.shape, lambda i: (0, 0))
    in_specs = [row]
    in_specs += [pl.BlockSpec((tm, a.shape[1]), cur) for a in acts]
    in_specs += [const(w) for w in ws]
    in_specs += [pl.BlockSpec((1, d), lambda i: (0, 0)), const(wr2), const(br)]
    return pl.pallas_call(
        functools.partial(_outproj_router_kernel, n_act=len(acts)),
        grid=(nt + 1,),
        in_specs=in_specs,
        out_specs=[row, pl.BlockSpec((tm, d // 2), prev),
                   pl.BlockSpec((1, ROUTE_FIELDS, tm), lambda i: prev(i) + (0,)),
                   pl.BlockSpec((1, ROUTE_LANES), lambda i: (0, 0))],
        out_shape=[jax.ShapeDtypeStruct((t, d), F32), jax.ShapeDtypeStruct((t, d // 2), jnp.uint32),
                   jax.ShapeDtypeStruct((nt, ROUTE_FIELDS, tm), F32),
                   jax.ShapeDtypeStruct((1, ROUTE_LANES), F32)],
        scratch_shapes=[pltpu.VMEM((1, ROUTE_LANES), F32), pltpu.VMEM((2, tm, d), F32)],
        compiler_params=_params("arbitrary"),
        name="outproj_router",
    )(x, *acts, *ws, g.reshape(1, d), wr2, br)


def _expert_changed(be_ref, b):
    return (b == 0) | (be_ref[b] != be_ref[jnp.maximum(b - 1, 0)])


def _moe_ffn_kernel(be_ref, nu_ref, ne_ref, src0_ref, src1_ref, src2_ref, h_hbm, wg_hbm, wu_hbm, wd_hbm,
                    out_ref, xbuf, stage_g, stage_u, stage_d, wg_bf, wu_bf, wd_bf, xsem, wsem, *, layer):
    b = pl.program_id(0)
    nu = nu_ref[0]
    nslot, blk = xbuf.shape[0], xbuf.shape[1]

    def start_rows(idx_ref, dst_slot):
        for r in range(blk):
            pltpu.make_async_copy(h_hbm.at[pl.ds(idx_ref[0, 0, r], 1)],
                                  xbuf.at[dst_slot, pl.ds(r, 1)], xsem.at[dst_slot]).start(priority=0)

    def wait_rows(dst_slot):
        pltpu.make_async_copy(h_hbm.at[pl.ds(0, blk)], xbuf.at[dst_slot], xsem.at[dst_slot]).wait()

    def weight_copies(e):
        return (pltpu.make_async_copy(wg_hbm.at[layer, e], stage_g, wsem.at[0]),
                pltpu.make_async_copy(wu_hbm.at[layer, e], stage_u, wsem.at[1]),
                pltpu.make_async_copy(wd_hbm.at[layer, e], stage_d, wsem.at[2]))

    @pl.when(b == 0)
    def _():
        for cp in weight_copies(be_ref[0]):
            cp.start(priority=1)
        start_rows(src0_ref, 0)
        start_rows(src1_ref, 1)

    @pl.when((b < nu) & _expert_changed(be_ref, b))
    def _():
        for cp in weight_copies(be_ref[b]):
            cp.wait()
        wg_bf[...] = stage_g[...].astype(BF16)
        wu_bf[...] = stage_u[...].astype(BF16)
        wd_bf[...] = stage_d[...].astype(BF16)

        @pl.when(ne_ref[b] >= 0)
        def _():
            for cp in weight_copies(ne_ref[b]):
                cp.start(priority=1)

    def compute(slot):
        wait_rows(slot)
        x = _unpack_bf16_pairs(xbuf[slot])
        start_rows(src2_ref, (slot + 2) % nslot)
        hg = jnp.dot(x, wg_bf[...], preferred_element_type=F32)
        hu = jnp.dot(x, wu_bf[...], preferred_element_type=F32)
        hid = (jax.nn.silu(hg) * hu).astype(BF16)
        out_ref[...] = _pack_bf16_pairs(jnp.dot(hid, wd_bf[...], preferred_element_type=F32))

        @pl.when(b == nu - 1)
        def _():
            wait_rows((slot + 1) % nslot)
            wait_rows((slot + 2) % nslot)

    for slot in range(nslot):
        pl.when((b < nu) & (b % nslot == slot))(functools.partial(compute, slot))

    @pl.when(b >= nu)
    def _():
        out_ref[...] = jnp.zeros_like(out_ref)


def _combine_kernel(*refs, final):
    if final:
        pos_ref, nxt_ref, x_ref, route_ref, eo_hbm, g_ref, o_ref, ybuf, sem = refs
    else:
        pos_ref, nxt_ref, x_ref, route_ref, eo_hbm, o_ref, ybuf, sem = refs
    i = pl.program_id(0)
    last = pl.num_programs(0) - 1
    tm = x_ref.shape[0]

    def start_rows(idx_ref, dst_slot):
        for r in range(tm):
            for k in range(TOP_K):
                pltpu.make_async_copy(eo_hbm.at[pl.ds(idx_ref[0, 0, k * tm + r], 1)],
                                      ybuf.at[dst_slot, k, pl.ds(r, 1)], sem.at[dst_slot]).start()

    def wait_rows(dst_slot):
        for k in range(TOP_K):
            pltpu.make_async_copy(eo_hbm.at[pl.ds(0, tm)], ybuf.at[dst_slot, k], sem.at[dst_slot]).wait()

    @pl.when(i == 0)
    def _():
        start_rows(pos_ref, 0)

    def step(slot):
        start_rows(nxt_ref, 1 - slot)
        wait_rows(slot)
        route = route_ref[0].T
        y = (_unpack_bf16_pairs(ybuf[slot, 0], F32) * route[:, TOP_K:TOP_K + 1]
             + _unpack_bf16_pairs(ybuf[slot, 1], F32) * route[:, TOP_K + 1:TOP_K + 2])
        x = x_ref[...] + y
        o_ref[...] = _rms(x, g_ref[...]) if final else x

        @pl.when(i == last)
        def _():
            wait_rows(1 - slot)

    for slot in range(2):
        pl.when(i % 2 == slot)(functools.partial(step, slot))


def _moe_dispatch(route, counts_slab, t):
    m = t * TOP_K
    nt, _, tm = route.shape
    expert = route[:, :TOP_K, :].astype(jnp.int32)
    rank = route[:, 2 * TOP_K:3 * TOP_K, :].astype(jnp.int32)
    counts = counts_slab[0, N_GROUPS:N_GROUPS + N_EXPERTS].astype(jnp.int32)
    padded = (counts + MOE_BLOCK - 1) // MOE_BLOCK * MOE_BLOCK
    padded_ends = jnp.cumsum(padded)
    padded_starts = padded_ends - padded
    ids = jnp.arange(N_EXPERTS, dtype=jnp.int32)
    start_of = jnp.sum(jnp.where(expert[..., None] == ids, padded_starts, 0), axis=-1)
    dest = start_of + rank
    n_blocks = -(-(m + N_EXPERTS * (MOE_BLOCK - 1)) // MOE_BLOCK)
    cap = n_blocks * MOE_BLOCK
    token = (jnp.arange(nt, dtype=jnp.int32)[:, None, None] * tm
             + jnp.arange(tm, dtype=jnp.int32)[None, None, :])
    token = jnp.broadcast_to(token, dest.shape)
    buf_src = jnp.zeros((cap,), jnp.int32).at[dest.reshape(m)].set(
        token.reshape(m), unique_indices=True, mode='promise_in_bounds')
    block_start = jnp.arange(n_blocks, dtype=jnp.int32) * MOE_BLOCK
    block_expert = jnp.minimum(
        jnp.sum((padded_ends[None, :] <= block_start[:, None]).astype(jnp.int32), axis=1), N_EXPERTS - 1)
    n_used = (padded_ends[-1] // MOE_BLOCK).astype(jnp.int32).reshape(1)
    ids = jnp.arange(N_EXPERTS, dtype=jnp.int32)
    later = (ids[None, :] > ids[:, None]) & (counts[None, :] > 0)
    next_active = jnp.min(jnp.where(later, ids[None, :], N_EXPERTS), axis=1)
    next_active = jnp.where(next_active < N_EXPERTS, next_active, -1)
    block_next = next_active[block_expert]
    return block_expert, n_used, block_next, buf_src.reshape(n_blocks, 1, MOE_BLOCK), dest


def _moe_experts(h_packed, block_expert, n_used, block_next, buf_src, w_gate, w_up, w_down, layer):
    d, ff = w_gate.shape[-2:]
    n_blocks = buf_src.shape[0]
    cap = n_blocks * MOE_BLOCK
    idx_block = (1, 1, MOE_BLOCK)
    hbm = pl.BlockSpec(memory_space=pl.ANY)

    def ahead(k, b, *_):
        return (jnp.minimum(b + k, n_blocks - 1), 0, 0)

    return pl.pallas_call(
        functools.partial(_moe_ffn_kernel, layer=layer),
        grid_spec=pltpu.PrefetchScalarGridSpec(
            num_scalar_prefetch=3,
            grid=(n_blocks,),
            in_specs=[pl.BlockSpec(idx_block, lambda b, *_: (0, 0, 0), memory_space=pltpu.SMEM),
                      pl.BlockSpec(idx_block, lambda b, *_: (min(1, n_blocks - 1), 0, 0),
                                   memory_space=pltpu.SMEM),
                      pl.BlockSpec(idx_block, functools.partial(ahead, 2), memory_space=pltpu.SMEM),
                      hbm, hbm, hbm, hbm],
            out_specs=pl.BlockSpec((MOE_BLOCK, d // 2), lambda b, *_: (b, 0)),
            scratch_shapes=[pltpu.VMEM((MOE_ROW_SLOTS, MOE_BLOCK, d // 2), jnp.uint32),
                            pltpu.VMEM((d, ff), F32), pltpu.VMEM((d, ff), F32), pltpu.VMEM((ff, d), F32),
                            pltpu.VMEM((d, ff), BF16), pltpu.VMEM((d, ff), BF16), pltpu.VMEM((ff, d), BF16),
                            pltpu.SemaphoreType.DMA((MOE_ROW_SLOTS,)), pltpu.SemaphoreType.DMA((3,))],
        ),
        out_shape=jax.ShapeDtypeStruct((cap, d // 2), jnp.uint32),
        compiler_params=pltpu.CompilerParams(dimension_semantics=("arbitrary",),
                                             vmem_limit_bytes=MOE_VMEM_LIMIT),
        name="moe_ffn",
    )(block_expert, n_used, block_next, buf_src, buf_src, buf_src, h_packed, w_gate, w_up, w_down)


def _combine(x, route, dest, expert_out, final_gain):
    t, d = x.shape
    nt, _, tm = route.shape
    final = final_gain is not None
    pos = dest.reshape(nt, 1, TOP_K * tm)
    idx_block = (1, 1, TOP_K * tm)
    row = pl.BlockSpec((tm, d), lambda i: (i, 0))
    in_specs = [pl.BlockSpec(idx_block, lambda i: (i, 0, 0), memory_space=pltpu.SMEM),
                pl.BlockSpec(idx_block, lambda i: (jnp.minimum(i + 1, nt - 1), 0, 0),
                             memory_space=pltpu.SMEM),
                row, pl.BlockSpec((1, ROUTE_FIELDS, tm), lambda i: (i, 0, 0)),
                pl.BlockSpec(memory_space=pl.ANY)]
    args = [pos, pos, x, route, expert_out]
    if final:
        in_specs.append(pl.BlockSpec((1, d), lambda i: (0, 0)))
        args.append(final_gain.reshape(1, d))
    return pl.pallas_call(
        functools.partial(_combine_kernel, final=final),
        grid=(nt,),
        in_specs=in_specs, out_specs=row,
        out_shape=jax.ShapeDtypeStruct((t, d), F32),
        scratch_shapes=[pltpu.VMEM((2, TOP_K, tm, d // 2), jnp.uint32), pltpu.SemaphoreType.DMA((2,))],
        compiler_params=_params("arbitrary"),
        name="moe_combine",
    )(*args)


def kernel(x, attn_norm, ffn_norm, final_norm, w_in_ab, ret_norm, gmlp_norm, gmlp_ws, gmlp_bs, w_out_ab, w_in_c, lb_params, hgrn_norm, w_out_c, router_w_group, router_b_group, router_w_expert, router_b_expert, w_gate, w_up, w_down):
    b, s, d = x.shape
    assert b == 1, "the sequence mixers carry state along the flattened token axis"
    depth = attn_norm.shape[0]
    lb_soft = jax.nn.softmax(lb_params.astype(F32), axis=0)
    lower_bounds = jnp.cumsum(lb_soft, axis=0) - lb_soft[0]
    xt = x.reshape(b * s, d)
    t = b * s
    for layer in range(depth):
        i = layer // 2
        if layer % 2 == 0:
            proj = _norm_matmul(xt, attn_norm[layer], w_in_ab[i].astype(BF16))
            ret = _retention(proj, ret_norm[i])
            gm = _gmlp(proj, gmlp_norm[i], gmlp_ws[i], gmlp_bs[i])
            w_out = w_out_ab[i].astype(BF16)
            nr = ret.shape[1]
            acts, w_outs = [ret, gm], [w_out[:nr], w_out[nr:]]
        else:
            proj = _norm_matmul(xt, attn_norm[layer], w_in_c[i].astype(BF16))
            acts = [_hgrn2(proj, lower_bounds[layer], hgrn_norm[i])]
            w_outs = [w_out_c[i].astype(BF16)]
        xt, h_packed, route, counts = _outproj_router(
            xt, acts, w_outs, ffn_norm[layer], router_w_group[layer], router_b_group[layer],
            router_w_expert[layer], router_b_expert[layer])
        block_expert, n_used, block_next, buf_src, dest = _moe_dispatch(route, counts, t)
        expert_out = _moe_experts(h_packed, block_expert, n_used, block_next, buf_src,
                                  w_gate, w_up, w_down, layer)
        xt = _combine(xt, route, dest, expert_out, final_norm if layer == depth - 1 else None)
    return xt.reshape(b, s, d)
```

```python
import functools

import jax
import jax.numpy as jnp
from jax import lax
from jax.experimental import pallas as pl
from jax.experimental.pallas import tpu as pltpu

F32 = jnp.float32
BF16 = jnp.bfloat16
EPS = 1e-6

RET_HEADS = 4
RET_CHUNK = 128
ROPE_BASE = 10000.0
GMLP_GROUPS = 4
GMLP_CHUNK = 128
HGRN_DK = 128
HGRN_CHUNK = 32
N_GROUPS = 4
EXPERTS_PER_GROUP = 8
N_EXPERTS = N_GROUPS * EXPERTS_PER_GROUP
TOP_K = 2
MOE_BLOCK = 128
MOE_ROW_SLOTS = 3
ROUTE_LANES = 128
NORM_SLAB = 256
ROUTE_FIELDS = 8
TOKEN_TILE = 256

VMEM_LIMIT = 48 * 1024 * 1024
MOE_VMEM_LIMIT = 56 * 1024 * 1024


def _params(*sem):
    return pltpu.CompilerParams(dimension_semantics=sem, vmem_limit_bytes=VMEM_LIMIT)


def _rms(x, g):
    return x * lax.rsqrt(jnp.mean(x * x, axis=-1, keepdims=True) + EPS) * g


def _norm_matmul_kernel(x_ref, g_ref, w_ref, proj_ref, xn_ref):
    @pl.when(pl.program_id(1) == 0)
    def _():
        for r0 in range(0, x_ref.shape[0], NORM_SLAB):
            rows = pl.ds(r0, NORM_SLAB)
            xn_ref[rows, :] = _rms(x_ref[rows, :], g_ref[...]).astype(BF16)

    proj_ref[...] = jnp.dot(xn_ref[...], w_ref[...],
                            preferred_element_type=F32).astype(proj_ref.dtype)


def _norm_matmul(x, g, w, *, tm=1024, tn=1024):
    t, d = x.shape
    n = w.shape[1]
    return pl.pallas_call(
        _norm_matmul_kernel,
        grid=(t // tm, n // tn),
        in_specs=[pl.BlockSpec((tm, d), lambda i, j: (i, 0)),
                  pl.BlockSpec((1, d), lambda i, j: (0, 0)),
                  pl.BlockSpec((d, tn), lambda i, j: (0, j))],
        out_specs=pl.BlockSpec((tm, tn), lambda i, j: (i, j)),
        out_shape=jax.ShapeDtypeStruct((t, n), BF16),
        scratch_shapes=[pltpu.VMEM((tm, d), BF16)],
        compiler_params=_params("arbitrary", "arbitrary"),
        name="norm_matmul",
    )(x, g.reshape(1, d), w)


def _rope(x, cos, sin):
    half = x.shape[-1] // 2
    x1, x2 = x[:, :half], x[:, half:]
    return jnp.concatenate([x1 * cos - x2 * sin, x2 * cos + x1 * sin], axis=-1)


def _retention_kernel(cd_ref, q_ref, k_ref, v_ref, g_ref, cos_ref, sin_ref, dint_ref, qd_ref, kd_ref,
                      gain_ref, o_ref, state_ref, *, chunk, nchunk):
    @pl.when(pl.program_id(1) == 0)
    def _():
        state_ref[...] = jnp.zeros_like(state_ref)

    dk = q_ref.shape[-1]
    dint = dint_ref[0]
    qd = qd_ref[0]
    kd = kd_ref[0]
    cd = cd_ref[pl.program_id(0)]
    gain = gain_ref[...]
    for c in range(nchunk):
        rows = pl.ds(c * chunk, chunk)
        cos = cos_ref[rows, :]
        sin = sin_ref[rows, :]
        q = _rope(q_ref[rows, :].astype(F32), cos, sin)
        k = _rope(k_ref[rows, :].astype(F32), cos, sin) * (dk ** -0.5)
        v = v_ref[rows, :]
        scores = lax.dot_general(q.astype(BF16), k.astype(BF16), (((1,), (1,)), ((), ())),
                                 preferred_element_type=F32) * dint
        state = state_ref[...]
        o = (jnp.dot(scores.astype(BF16), v, preferred_element_type=F32)
             + jnp.dot((q * qd).astype(BF16), state.astype(BF16), preferred_element_type=F32))
        state_ref[...] = cd * state + lax.dot_general(
            (k * kd).astype(BF16), v, (((0,), (0,)), ((), ())), preferred_element_type=F32)
        o = o - jnp.mean(o, axis=-1, keepdims=True)
        o = o * lax.rsqrt(jnp.mean(o * o, axis=-1, keepdims=True) + EPS) * gain
        o_ref[rows, :] = (jax.nn.silu(g_ref[rows, :].astype(F32)) * o).astype(o_ref.dtype)


def _retention(proj, gain, *, rows=1024):
    t = proj.shape[0]
    h = RET_HEADS
    dk = proj.shape[1] // 6 // h
    dv = dk
    c = RET_CHUNK
    f32 = F32
    inv = ROPE_BASE ** (-jnp.arange(0, dk, 2, dtype=f32) / dk)
    ang = jnp.arange(t, dtype=f32)[:, None] * inv[None, :]
    cos, sin = jnp.cos(ang), jnp.sin(ang)
    log_gamma = jnp.log(1.0 - jnp.exp2(-5.0 - jnp.arange(h, dtype=f32)))
    idx = jnp.arange(c, dtype=f32)
    diff = idx[:, None] - idx[None, :]
    d_intra = jnp.where(diff >= 0, jnp.exp(jnp.maximum(diff, 0.0) * log_gamma[:, None, None]), 0.0)
    q_decay = jnp.broadcast_to(jnp.exp((idx + 1.0)[None, :] * log_gamma[:, None])[..., None], (h, c, dk))
    k_decay = jnp.broadcast_to(jnp.exp((c - 1.0 - idx)[None, :] * log_gamma[:, None])[..., None], (h, c, dk))
    chunk_decay = jnp.exp(c * log_gamma)

    def col(off):
        return pl.BlockSpec((rows, dk), lambda hh, i: (i, off + hh))

    tab = pl.BlockSpec((rows, dk // 2), lambda hh, i: (i, 0))
    per_head = lambda shp: pl.BlockSpec((1,) + shp, lambda hh, i: (hh, 0, 0))
    return pl.pallas_call(
        functools.partial(_retention_kernel, chunk=c, nchunk=rows // c),
        grid=(h, t // rows),
        in_specs=[pl.BlockSpec(memory_space=pltpu.SMEM),
                  col(0), col(h), col(2 * h), col(3 * h), tab, tab,
                  per_head((c, c)), per_head((c, dk)), per_head((c, dk)),
                  pl.BlockSpec((1, dv), lambda hh, i: (0, hh))],
        out_specs=pl.BlockSpec((rows, dv), lambda hh, i: (i, hh)),
        out_shape=jax.ShapeDtypeStruct((t, h * dv), BF16),
        scratch_shapes=[pltpu.VMEM((dk, dv), F32)],
        compiler_params=_params("arbitrary", "arbitrary"),
        name="retention",
    )(chunk_decay, proj, proj, proj, proj, cos, sin, d_intra, q_decay, k_decay, gain.reshape(1, h * dv))


def _gmlp_kernel(u_ref, vs_ref, gain_ref, ws_ref, bs_ref, o_ref, *, chunk, nchunk):
    r = lax.broadcasted_iota(jnp.int32, (chunk, chunk), 0)
    s = lax.broadcasted_iota(jnp.int32, (chunk, chunk), 1)
    w = jnp.where(r >= s, ws_ref[0], 0.0).astype(BF16)
    gain = gain_ref[...]
    bs = bs_ref[0]
    for c in range(nchunk):
        rows = pl.ds(c * chunk, chunk)
        v = jax.nn.gelu(vs_ref[rows, :].astype(F32))
        v = v - jnp.mean(v, axis=-1, keepdims=True)
        v = v * lax.rsqrt(jnp.mean(v * v, axis=-1, keepdims=True) + EPS) * gain
        mixed = jnp.dot(w, v.astype(BF16), preferred_element_type=F32) + bs
        o_ref[rows, :] = (jax.nn.gelu(u_ref[rows, :].astype(F32)) * mixed).astype(o_ref.dtype)


def _gmlp(proj, gain, ws, bs, *, rows=1024):
    t = proj.shape[0]
    g = GMLP_GROUPS
    dim = proj.shape[1] // 6 // g
    c = GMLP_CHUNK
    return pl.pallas_call(
        functools.partial(_gmlp_kernel, chunk=c, nchunk=rows // c),
        grid=(g, t // rows),
        in_specs=[pl.BlockSpec((rows, dim), lambda gg, i: (i, 4 * g + gg)),
                  pl.BlockSpec((rows, dim), lambda gg, i: (i, 5 * g + gg)),
                  pl.BlockSpec((1, dim), lambda gg, i: (0, gg)),
                  pl.BlockSpec((1, c, c), lambda gg, i: (gg, 0, 0)),
                  pl.BlockSpec((1, c, 1), lambda gg, i: (gg, 0, 0))],
        out_specs=pl.BlockSpec((rows, dim), lambda gg, i: (i, gg)),
        out_shape=jax.ShapeDtypeStruct((t, g * dim), BF16),
        compiler_params=_params("arbitrary", "arbitrary"),
        name="gmlp",
    )(proj, proj, gain.reshape(1, g * dim), ws, bs.reshape(g, c, 1))


def _hgrn_kernel(zq_ref, zf_ref, zi_ref, zg_ref, lb_ref, gain_ref, tri_ref, keep_ref, o_ref, state_ref,
                 *, chunk, nchunk, heads, dk):
    @pl.when(pl.program_id(1) == 0)
    def _():
        state_ref[...] = jnp.zeros_like(state_ref)

    rows = chunk * nchunk
    nt = (((1,), (1,)), ((), ()))
    tri = tri_ref[...]
    keep = keep_ref[...] > 0.0
    chunk_of_row = lax.broadcasted_iota(jnp.int32, (rows, dk), 0) // chunk
    hw = heads * dk
    head_cols = [slice(hd * dk, (hd + 1) * dk) for hd in range(heads)]
    lb = lb_ref[...]
    zf = zf_ref[...].astype(F32)
    f = lb + (1.0 - lb) * jax.nn.sigmoid(zf)
    kk = (1.0 - lb) * jax.nn.sigmoid(-zf)
    log_f = jnp.log(f)
    p0 = log_f.astype(BF16)
    r0 = log_f - p0.astype(F32)
    p1 = r0.astype(BF16)
    p2 = (r0 - p1.astype(F32)).astype(BF16)
    cum = (jnp.dot(tri, p0, preferred_element_type=F32)
           + jnp.dot(tri, p1, preferred_element_type=F32)
           + jnp.dot(tri, p2, preferred_element_type=F32))
    lasts = [cum[(c + 1) * chunk - 1:(c + 1) * chunk, :] for c in range(nchunk)]
    last_rows = jnp.concatenate([jnp.broadcast_to(l, (chunk, hw)) for l in lasts], axis=0)
    decays = [jnp.exp(l) for l in lasts]
    q_dec = (jax.nn.silu(zq_ref[...].astype(F32)) * jnp.exp(cum)).astype(BF16)
    k_dec = (kk * jnp.exp(-cum)).astype(BF16)
    k_out = (kk * jnp.exp(last_rows - cum)).astype(BF16)
    v = zi_ref[...]
    gate = jax.nn.silu(zg_ref[...].astype(F32))
    scores = [lax.dot_general(q_dec[:, cs], k_dec[:, cs], nt, preferred_element_type=F32)
              for cs in head_cols]
    scores = [jnp.where(keep, s, 0.0).astype(BF16) for s in scores]
    o_intra = [jnp.dot(s, v[:, cs], preferred_element_type=F32) for s, cs in zip(scores, head_cols)]
    v_t = [v[:, cs].astype(F32).T.astype(BF16) for cs in head_cols]
    zero = jnp.zeros((rows, dk), BF16)
    k_blocks = [jnp.concatenate([jnp.where(chunk_of_row == c, k_out[:, cs], zero) for c in range(nchunk)],
                                axis=1) for cs in head_cols]
    incr = [jnp.dot(v_t[hd], k_blocks[hd], preferred_element_type=F32) for hd in range(heads)]
    states = [state_ref[hd] for hd in range(heads)]
    inter = [[] for _ in range(heads)]
    for c in range(nchunk):
        sl = slice(c * chunk, (c + 1) * chunk)
        for hd, cs in enumerate(head_cols):
            inter[hd].append(lax.dot_general(q_dec[sl, cs], states[hd].astype(BF16), nt,
                                             preferred_element_type=F32))
            states[hd] = decays[c][:, cs] * states[hd] + incr[hd][:, c * dk:(c + 1) * dk]
    for hd, cs in enumerate(head_cols):
        state_ref[hd] = states[hd]
        o = o_intra[hd] + jnp.concatenate(inter[hd], axis=0)
        o = o * lax.rsqrt(jnp.mean(o * o, axis=-1, keepdims=True) + EPS) * gain_ref[:, cs]
        o_ref[:, cs] = (o * gate[:, cs]).astype(o_ref.dtype)


def _hgrn2(proj, lb, gain, *, rows=256, heads=4):
    t = proj.shape[0]
    width = proj.shape[1] // 4
    dk = HGRN_DK
    h = width // dk
    c = HGRN_CHUNK
    hw = heads * dk
    groups = h // heads
    r = jnp.arange(rows, dtype=jnp.int32)
    tri = ((r[:, None] >= r[None, :]) & ((r[:, None] // c) == (r[None, :] // c))).astype(F32)

    def col(off):
        return pl.BlockSpec((rows, hw), lambda hh, i: (i, off + hh))

    vec = pl.BlockSpec((1, hw), lambda hh, i: (0, hh))
    mask = pl.BlockSpec((rows, rows), lambda hh, i: (0, 0))
    return pl.pallas_call(
        functools.partial(_hgrn_kernel, chunk=c, nchunk=rows // c, heads=heads, dk=dk),
        grid=(groups, t // rows),
        in_specs=[col(0), col(groups), col(2 * groups), col(3 * groups), vec, vec, mask, mask],
        out_specs=pl.BlockSpec((rows, hw), lambda hh, i: (i, hh)),
        out_shape=jax.ShapeDtypeStruct((t, width), BF16),
        scratch_shapes=[pltpu.VMEM((heads, dk, dk), F32)],
        compiler_params=_params("arbitrary", "arbitrary"),
        name="hgrn2",
    )(proj, proj, proj, proj, lb.reshape(1, width), gain.reshape(1, width), tri.astype(BF16), tri)


def _pack_bf16_pairs(h):
    half = h.shape[1] // 2
    bits = pltpu.bitcast(h.astype(BF16).astype(F32), jnp.uint32)
    return (bits[:, :half] >> 16) | bits[:, half:]


def _unpack_bf16_pairs(words):
    lo = pltpu.bitcast(words << 16, F32)
    hi = pltpu.bitcast(words & jnp.uint32(0xFFFF0000), F32)
    return jnp.concatenate([lo, hi], axis=1).astype(BF16)


def _route(logits, run):
    lane = lax.broadcasted_iota(jnp.int32, logits.shape, 1)
    lane_f = lane.astype(F32)
    neg = -jnp.inf
    big = float(ROUTE_LANES)
    lg = jnp.where(lane < N_GROUPS, logits, neg)
    mg = jnp.max(lg, axis=-1, keepdims=True)
    eg = jnp.exp(lg - mg)
    pg = eg / jnp.sum(eg, axis=-1, keepdims=True)
    p_sel = jnp.max(pg, axis=-1, keepdims=True)
    g_sel = jnp.min(jnp.where(lg == mg, lane_f, big), axis=-1, keepdims=True)
    e_grp = ((lane - N_GROUPS) // EXPERTS_PER_GROUP).astype(F32)
    in_grp = jnp.where(lane >= N_GROUPS, e_grp, -1.0) == g_sel
    le = jnp.where(in_grp, logits, neg)
    t1 = jnp.max(le, axis=-1, keepdims=True)
    i1 = jnp.min(jnp.where(le == t1, lane_f, big), axis=-1, keepdims=True)
    le2 = jnp.where(lane_f == i1, neg, le)
    t2 = jnp.max(le2, axis=-1, keepdims=True)
    i2 = jnp.min(jnp.where(le2 == t2, lane_f, big), axis=-1, keepdims=True)
    e2 = jnp.exp(t2 - t1)
    den = 1.0 + e2
    gate1 = p_sel * (1.0 / den)
    gate2 = p_sel * (e2 / den)
    tm = logits.shape[0]
    hit1 = jnp.where(lane_f == i1, 1.0, 0.0)
    hit2 = jnp.where(lane_f == i2, 1.0, 0.0)
    hits = hit1 + hit2
    r = lax.broadcasted_iota(jnp.int32, (tm, tm), 0)
    s = lax.broadcasted_iota(jnp.int32, (tm, tm), 1)
    before = jnp.where(r > s, 1.0, 0.0).astype(BF16)
    prefix = jnp.dot(before, hits.astype(BF16), preferred_element_type=F32) + run
    rank1 = jnp.sum(prefix * hit1, axis=-1, keepdims=True)
    rank2 = jnp.sum(prefix * hit2, axis=-1, keepdims=True)
    vals = (i1 - N_GROUPS, i2 - N_GROUPS, gate1, gate2, rank1, rank2)
    slab = jnp.zeros_like(logits)
    for pos, val in enumerate(vals):
        slab = jnp.where(lane == pos, val, slab)
    return slab, run + jnp.sum(hits, axis=0, keepdims=True)


def _outproj_router_kernel(*refs, n_act):
    x_ref = refs[0]
    a_refs = refs[1:1 + n_act]
    w_refs = refs[1 + n_act:1 + 2 * n_act]
    g_ref, wr_ref, br_ref, xo_ref, hp_ref, route_ref, count_ref, run_ref, xs_ref = refs[1 + 2 * n_act:]
    i = pl.program_id(0)

    @pl.when(i == 0)
    def _():
        run_ref[...] = jnp.zeros_like(run_ref)
        xs_ref[1] = jnp.zeros(xs_ref.shape[1:], xs_ref.dtype)

    def step(slot):
        x = x_ref[...]
        for a_ref, w_ref in zip(a_refs, w_refs):
            x = x + jnp.dot(a_ref[...], w_ref[...], preferred_element_type=F32)
        xo_ref[...] = x
        xs_ref[slot] = x

        x_prev = xs_ref[1 - slot]
        h = _rms(x_prev, g_ref[...])
        hp_ref[...] = _pack_bf16_pairs(h)
        h_hi = h.astype(BF16)
        h_lo = (h - h_hi.astype(F32)).astype(BF16)
        p = (jnp.dot(h_hi, wr_ref[...], preferred_element_type=F32)
             + jnp.dot(h_lo, wr_ref[...], preferred_element_type=F32))
        logits = p[:, :ROUTE_LANES] + p[:, ROUTE_LANES:] + br_ref[...]
        run = run_ref[...]
        slab, run_next = _route(logits, run)
        route_ref[0] = slab.T[:ROUTE_FIELDS, :]
        run = jnp.where(i >= 1, run_next, run)
        run_ref[...] = run
        count_ref[...] = run

    for slot in range(2):
        pl.when(i % 2 == slot)(functools.partial(step, slot))


def _outproj_router(x, acts, ws, g, w_rg, b_rg, w_re, b_re, *, tm=TOKEN_TILE):
    t, d = x.shape
    used = N_GROUPS + N_EXPERTS
    wr = jnp.zeros((d, ROUTE_LANES), F32).at[:, :N_GROUPS].set(w_rg).at[:, N_GROUPS:used].set(w_re)
    wr_hi = wr.astype(BF16)
    wr_lo = (wr - wr_hi.astype(F32)).astype(BF16)
    wr2 = jnp.concatenate([wr_hi, wr_lo], axis=1)
    br = jnp.zeros((1, ROUTE_LANES), F32).at[0, :N_GROUPS].set(b_rg).at[0, N_GROUPS:used].set(b_re)
    nt = t // tm
    cur = lambda i: (jnp.minimum(i, nt - 1), 0)
    prev = lambda i: (jnp.maximum(i - 1, 0), 0)
    row = pl.BlockSpec((tm, d), cur)
    const = lambda a: pl.BlockSpec(a.shape, lambda i: (0, 0))
    in_specs = [row]
    in_specs += [pl.BlockSpec((tm, a.shape[1]), cur) for a in acts]
    in_specs += [const(w) for w in ws]
    in_specs += [pl.BlockSpec((1, d), lambda i: (0, 0)), const(wr2), const(br)]
    return pl.pallas_call(
        functools.partial(_outproj_router_kernel, n_act=len(acts)),
        grid=(nt + 1,),
        in_specs=in_specs,
        out_specs=[row, pl.BlockSpec((tm, d // 2), prev),
                   pl.BlockSpec((1, ROUTE_FIELDS, tm), lambda i: prev(i) + (0,)),
                   pl.BlockSpec((1, ROUTE_LANES), lambda i: (0, 0))],
        out_shape=[jax.ShapeDtypeStruct((t, d), F32), jax.ShapeDtypeStruct((t, d // 2), jnp.uint32),
                   jax.ShapeDtypeStruct((nt, ROUTE_FIELDS, tm), F32),
                   jax.ShapeDtypeStruct((1, ROUTE_LANES), F32)],
        scratch_shapes=[pltpu.VMEM((1, ROUTE_LANES), F32), pltpu.VMEM((2, tm, d), F32)],
        compiler_params=_params("arbitrary"),
        name="outproj_router",
    )(x, *acts, *ws, g.reshape(1, d), wr2, br)


def _expert_changed(be_ref, b):
    return (b == 0) | (be_ref[b] != be_ref[jnp.maximum(b - 1, 0)])


def _moe_ffn_kernel(be_ref, nu_ref, ne_ref, src0_ref, src1_ref, src2_ref, h_hbm, wg_hbm, wu_hbm, wd_hbm,
                    out_ref, xbuf, stage_g, stage_u, stage_d, wg_bf, wu_bf, wd_bf, xsem, wsem, *, layer):
    b = pl.program_id(0)
    nu = nu_ref[0]
    nslot, blk = xbuf.shape[0], xbuf.shape[1]

    def start_rows(idx_ref, dst_slot):
        for r in range(blk):
            pltpu.make_async_copy(h_hbm.at[pl.ds(idx_ref[0, 0, r], 1)],
                                  xbuf.at[dst_slot, pl.ds(r, 1)], xsem.at[dst_slot]).start(priority=0)

    def wait_rows(dst_slot):
        pltpu.make_async_copy(h_hbm.at[pl.ds(0, blk)], xbuf.at[dst_slot], xsem.at[dst_slot]).wait()

    def weight_copies(e):
        return (pltpu.make_async_copy(wg_hbm.at[layer, e], stage_g, wsem.at[0]),
                pltpu.make_async_copy(wu_hbm.at[layer, e], stage_u, wsem.at[1]),
                pltpu.make_async_copy(wd_hbm.at[layer, e], stage_d, wsem.at[2]))

    @pl.when(b == 0)
    def _():
        for cp in weight_copies(be_ref[0]):
            cp.start(priority=1)
        start_rows(src0_ref, 0)
        start_rows(src1_ref, 1)

    @pl.when((b < nu) & _expert_changed(be_ref, b))
    def _():
        for cp in weight_copies(be_ref[b]):
            cp.wait()
        wg_bf[...] = stage_g[...].astype(BF16)
        wu_bf[...] = stage_u[...].astype(BF16)
        wd_bf[...] = stage_d[...].astype(BF16)

        @pl.when(ne_ref[b] >= 0)
        def _():
            for cp in weight_copies(ne_ref[b]):
                cp.start(priority=1)

    def compute(slot):
        wait_rows(slot)
        x = _unpack_bf16_pairs(xbuf[slot])
        start_rows(src2_ref, (slot + 2) % nslot)
        hg = jnp.dot(x, wg_bf[...], preferred_element_type=F32)
        hu = jnp.dot(x, wu_bf[...], preferred_element_type=F32)
        hid = (jax.nn.silu(hg) * hu).astype(BF16)
        out_ref[...] = jnp.dot(hid, wd_bf[...], preferred_element_type=F32)

        @pl.when(b == nu - 1)
        def _():
            wait_rows((slot + 1) % nslot)
            wait_rows((slot + 2) % nslot)

    for slot in range(nslot):
        pl.when((b < nu) & (b % nslot == slot))(functools.partial(compute, slot))

    @pl.when(b >= nu)
    def _():
        out_ref[...] = jnp.zeros_like(out_ref)


def _combine_kernel(*refs, final):
    if final:
        pos_ref, nxt_ref, x_ref, route_ref, eo_hbm, g_ref, o_ref, ybuf, sem = refs
    else:
        pos_ref, nxt_ref, x_ref, route_ref, eo_hbm, o_ref, ybuf, sem = refs
    i = pl.program_id(0)
    last = pl.num_programs(0) - 1
    tm = x_ref.shape[0]

    def start_rows(idx_ref, dst_slot):
        for r in range(tm):
            for k in range(TOP_K):
                pltpu.make_async_copy(eo_hbm.at[pl.ds(idx_ref[0, 0, k * tm + r], 1)],
                                      ybuf.at[dst_slot, k, pl.ds(r, 1)], sem.at[dst_slot]).start()

    def wait_rows(dst_slot):
        for k in range(TOP_K):
            pltpu.make_async_copy(eo_hbm.at[pl.ds(0, tm)], ybuf.at[dst_slot, k], sem.at[dst_slot]).wait()

    @pl.when(i == 0)
    def _():
        start_rows(pos_ref, 0)

    def step(slot):
        start_rows(nxt_ref, 1 - slot)
        wait_rows(slot)
        route = route_ref[0].T
        y = ybuf[slot, 0] * route[:, TOP_K:TOP_K + 1] + ybuf[slot, 1] * route[:, TOP_K + 1:TOP_K + 2]
        x = x_ref[...] + y
        o_ref[...] = _rms(x, g_ref[...]) if final else x

        @pl.when(i == last)
        def _():
            wait_rows(1 - slot)

    for slot in range(2):
        pl.when(i % 2 == slot)(functools.partial(step, slot))


def _moe_dispatch(route, counts_slab, t):
    m = t * TOP_K
    nt, _, tm = route.shape
    expert = route[:, :TOP_K, :].astype(jnp.int32)
    rank = route[:, 2 * TOP_K:3 * TOP_K, :].astype(jnp.int32)
    counts = counts_slab[0, N_GROUPS:N_GROUPS + N_EXPERTS].astype(jnp.int32)
    padded = (counts + MOE_BLOCK - 1) // MOE_BLOCK * MOE_BLOCK
    padded_ends = jnp.cumsum(padded)
    padded_starts = padded_ends - padded
    ids = jnp.arange(N_EXPERTS, dtype=jnp.int32)
    start_of = jnp.sum(jnp.where(expert[..., None] == ids, padded_starts, 0), axis=-1)
    dest = start_of + rank
    n_blocks = -(-(m + N_EXPERTS * (MOE_BLOCK - 1)) // MOE_BLOCK)
    cap = n_blocks * MOE_BLOCK
    token = (jnp.arange(nt, dtype=jnp.int32)[:, None, None] * tm
             + jnp.arange(tm, dtype=jnp.int32)[None, None, :])
    token = jnp.broadcast_to(token, dest.shape)
    assert cap == m + N_EXPERTS * MOE_BLOCK
    pad_i = jnp.arange(MOE_BLOCK, dtype=jnp.int32)[None, :]
    pad_row = jnp.where(pad_i < (padded - counts)[:, None],
                        (padded_starts + counts)[:, None] + pad_i,
                        cap + ids[:, None] * MOE_BLOCK + pad_i)
    rows = jnp.concatenate([dest.reshape(m), pad_row.reshape(-1)])
    toks = jnp.concatenate([token.reshape(m), jnp.zeros((N_EXPERTS * MOE_BLOCK,), jnp.int32)])
    _, buf_src = lax.sort_key_val(rows, toks)
    block_start = jnp.arange(n_blocks, dtype=jnp.int32) * MOE_BLOCK
    block_expert = jnp.minimum(
        jnp.sum((padded_ends[None, :] <= block_start[:, None]).astype(jnp.int32), axis=1), N_EXPERTS - 1)
    n_used = (padded_ends[-1] // MOE_BLOCK).astype(jnp.int32).reshape(1)
    ids = jnp.arange(N_EXPERTS, dtype=jnp.int32)
    later = (ids[None, :] > ids[:, None]) & (counts[None, :] > 0)
    next_active = jnp.min(jnp.where(later, ids[None, :], N_EXPERTS), axis=1)
    next_active = jnp.where(next_active < N_EXPERTS, next_active, -1)
    block_next = next_active[block_expert]
    return block_expert, n_used, block_next, buf_src.reshape(n_blocks, 1, MOE_BLOCK), dest


def _moe_experts(h_packed, block_expert, n_used, block_next, buf_src, w_gate, w_up, w_down, layer):
    d, ff = w_gate.shape[-2:]
    n_blocks = buf_src.shape[0]
    cap = n_blocks * MOE_BLOCK
    idx_block = (1, 1, MOE_BLOCK)
    hbm = pl.BlockSpec(memory_space=pl.ANY)

    def ahead(k, b, *_):
        return (jnp.minimum(b + k, n_blocks - 1), 0, 0)

    return pl.pallas_call(
        functools.partial(_moe_ffn_kernel, layer=layer),
        grid_spec=pltpu.PrefetchScalarGridSpec(
            num_scalar_prefetch=3,
            grid=(n_blocks,),
            in_specs=[pl.BlockSpec(idx_block, lambda b, *_: (0, 0, 0), memory_space=pltpu.SMEM),
                      pl.BlockSpec(idx_block, lambda b, *_: (min(1, n_blocks - 1), 0, 0),
                                   memory_space=pltpu.SMEM),
                      pl.BlockSpec(idx_block, functools.partial(ahead, 2), memory_space=pltpu.SMEM),
                      hbm, hbm, hbm, hbm],
            out_specs=pl.BlockSpec((MOE_BLOCK, d), lambda b, *_: (b, 0)),
            scratch_shapes=[pltpu.VMEM((MOE_ROW_SLOTS, MOE_BLOCK, d // 2), jnp.uint32),
                            pltpu.VMEM((d, ff), F32), pltpu.VMEM((d, ff), F32), pltpu.VMEM((ff, d), F32),
                            pltpu.VMEM((d, ff), BF16), pltpu.VMEM((d, ff), BF16), pltpu.VMEM((ff, d), BF16),
                            pltpu.SemaphoreType.DMA((MOE_ROW_SLOTS,)), pltpu.SemaphoreType.DMA((3,))],
        ),
        out_shape=jax.ShapeDtypeStruct((cap, d), F32),
        compiler_params=pltpu.CompilerParams(dimension_semantics=("arbitrary",),
                                             vmem_limit_bytes=MOE_VMEM_LIMIT),
        name="moe_ffn",
    )(block_expert, n_used, block_next, buf_src, buf_src, buf_src, h_packed, w_gate, w_up, w_down)


def _combine(x, route, dest, expert_out, final_gain):
    t, d = x.shape
    nt, _, tm = route.shape
    final = final_gain is not None
    pos = dest.reshape(nt, 1, TOP_K * tm)
    idx_block = (1, 1, TOP_K * tm)
    row = pl.BlockSpec((tm, d), lambda i: (i, 0))
    in_specs = [pl.BlockSpec(idx_block, lambda i: (i, 0, 0), memory_space=pltpu.SMEM),
                pl.BlockSpec(idx_block, lambda i: (jnp.minimum(i + 1, nt - 1), 0, 0),
                             memory_space=pltpu.SMEM),
                row, pl.BlockSpec((1, ROUTE_FIELDS, tm), lambda i: (i, 0, 0)),
                pl.BlockSpec(memory_space=pl.ANY)]
    args = [pos, pos, x, route, expert_out]
    if final:
        in_specs.append(pl.BlockSpec((1, d), lambda i: (0, 0)))
        args.append(final_gain.reshape(1, d))
    return pl.pallas_call(
        functools.partial(_combine_kernel, final=final),
        grid=(nt,),
        in_specs=in_specs, out_specs=row,
        out_shape=jax.ShapeDtypeStruct((t, d), F32),
        scratch_shapes=[pltpu.VMEM((2, TOP_K, tm, d), F32), pltpu.SemaphoreType.DMA((2,))],
        compiler_params=_params("arbitrary"),
        name="moe_combine",
    )(*args)


def kernel(x, attn_norm, ffn_norm, final_norm, w_in_ab, ret_norm, gmlp_norm, gmlp_ws, gmlp_bs, w_out_ab, w_in_c, lb_params, hgrn_norm, w_out_c, router_w_group, router_b_group, router_w_expert, router_b_expert, w_gate, w_up, w_down):
    b, s, d = x.shape
    assert b == 1, "the sequence mixers carry state along the flattened token axis"
    depth = attn_norm.shape[0]
    lb_soft = jax.nn.softmax(lb_params.astype(F32), axis=0)
    lower_bounds = jnp.cumsum(lb_soft, axis=0) - lb_soft[0]
    xt = x.reshape(b * s, d)
    t = b * s
    for layer in range(depth):
        i = layer // 2
        if layer % 2 == 0:
            proj = _norm_matmul(xt, attn_norm[layer], w_in_ab[i].astype(BF16))
            ret = _retention(proj, ret_norm[i])
            gm = _gmlp(proj, gmlp_norm[i], gmlp_ws[i], gmlp_bs[i])
            w_out = w_out_ab[i].astype(BF16)
            nr = ret.shape[1]
            acts, w_outs = [ret, gm], [w_out[:nr], w_out[nr:]]
        else:
            proj = _norm_matmul(xt, attn_norm[layer], w_in_c[i].astype(BF16))
            acts = [_hgrn2(proj, lower_bounds[layer], hgrn_norm[i])]
            w_outs = [w_out_c[i].astype(BF16)]
        xt, h_packed, route, counts = _outproj_router(
            xt, acts, w_outs, ffn_norm[layer], router_w_group[layer], router_b_group[layer],
            router_w_expert[layer], router_b_expert[layer])
        block_expert, n_used, block_next, buf_src, dest = _moe_dispatch(route, counts, t)
        expert_out = _moe_experts(h_packed, block_expert, n_used, block_next, buf_src,
                                  w_gate, w_up, w_down, layer)
        xt = _combine(xt, route, dest, expert_out, final_norm if layer == depth - 1 else None)
    return xt.reshape(b, s, d)
```

```python
import functools

import jax
import jax.numpy as jnp
from jax import lax
from jax.experimental import pallas as pl
from jax.experimental.pallas import tpu as pltpu

F32 = jnp.float32
BF16 = jnp.bfloat16
EPS = 1e-6

RET_HEADS = 4
RET_CHUNK = 128
ROPE_BASE = 10000.0
GMLP_GROUPS = 4
GMLP_CHUNK = 128
HGRN_DK = 128
HGRN_CHUNK = 32
N_GROUPS = 4
EXPERTS_PER_GROUP = 8
N_EXPERTS = N_GROUPS * EXPERTS_PER_GROUP
TOP_K = 2
MOE_BLOCK = 128
MOE_ROW_SLOTS = 3
ROUTE_LANES = 128
NORM_SLAB = 256
ROUTE_FIELDS = 8
TOKEN_TILE = 256

VMEM_LIMIT = 48 * 1024 * 1024
MOE_VMEM_LIMIT = 56 * 1024 * 1024


def _params(*sem):
    return pltpu.CompilerParams(dimension_semantics=sem, vmem_limit_bytes=VMEM_LIMIT)


def _rms(x, g):
    return x * lax.rsqrt(jnp.mean(x * x, axis=-1, keepdims=True) + EPS) * g


def _norm_matmul_kernel(x_ref, g_ref, w_ref, proj_ref, xn_ref):
    @pl.when(pl.program_id(1) == 0)
    def _():
        for r0 in range(0, x_ref.shape[0], NORM_SLAB):
            rows = pl.ds(r0, NORM_SLAB)
            xn_ref[rows, :] = _rms(x_ref[rows, :], g_ref[...]).astype(BF16)

    proj_ref[...] = jnp.dot(xn_ref[...], w_ref[...],
                            preferred_element_type=F32).astype(proj_ref.dtype)


def _norm_matmul(x, g, w, *, tm=1024, tn=1024):
    t, d = x.shape
    n = w.shape[1]
    return pl.pallas_call(
        _norm_matmul_kernel,
        grid=(t // tm, n // tn),
        in_specs=[pl.BlockSpec((tm, d), lambda i, j: (i, 0)),
                  pl.BlockSpec((1, d), lambda i, j: (0, 0)),
                  pl.BlockSpec((d, tn), lambda i, j: (0, j))],
        out_specs=pl.BlockSpec((tm, tn), lambda i, j: (i, j)),
        out_shape=jax.ShapeDtypeStruct((t, n), BF16),
        scratch_shapes=[pltpu.VMEM((tm, d), BF16)],
        compiler_params=_params("arbitrary", "arbitrary"),
        name="norm_matmul",
    )(x, g.reshape(1, d), w)


def _rope(x, cos, sin):
    half = x.shape[-1] // 2
    x1, x2 = x[:, :half], x[:, half:]
    return jnp.concatenate([x1 * cos - x2 * sin, x2 * cos + x1 * sin], axis=-1)


def _retention_kernel(cd_ref, q_ref, k_ref, v_ref, g_ref, cos_ref, sin_ref, dint_ref, qd_ref, kd_ref,
                      gain_ref, o_ref, state_ref, *, chunk, nchunk):
    @pl.when(pl.program_id(1) == 0)
    def _():
        state_ref[...] = jnp.zeros_like(state_ref)

    dk = q_ref.shape[-1]
    dint = dint_ref[0]
    qd = qd_ref[0]
    kd = kd_ref[0]
    cd = cd_ref[pl.program_id(0)]
    gain = gain_ref[...]
    for c in range(nchunk):
        rows = pl.ds(c * chunk, chunk)
        cos = cos_ref[rows, :]
        sin = sin_ref[rows, :]
        q = _rope(q_ref[rows, :].astype(F32), cos, sin)
        k = _rope(k_ref[rows, :].astype(F32), cos, sin) * (dk ** -0.5)
        v = v_ref[rows, :]
        scores = lax.dot_general(q.astype(BF16), k.astype(BF16), (((1,), (1,)), ((), ())),
                                 preferred_element_type=F32) * dint
        state = state_ref[...]
        o = (jnp.dot(scores.astype(BF16), v, preferred_element_type=F32)
             + jnp.dot((q * qd).astype(BF16), state.astype(BF16), preferred_element_type=F32))
        state_ref[...] = cd * state + lax.dot_general(
            (k * kd).astype(BF16), v, (((0,), (0,)), ((), ())), preferred_element_type=F32)
        o = o - jnp.mean(o, axis=-1, keepdims=True)
        o = o * lax.rsqrt(jnp.mean(o * o, axis=-1, keepdims=True) + EPS) * gain
        o_ref[rows, :] = (jax.nn.silu(g_ref[rows, :].astype(F32)) * o).astype(o_ref.dtype)


def _retention(proj, gain, *, rows=1024):
    t = proj.shape[0]
    h = RET_HEADS
    dk = proj.shape[1] // 6 // h
    dv = dk
    c = RET_CHUNK
    f32 = F32
    inv = ROPE_BASE ** (-jnp.arange(0, dk, 2, dtype=f32) / dk)
    ang = jnp.arange(t, dtype=f32)[:, None] * inv[None, :]
    cos, sin = jnp.cos(ang), jnp.sin(ang)
    log_gamma = jnp.log(1.0 - jnp.exp2(-5.0 - jnp.arange(h, dtype=f32)))
    idx = jnp.arange(c, dtype=f32)
    diff = idx[:, None] - idx[None, :]
    d_intra = jnp.where(diff >= 0, jnp.exp(jnp.maximum(diff, 0.0) * log_gamma[:, None, None]), 0.0)
    q_decay = jnp.broadcast_to(jnp.exp((idx + 1.0)[None, :] * log_gamma[:, None])[..., None], (h, c, dk))
    k_decay = jnp.broadcast_to(jnp.exp((c - 1.0 - idx)[None, :] * log_gamma[:, None])[..., None], (h, c, dk))
    chunk_decay = jnp.exp(c * log_gamma)

    def col(off):
        return pl.BlockSpec((rows, dk), lambda hh, i: (i, off + hh))

    tab = pl.BlockSpec((rows, dk // 2), lambda hh, i: (i, 0))
    per_head = lambda shp: pl.BlockSpec((1,) + shp, lambda hh, i: (hh, 0, 0))
    return pl.pallas_call(
        functools.partial(_retention_kernel, chunk=c, nchunk=rows // c),
        grid=(h, t // rows),
        in_specs=[pl.BlockSpec(memory_space=pltpu.SMEM),
                  col(0), col(h), col(2 * h), col(3 * h), tab, tab,
                  per_head((c, c)), per_head((c, dk)), per_head((c, dk)),
                  pl.BlockSpec((1, dv), lambda hh, i: (0, hh))],
        out_specs=pl.BlockSpec((rows, dv), lambda hh, i: (i, hh)),
        out_shape=jax.ShapeDtypeStruct((t, h * dv), BF16),
        scratch_shapes=[pltpu.VMEM((dk, dv), F32)],
        compiler_params=_params("arbitrary", "arbitrary"),
        name="retention",
    )(chunk_decay, proj, proj, proj, proj, cos, sin, d_intra, q_decay, k_decay, gain.reshape(1, h * dv))


def _gmlp_kernel(u_ref, vs_ref, gain_ref, ws_ref, bs_ref, o_ref, *, chunk, nchunk):
    r = lax.broadcasted_iota(jnp.int32, (chunk, chunk), 0)
    s = lax.broadcasted_iota(jnp.int32, (chunk, chunk), 1)
    w = jnp.where(r >= s, ws_ref[0], 0.0).astype(BF16)
    gain = gain_ref[...]
    bs = bs_ref[0]
    for c in range(nchunk):
        rows = pl.ds(c * chunk, chunk)
        v = jax.nn.gelu(vs_ref[rows, :].astype(F32))
        v = v - jnp.mean(v, axis=-1, keepdims=True)
        v = v * lax.rsqrt(jnp.mean(v * v, axis=-1, keepdims=True) + EPS) * gain
        mixed = jnp.dot(w, v.astype(BF16), preferred_element_type=F32) + bs
        o_ref[rows, :] = (jax.nn.gelu(u_ref[rows, :].astype(F32)) * mixed).astype(o_ref.dtype)


def _gmlp(proj, gain, ws, bs, *, rows=1024):
    t = proj.shape[0]
    g = GMLP_GROUPS
    dim = proj.shape[1] // 6 // g
    c = GMLP_CHUNK
    return pl.pallas_call(
        functools.partial(_gmlp_kernel, chunk=c, nchunk=rows // c),
        grid=(g, t // rows),
        in_specs=[pl.BlockSpec((rows, dim), lambda gg, i: (i, 4 * g + gg)),
                  pl.BlockSpec((rows, dim), lambda gg, i: (i, 5 * g + gg)),
                  pl.BlockSpec((1, dim), lambda gg, i: (0, gg)),
                  pl.BlockSpec((1, c, c), lambda gg, i: (gg, 0, 0)),
                  pl.BlockSpec((1, c, 1), lambda gg, i: (gg, 0, 0))],
        out_specs=pl.BlockSpec((rows, dim), lambda gg, i: (i, gg)),
        out_shape=jax.ShapeDtypeStruct((t, g * dim), BF16),
        compiler_params=_params("arbitrary", "arbitrary"),
        name="gmlp",
    )(proj, proj, gain.reshape(1, g * dim), ws, bs.reshape(g, c, 1))


def _hgrn_kernel(zq_ref, zf_ref, zi_ref, zg_ref, lb_ref, gain_ref, tri_ref, keep_ref, o_ref, state_ref,
                 *, chunk, nchunk, heads, dk):
    @pl.when(pl.program_id(1) == 0)
    def _():
        state_ref[...] = jnp.zeros_like(state_ref)

    rows = chunk * nchunk
    nt = (((1,), (1,)), ((), ()))
    tri = tri_ref[...]
    keep = keep_ref[...] > 0.0
    chunk_of_row = lax.broadcasted_iota(jnp.int32, (rows, dk), 0) // chunk
    hw = heads * dk
    head_cols = [slice(hd * dk, (hd + 1) * dk) for hd in range(heads)]
    lb = lb_ref[...]
    zf = zf_ref[...].astype(F32)
    f = lb + (1.0 - lb) * jax.nn.sigmoid(zf)
    kk = (1.0 - lb) * jax.nn.sigmoid(-zf)
    log_f = jnp.log(f)
    p0 = log_f.astype(BF16)
    r0 = log_f - p0.astype(F32)
    p1 = r0.astype(BF16)
    p2 = (r0 - p1.astype(F32)).astype(BF16)
    cum = (jnp.dot(tri, p0, preferred_element_type=F32)
           + jnp.dot(tri, p1, preferred_element_type=F32)
           + jnp.dot(tri, p2, preferred_element_type=F32))
    lasts = [cum[(c + 1) * chunk - 1:(c + 1) * chunk, :] for c in range(nchunk)]
    last_rows = jnp.concatenate([jnp.broadcast_to(l, (chunk, hw)) for l in lasts], axis=0)
    decays = [jnp.exp(l) for l in lasts]
    q_dec = (jax.nn.silu(zq_ref[...].astype(F32)) * jnp.exp(cum)).astype(BF16)
    k_dec = (kk * jnp.exp(-cum)).astype(BF16)
    k_out = (kk * jnp.exp(last_rows - cum)).astype(BF16)
    v = zi_ref[...]
    gate = jax.nn.silu(zg_ref[...].astype(F32))
    scores = [lax.dot_general(q_dec[:, cs], k_dec[:, cs], nt, preferred_element_type=F32)
              for cs in head_cols]
    scores = [jnp.where(keep, s, 0.0).astype(BF16) for s in scores]
    o_intra = [jnp.dot(s, v[:, cs], preferred_element_type=F32) for s, cs in zip(scores, head_cols)]
    v_t = [v[:, cs].astype(F32).T.astype(BF16) for cs in head_cols]
    zero = jnp.zeros((rows, dk), BF16)
    k_blocks = [jnp.concatenate([jnp.where(chunk_of_row == c, k_out[:, cs], zero) for c in range(nchunk)],
                                axis=1) for cs in head_cols]
    incr = [jnp.dot(v_t[hd], k_blocks[hd], preferred_element_type=F32) for hd in range(heads)]
    states = [state_ref[hd] for hd in range(heads)]
    inter = [[] for _ in range(heads)]
    for c in range(nchunk):
        sl = slice(c * chunk, (c + 1) * chunk)
        for hd, cs in enumerate(head_cols):
            inter[hd].append(lax.dot_general(q_dec[sl, cs], states[hd].astype(BF16), nt,
                                             preferred_element_type=F32))
            states[hd] = decays[c][:, cs] * states[hd] + incr[hd][:, c * dk:(c + 1) * dk]
    for hd, cs in enumerate(head_cols):
        state_ref[hd] = states[hd]
        o = o_intra[hd] + jnp.concatenate(inter[hd], axis=0)
        o = o * lax.rsqrt(jnp.mean(o * o, axis=-1, keepdims=True) + EPS) * gain_ref[:, cs]
        o_ref[:, cs] = (o * gate[:, cs]).astype(o_ref.dtype)


def _hgrn2(proj, lb, gain, *, rows=256, heads=4):
    t = proj.shape[0]
    width = proj.shape[1] // 4
    dk = HGRN_DK
    h = width // dk
    c = HGRN_CHUNK
    hw = heads * dk
    groups = h // heads
    r = jnp.arange(rows, dtype=jnp.int32)
    tri = ((r[:, None] >= r[None, :]) & ((r[:, None] // c) == (r[None, :] // c))).astype(F32)

    def col(off):
        return pl.BlockSpec((rows, hw), lambda hh, i: (i, off + hh))

    vec = pl.BlockSpec((1, hw), lambda hh, i: (0, hh))
    mask = pl.BlockSpec((rows, rows), lambda hh, i: (0, 0))
    return pl.pallas_call(
        functools.partial(_hgrn_kernel, chunk=c, nchunk=rows // c, heads=heads, dk=dk),
        grid=(groups, t // rows),
        in_specs=[col(0), col(groups), col(2 * groups), col(3 * groups), vec, vec, mask, mask],
        out_specs=pl.BlockSpec((rows, hw), lambda hh, i: (i, hh)),
        out_shape=jax.ShapeDtypeStruct((t, width), BF16),
        scratch_shapes=[pltpu.VMEM((heads, dk, dk), F32)],
        compiler_params=_params("arbitrary", "arbitrary"),
        name="hgrn2",
    )(proj, proj, proj, proj, lb.reshape(1, width), gain.reshape(1, width), tri.astype(BF16), tri)


def _pack_bf16_pairs(h):
    half = h.shape[1] // 2
    bits = pltpu.bitcast(h.astype(BF16).astype(F32), jnp.uint32)
    return (bits[:, :half] >> 16) | bits[:, half:]


def _unpack_bf16_pairs(words):
    lo = pltpu.bitcast(words << 16, F32)
    hi = pltpu.bitcast(words & jnp.uint32(0xFFFF0000), F32)
    return jnp.concatenate([lo, hi], axis=1).astype(BF16)


def _route(logits, run):
    lane = lax.broadcasted_iota(jnp.int32, logits.shape, 1)
    lane_f = lane.astype(F32)
    neg = -jnp.inf
    big = float(ROUTE_LANES)
    lg = jnp.where(lane < N_GROUPS, logits, neg)
    mg = jnp.max(lg, axis=-1, keepdims=True)
    eg = jnp.exp(lg - mg)
    pg = eg / jnp.sum(eg, axis=-1, keepdims=True)
    p_sel = jnp.max(pg, axis=-1, keepdims=True)
    g_sel = jnp.min(jnp.where(lg == mg, lane_f, big), axis=-1, keepdims=True)
    e_grp = ((lane - N_GROUPS) // EXPERTS_PER_GROUP).astype(F32)
    in_grp = jnp.where(lane >= N_GROUPS, e_grp, -1.0) == g_sel
    le = jnp.where(in_grp, logits, neg)
    t1 = jnp.max(le, axis=-1, keepdims=True)
    i1 = jnp.min(jnp.where(le == t1, lane_f, big), axis=-1, keepdims=True)
    le2 = jnp.where(lane_f == i1, neg, le)
    t2 = jnp.max(le2, axis=-1, keepdims=True)
    i2 = jnp.min(jnp.where(le2 == t2, lane_f, big), axis=-1, keepdims=True)
    e2 = jnp.exp(t2 - t1)
    den = 1.0 + e2
    gate1 = p_sel * (1.0 / den)
    gate2 = p_sel * (e2 / den)
    tm = logits.shape[0]
    hit1 = jnp.where(lane_f == i1, 1.0, 0.0)
    hit2 = jnp.where(lane_f == i2, 1.0, 0.0)
    hits = hit1 + hit2
    r = lax.broadcasted_iota(jnp.int32, (tm, tm), 0)
    s = lax.broadcasted_iota(jnp.int32, (tm, tm), 1)
    before = jnp.where(r > s, 1.0, 0.0).astype(BF16)
    prefix = jnp.dot(before, hits.astype(BF16), preferred_element_type=F32) + run
    rank1 = jnp.sum(prefix * hit1, axis=-1, keepdims=True)
    rank2 = jnp.sum(prefix * hit2, axis=-1, keepdims=True)
    vals = (i1 - N_GROUPS, i2 - N_GROUPS, gate1, gate2, rank1, rank2)
    slab = jnp.zeros_like(logits)
    for pos, val in enumerate(vals):
        slab = jnp.where(lane == pos, val, slab)
    return slab, run + jnp.sum(hits, axis=0, keepdims=True)


def _outproj_router_kernel(*refs, n_act):
    x_ref = refs[0]
    a_refs = refs[1:1 + n_act]
    w_refs = refs[1 + n_act:1 + 2 * n_act]
    g_ref, wr_ref, br_ref, xo_ref, hp_ref, route_ref, count_ref, run_ref, xs_ref = refs[1 + 2 * n_act:]
    i = pl.program_id(0)

    @pl.when(i == 0)
    def _():
        run_ref[...] = jnp.zeros_like(run_ref)
        xs_ref[1] = jnp.zeros(xs_ref.shape[1:], xs_ref.dtype)

    def step(slot):
        x = x_ref[...]
        for a_ref, w_ref in zip(a_refs, w_refs):
            x = x + jnp.dot(a_ref[...], w_ref[...], preferred_element_type=F32)
        xo_ref[...] = x
        xs_ref[slot] = x

        x_prev = xs_ref[1 - slot]
        h = _rms(x_prev, g_ref[...])
        hp_ref[...] = _pack_bf16_pairs(h)
        h_hi = h.astype(BF16)
        h_lo = (h - h_hi.astype(F32)).astype(BF16)
        p = (jnp.dot(h_hi, wr_ref[...], preferred_element_type=F32)
             + jnp.dot(h_lo, wr_ref[...], preferred_element_type=F32))
        logits = p[:, :ROUTE_LANES] + p[:, ROUTE_LANES:] + br_ref[...]
        run = run_ref[...]
        slab, run_next = _route(logits, run)
        route_ref[0] = slab.T[:ROUTE_FIELDS, :]
        run = jnp.where(i >= 1, run_next, run)
        run_ref[...] = run
        count_ref[...] = run

    for slot in range(2):
        pl.when(i % 2 == slot)(functools.partial(step, slot))


def _outproj_router(x, acts, ws, g, w_rg, b_rg, w_re, b_re, *, tm=TOKEN_TILE):
    t, d = x.shape
    used = N_GROUPS + N_EXPERTS
    wr = jnp.zeros((d, ROUTE_LANES), F32).at[:, :N_GROUPS].set(w_rg).at[:, N_GROUPS:used].set(w_re)
    wr_hi = wr.astype(BF16)
    wr_lo = (wr - wr_hi.astype(F32)).astype(BF16)
    wr2 = jnp.concatenate([wr_hi, wr_lo], axis=1)
    br = jnp.zeros((1, ROUTE_LANES), F32).at[0, :N_GROUPS].set(b_rg).at[0, N_GROUPS:used].set(b_re)
    nt = t // tm
    cur = lambda i: (jnp.minimum(i, nt - 1), 0)
    prev = lambda i: (jnp.maximum(i - 1, 0), 0)
    row = pl.BlockSpec((tm, d), cur)
    const = lambda a: pl.BlockSpec(a.shape, lambda i: (0, 0))
    in_specs = [row]
    in_specs += [pl.BlockSpec((tm, a.shape[1]), cur) for a in acts]
    in_specs += [const(w) for w in ws]
    in_specs += [pl.BlockSpec((1, d), lambda i: (0, 0)), const(wr2), const(br)]
    return pl.pallas_call(
        functools.partial(_outproj_router_kernel, n_act=len(acts)),
        grid=(nt + 1,),
        in_specs=in_specs,
        out_specs=[row, pl.BlockSpec((tm, d // 2), prev),
                   pl.BlockSpec((1, ROUTE_FIELDS, tm), lambda i: prev(i) + (0,)),
                   pl.BlockSpec((1, ROUTE_LANES), lambda i: (0, 0))],
        out_shape=[jax.ShapeDtypeStruct((t, d), F32), jax.ShapeDtypeStruct((t, d // 2), jnp.uint32),
                   jax.ShapeDtypeStruct((nt, ROUTE_FIELDS, tm), F32),
                   jax.ShapeDtypeStruct((1, ROUTE_LANES), F32)],
        scratch_shapes=[pltpu.VMEM((1, ROUTE_LANES), F32), pltpu.VMEM((2, tm, d), F32)],
        compiler_params=_params("arbitrary"),
        name="outproj_router",
    )(x, *acts, *ws, g.reshape(1, d), wr2, br)


def _expert_changed(be_ref, b):
    return (b == 0) | (be_ref[b] != be_ref[jnp.maximum(b - 1, 0)])


def _moe_ffn_kernel(be_ref, nu_ref, ne_ref, src0_ref, src1_ref, src2_ref, h_hbm, wg_hbm, wu_hbm, wd_hbm,
                    out_ref, xbuf, stage_g, stage_u, stage_d, wg_bf, wu_bf, wd_bf, xsem, wsem, *, layer):
    b = pl.program_id(0)
    nu = nu_ref[0]
    nslot, blk = xbuf.shape[0], xbuf.shape[1]

    def start_rows(idx_ref, dst_slot):
        for r in range(blk):
            pltpu.make_async_copy(h_hbm.at[pl.ds(idx_ref[0, 0, r], 1)],
                                  xbuf.at[dst_slot, pl.ds(r, 1)], xsem.at[dst_slot]).start(priority=0)

    def wait_rows(dst_slot):
        pltpu.make_async_copy(h_hbm.at[pl.ds(0, blk)], xbuf.at[dst_slot], xsem.at[dst_slot]).wait()

    def weight_copies(e):
        return (pltpu.make_async_copy(wg_hbm.at[layer, e], stage_g, wsem.at[0]),
                pltpu.make_async_copy(wu_hbm.at[layer, e], stage_u, wsem.at[1]),
                pltpu.make_async_copy(wd_hbm.at[layer, e], stage_d, wsem.at[2]))

    @pl.when(b == 0)
    def _():
        for cp in weight_copies(be_ref[0]):
            cp.start(priority=1)
        start_rows(src0_ref, 0)
        start_rows(src1_ref, 1)

    changed = _expert_changed(be_ref, b)

    def compute(slot, new_expert):
        if new_expert:
            for cp in weight_copies(be_ref[b]):
                cp.wait()
        wait_rows(slot)
        x = _unpack_bf16_pairs(xbuf[slot])
        start_rows(src2_ref, (slot + 2) % nslot)
        if new_expert:
            wg_bf[...] = stage_g[...].astype(BF16)
        hg = jnp.dot(x, wg_bf[...], preferred_element_type=F32)
        if new_expert:
            wu_bf[...] = stage_u[...].astype(BF16)
        hu = jnp.dot(x, wu_bf[...], preferred_element_type=F32)
        if new_expert:
            wd_bf[...] = stage_d[...].astype(BF16)
        hid = (jax.nn.silu(hg) * hu).astype(BF16)
        out_ref[...] = jnp.dot(hid, wd_bf[...], preferred_element_type=F32)

        if new_expert:
            @pl.when(ne_ref[b] >= 0)
            def _():
                for cp in weight_copies(ne_ref[b]):
                    cp.start(priority=1)

        @pl.when(b == nu - 1)
        def _():
            wait_rows((slot + 1) % nslot)
            wait_rows((slot + 2) % nslot)

    for slot in range(nslot):
        here = (b < nu) & (b % nslot == slot)
        pl.when(here & changed)(functools.partial(compute, slot, True))
        pl.when(here & jnp.logical_not(changed))(functools.partial(compute, slot, False))

    @pl.when(b >= nu)
    def _():
        out_ref[...] = jnp.zeros_like(out_ref)


def _combine_kernel(*refs, final):
    if final:
        pos_ref, nxt_ref, x_ref, route_ref, eo_hbm, g_ref, o_ref, ybuf, sem = refs
    else:
        pos_ref, nxt_ref, x_ref, route_ref, eo_hbm, o_ref, ybuf, sem = refs
    i = pl.program_id(0)
    last = pl.num_programs(0) - 1
    tm = x_ref.shape[0]

    def start_rows(idx_ref, dst_slot):
        for r in range(tm):
            for k in range(TOP_K):
                pltpu.make_async_copy(eo_hbm.at[pl.ds(idx_ref[0, 0, k * tm + r], 1)],
                                      ybuf.at[dst_slot, k, pl.ds(r, 1)], sem.at[dst_slot]).start(priority=k)

    def wait_rows(dst_slot):
        for k in range(TOP_K):
            pltpu.make_async_copy(eo_hbm.at[pl.ds(0, tm)], ybuf.at[dst_slot, k], sem.at[dst_slot]).wait()

    @pl.when(i == 0)
    def _():
        start_rows(pos_ref, 0)

    def step(slot):
        start_rows(nxt_ref, 1 - slot)
        wait_rows(slot)
        route = route_ref[0].T
        y = ybuf[slot, 0] * route[:, TOP_K:TOP_K + 1] + ybuf[slot, 1] * route[:, TOP_K + 1:TOP_K + 2]
        x = x_ref[...] + y
        o_ref[...] = _rms(x, g_ref[...]) if final else x

        @pl.when(i == last)
        def _():
            wait_rows(1 - slot)

    for slot in range(2):
        pl.when(i % 2 == slot)(functools.partial(step, slot))


def _moe_dispatch(route, counts_slab, t):
    m = t * TOP_K
    nt, _, tm = route.shape
    expert = route[:, :TOP_K, :].astype(jnp.int32)
    rank = route[:, 2 * TOP_K:3 * TOP_K, :].astype(jnp.int32)
    counts = counts_slab[0, N_GROUPS:N_GROUPS + N_EXPERTS].astype(jnp.int32)
    padded = (counts + MOE_BLOCK - 1) // MOE_BLOCK * MOE_BLOCK
    padded_ends = jnp.cumsum(padded)
    padded_starts = padded_ends - padded
    ids = jnp.arange(N_EXPERTS, dtype=jnp.int32)
    start_of = jnp.sum(jnp.where(expert[..., None] == ids, padded_starts, 0), axis=-1)
    dest = start_of + rank
    n_blocks = -(-(m + N_EXPERTS * (MOE_BLOCK - 1)) // MOE_BLOCK)
    cap = n_blocks * MOE_BLOCK
    token = (jnp.arange(nt, dtype=jnp.int32)[:, None, None] * tm
             + jnp.arange(tm, dtype=jnp.int32)[None, None, :])
    token = jnp.broadcast_to(token, dest.shape)
    assert cap == m + N_EXPERTS * MOE_BLOCK
    pad_i = jnp.arange(MOE_BLOCK, dtype=jnp.int32)[None, :]
    pad_row = jnp.where(pad_i < (padded - counts)[:, None],
                        (padded_starts + counts)[:, None] + pad_i,
                        cap + ids[:, None] * MOE_BLOCK + pad_i)
    rows = jnp.concatenate([dest.reshape(m), pad_row.reshape(-1)])
    toks = jnp.concatenate([token.reshape(m), jnp.zeros((N_EXPERTS * MOE_BLOCK,), jnp.int32)])
    _, buf_src = lax.sort_key_val(rows, toks)
    block_start = jnp.arange(n_blocks, dtype=jnp.int32) * MOE_BLOCK
    block_expert = jnp.minimum(
        jnp.sum((padded_ends[None, :] <= block_start[:, None]).astype(jnp.int32), axis=1), N_EXPERTS - 1)
    n_used = (padded_ends[-1] // MOE_BLOCK).astype(jnp.int32).reshape(1)
    ids = jnp.arange(N_EXPERTS, dtype=jnp.int32)
    later = (ids[None, :] > ids[:, None]) & (counts[None, :] > 0)
    next_active = jnp.min(jnp.where(later, ids[None, :], N_EXPERTS), axis=1)
    next_active = jnp.where(next_active < N_EXPERTS, next_active, -1)
    block_next = next_active[block_expert]
    return block_expert, n_used, block_next, buf_src.reshape(n_blocks, 1, MOE_BLOCK), dest


def _moe_experts(h_packed, block_expert, n_used, block_next, buf_src, w_gate, w_up, w_down, layer):
    d, ff = w_gate.shape[-2:]
    n_blocks = buf_src.shape[0]
    cap = n_blocks * MOE_BLOCK
    idx_block = (1, 1, MOE_BLOCK)
    hbm = pl.BlockSpec(memory_space=pl.ANY)

    def ahead(k, b, *_):
        return (jnp.minimum(b + k, n_blocks - 1), 0, 0)

    return pl.pallas_call(
        functools.partial(_moe_ffn_kernel, layer=layer),
        grid_spec=pltpu.PrefetchScalarGridSpec(
            num_scalar_prefetch=3,
            grid=(n_blocks,),
            in_specs=[pl.BlockSpec(idx_block, lambda b, *_: (0, 0, 0), memory_space=pltpu.SMEM),
                      pl.BlockSpec(idx_block, lambda b, *_: (min(1, n_blocks - 1), 0, 0),
                                   memory_space=pltpu.SMEM),
                      pl.BlockSpec(idx_block, functools.partial(ahead, 2), memory_space=pltpu.SMEM),
                      hbm, hbm, hbm, hbm],
            out_specs=pl.BlockSpec((MOE_BLOCK, d), lambda b, *_: (b, 0)),
            scratch_shapes=[pltpu.VMEM((MOE_ROW_SLOTS, MOE_BLOCK, d // 2), jnp.uint32),
                            pltpu.VMEM((d, ff), F32), pltpu.VMEM((d, ff), F32), pltpu.VMEM((ff, d), F32),
                            pltpu.VMEM((d, ff), BF16), pltpu.VMEM((d, ff), BF16), pltpu.VMEM((ff, d), BF16),
                            pltpu.SemaphoreType.DMA((MOE_ROW_SLOTS,)), pltpu.SemaphoreType.DMA((3,))],
        ),
        out_shape=jax.ShapeDtypeStruct((cap, d), F32),
        compiler_params=pltpu.CompilerParams(dimension_semantics=("arbitrary",),
                                             vmem_limit_bytes=MOE_VMEM_LIMIT),
        name="moe_ffn",
    )(block_expert, n_used, block_next, buf_src, buf_src, buf_src, h_packed, w_gate, w_up, w_down)


def _combine(x, route, dest, expert_out, final_gain):
    t, d = x.shape
    nt, _, tm = route.shape
    final = final_gain is not None
    pos = dest.reshape(nt, 1, TOP_K * tm)
    idx_block = (1, 1, TOP_K * tm)
    row = pl.BlockSpec((tm, d), lambda i: (i, 0))
    in_specs = [pl.BlockSpec(idx_block, lambda i: (i, 0, 0), memory_space=pltpu.SMEM),
                pl.BlockSpec(idx_block, lambda i: (jnp.minimum(i + 1, nt - 1), 0, 0),
                             memory_space=pltpu.SMEM),
                row, pl.BlockSpec((1, ROUTE_FIELDS, tm), lambda i: (i, 0, 0)),
                pl.BlockSpec(memory_space=pl.ANY)]
    args = [pos, pos, x, route, expert_out]
    if final:
        in_specs.append(pl.BlockSpec((1, d), lambda i: (0, 0)))
        args.append(final_gain.reshape(1, d))
    return pl.pallas_call(
        functools.partial(_combine_kernel, final=final),
        grid=(nt,),
        in_specs=in_specs, out_specs=row,
        out_shape=jax.ShapeDtypeStruct((t, d), F32),
        scratch_shapes=[pltpu.VMEM((2, TOP_K, tm, d), F32), pltpu.SemaphoreType.DMA((2,))],
        compiler_params=_params("arbitrary"),
        name="moe_combine",
    )(*args)


def kernel(x, attn_norm, ffn_norm, final_norm, w_in_ab, ret_norm, gmlp_norm, gmlp_ws, gmlp_bs, w_out_ab, w_in_c, lb_params, hgrn_norm, w_out_c, router_w_group, router_b_group, router_w_expert, router_b_expert, w_gate, w_up, w_down):
    b, s, d = x.shape
    assert b == 1, "the sequence mixers carry state along the flattened token axis"
    depth = attn_norm.shape[0]
    lb_soft = jax.nn.softmax(lb_params.astype(F32), axis=0)
    lower_bounds = jnp.cumsum(lb_soft, axis=0) - lb_soft[0]
    xt = x.reshape(b * s, d)
    t = b * s
    for layer in range(depth):
        i = layer // 2
        if layer % 2 == 0:
            proj = _norm_matmul(xt, attn_norm[layer], w_in_ab[i].astype(BF16))
            ret = _retention(proj, ret_norm[i])
            gm = _gmlp(proj, gmlp_norm[i], gmlp_ws[i], gmlp_bs[i])
            w_out = w_out_ab[i].astype(BF16)
            nr = ret.shape[1]
            acts, w_outs = [ret, gm], [w_out[:nr], w_out[nr:]]
        else:
            proj = _norm_matmul(xt, attn_norm[layer], w_in_c[i].astype(BF16))
            acts = [_hgrn2(proj, lower_bounds[layer], hgrn_norm[i])]
            w_outs = [w_out_c[i].astype(BF16)]
        xt, h_packed, route, counts = _outproj_router(
            xt, acts, w_outs, ffn_norm[layer], router_w_group[layer], router_b_group[layer],
            router_w_expert[layer], router_b_expert[layer])
        block_expert, n_used, block_next, buf_src, dest = _moe_dispatch(route, counts, t)
        expert_out = _moe_experts(h_packed, block_expert, n_used, block_next, buf_src,
                                  w_gate, w_up, w_down, layer)
        xt = _combine(xt, route, dest, expert_out, final_norm if layer == depth - 1 else None)
    return xt.reshape(b, s, d)
```

```python
import functools

import jax
import jax.numpy as jnp
from jax import lax
from jax.experimental import pallas as pl
from jax.experimental.pallas import tpu as pltpu

F32 = jnp.float32
BF16 = jnp.bfloat16
EPS = 1e-6

RET_HEADS = 4
RET_CHUNK = 128
ROPE_BASE = 10000.0
GMLP_GROUPS = 4
GMLP_CHUNK = 128
HGRN_DK = 128
HGRN_CHUNK = 32
N_GROUPS = 4
EXPERTS_PER_GROUP = 8
N_EXPERTS = N_GROUPS * EXPERTS_PER_GROUP
TOP_K = 2
MOE_BLOCK = 128
MOE_ROW_SLOTS = 3
WEIGHT_CHUNKS = 4
ROUTE_LANES = 128
NORM_SLAB = 256
ROUTE_FIELDS = 8
TOKEN_TILE = 256

VMEM_LIMIT = 48 * 1024 * 1024
MOE_VMEM_LIMIT = 56 * 1024 * 1024


def _params(*sem):
    return pltpu.CompilerParams(dimension_semantics=sem, vmem_limit_bytes=VMEM_LIMIT)


def _rms(x, g):
    return x * lax.rsqrt(jnp.mean(x * x, axis=-1, keepdims=True) + EPS) * g


def _norm_matmul_kernel(x_ref, g_ref, w_ref, proj_ref, xn_ref):
    @pl.when(pl.program_id(1) == 0)
    def _():
        for r0 in range(0, x_ref.shape[0], NORM_SLAB):
            rows = pl.ds(r0, NORM_SLAB)
            xn_ref[rows, :] = _rms(x_ref[rows, :], g_ref[...]).astype(BF16)

    proj_ref[...] = jnp.dot(xn_ref[...], w_ref[...],
                            preferred_element_type=F32).astype(proj_ref.dtype)


def _norm_matmul(x, g, w, *, tm=1024, tn=1024):
    t, d = x.shape
    n = w.shape[1]
    return pl.pallas_call(
        _norm_matmul_kernel,
        grid=(t // tm, n // tn),
        in_specs=[pl.BlockSpec((tm, d), lambda i, j: (i, 0)),
                  pl.BlockSpec((1, d), lambda i, j: (0, 0)),
                  pl.BlockSpec((d, tn), lambda i, j: (0, j))],
        out_specs=pl.BlockSpec((tm, tn), lambda i, j: (i, j)),
        out_shape=jax.ShapeDtypeStruct((t, n), BF16),
        scratch_shapes=[pltpu.VMEM((tm, d), BF16)],
        compiler_params=_params("arbitrary", "arbitrary"),
        name="norm_matmul",
    )(x, g.reshape(1, d), w)


def _rope(x, cos, sin):
    half = x.shape[-1] // 2
    x1, x2 = x[:, :half], x[:, half:]
    return jnp.concatenate([x1 * cos - x2 * sin, x2 * cos + x1 * sin], axis=-1)


def _retention_kernel(cd_ref, q_ref, k_ref, v_ref, g_ref, cos_ref, sin_ref, dint_ref, qd_ref, kd_ref,
                      gain_ref, o_ref, state_ref, *, chunk, nchunk):
    @pl.when(pl.program_id(1) == 0)
    def _():
        state_ref[...] = jnp.zeros_like(state_ref)

    dk = q_ref.shape[-1]
    dint = dint_ref[0]
    qd = qd_ref[0]
    kd = kd_ref[0]
    cd = cd_ref[pl.program_id(0)]
    gain = gain_ref[...]
    for c in range(nchunk):
        rows = pl.ds(c * chunk, chunk)
        cos = cos_ref[rows, :]
        sin = sin_ref[rows, :]
        q = _rope(q_ref[rows, :].astype(F32), cos, sin)
        k = _rope(k_ref[rows, :].astype(F32), cos, sin) * (dk ** -0.5)
        v = v_ref[rows, :]
        scores = lax.dot_general(q.astype(BF16), k.astype(BF16), (((1,), (1,)), ((), ())),
                                 preferred_element_type=F32) * dint
        state = state_ref[...]
        o = (jnp.dot(scores.astype(BF16), v, preferred_element_type=F32)
             + jnp.dot((q * qd).astype(BF16), state.astype(BF16), preferred_element_type=F32))
        state_ref[...] = cd * state + lax.dot_general(
            (k * kd).astype(BF16), v, (((0,), (0,)), ((), ())), preferred_element_type=F32)
        o = o - jnp.mean(o, axis=-1, keepdims=True)
        o = o * lax.rsqrt(jnp.mean(o * o, axis=-1, keepdims=True) + EPS) * gain
        o_ref[rows, :] = (jax.nn.silu(g_ref[rows, :].astype(F32)) * o).astype(o_ref.dtype)


def _retention(proj, gain, *, rows=1024):
    t = proj.shape[0]
    h = RET_HEADS
    dk = proj.shape[1] // 6 // h
    dv = dk
    c = RET_CHUNK
    f32 = F32
    inv = ROPE_BASE ** (-jnp.arange(0, dk, 2, dtype=f32) / dk)
    ang = jnp.arange(t, dtype=f32)[:, None] * inv[None, :]
    cos, sin = jnp.cos(ang), jnp.sin(ang)
    log_gamma = jnp.log(1.0 - jnp.exp2(-5.0 - jnp.arange(h, dtype=f32)))
    idx = jnp.arange(c, dtype=f32)
    diff = idx[:, None] - idx[None, :]
    d_intra = jnp.where(diff >= 0, jnp.exp(jnp.maximum(diff, 0.0) * log_gamma[:, None, None]), 0.0)
    q_decay = jnp.broadcast_to(jnp.exp((idx + 1.0)[None, :] * log_gamma[:, None])[..., None], (h, c, dk))
    k_decay = jnp.broadcast_to(jnp.exp((c - 1.0 - idx)[None, :] * log_gamma[:, None])[..., None], (h, c, dk))
    chunk_decay = jnp.exp(c * log_gamma)

    def col(off):
        return pl.BlockSpec((rows, dk), lambda hh, i: (i, off + hh))

    tab = pl.BlockSpec((rows, dk // 2), lambda hh, i: (i, 0))
    per_head = lambda shp: pl.BlockSpec((1,) + shp, lambda hh, i: (hh, 0, 0))
    return pl.pallas_call(
        functools.partial(_retention_kernel, chunk=c, nchunk=rows // c),
        grid=(h, t // rows),
        in_specs=[pl.BlockSpec(memory_space=pltpu.SMEM),
                  col(0), col(h), col(2 * h), col(3 * h), tab, tab,
                  per_head((c, c)), per_head((c, dk)), per_head((c, dk)),
                  pl.BlockSpec((1, dv), lambda hh, i: (0, hh))],
        out_specs=pl.BlockSpec((rows, dv), lambda hh, i: (i, hh)),
        out_shape=jax.ShapeDtypeStruct((t, h * dv), BF16),
        scratch_shapes=[pltpu.VMEM((dk, dv), F32)],
        compiler_params=_params("arbitrary", "arbitrary"),
        name="retention",
    )(chunk_decay, proj, proj, proj, proj, cos, sin, d_intra, q_decay, k_decay, gain.reshape(1, h * dv))


def _gmlp_kernel(u_ref, vs_ref, gain_ref, ws_ref, bs_ref, o_ref, *, chunk, nchunk):
    r = lax.broadcasted_iota(jnp.int32, (chunk, chunk), 0)
    s = lax.broadcasted_iota(jnp.int32, (chunk, chunk), 1)
    w = jnp.where(r >= s, ws_ref[0], 0.0).astype(BF16)
    gain = gain_ref[...]
    bs = bs_ref[0]
    for c in range(nchunk):
        rows = pl.ds(c * chunk, chunk)
        v = jax.nn.gelu(vs_ref[rows, :].astype(F32))
        v = v - jnp.mean(v, axis=-1, keepdims=True)
        v = v * lax.rsqrt(jnp.mean(v * v, axis=-1, keepdims=True) + EPS) * gain
        mixed = jnp.dot(w, v.astype(BF16), preferred_element_type=F32) + bs
        o_ref[rows, :] = (jax.nn.gelu(u_ref[rows, :].astype(F32)) * mixed).astype(o_ref.dtype)


def _gmlp(proj, gain, ws, bs, *, rows=1024):
    t = proj.shape[0]
    g = GMLP_GROUPS
    dim = proj.shape[1] // 6 // g
    c = GMLP_CHUNK
    return pl.pallas_call(
        functools.partial(_gmlp_kernel, chunk=c, nchunk=rows // c),
        grid=(g, t // rows),
        in_specs=[pl.BlockSpec((rows, dim), lambda gg, i: (i, 4 * g + gg)),
                  pl.BlockSpec((rows, dim), lambda gg, i: (i, 5 * g + gg)),
                  pl.BlockSpec((1, dim), lambda gg, i: (0, gg)),
                  pl.BlockSpec((1, c, c), lambda gg, i: (gg, 0, 0)),
                  pl.BlockSpec((1, c, 1), lambda gg, i: (gg, 0, 0))],
        out_specs=pl.BlockSpec((rows, dim), lambda gg, i: (i, gg)),
        out_shape=jax.ShapeDtypeStruct((t, g * dim), BF16),
        compiler_params=_params("arbitrary", "arbitrary"),
        name="gmlp",
    )(proj, proj, gain.reshape(1, g * dim), ws, bs.reshape(g, c, 1))


def _hgrn_kernel(zq_ref, zf_ref, zi_ref, zg_ref, lb_ref, gain_ref, tri_ref, keep_ref, o_ref, state_ref,
                 *, chunk, nchunk, heads, dk):
    @pl.when(pl.program_id(1) == 0)
    def _():
        state_ref[...] = jnp.zeros_like(state_ref)

    rows = chunk * nchunk
    nt = (((1,), (1,)), ((), ()))
    tri = tri_ref[...]
    keep = keep_ref[...] > 0.0
    chunk_of_row = lax.broadcasted_iota(jnp.int32, (rows, dk), 0) // chunk
    hw = heads * dk
    head_cols = [slice(hd * dk, (hd + 1) * dk) for hd in range(heads)]
    lb = lb_ref[...]
    zf = zf_ref[...].astype(F32)
    f = lb + (1.0 - lb) * jax.nn.sigmoid(zf)
    kk = (1.0 - lb) * jax.nn.sigmoid(-zf)
    log_f = jnp.log(f)
    p0 = log_f.astype(BF16)
    r0 = log_f - p0.astype(F32)
    p1 = r0.astype(BF16)
    p2 = (r0 - p1.astype(F32)).astype(BF16)
    cum = (jnp.dot(tri, p0, preferred_element_type=F32)
           + jnp.dot(tri, p1, preferred_element_type=F32)
           + jnp.dot(tri, p2, preferred_element_type=F32))
    lasts = [cum[(c + 1) * chunk - 1:(c + 1) * chunk, :] for c in range(nchunk)]
    last_rows = jnp.concatenate([jnp.broadcast_to(l, (chunk, hw)) for l in lasts], axis=0)
    decays = [jnp.exp(l) for l in lasts]
    q_dec = (jax.nn.silu(zq_ref[...].astype(F32)) * jnp.exp(cum)).astype(BF16)
    k_dec = (kk * jnp.exp(-cum)).astype(BF16)
    k_out = (kk * jnp.exp(last_rows - cum)).astype(BF16)
    v = zi_ref[...]
    gate = jax.nn.silu(zg_ref[...].astype(F32))
    scores = [lax.dot_general(q_dec[:, cs], k_dec[:, cs], nt, preferred_element_type=F32)
              for cs in head_cols]
    scores = [jnp.where(keep, s, 0.0).astype(BF16) for s in scores]
    o_intra = [jnp.dot(s, v[:, cs], preferred_element_type=F32) for s, cs in zip(scores, head_cols)]
    v_t = [v[:, cs].astype(F32).T.astype(BF16) for cs in head_cols]
    zero = jnp.zeros((rows, dk), BF16)
    k_blocks = [jnp.concatenate([jnp.where(chunk_of_row == c, k_out[:, cs], zero) for c in range(nchunk)],
                                axis=1) for cs in head_cols]
    incr = [jnp.dot(v_t[hd], k_blocks[hd], preferred_element_type=F32) for hd in range(heads)]
    states = [state_ref[hd] for hd in range(heads)]
    inter = [[] for _ in range(heads)]
    for c in range(nchunk):
        sl = slice(c * chunk, (c + 1) * chunk)
        for hd, cs in enumerate(head_cols):
            inter[hd].append(lax.dot_general(q_dec[sl, cs], states[hd].astype(BF16), nt,
                                             preferred_element_type=F32))
            states[hd] = decays[c][:, cs] * states[hd] + incr[hd][:, c * dk:(c + 1) * dk]
    for hd, cs in enumerate(head_cols):
        state_ref[hd] = states[hd]
        o = o_intra[hd] + jnp.concatenate(inter[hd], axis=0)
        o = o * lax.rsqrt(jnp.mean(o * o, axis=-1, keepdims=True) + EPS) * gain_ref[:, cs]
        o_ref[:, cs] = (o * gate[:, cs]).astype(o_ref.dtype)


def _hgrn2(proj, lb, gain, *, rows=256, heads=4):
    t = proj.shape[0]
    width = proj.shape[1] // 4
    dk = HGRN_DK
    h = width // dk
    c = HGRN_CHUNK
    hw = heads * dk
    groups = h // heads
    r = jnp.arange(rows, dtype=jnp.int32)
    tri = ((r[:, None] >= r[None, :]) & ((r[:, None] // c) == (r[None, :] // c))).astype(F32)

    def col(off):
        return pl.BlockSpec((rows, hw), lambda hh, i: (i, off + hh))

    vec = pl.BlockSpec((1, hw), lambda hh, i: (0, hh))
    mask = pl.BlockSpec((rows, rows), lambda hh, i: (0, 0))
    return pl.pallas_call(
        functools.partial(_hgrn_kernel, chunk=c, nchunk=rows // c, heads=heads, dk=dk),
        grid=(groups, t // rows),
        in_specs=[col(0), col(groups), col(2 * groups), col(3 * groups), vec, vec, mask, mask],
        out_specs=pl.BlockSpec((rows, hw), lambda hh, i: (i, hh)),
        out_shape=jax.ShapeDtypeStruct((t, width), BF16),
        scratch_shapes=[pltpu.VMEM((heads, dk, dk), F32)],
        compiler_params=_params("arbitrary", "arbitrary"),
        name="hgrn2",
    )(proj, proj, proj, proj, lb.reshape(1, width), gain.reshape(1, width), tri.astype(BF16), tri)


def _pack_bf16_pairs(h):
    half = h.shape[1] // 2
    bits = pltpu.bitcast(h.astype(BF16).astype(F32), jnp.uint32)
    return (bits[:, :half] >> 16) | bits[:, half:]


def _unpack_bf16_pairs(words):
    lo = pltpu.bitcast(words << 16, F32)
    hi = pltpu.bitcast(words & jnp.uint32(0xFFFF0000), F32)
    return jnp.concatenate([lo, hi], axis=1).astype(BF16)


def _route(logits, run):
    lane = lax.broadcasted_iota(jnp.int32, logits.shape, 1)
    lane_f = lane.astype(F32)
    neg = -jnp.inf
    big = float(ROUTE_LANES)
    lg = jnp.where(lane < N_GROUPS, logits, neg)
    mg = jnp.max(lg, axis=-1, keepdims=True)
    eg = jnp.exp(lg - mg)
    pg = eg / jnp.sum(eg, axis=-1, keepdims=True)
    p_sel = jnp.max(pg, axis=-1, keepdims=True)
    g_sel = jnp.min(jnp.where(lg == mg, lane_f, big), axis=-1, keepdims=True)
    e_grp = ((lane - N_GROUPS) // EXPERTS_PER_GROUP).astype(F32)
    in_grp = jnp.where(lane >= N_GROUPS, e_grp, -1.0) == g_sel
    le = jnp.where(in_grp, logits, neg)
    t1 = jnp.max(le, axis=-1, keepdims=True)
    i1 = jnp.min(jnp.where(le == t1, lane_f, big), axis=-1, keepdims=True)
    le2 = jnp.where(lane_f == i1, neg, le)
    t2 = jnp.max(le2, axis=-1, keepdims=True)
    i2 = jnp.min(jnp.where(le2 == t2, lane_f, big), axis=-1, keepdims=True)
    e2 = jnp.exp(t2 - t1)
    den = 1.0 + e2
    gate1 = p_sel * (1.0 / den)
    gate2 = p_sel * (e2 / den)
    tm = logits.shape[0]
    hit1 = jnp.where(lane_f == i1, 1.0, 0.0)
    hit2 = jnp.where(lane_f == i2, 1.0, 0.0)
    hits = hit1 + hit2
    r = lax.broadcasted_iota(jnp.int32, (tm, tm), 0)
    s = lax.broadcasted_iota(jnp.int32, (tm, tm), 1)
    before = jnp.where(r > s, 1.0, 0.0).astype(BF16)
    prefix = jnp.dot(before, hits.astype(BF16), preferred_element_type=F32) + run
    rank1 = jnp.sum(prefix * hit1, axis=-1, keepdims=True)
    rank2 = jnp.sum(prefix * hit2, axis=-1, keepdims=True)
    vals = (i1 - N_GROUPS, i2 - N_GROUPS, gate1, gate2, rank1, rank2)
    slab = jnp.zeros_like(logits)
    for pos, val in enumerate(vals):
        slab = jnp.where(lane == pos, val, slab)
    return slab, run + jnp.sum(hits, axis=0, keepdims=True)


def _outproj_router_kernel(*refs, n_act):
    x_ref = refs[0]
    a_refs = refs[1:1 + n_act]
    w_refs = refs[1 + n_act:1 + 2 * n_act]
    g_ref, wr_ref, br_ref, xo_ref, hp_ref, route_ref, count_ref, run_ref, xs_ref = refs[1 + 2 * n_act:]
    i = pl.program_id(0)

    @pl.when(i == 0)
    def _():
        run_ref[...] = jnp.zeros_like(run_ref)
        xs_ref[1] = jnp.zeros(xs_ref.shape[1:], xs_ref.dtype)

    def step(slot):
        x = x_ref[...]
        for a_ref, w_ref in zip(a_refs, w_refs):
            x = x + jnp.dot(a_ref[...], w_ref[...], preferred_element_type=F32)
        xo_ref[...] = x
        xs_ref[slot] = x

        x_prev = xs_ref[1 - slot]
        h = _rms(x_prev, g_ref[...])
        hp_ref[...] = _pack_bf16_pairs(h)
        h_hi = h.astype(BF16)
        h_lo = (h - h_hi.astype(F32)).astype(BF16)
        p = (jnp.dot(h_hi, wr_ref[...], preferred_element_type=F32)
             + jnp.dot(h_lo, wr_ref[...], preferred_element_type=F32))
        logits = p[:, :ROUTE_LANES] + p[:, ROUTE_LANES:] + br_ref[...]
        run = run_ref[...]
        slab, run_next = _route(logits, run)
        route_ref[0] = slab.T[:ROUTE_FIELDS, :]
        run = jnp.where(i >= 1, run_next, run)
        run_ref[...] = run
        count_ref[...] = run

    for slot in range(2):
        pl.when(i % 2 == slot)(functools.partial(step, slot))


def _outproj_router(x, acts, ws, g, w_rg, b_rg, w_re, b_re, *, tm=TOKEN_TILE):
    t, d = x.shape
    used = N_GROUPS + N_EXPERTS
    wr = jnp.zeros((d, ROUTE_LANES), F32).at[:, :N_GROUPS].set(w_rg).at[:, N_GROUPS:used].set(w_re)
    wr_hi = wr.astype(BF16)
    wr_lo = (wr - wr_hi.astype(F32)).astype(BF16)
    wr2 = jnp.concatenate([wr_hi, wr_lo], axis=1)
    br = jnp.zeros((1, ROUTE_LANES), F32).at[0, :N_GROUPS].set(b_rg).at[0, N_GROUPS:used].set(b_re)
    nt = t // tm
    cur = lambda i: (jnp.minimum(i, nt - 1), 0)
    prev = lambda i: (jnp.maximum(i - 1, 0), 0)
    row = pl.BlockSpec((tm, d), cur)
    const = lambda a: pl.BlockSpec(a.shape, lambda i: (0, 0))
    in_specs = [row]
    in_specs += [pl.BlockSpec((tm, a.shape[1]), cur) for a in acts]
    in_specs += [const(w) for w in ws]
    in_specs += [pl.BlockSpec((1, d), lambda i: (0, 0)), const(wr2), const(br)]
    return pl.pallas_call(
        functools.partial(_outproj_router_kernel, n_act=len(acts)),
        grid=(nt + 1,),
        in_specs=in_specs,
        out_specs=[row, pl.BlockSpec((tm, d // 2), prev),
                   pl.BlockSpec((1, ROUTE_FIELDS, tm), lambda i: prev(i) + (0,)),
                   pl.BlockSpec((1, ROUTE_LANES), lambda i: (0, 0))],
        out_shape=[jax.ShapeDtypeStruct((t, d), F32), jax.ShapeDtypeStruct((t, d // 2), jnp.uint32),
                   jax.ShapeDtypeStruct((nt, ROUTE_FIELDS, tm), F32),
                   jax.ShapeDtypeStruct((1, ROUTE_LANES), F32)],
        scratch_shapes=[pltpu.VMEM((1, ROUTE_LANES), F32), pltpu.VMEM((2, tm, d), F32)],
        compiler_params=_params("arbitrary"),
        name="outproj_router",
    )(x, *acts, *ws, g.reshape(1, d), wr2, br)


def _expert_changed(be_ref, b):
    return (b == 0) | (be_ref[b] != be_ref[jnp.maximum(b - 1, 0)])


def _moe_ffn_kernel(be_ref, nu_ref, ne_ref, src0_ref, src1_ref, src2_ref, h_hbm, wg_hbm, wu_hbm, wd_hbm,
                    out_ref, xbuf, stage_g, stage_u, stage_d, wg_bf, wu_bf, wd_bf, xsem, wsem, *, layer):
    b = pl.program_id(0)
    nu = nu_ref[0]
    nslot, blk = xbuf.shape[0], xbuf.shape[1]

    def start_rows(idx_ref, dst_slot):
        for r in range(blk):
            pltpu.make_async_copy(h_hbm.at[pl.ds(idx_ref[0, 0, r], 1)],
                                  xbuf.at[dst_slot, pl.ds(r, 1)], xsem.at[dst_slot]).start(priority=0)

    def wait_rows(dst_slot):
        pltpu.make_async_copy(h_hbm.at[pl.ds(0, blk)], xbuf.at[dst_slot], xsem.at[dst_slot]).wait()

    def weight_copies(e):
        copies = []
        for k, (w_hbm, stage) in enumerate(((wg_hbm, stage_g), (wu_hbm, stage_u), (wd_hbm, stage_d))):
            rows = stage.shape[0] // WEIGHT_CHUNKS
            for c in range(WEIGHT_CHUNKS):
                sl = pl.ds(c * rows, rows)
                copies.append(pltpu.make_async_copy(w_hbm.at[layer, e, sl], stage.at[sl], wsem.at[k]))
        return copies

    def start_weights(e):
        for n, cp in enumerate(weight_copies(e)):
            cp.start(priority=n % 2)

    @pl.when(b == 0)
    def _():
        start_weights(be_ref[0])
        start_rows(src0_ref, 0)
        start_rows(src1_ref, 1)

    changed = _expert_changed(be_ref, b)

    def compute(slot, new_expert):
        if new_expert:
            for cp in weight_copies(be_ref[b]):
                cp.wait()
        wait_rows(slot)
        x = _unpack_bf16_pairs(xbuf[slot])
        start_rows(src2_ref, (slot + 2) % nslot)
        if new_expert:
            wg_bf[...] = stage_g[...].astype(BF16)
        hg = jnp.dot(x, wg_bf[...], preferred_element_type=F32)
        if new_expert:
            wu_bf[...] = stage_u[...].astype(BF16)
        hu = jnp.dot(x, wu_bf[...], preferred_element_type=F32)
        if new_expert:
            wd_bf[...] = stage_d[...].astype(BF16)
        hid = (jax.nn.silu(hg) * hu).astype(BF16)
        out_ref[...] = jnp.dot(hid, wd_bf[...], preferred_element_type=F32)

        if new_expert:
            @pl.when(ne_ref[b] >= 0)
            def _():
                start_weights(ne_ref[b])

        @pl.when(b == nu - 1)
        def _():
            wait_rows((slot + 1) % nslot)
            wait_rows((slot + 2) % nslot)

    for slot in range(nslot):
        here = (b < nu) & (b % nslot == slot)
        pl.when(here & changed)(functools.partial(compute, slot, True))
        pl.when(here & jnp.logical_not(changed))(functools.partial(compute, slot, False))

    @pl.when(b >= nu)
    def _():
        out_ref[...] = jnp.zeros_like(out_ref)


def _combine_kernel(*refs, final):
    if final:
        pos_ref, nxt_ref, x_ref, route_ref, eo_hbm, g_ref, o_ref, ybuf, sem = refs
    else:
        pos_ref, nxt_ref, x_ref, route_ref, eo_hbm, o_ref, ybuf, sem = refs
    i = pl.program_id(0)
    last = pl.num_programs(0) - 1
    tm = x_ref.shape[0]

    def start_rows(idx_ref, dst_slot):
        for r in range(tm):
            for k in range(TOP_K):
                pltpu.make_async_copy(eo_hbm.at[pl.ds(idx_ref[0, 0, k * tm + r], 1)],
                                      ybuf.at[dst_slot, k, pl.ds(r, 1)], sem.at[dst_slot]).start(priority=k)

    def wait_rows(dst_slot):
        for k in range(TOP_K):
            pltpu.make_async_copy(eo_hbm.at[pl.ds(0, tm)], ybuf.at[dst_slot, k], sem.at[dst_slot]).wait()

    @pl.when(i == 0)
    def _():
        start_rows(pos_ref, 0)

    def step(slot):
        start_rows(nxt_ref, 1 - slot)
        wait_rows(slot)
        route = route_ref[0].T
        y = ybuf[slot, 0] * route[:, TOP_K:TOP_K + 1] + ybuf[slot, 1] * route[:, TOP_K + 1:TOP_K + 2]
        x = x_ref[...] + y
        o_ref[...] = _rms(x, g_ref[...]) if final else x

        @pl.when(i == last)
        def _():
            wait_rows(1 - slot)

    for slot in range(2):
        pl.when(i % 2 == slot)(functools.partial(step, slot))


def _moe_dispatch(route, counts_slab, t):
    m = t * TOP_K
    nt, _, tm = route.shape
    expert = route[:, :TOP_K, :].astype(jnp.int32)
    rank = route[:, 2 * TOP_K:3 * TOP_K, :].astype(jnp.int32)
    counts = counts_slab[0, N_GROUPS:N_GROUPS + N_EXPERTS].astype(jnp.int32)
    padded = (counts + MOE_BLOCK - 1) // MOE_BLOCK * MOE_BLOCK
    padded_ends = jnp.cumsum(padded)
    padded_starts = padded_ends - padded
    ids = jnp.arange(N_EXPERTS, dtype=jnp.int32)
    start_of = jnp.sum(jnp.where(expert[..., None] == ids, padded_starts, 0), axis=-1)
    dest = start_of + rank
    n_blocks = -(-(m + N_EXPERTS * (MOE_BLOCK - 1)) // MOE_BLOCK)
    cap = n_blocks * MOE_BLOCK
    token = (jnp.arange(nt, dtype=jnp.int32)[:, None, None] * tm
             + jnp.arange(tm, dtype=jnp.int32)[None, None, :])
    token = jnp.broadcast_to(token, dest.shape)
    assert cap == m + N_EXPERTS * MOE_BLOCK
    pad_i = jnp.arange(MOE_BLOCK, dtype=jnp.int32)[None, :]
    pad_row = jnp.where(pad_i < (padded - counts)[:, None],
                        (padded_starts + counts)[:, None] + pad_i,
                        cap + ids[:, None] * MOE_BLOCK + pad_i)
    rows = jnp.concatenate([dest.reshape(m), pad_row.reshape(-1)])
    toks = jnp.concatenate([token.reshape(m), jnp.zeros((N_EXPERTS * MOE_BLOCK,), jnp.int32)])
    _, buf_src = lax.sort_key_val(rows, toks)
    block_start = jnp.arange(n_blocks, dtype=jnp.int32) * MOE_BLOCK
    block_expert = jnp.minimum(
        jnp.sum((padded_ends[None, :] <= block_start[:, None]).astype(jnp.int32), axis=1), N_EXPERTS - 1)
    n_used = (padded_ends[-1] // MOE_BLOCK).astype(jnp.int32).reshape(1)
    ids = jnp.arange(N_EXPERTS, dtype=jnp.int32)
    later = (ids[None, :] > ids[:, None]) & (counts[None, :] > 0)
    next_active = jnp.min(jnp.where(later, ids[None, :], N_EXPERTS), axis=1)
    next_active = jnp.where(next_active < N_EXPERTS, next_active, -1)
    block_next = next_active[block_expert]
    return block_expert, n_used, block_next, buf_src.reshape(n_blocks, 1, MOE_BLOCK), dest


def _moe_experts(h_packed, block_expert, n_used, block_next, buf_src, w_gate, w_up, w_down, layer):
    d, ff = w_gate.shape[-2:]
    n_blocks = buf_src.shape[0]
    cap = n_blocks * MOE_BLOCK
    idx_block = (1, 1, MOE_BLOCK)
    hbm = pl.BlockSpec(memory_space=pl.ANY)

    def ahead(k, b, *_):
        return (jnp.minimum(b + k, n_blocks - 1), 0, 0)

    return pl.pallas_call(
        functools.partial(_moe_ffn_kernel, layer=layer),
        grid_spec=pltpu.PrefetchScalarGridSpec(
            num_scalar_prefetch=3,
            grid=(n_blocks,),
            in_specs=[pl.BlockSpec(idx_block, lambda b, *_: (0, 0, 0), memory_space=pltpu.SMEM),
                      pl.BlockSpec(idx_block, lambda b, *_: (min(1, n_blocks - 1), 0, 0),
                                   memory_space=pltpu.SMEM),
                      pl.BlockSpec(idx_block, functools.partial(ahead, 2), memory_space=pltpu.SMEM),
                      hbm, hbm, hbm, hbm],
            out_specs=pl.BlockSpec((MOE_BLOCK, d), lambda b, *_: (b, 0)),
            scratch_shapes=[pltpu.VMEM((MOE_ROW_SLOTS, MOE_BLOCK, d // 2), jnp.uint32),
                            pltpu.VMEM((d, ff), F32), pltpu.VMEM((d, ff), F32), pltpu.VMEM((ff, d), F32),
                            pltpu.VMEM((d, ff), BF16), pltpu.VMEM((d, ff), BF16), pltpu.VMEM((ff, d), BF16),
                            pltpu.SemaphoreType.DMA((MOE_ROW_SLOTS,)), pltpu.SemaphoreType.DMA((3,))],
        ),
        out_shape=jax.ShapeDtypeStruct((cap, d), F32),
        compiler_params=pltpu.CompilerParams(dimension_semantics=("arbitrary",),
                                             vmem_limit_bytes=MOE_VMEM_LIMIT),
        name="moe_ffn",
    )(block_expert, n_used, block_next, buf_src, buf_src, buf_src, h_packed, w_gate, w_up, w_down)


def _combine(x, route, dest, expert_out, final_gain):
    t, d = x.shape
    nt, _, tm = route.shape
    final = final_gain is not None
    pos = dest.reshape(nt, 1, TOP_K * tm)
    idx_block = (1, 1, TOP_K * tm)
    row = pl.BlockSpec((tm, d), lambda i: (i, 0))
    in_specs = [pl.BlockSpec(idx_block, lambda i: (i, 0, 0), memory_space=pltpu.SMEM),
                pl.BlockSpec(idx_block, lambda i: (jnp.minimum(i + 1, nt - 1), 0, 0),
                             memory_space=pltpu.SMEM),
                row, pl.BlockSpec((1, ROUTE_FIELDS, tm), lambda i: (i, 0, 0)),
                pl.BlockSpec(memory_space=pl.ANY)]
    args = [pos, pos, x, route, expert_out]
    if final:
        in_specs.append(pl.BlockSpec((1, d), lambda i: (0, 0)))
        args.append(final_gain.reshape(1, d))
    return pl.pallas_call(
        functools.partial(_combine_kernel, final=final),
        grid=(nt,),
        in_specs=in_specs, out_specs=row,
        out_shape=jax.ShapeDtypeStruct((t, d), F32),
        scratch_shapes=[pltpu.VMEM((2, TOP_K, tm, d), F32), pltpu.SemaphoreType.DMA((2,))],
        compiler_params=_params("arbitrary"),
        name="moe_combine",
    )(*args)


def kernel(x, attn_norm, ffn_norm, final_norm, w_in_ab, ret_norm, gmlp_norm, gmlp_ws, gmlp_bs, w_out_ab, w_in_c, lb_params, hgrn_norm, w_out_c, router_w_group, router_b_group, router_w_expert, router_b_expert, w_gate, w_up, w_down):
    b, s, d = x.shape
    assert b == 1, "the sequence mixers carry state along the flattened token axis"
    depth = attn_norm.shape[0]
    lb_soft = jax.nn.softmax(lb_params.astype(F32), axis=0)
    lower_bounds = jnp.cumsum(lb_soft, axis=0) - lb_soft[0]
    xt = x.reshape(b * s, d)
    t = b * s
    for layer in range(depth):
        i = layer // 2
        if layer % 2 == 0:
            proj = _norm_matmul(xt, attn_norm[layer], w_in_ab[i].astype(BF16))
            ret = _retention(proj, ret_norm[i])
            gm = _gmlp(proj, gmlp_norm[i], gmlp_ws[i], gmlp_bs[i])
            w_out = w_out_ab[i].astype(BF16)
            nr = ret.shape[1]
            acts, w_outs = [ret, gm], [w_out[:nr], w_out[nr:]]
        else:
            proj = _norm_matmul(xt, attn_norm[layer], w_in_c[i].astype(BF16))
            acts = [_hgrn2(proj, lower_bounds[layer], hgrn_norm[i])]
            w_outs = [w_out_c[i].astype(BF16)]
        xt, h_packed, route, counts = _outproj_router(
            xt, acts, w_outs, ffn_norm[layer], router_w_group[layer], router_b_group[layer],
            router_w_expert[layer], router_b_expert[layer])
        block_expert, n_used, block_next, buf_src, dest = _moe_dispatch(route, counts, t)
        expert_out = _moe_experts(h_packed, block_expert, n_used, block_next, buf_src,
                                  w_gate, w_up, w_down, layer)
        xt = _combine(xt, route, dest, expert_out, final_norm if layer == depth - 1 else None)
    return xt.reshape(b, s, d)
```

```python
import functools

import jax
import jax.numpy as jnp
from jax import lax
from jax.experimental import pallas as pl
from jax.experimental.pallas import tpu as pltpu

F32 = jnp.float32
BF16 = jnp.bfloat16
EPS = 1e-6

RET_HEADS = 4
RET_CHUNK = 128
ROPE_BASE = 10000.0
GMLP_GROUPS = 4
GMLP_CHUNK = 128
HGRN_DK = 128
HGRN_CHUNK = 32
N_GROUPS = 4
EXPERTS_PER_GROUP = 8
N_EXPERTS = N_GROUPS * EXPERTS_PER_GROUP
TOP_K = 2
MOE_BLOCK = 128
MOE_ROW_SLOTS = 4
ROUTE_LANES = 128
NORM_SLAB = 256
ROUTE_FIELDS = 8
TOKEN_TILE = 256

VMEM_LIMIT = 48 * 1024 * 1024
MOE_VMEM_LIMIT = 56 * 1024 * 1024


def _params(*sem):
    return pltpu.CompilerParams(dimension_semantics=sem, vmem_limit_bytes=VMEM_LIMIT)


def _rms(x, g):
    return x * lax.rsqrt(jnp.mean(x * x, axis=-1, keepdims=True) + EPS) * g


def _norm_matmul_kernel(x_ref, g_ref, w_ref, proj_ref, xn_ref):
    @pl.when(pl.program_id(1) == 0)
    def _():
        for r0 in range(0, x_ref.shape[0], NORM_SLAB):
            rows = pl.ds(r0, NORM_SLAB)
            xn_ref[rows, :] = _rms(x_ref[rows, :], g_ref[...]).astype(BF16)

    proj_ref[...] = jnp.dot(xn_ref[...], w_ref[...],
                            preferred_element_type=F32).astype(proj_ref.dtype)


def _norm_matmul(x, g, w, *, tm=1024, tn=1024):
    t, d = x.shape
    n = w.shape[1]
    return pl.pallas_call(
        _norm_matmul_kernel,
        grid=(t // tm, n // tn),
        in_specs=[pl.BlockSpec((tm, d), lambda i, j: (i, 0)),
                  pl.BlockSpec((1, d), lambda i, j: (0, 0)),
                  pl.BlockSpec((d, tn), lambda i, j: (0, j))],
        out_specs=pl.BlockSpec((tm, tn), lambda i, j: (i, j)),
        out_shape=jax.ShapeDtypeStruct((t, n), BF16),
        scratch_shapes=[pltpu.VMEM((tm, d), BF16)],
        compiler_params=_params("arbitrary", "arbitrary"),
        name="norm_matmul",
    )(x, g.reshape(1, d), w)


def _rope(x, cos, sin):
    half = x.shape[-1] // 2
    x1, x2 = x[:, :half], x[:, half:]
    return jnp.concatenate([x1 * cos - x2 * sin, x2 * cos + x1 * sin], axis=-1)


def _retention_kernel(cd_ref, q_ref, k_ref, v_ref, g_ref, cos_ref, sin_ref, dint_ref, qd_ref, kd_ref,
                      gain_ref, o_ref, state_ref, *, chunk, nchunk):
    @pl.when(pl.program_id(1) == 0)
    def _():
        state_ref[...] = jnp.zeros_like(state_ref)

    dk = q_ref.shape[-1]
    dint = dint_ref[0]
    qd = qd_ref[0]
    kd = kd_ref[0]
    cd = cd_ref[pl.program_id(0)]
    gain = gain_ref[...]
    for c in range(nchunk):
        rows = pl.ds(c * chunk, chunk)
        cos = cos_ref[rows, :]
        sin = sin_ref[rows, :]
        q = _rope(q_ref[rows, :].astype(F32), cos, sin)
        k = _rope(k_ref[rows, :].astype(F32), cos, sin) * (dk ** -0.5)
        v = v_ref[rows, :]
        scores = lax.dot_general(q.astype(BF16), k.astype(BF16), (((1,), (1,)), ((), ())),
                                 preferred_element_type=F32) * dint
        state = state_ref[...]
        o = (jnp.dot(scores.astype(BF16), v, preferred_element_type=F32)
             + jnp.dot((q * qd).astype(BF16), state.astype(BF16), preferred_element_type=F32))
        state_ref[...] = cd * state + lax.dot_general(
            (k * kd).astype(BF16), v, (((0,), (0,)), ((), ())), preferred_element_type=F32)
        o = o - jnp.mean(o, axis=-1, keepdims=True)
        o = o * lax.rsqrt(jnp.mean(o * o, axis=-1, keepdims=True) + EPS) * gain
        o_ref[rows, :] = (jax.nn.silu(g_ref[rows, :].astype(F32)) * o).astype(o_ref.dtype)


def _retention(proj, gain, *, rows=1024):
    t = proj.shape[0]
    h = RET_HEADS
    dk = proj.shape[1] // 6 // h
    dv = dk
    c = RET_CHUNK
    f32 = F32
    inv = ROPE_BASE ** (-jnp.arange(0, dk, 2, dtype=f32) / dk)
    ang = jnp.arange(t, dtype=f32)[:, None] * inv[None, :]
    cos, sin = jnp.cos(ang), jnp.sin(ang)
    log_gamma = jnp.log(1.0 - jnp.exp2(-5.0 - jnp.arange(h, dtype=f32)))
    idx = jnp.arange(c, dtype=f32)
    diff = idx[:, None] - idx[None, :]
    d_intra = jnp.where(diff >= 0, jnp.exp(jnp.maximum(diff, 0.0) * log_gamma[:, None, None]), 0.0)
    q_decay = jnp.broadcast_to(jnp.exp((idx + 1.0)[None, :] * log_gamma[:, None])[..., None], (h, c, dk))
    k_decay = jnp.broadcast_to(jnp.exp((c - 1.0 - idx)[None, :] * log_gamma[:, None])[..., None], (h, c, dk))
    chunk_decay = jnp.exp(c * log_gamma)

    def col(off):
        return pl.BlockSpec((rows, dk), lambda hh, i: (i, off + hh))

    tab = pl.BlockSpec((rows, dk // 2), lambda hh, i: (i, 0))
    per_head = lambda shp: pl.BlockSpec((1,) + shp, lambda hh, i: (hh, 0, 0))
    return pl.pallas_call(
        functools.partial(_retention_kernel, chunk=c, nchunk=rows // c),
        grid=(h, t // rows),
        in_specs=[pl.BlockSpec(memory_space=pltpu.SMEM),
                  col(0), col(h), col(2 * h), col(3 * h), tab, tab,
                  per_head((c, c)), per_head((c, dk)), per_head((c, dk)),
                  pl.BlockSpec((1, dv), lambda hh, i: (0, hh))],
        out_specs=pl.BlockSpec((rows, dv), lambda hh, i: (i, hh)),
        out_shape=jax.ShapeDtypeStruct((t, h * dv), BF16),
        scratch_shapes=[pltpu.VMEM((dk, dv), F32)],
        compiler_params=_params("arbitrary", "arbitrary"),
        name="retention",
    )(chunk_decay, proj, proj, proj, proj, cos, sin, d_intra, q_decay, k_decay, gain.reshape(1, h * dv))


def _gmlp_kernel(u_ref, vs_ref, gain_ref, ws_ref, bs_ref, o_ref, *, chunk, nchunk):
    r = lax.broadcasted_iota(jnp.int32, (chunk, chunk), 0)
    s = lax.broadcasted_iota(jnp.int32, (chunk, chunk), 1)
    w = jnp.where(r >= s, ws_ref[0], 0.0).astype(BF16)
    gain = gain_ref[...]
    bs = bs_ref[0]
    for c in range(nchunk):
        rows = pl.ds(c * chunk, chunk)
        v = jax.nn.gelu(vs_ref[rows, :].astype(F32))
        v = v - jnp.mean(v, axis=-1, keepdims=True)
        v = v * lax.rsqrt(jnp.mean(v * v, axis=-1, keepdims=True) + EPS) * gain
        mixed = jnp.dot(w, v.astype(BF16), preferred_element_type=F32) + bs
        o_ref[rows, :] = (jax.nn.gelu(u_ref[rows, :].astype(F32)) * mixed).astype(o_ref.dtype)


def _gmlp(proj, gain, ws, bs, *, rows=1024):
    t = proj.shape[0]
    g = GMLP_GROUPS
    dim = proj.shape[1] // 6 // g
    c = GMLP_CHUNK
    return pl.pallas_call(
        functools.partial(_gmlp_kernel, chunk=c, nchunk=rows // c),
        grid=(g, t // rows),
        in_specs=[pl.BlockSpec((rows, dim), lambda gg, i: (i, 4 * g + gg)),
                  pl.BlockSpec((rows, dim), lambda gg, i: (i, 5 * g + gg)),
                  pl.BlockSpec((1, dim), lambda gg, i: (0, gg)),
                  pl.BlockSpec((1, c, c), lambda gg, i: (gg, 0, 0)),
                  pl.BlockSpec((1, c, 1), lambda gg, i: (gg, 0, 0))],
        out_specs=pl.BlockSpec((rows, dim), lambda gg, i: (i, gg)),
        out_shape=jax.ShapeDtypeStruct((t, g * dim), BF16),
        compiler_params=_params("arbitrary", "arbitrary"),
        name="gmlp",
    )(proj, proj, gain.reshape(1, g * dim), ws, bs.reshape(g, c, 1))


def _hgrn_kernel(zq_ref, zf_ref, zi_ref, zg_ref, lb_ref, gain_ref, tri_ref, keep_ref, o_ref, state_ref,
                 *, chunk, nchunk, heads, dk):
    @pl.when(pl.program_id(1) == 0)
    def _():
        state_ref[...] = jnp.zeros_like(state_ref)

    rows = chunk * nchunk
    nt = (((1,), (1,)), ((), ()))
    tri = tri_ref[...]
    keep = keep_ref[...] > 0.0
    chunk_of_row = lax.broadcasted_iota(jnp.int32, (rows, dk), 0) // chunk
    hw = heads * dk
    head_cols = [slice(hd * dk, (hd + 1) * dk) for hd in range(heads)]
    lb = lb_ref[...]
    zf = zf_ref[...].astype(F32)
    f = lb + (1.0 - lb) * jax.nn.sigmoid(zf)
    kk = (1.0 - lb) * jax.nn.sigmoid(-zf)
    log_f = jnp.log(f)
    p0 = log_f.astype(BF16)
    r0 = log_f - p0.astype(F32)
    p1 = r0.astype(BF16)
    p2 = (r0 - p1.astype(F32)).astype(BF16)
    cum = (jnp.dot(tri, p0, preferred_element_type=F32)
           + jnp.dot(tri, p1, preferred_element_type=F32)
           + jnp.dot(tri, p2, preferred_element_type=F32))
    lasts = [cum[(c + 1) * chunk - 1:(c + 1) * chunk, :] for c in range(nchunk)]
    last_rows = jnp.concatenate([jnp.broadcast_to(l, (chunk, hw)) for l in lasts], axis=0)
    decays = [jnp.exp(l) for l in lasts]
    q_dec = (jax.nn.silu(zq_ref[...].astype(F32)) * jnp.exp(cum)).astype(BF16)
    k_dec = (kk * jnp.exp(-cum)).astype(BF16)
    k_out = (kk * jnp.exp(last_rows - cum)).astype(BF16)
    v = zi_ref[...]
    gate = jax.nn.silu(zg_ref[...].astype(F32))
    scores = [lax.dot_general(q_dec[:, cs], k_dec[:, cs], nt, preferred_element_type=F32)
              for cs in head_cols]
    scores = [jnp.where(keep, s, 0.0).astype(BF16) for s in scores]
    o_intra = [jnp.dot(s, v[:, cs], preferred_element_type=F32) for s, cs in zip(scores, head_cols)]
    v_t = [v[:, cs].astype(F32).T.astype(BF16) for cs in head_cols]
    zero = jnp.zeros((rows, dk), BF16)
    k_blocks = [jnp.concatenate([jnp.where(chunk_of_row == c, k_out[:, cs], zero) for c in range(nchunk)],
                                axis=1) for cs in head_cols]
    incr = [jnp.dot(v_t[hd], k_blocks[hd], preferred_element_type=F32) for hd in range(heads)]
    states = [state_ref[hd] for hd in range(heads)]
    inter = [[] for _ in range(heads)]
    for c in range(nchunk):
        sl = slice(c * chunk, (c + 1) * chunk)
        for hd, cs in enumerate(head_cols):
            inter[hd].append(lax.dot_general(q_dec[sl, cs], states[hd].astype(BF16), nt,
                                             preferred_element_type=F32))
            states[hd] = decays[c][:, cs] * states[hd] + incr[hd][:, c * dk:(c + 1) * dk]
    for hd, cs in enumerate(head_cols):
        state_ref[hd] = states[hd]
        o = o_intra[hd] + jnp.concatenate(inter[hd], axis=0)
        o = o * lax.rsqrt(jnp.mean(o * o, axis=-1, keepdims=True) + EPS) * gain_ref[:, cs]
        o_ref[:, cs] = (o * gate[:, cs]).astype(o_ref.dtype)


def _hgrn2(proj, lb, gain, *, rows=256, heads=4):
    t = proj.shape[0]
    width = proj.shape[1] // 4
    dk = HGRN_DK
    h = width // dk
    c = HGRN_CHUNK
    hw = heads * dk
    groups = h // heads
    r = jnp.arange(rows, dtype=jnp.int32)
    tri = ((r[:, None] >= r[None, :]) & ((r[:, None] // c) == (r[None, :] // c))).astype(F32)

    def col(off):
        return pl.BlockSpec((rows, hw), lambda hh, i: (i, off + hh))

    vec = pl.BlockSpec((1, hw), lambda hh, i: (0, hh))
    mask = pl.BlockSpec((rows, rows), lambda hh, i: (0, 0))
    return pl.pallas_call(
        functools.partial(_hgrn_kernel, chunk=c, nchunk=rows // c, heads=heads, dk=dk),
        grid=(groups, t // rows),
        in_specs=[col(0), col(groups), col(2 * groups), col(3 * groups), vec, vec, mask, mask],
        out_specs=pl.BlockSpec((rows, hw), lambda hh, i: (i, hh)),
        out_shape=jax.ShapeDtypeStruct((t, width), BF16),
        scratch_shapes=[pltpu.VMEM((heads, dk, dk), F32)],
        compiler_params=_params("arbitrary", "arbitrary"),
        name="hgrn2",
    )(proj, proj, proj, proj, lb.reshape(1, width), gain.reshape(1, width), tri.astype(BF16), tri)


def _pack_bf16_pairs(h):
    half = h.shape[1] // 2
    bits = pltpu.bitcast(h.astype(BF16).astype(F32), jnp.uint32)
    return (bits[:, :half] >> 16) | bits[:, half:]


def _unpack_bf16_pairs(words):
    lo = pltpu.bitcast(words << 16, F32)
    hi = pltpu.bitcast(words & jnp.uint32(0xFFFF0000), F32)
    return jnp.concatenate([lo, hi], axis=1).astype(BF16)


def _route(logits, run):
    lane = lax.broadcasted_iota(jnp.int32, logits.shape, 1)
    lane_f = lane.astype(F32)
    neg = -jnp.inf
    big = float(ROUTE_LANES)
    lg = jnp.where(lane < N_GROUPS, logits, neg)
    mg = jnp.max(lg, axis=-1, keepdims=True)
    eg = jnp.exp(lg - mg)
    pg = eg / jnp.sum(eg, axis=-1, keepdims=True)
    p_sel = jnp.max(pg, axis=-1, keepdims=True)
    g_sel = jnp.min(jnp.where(lg == mg, lane_f, big), axis=-1, keepdims=True)
    e_grp = ((lane - N_GROUPS) // EXPERTS_PER_GROUP).astype(F32)
    in_grp = jnp.where(lane >= N_GROUPS, e_grp, -1.0) == g_sel
    le = jnp.where(in_grp, logits, neg)
    t1 = jnp.max(le, axis=-1, keepdims=True)
    i1 = jnp.min(jnp.where(le == t1, lane_f, big), axis=-1, keepdims=True)
    le2 = jnp.where(lane_f == i1, neg, le)
    t2 = jnp.max(le2, axis=-1, keepdims=True)
    i2 = jnp.min(jnp.where(le2 == t2, lane_f, big), axis=-1, keepdims=True)
    e2 = jnp.exp(t2 - t1)
    den = 1.0 + e2
    gate1 = p_sel * (1.0 / den)
    gate2 = p_sel * (e2 / den)
    tm = logits.shape[0]
    hit1 = jnp.where(lane_f == i1, 1.0, 0.0)
    hit2 = jnp.where(lane_f == i2, 1.0, 0.0)
    hits = hit1 + hit2
    r = lax.broadcasted_iota(jnp.int32, (tm, tm), 0)
    s = lax.broadcasted_iota(jnp.int32, (tm, tm), 1)
    before = jnp.where(r > s, 1.0, 0.0).astype(BF16)
    prefix = jnp.dot(before, hits.astype(BF16), preferred_element_type=F32) + run
    rank1 = jnp.sum(prefix * hit1, axis=-1, keepdims=True)
    rank2 = jnp.sum(prefix * hit2, axis=-1, keepdims=True)
    vals = (i1 - N_GROUPS, i2 - N_GROUPS, gate1, gate2, rank1, rank2)
    slab = jnp.zeros_like(logits)
    for pos, val in enumerate(vals):
        slab = jnp.where(lane == pos, val, slab)
    return slab, run + jnp.sum(hits, axis=0, keepdims=True)


def _outproj_router_kernel(*refs, n_act):
    x_ref = refs[0]
    a_refs = refs[1:1 + n_act]
    w_refs = refs[1 + n_act:1 + 2 * n_act]
    g_ref, wr_ref, br_ref, xo_ref, hp_ref, route_ref, count_ref, run_ref, xs_ref = refs[1 + 2 * n_act:]
    i = pl.program_id(0)

    @pl.when(i == 0)
    def _():
        run_ref[...] = jnp.zeros_like(run_ref)
        xs_ref[1] = jnp.zeros(xs_ref.shape[1:], xs_ref.dtype)

    def step(slot):
        x = x_ref[...]
        for a_ref, w_ref in zip(a_refs, w_refs):
            x = x + jnp.dot(a_ref[...], w_ref[...], preferred_element_type=F32)
        xo_ref[...] = x
        xs_ref[slot] = x

        x_prev = xs_ref[1 - slot]
        h = _rms(x_prev, g_ref[...])
        hp_ref[...] = _pack_bf16_pairs(h)
        h_hi = h.astype(BF16)
        h_lo = (h - h_hi.astype(F32)).astype(BF16)
        p = (jnp.dot(h_hi, wr_ref[...], preferred_element_type=F32)
             + jnp.dot(h_lo, wr_ref[...], preferred_element_type=F32))
        logits = p[:, :ROUTE_LANES] + p[:, ROUTE_LANES:] + br_ref[...]
        run = run_ref[...]
        slab, run_next = _route(logits, run)
        route_ref[0] = slab.T[:ROUTE_FIELDS, :]
        run = jnp.where(i >= 1, run_next, run)
        run_ref[...] = run
        count_ref[...] = run

    for slot in range(2):
        pl.when(i % 2 == slot)(functools.partial(step, slot))


def _outproj_router(x, acts, ws, g, w_rg, b_rg, w_re, b_re, *, tm=TOKEN_TILE):
    t, d = x.shape
    used = N_GROUPS + N_EXPERTS
    wr = jnp.zeros((d, ROUTE_LANES), F32).at[:, :N_GROUPS].set(w_rg).at[:, N_GROUPS:used].set(w_re)
    wr_hi = wr.astype(BF16)
    wr_lo = (wr - wr_hi.astype(F32)).astype(BF16)
    wr2 = jnp.concatenate([wr_hi, wr_lo], axis=1)
    br = jnp.zeros((1, ROUTE_LANES), F32).at[0, :N_GROUPS].set(b_rg).at[0, N_GROUPS:used].set(b_re)
    nt = t // tm
    cur = lambda i: (jnp.minimum(i, nt - 1), 0)
    prev = lambda i: (jnp.maximum(i - 1, 0), 0)
    row = pl.BlockSpec((tm, d), cur)
    const = lambda a: pl.BlockSpec(a.shape, lambda i: (0, 0))
    in_specs = [row]
    in_specs += [pl.BlockSpec((tm, a.shape[1]), cur) for a in acts]
    in_specs += [const(w) for w in ws]
    in_specs += [pl.BlockSpec((1, d), lambda i: (0, 0)), const(wr2), const(br)]
    return pl.pallas_call(
        functools.partial(_outproj_router_kernel, n_act=len(acts)),
        grid=(nt + 1,),
        in_specs=in_specs,
        out_specs=[row, pl.BlockSpec((tm, d // 2), prev),
                   pl.BlockSpec((1, ROUTE_FIELDS, tm), lambda i: prev(i) + (0,)),
                   pl.BlockSpec((1, ROUTE_LANES), lambda i: (0, 0))],
        out_shape=[jax.ShapeDtypeStruct((t, d), F32), jax.ShapeDtypeStruct((t, d // 2), jnp.uint32),
                   jax.ShapeDtypeStruct((nt, ROUTE_FIELDS, tm), F32),
                   jax.ShapeDtypeStruct((1, ROUTE_LANES), F32)],
        scratch_shapes=[pltpu.VMEM((1, ROUTE_LANES), F32), pltpu.VMEM((2, tm, d), F32)],
        compiler_params=_params("arbitrary"),
        name="outproj_router",
    )(x, *acts, *ws, g.reshape(1, d), wr2, br)


def _expert_changed(be_ref, b):
    return (b == 0) | (be_ref[b] != be_ref[jnp.maximum(b - 1, 0)])


def _moe_ffn_kernel(be_ref, nu_ref, ne_ref, *refs, layer):
    nslot = MOE_ROW_SLOTS
    first_refs, ahead_ref = refs[:nslot - 1], refs[nslot - 1]
    (h_hbm, wg_hbm, wu_hbm, wd_hbm, out_ref,
     xbuf, stage_g, stage_u, stage_d, wg_bf, wu_bf, wd_bf, xsem, wsem) = refs[nslot:]
    b = pl.program_id(0)
    nu = nu_ref[0]
    blk = xbuf.shape[1]

    def start_rows(idx_ref, dst_slot):
        for r in range(blk):
            pltpu.make_async_copy(h_hbm.at[pl.ds(idx_ref[0, 0, r], 1)],
                                  xbuf.at[dst_slot, pl.ds(r, 1)], xsem.at[dst_slot]).start(priority=0)

    def wait_rows(dst_slot):
        pltpu.make_async_copy(h_hbm.at[pl.ds(0, blk)], xbuf.at[dst_slot], xsem.at[dst_slot]).wait()

    def weight_copies(e):
        return (pltpu.make_async_copy(wg_hbm.at[layer, e], stage_g, wsem.at[0]),
                pltpu.make_async_copy(wu_hbm.at[layer, e], stage_u, wsem.at[1]),
                pltpu.make_async_copy(wd_hbm.at[layer, e], stage_d, wsem.at[2]))

    def start_weights(e):
        for cp in weight_copies(e):
            cp.start(priority=1)

    @pl.when(b == 0)
    def _():
        start_weights(be_ref[0])
        for slot, idx_ref in enumerate(first_refs):
            start_rows(idx_ref, slot)

    changed = _expert_changed(be_ref, b)

    def compute(slot, new_expert):
        if new_expert:
            for cp in weight_copies(be_ref[b]):
                cp.wait()
        wait_rows(slot)
        x = _unpack_bf16_pairs(xbuf[slot])
        start_rows(ahead_ref, (slot + nslot - 1) % nslot)
        if new_expert:
            wg_bf[...] = stage_g[...].astype(BF16)
        hg = jnp.dot(x, wg_bf[...], preferred_element_type=F32)
        if new_expert:
            wu_bf[...] = stage_u[...].astype(BF16)
        hu = jnp.dot(x, wu_bf[...], preferred_element_type=F32)
        if new_expert:
            wd_bf[...] = stage_d[...].astype(BF16)
        hid = (jax.nn.silu(hg) * hu).astype(BF16)
        out_ref[...] = jnp.dot(hid, wd_bf[...], preferred_element_type=F32)

        if new_expert:
            @pl.when(ne_ref[b] >= 0)
            def _():
                start_weights(ne_ref[b])

        @pl.when(b == nu - 1)
        def _():
            for k in range(1, nslot):
                wait_rows((slot + k) % nslot)

    for slot in range(nslot):
        here = (b < nu) & (b % nslot == slot)
        pl.when(here & changed)(functools.partial(compute, slot, True))
        pl.when(here & jnp.logical_not(changed))(functools.partial(compute, slot, False))

    @pl.when(b >= nu)
    def _():
        out_ref[...] = jnp.zeros_like(out_ref)


def _combine_kernel(*refs, final):
    if final:
        pos_ref, nxt_ref, x_ref, route_ref, eo_hbm, g_ref, o_ref, ybuf, sem = refs
    else:
        pos_ref, nxt_ref, x_ref, route_ref, eo_hbm, o_ref, ybuf, sem = refs
    i = pl.program_id(0)
    last = pl.num_programs(0) - 1
    tm = x_ref.shape[0]

    def start_rows(idx_ref, dst_slot):
        for r in range(tm):
            for k in range(TOP_K):
                pltpu.make_async_copy(eo_hbm.at[pl.ds(idx_ref[0, 0, k * tm + r], 1)],
                                      ybuf.at[dst_slot, k, pl.ds(r, 1)], sem.at[dst_slot]).start(priority=k)

    def wait_rows(dst_slot):
        for k in range(TOP_K):
            pltpu.make_async_copy(eo_hbm.at[pl.ds(0, tm)], ybuf.at[dst_slot, k], sem.at[dst_slot]).wait()

    @pl.when(i == 0)
    def _():
        start_rows(pos_ref, 0)

    def step(slot):
        start_rows(nxt_ref, 1 - slot)
        wait_rows(slot)
        route = route_ref[0].T
        y = ybuf[slot, 0] * route[:, TOP_K:TOP_K + 1] + ybuf[slot, 1] * route[:, TOP_K + 1:TOP_K + 2]
        x = x_ref[...] + y
        o_ref[...] = _rms(x, g_ref[...]) if final else x

        @pl.when(i == last)
        def _():
            wait_rows(1 - slot)

    for slot in range(2):
        pl.when(i % 2 == slot)(functools.partial(step, slot))


def _moe_dispatch(route, counts_slab, t):
    m = t * TOP_K
    nt, _, tm = route.shape
    expert = route[:, :TOP_K, :].astype(jnp.int32)
    rank = route[:, 2 * TOP_K:3 * TOP_K, :].astype(jnp.int32)
    counts = counts_slab[0, N_GROUPS:N_GROUPS + N_EXPERTS].astype(jnp.int32)
    padded = (counts + MOE_BLOCK - 1) // MOE_BLOCK * MOE_BLOCK
    padded_ends = jnp.cumsum(padded)
    padded_starts = padded_ends - padded
    ids = jnp.arange(N_EXPERTS, dtype=jnp.int32)
    start_of = jnp.sum(jnp.where(expert[..., None] == ids, padded_starts, 0), axis=-1)
    dest = start_of + rank
    n_blocks = -(-(m + N_EXPERTS * (MOE_BLOCK - 1)) // MOE_BLOCK)
    cap = n_blocks * MOE_BLOCK
    token = (jnp.arange(nt, dtype=jnp.int32)[:, None, None] * tm
             + jnp.arange(tm, dtype=jnp.int32)[None, None, :])
    token = jnp.broadcast_to(token, dest.shape)
    assert cap == m + N_EXPERTS * MOE_BLOCK
    pad_i = jnp.arange(MOE_BLOCK, dtype=jnp.int32)[None, :]
    pad_row = jnp.where(pad_i < (padded - counts)[:, None],
                        (padded_starts + counts)[:, None] + pad_i,
                        cap + ids[:, None] * MOE_BLOCK + pad_i)
    rows = jnp.concatenate([dest.reshape(m), pad_row.reshape(-1)])
    toks = jnp.concatenate([token.reshape(m), jnp.zeros((N_EXPERTS * MOE_BLOCK,), jnp.int32)])
    _, buf_src = lax.sort_key_val(rows, toks)
    block_start = jnp.arange(n_blocks, dtype=jnp.int32) * MOE_BLOCK
    block_expert = jnp.minimum(
        jnp.sum((padded_ends[None, :] <= block_start[:, None]).astype(jnp.int32), axis=1), N_EXPERTS - 1)
    n_used = (padded_ends[-1] // MOE_BLOCK).astype(jnp.int32).reshape(1)
    ids = jnp.arange(N_EXPERTS, dtype=jnp.int32)
    later = (ids[None, :] > ids[:, None]) & (counts[None, :] > 0)
    next_active = jnp.min(jnp.where(later, ids[None, :], N_EXPERTS), axis=1)
    next_active = jnp.where(next_active < N_EXPERTS, next_active, -1)
    block_next = next_active[block_expert]
    return block_expert, n_used, block_next, buf_src.reshape(n_blocks, 1, MOE_BLOCK), dest


def _moe_experts(h_packed, block_expert, n_used, block_next, buf_src, w_gate, w_up, w_down, layer):
    d, ff = w_gate.shape[-2:]
    n_blocks = buf_src.shape[0]
    cap = n_blocks * MOE_BLOCK
    idx_block = (1, 1, MOE_BLOCK)
    hbm = pl.BlockSpec(memory_space=pl.ANY)

    look = MOE_ROW_SLOTS - 1

    def fixed(k, b, *_):
        return (min(k, n_blocks - 1), 0, 0)

    def ahead(b, *_):
        return (jnp.minimum(b + look, n_blocks - 1), 0, 0)

    idx_specs = [pl.BlockSpec(idx_block, functools.partial(fixed, k), memory_space=pltpu.SMEM)
                 for k in range(look)]
    idx_specs.append(pl.BlockSpec(idx_block, ahead, memory_space=pltpu.SMEM))
    return pl.pallas_call(
        functools.partial(_moe_ffn_kernel, layer=layer),
        grid_spec=pltpu.PrefetchScalarGridSpec(
            num_scalar_prefetch=3,
            grid=(n_blocks,),
            in_specs=idx_specs + [hbm, hbm, hbm, hbm],
            out_specs=pl.BlockSpec((MOE_BLOCK, d), lambda b, *_: (b, 0)),
            scratch_shapes=[pltpu.VMEM((MOE_ROW_SLOTS, MOE_BLOCK, d // 2), jnp.uint32),
                            pltpu.VMEM((d, ff), F32), pltpu.VMEM((d, ff), F32), pltpu.VMEM((ff, d), F32),
                            pltpu.VMEM((d, ff), BF16), pltpu.VMEM((d, ff), BF16), pltpu.VMEM((ff, d), BF16),
                            pltpu.SemaphoreType.DMA((MOE_ROW_SLOTS,)), pltpu.SemaphoreType.DMA((3,))],
        ),
        out_shape=jax.ShapeDtypeStruct((cap, d), F32),
        compiler_params=pltpu.CompilerParams(dimension_semantics=("arbitrary",),
                                             vmem_limit_bytes=MOE_VMEM_LIMIT),
        name="moe_ffn",
    )(block_expert, n_used, block_next, *([buf_src] * MOE_ROW_SLOTS), h_packed, w_gate, w_up, w_down)


def _combine(x, route, dest, expert_out, final_gain):
    t, d = x.shape
    nt, _, tm = route.shape
    final = final_gain is not None
    pos = dest.reshape(nt, 1, TOP_K * tm)
    idx_block = (1, 1, TOP_K * tm)
    row = pl.BlockSpec((tm, d), lambda i: (i, 0))
    in_specs = [pl.BlockSpec(idx_block, lambda i: (i, 0, 0), memory_space=pltpu.SMEM),
                pl.BlockSpec(idx_block, lambda i: (jnp.minimum(i + 1, nt - 1), 0, 0),
                             memory_space=pltpu.SMEM),
                row, pl.BlockSpec((1, ROUTE_FIELDS, tm), lambda i: (i, 0, 0)),
                pl.BlockSpec(memory_space=pl.ANY)]
    args = [pos, pos, x, route, expert_out]
    if final:
        in_specs.append(pl.BlockSpec((1, d), lambda i: (0, 0)))
        args.append(final_gain.reshape(1, d))
    return pl.pallas_call(
        functools.partial(_combine_kernel, final=final),
        grid=(nt,),
        in_specs=in_specs, out_specs=row,
        out_shape=jax.ShapeDtypeStruct((t, d), F32),
        scratch_shapes=[pltpu.VMEM((2, TOP_K, tm, d), F32), pltpu.SemaphoreType.DMA((2,))],
        compiler_params=_params("arbitrary"),
        name="moe_combine",
    )(*args)


def kernel(x, attn_norm, ffn_norm, final_norm, w_in_ab, ret_norm, gmlp_norm, gmlp_ws, gmlp_bs, w_out_ab, w_in_c, lb_params, hgrn_norm, w_out_c, router_w_group, router_b_group, router_w_expert, router_b_expert, w_gate, w_up, w_down):
    b, s, d = x.shape
    assert b == 1, "the sequence mixers carry state along the flattened token axis"
    depth = attn_norm.shape[0]
    lb_soft = jax.nn.softmax(lb_params.astype(F32), axis=0)
    lower_bounds = jnp.cumsum(lb_soft, axis=0) - lb_soft[0]
    xt = x.reshape(b * s, d)
    t = b * s
    for layer in range(depth):
        i = layer // 2
        if layer % 2 == 0:
            proj = _norm_matmul(xt, attn_norm[layer], w_in_ab[i].astype(BF16))
            ret = _retention(proj, ret_norm[i])
            gm = _gmlp(proj, gmlp_norm[i], gmlp_ws[i], gmlp_bs[i])
            w_out = w_out_ab[i].astype(BF16)
            nr = ret.shape[1]
            acts, w_outs = [ret, gm], [w_out[:nr], w_out[nr:]]
        else:
            proj = _norm_matmul(xt, attn_norm[layer], w_in_c[i].astype(BF16))
            acts = [_hgrn2(proj, lower_bounds[layer], hgrn_norm[i])]
            w_outs = [w_out_c[i].astype(BF16)]
        xt, h_packed, route, counts = _outproj_router(
            xt, acts, w_outs, ffn_norm[layer], router_w_group[layer], router_b_group[layer],
            router_w_expert[layer], router_b_expert[layer])
        block_expert, n_used, block_next, buf_src, dest = _moe_dispatch(route, counts, t)
        expert_out = _moe_experts(h_packed, block_expert, n_used, block_next, buf_src,
                                  w_gate, w_up, w_down, layer)
        xt = _combine(xt, route, dest, expert_out, final_norm if layer == depth - 1 else None)
    return xt.reshape(b, s, d)
```

```python
import functools

import jax
import jax.numpy as jnp
from jax import lax
from jax.experimental import pallas as pl
from jax.experimental.pallas import tpu as pltpu

F32 = jnp.float32
BF16 = jnp.bfloat16
EPS = 1e-6

RET_HEADS = 4
RET_CHUNK = 128
ROPE_BASE = 10000.0
GMLP_GROUPS = 4
GMLP_CHUNK = 128
HGRN_DK = 128
HGRN_CHUNK = 32
N_GROUPS = 4
EXPERTS_PER_GROUP = 8
N_EXPERTS = N_GROUPS * EXPERTS_PER_GROUP
TOP_K = 2
MOE_BLOCK = 128
MOE_ROW_SLOTS = 4
COMBINE_SLOTS = 3
ROUTE_LANES = 128
NORM_SLAB = 256
ROUTE_FIELDS = 8
TOKEN_TILE = 256

VMEM_LIMIT = 48 * 1024 * 1024
MOE_VMEM_LIMIT = 56 * 1024 * 1024


def _params(*sem):
    return pltpu.CompilerParams(dimension_semantics=sem, vmem_limit_bytes=VMEM_LIMIT)


def _rms(x, g):
    return x * lax.rsqrt(jnp.mean(x * x, axis=-1, keepdims=True) + EPS) * g


def _norm_matmul_kernel(x_ref, g_ref, w_ref, proj_ref, xn_ref):
    @pl.when(pl.program_id(1) == 0)
    def _():
        for r0 in range(0, x_ref.shape[0], NORM_SLAB):
            rows = pl.ds(r0, NORM_SLAB)
            xn_ref[rows, :] = _rms(x_ref[rows, :], g_ref[...]).astype(BF16)

    proj_ref[...] = jnp.dot(xn_ref[...], w_ref[...],
                            preferred_element_type=F32).astype(proj_ref.dtype)


def _norm_matmul(x, g, w, *, tm=1024, tn=1024):
    t, d = x.shape
    n = w.shape[1]
    return pl.pallas_call(
        _norm_matmul_kernel,
        grid=(t // tm, n // tn),
        in_specs=[pl.BlockSpec((tm, d), lambda i, j: (i, 0)),
                  pl.BlockSpec((1, d), lambda i, j: (0, 0)),
                  pl.BlockSpec((d, tn), lambda i, j: (0, j))],
        out_specs=pl.BlockSpec((tm, tn), lambda i, j: (i, j)),
        out_shape=jax.ShapeDtypeStruct((t, n), BF16),
        scratch_shapes=[pltpu.VMEM((tm, d), BF16)],
        compiler_params=_params("arbitrary", "arbitrary"),
        name="norm_matmul",
    )(x, g.reshape(1, d), w)


def _rope(x, cos, sin):
    half = x.shape[-1] // 2
    x1, x2 = x[:, :half], x[:, half:]
    return jnp.concatenate([x1 * cos - x2 * sin, x2 * cos + x1 * sin], axis=-1)


def _retention_kernel(cd_ref, q_ref, k_ref, v_ref, g_ref, cos_ref, sin_ref, dint_ref, qd_ref, kd_ref,
                      gain_ref, o_ref, state_ref, *, chunk, nchunk):
    @pl.when(pl.program_id(1) == 0)
    def _():
        state_ref[...] = jnp.zeros_like(state_ref)

    dk = q_ref.shape[-1]
    dint = dint_ref[0]
    qd = qd_ref[0]
    kd = kd_ref[0]
    cd = cd_ref[pl.program_id(0)]
    gain = gain_ref[...]
    for c in range(nchunk):
        rows = pl.ds(c * chunk, chunk)
        cos = cos_ref[rows, :]
        sin = sin_ref[rows, :]
        q = _rope(q_ref[rows, :].astype(F32), cos, sin)
        k = _rope(k_ref[rows, :].astype(F32), cos, sin) * (dk ** -0.5)
        v = v_ref[rows, :]
        scores = lax.dot_general(q.astype(BF16), k.astype(BF16), (((1,), (1,)), ((), ())),
                                 preferred_element_type=F32) * dint
        state = state_ref[...]
        o = (jnp.dot(scores.astype(BF16), v, preferred_element_type=F32)
             + jnp.dot((q * qd).astype(BF16), state.astype(BF16), preferred_element_type=F32))
        state_ref[...] = cd * state + lax.dot_general(
            (k * kd).astype(BF16), v, (((0,), (0,)), ((), ())), preferred_element_type=F32)
        o = o - jnp.mean(o, axis=-1, keepdims=True)
        o = o * lax.rsqrt(jnp.mean(o * o, axis=-1, keepdims=True) + EPS) * gain
        o_ref[rows, :] = (jax.nn.silu(g_ref[rows, :].astype(F32)) * o).astype(o_ref.dtype)


def _retention(proj, gain, *, rows=1024):
    t = proj.shape[0]
    h = RET_HEADS
    dk = proj.shape[1] // 6 // h
    dv = dk
    c = RET_CHUNK
    f32 = F32
    inv = ROPE_BASE ** (-jnp.arange(0, dk, 2, dtype=f32) / dk)
    ang = jnp.arange(t, dtype=f32)[:, None] * inv[None, :]
    cos, sin = jnp.cos(ang), jnp.sin(ang)
    log_gamma = jnp.log(1.0 - jnp.exp2(-5.0 - jnp.arange(h, dtype=f32)))
    idx = jnp.arange(c, dtype=f32)
    diff = idx[:, None] - idx[None, :]
    d_intra = jnp.where(diff >= 0, jnp.exp(jnp.maximum(diff, 0.0) * log_gamma[:, None, None]), 0.0)
    q_decay = jnp.broadcast_to(jnp.exp((idx + 1.0)[None, :] * log_gamma[:, None])[..., None], (h, c, dk))
    k_decay = jnp.broadcast_to(jnp.exp((c - 1.0 - idx)[None, :] * log_gamma[:, None])[..., None], (h, c, dk))
    chunk_decay = jnp.exp(c * log_gamma)

    def col(off):
        return pl.BlockSpec((rows, dk), lambda hh, i: (i, off + hh))

    tab = pl.BlockSpec((rows, dk // 2), lambda hh, i: (i, 0))
    per_head = lambda shp: pl.BlockSpec((1,) + shp, lambda hh, i: (hh, 0, 0))
    return pl.pallas_call(
        functools.partial(_retention_kernel, chunk=c, nchunk=rows // c),
        grid=(h, t // rows),
        in_specs=[pl.BlockSpec(memory_space=pltpu.SMEM),
                  col(0), col(h), col(2 * h), col(3 * h), tab, tab,
                  per_head((c, c)), per_head((c, dk)), per_head((c, dk)),
                  pl.BlockSpec((1, dv), lambda hh, i: (0, hh))],
        out_specs=pl.BlockSpec((rows, dv), lambda hh, i: (i, hh)),
        out_shape=jax.ShapeDtypeStruct((t, h * dv), BF16),
        scratch_shapes=[pltpu.VMEM((dk, dv), F32)],
        compiler_params=_params("arbitrary", "arbitrary"),
        name="retention",
    )(chunk_decay, proj, proj, proj, proj, cos, sin, d_intra, q_decay, k_decay, gain.reshape(1, h * dv))


def _gmlp_kernel(u_ref, vs_ref, gain_ref, ws_ref, bs_ref, o_ref, *, chunk, nchunk):
    r = lax.broadcasted_iota(jnp.int32, (chunk, chunk), 0)
    s = lax.broadcasted_iota(jnp.int32, (chunk, chunk), 1)
    w = jnp.where(r >= s, ws_ref[0], 0.0).astype(BF16)
    gain = gain_ref[...]
    bs = bs_ref[0]
    for c in range(nchunk):
        rows = pl.ds(c * chunk, chunk)
        v = jax.nn.gelu(vs_ref[rows, :].astype(F32))
        v = v - jnp.mean(v, axis=-1, keepdims=True)
        v = v * lax.rsqrt(jnp.mean(v * v, axis=-1, keepdims=True) + EPS) * gain
        mixed = jnp.dot(w, v.astype(BF16), preferred_element_type=F32) + bs
        o_ref[rows, :] = (jax.nn.gelu(u_ref[rows, :].astype(F32)) * mixed).astype(o_ref.dtype)


def _gmlp(proj, gain, ws, bs, *, rows=1024):
    t = proj.shape[0]
    g = GMLP_GROUPS
    dim = proj.shape[1] // 6 // g
    c = GMLP_CHUNK
    return pl.pallas_call(
        functools.partial(_gmlp_kernel, chunk=c, nchunk=rows // c),
        grid=(g, t // rows),
        in_specs=[pl.BlockSpec((rows, dim), lambda gg, i: (i, 4 * g + gg)),
                  pl.BlockSpec((rows, dim), lambda gg, i: (i, 5 * g + gg)),
                  pl.BlockSpec((1, dim), lambda gg, i: (0, gg)),
                  pl.BlockSpec((1, c, c), lambda gg, i: (gg, 0, 0)),
                  pl.BlockSpec((1, c, 1), lambda gg, i: (gg, 0, 0))],
        out_specs=pl.BlockSpec((rows, dim), lambda gg, i: (i, gg)),
        out_shape=jax.ShapeDtypeStruct((t, g * dim), BF16),
        compiler_params=_params("arbitrary", "arbitrary"),
        name="gmlp",
    )(proj, proj, gain.reshape(1, g * dim), ws, bs.reshape(g, c, 1))


def _hgrn_kernel(zq_ref, zf_ref, zi_ref, zg_ref, lb_ref, gain_ref, tri_ref, keep_ref, o_ref, state_ref,
                 *, chunk, nchunk, heads, dk):
    @pl.when(pl.program_id(1) == 0)
    def _():
        state_ref[...] = jnp.zeros_like(state_ref)

    rows = chunk * nchunk
    nt = (((1,), (1,)), ((), ()))
    tri = tri_ref[...]
    keep = keep_ref[...] > 0.0
    chunk_of_row = lax.broadcasted_iota(jnp.int32, (rows, dk), 0) // chunk
    hw = heads * dk
    head_cols = [slice(hd * dk, (hd + 1) * dk) for hd in range(heads)]
    lb = lb_ref[...]
    zf = zf_ref[...].astype(F32)
    f = lb + (1.0 - lb) * jax.nn.sigmoid(zf)
    kk = (1.0 - lb) * jax.nn.sigmoid(-zf)
    log_f = jnp.log(f)
    p0 = log_f.astype(BF16)
    r0 = log_f - p0.astype(F32)
    p1 = r0.astype(BF16)
    p2 = (r0 - p1.astype(F32)).astype(BF16)
    cum = (jnp.dot(tri, p0, preferred_element_type=F32)
           + jnp.dot(tri, p1, preferred_element_type=F32)
           + jnp.dot(tri, p2, preferred_element_type=F32))
    lasts = [cum[(c + 1) * chunk - 1:(c + 1) * chunk, :] for c in range(nchunk)]
    last_rows = jnp.concatenate([jnp.broadcast_to(l, (chunk, hw)) for l in lasts], axis=0)
    decays = [jnp.exp(l) for l in lasts]
    q_dec = (jax.nn.silu(zq_ref[...].astype(F32)) * jnp.exp(cum)).astype(BF16)
    k_dec = (kk * jnp.exp(-cum)).astype(BF16)
    k_out = (kk * jnp.exp(last_rows - cum)).astype(BF16)
    v = zi_ref[...]
    gate = jax.nn.silu(zg_ref[...].astype(F32))
    scores = [lax.dot_general(q_dec[:, cs], k_dec[:, cs], nt, preferred_element_type=F32)
              for cs in head_cols]
    scores = [jnp.where(keep, s, 0.0).astype(BF16) for s in scores]
    o_intra = [jnp.dot(s, v[:, cs], preferred_element_type=F32) for s, cs in zip(scores, head_cols)]
    v_t = [v[:, cs].astype(F32).T.astype(BF16) for cs in head_cols]
    zero = jnp.zeros((rows, dk), BF16)
    k_blocks = [jnp.concatenate([jnp.where(chunk_of_row == c, k_out[:, cs], zero) for c in range(nchunk)],
                                axis=1) for cs in head_cols]
    incr = [jnp.dot(v_t[hd], k_blocks[hd], preferred_element_type=F32) for hd in range(heads)]
    states = [state_ref[hd] for hd in range(heads)]
    inter = [[] for _ in range(heads)]
    for c in range(nchunk):
        sl = slice(c * chunk, (c + 1) * chunk)
        for hd, cs in enumerate(head_cols):
            inter[hd].append(lax.dot_general(q_dec[sl, cs], states[hd].astype(BF16), nt,
                                             preferred_element_type=F32))
            states[hd] = decays[c][:, cs] * states[hd] + incr[hd][:, c * dk:(c + 1) * dk]
    for hd, cs in enumerate(head_cols):
        state_ref[hd] = states[hd]
        o = o_intra[hd] + jnp.concatenate(inter[hd], axis=0)
        o = o * lax.rsqrt(jnp.mean(o * o, axis=-1, keepdims=True) + EPS) * gain_ref[:, cs]
        o_ref[:, cs] = (o * gate[:, cs]).astype(o_ref.dtype)


def _hgrn2(proj, lb, gain, *, rows=256, heads=4):
    t = proj.shape[0]
    width = proj.shape[1] // 4
    dk = HGRN_DK
    h = width // dk
    c = HGRN_CHUNK
    hw = heads * dk
    groups = h // heads
    r = jnp.arange(rows, dtype=jnp.int32)
    tri = ((r[:, None] >= r[None, :]) & ((r[:, None] // c) == (r[None, :] // c))).astype(F32)

    def col(off):
        return pl.BlockSpec((rows, hw), lambda hh, i: (i, off + hh))

    vec = pl.BlockSpec((1, hw), lambda hh, i: (0, hh))
    mask = pl.BlockSpec((rows, rows), lambda hh, i: (0, 0))
    return pl.pallas_call(
        functools.partial(_hgrn_kernel, chunk=c, nchunk=rows // c, heads=heads, dk=dk),
        grid=(groups, t // rows),
        in_specs=[col(0), col(groups), col(2 * groups), col(3 * groups), vec, vec, mask, mask],
        out_specs=pl.BlockSpec((rows, hw), lambda hh, i: (i, hh)),
        out_shape=jax.ShapeDtypeStruct((t, width), BF16),
        scratch_shapes=[pltpu.VMEM((heads, dk, dk), F32)],
        compiler_params=_params("arbitrary", "arbitrary"),
        name="hgrn2",
    )(proj, proj, proj, proj, lb.reshape(1, width), gain.reshape(1, width), tri.astype(BF16), tri)


def _pack_bf16_pairs(h):
    half = h.shape[1] // 2
    bits = pltpu.bitcast(h.astype(BF16).astype(F32), jnp.uint32)
    return (bits[:, :half] >> 16) | bits[:, half:]


def _unpack_bf16_pairs(words):
    lo = pltpu.bitcast(words << 16, F32)
    hi = pltpu.bitcast(words & jnp.uint32(0xFFFF0000), F32)
    return jnp.concatenate([lo, hi], axis=1).astype(BF16)


def _route(logits, run):
    lane = lax.broadcasted_iota(jnp.int32, logits.shape, 1)
    lane_f = lane.astype(F32)
    neg = -jnp.inf
    big = float(ROUTE_LANES)
    lg = jnp.where(lane < N_GROUPS, logits, neg)
    mg = jnp.max(lg, axis=-1, keepdims=True)
    eg = jnp.exp(lg - mg)
    pg = eg / jnp.sum(eg, axis=-1, keepdims=True)
    p_sel = jnp.max(pg, axis=-1, keepdims=True)
    g_sel = jnp.min(jnp.where(lg == mg, lane_f, big), axis=-1, keepdims=True)
    e_grp = ((lane - N_GROUPS) // EXPERTS_PER_GROUP).astype(F32)
    in_grp = jnp.where(lane >= N_GROUPS, e_grp, -1.0) == g_sel
    le = jnp.where(in_grp, logits, neg)
    t1 = jnp.max(le, axis=-1, keepdims=True)
    i1 = jnp.min(jnp.where(le == t1, lane_f, big), axis=-1, keepdims=True)
    le2 = jnp.where(lane_f == i1, neg, le)
    t2 = jnp.max(le2, axis=-1, keepdims=True)
    i2 = jnp.min(jnp.where(le2 == t2, lane_f, big), axis=-1, keepdims=True)
    e2 = jnp.exp(t2 - t1)
    den = 1.0 + e2
    gate1 = p_sel * (1.0 / den)
    gate2 = p_sel * (e2 / den)
    tm = logits.shape[0]
    hit1 = jnp.where(lane_f == i1, 1.0, 0.0)
    hit2 = jnp.where(lane_f == i2, 1.0, 0.0)
    hits = hit1 + hit2
    r = lax.broadcasted_iota(jnp.int32, (tm, tm), 0)
    s = lax.broadcasted_iota(jnp.int32, (tm, tm), 1)
    before = jnp.where(r > s, 1.0, 0.0).astype(BF16)
    prefix = jnp.dot(before, hits.astype(BF16), preferred_element_type=F32) + run
    rank1 = jnp.sum(prefix * hit1, axis=-1, keepdims=True)
    rank2 = jnp.sum(prefix * hit2, axis=-1, keepdims=True)
    vals = (i1 - N_GROUPS, i2 - N_GROUPS, gate1, gate2, rank1, rank2)
    slab = jnp.zeros_like(logits)
    for pos, val in enumerate(vals):
        slab = jnp.where(lane == pos, val, slab)
    return slab, run + jnp.sum(hits, axis=0, keepdims=True)


def _outproj_router_kernel(*refs, n_act):
    x_ref = refs[0]
    a_refs = refs[1:1 + n_act]
    w_refs = refs[1 + n_act:1 + 2 * n_act]
    (g_ref, wr_ref, br_ref, xo_ref, hp_ref, route_ref, count_ref,
     run_ref, xs_ref, lg_ref) = refs[1 + 2 * n_act:]
    i = pl.program_id(0)

    @pl.when(i == 0)
    def _():
        run_ref[...] = jnp.zeros_like(run_ref)
        xs_ref[1] = jnp.zeros(xs_ref.shape[1:], xs_ref.dtype)
        lg_ref[1] = jnp.zeros(lg_ref.shape[1:], lg_ref.dtype)

    def step(slot):
        x = x_ref[...]
        for a_ref, w_ref in zip(a_refs, w_refs):
            x = x + jnp.dot(a_ref[...], w_ref[...], preferred_element_type=F32)
        xo_ref[...] = x
        xs_ref[slot] = x

        run = run_ref[...]
        slab, run_next = _route(lg_ref[1 - slot], run)
        route_ref[0] = slab.T[:ROUTE_FIELDS, :]
        run = jnp.where(i >= 2, run_next, run)
        run_ref[...] = run
        count_ref[...] = run

        h = _rms(xs_ref[1 - slot], g_ref[...])
        hp_ref[...] = _pack_bf16_pairs(h)
        h_hi = h.astype(BF16)
        h_lo = (h - h_hi.astype(F32)).astype(BF16)
        p = (jnp.dot(h_hi, wr_ref[...], preferred_element_type=F32)
             + jnp.dot(h_lo, wr_ref[...], preferred_element_type=F32))
        lg_ref[slot] = p[:, :ROUTE_LANES] + p[:, ROUTE_LANES:] + br_ref[...]

    for slot in range(2):
        pl.when(i % 2 == slot)(functools.partial(step, slot))


def _outproj_router(x, acts, ws, g, w_rg, b_rg, w_re, b_re, *, tm=TOKEN_TILE):
    t, d = x.shape
    used = N_GROUPS + N_EXPERTS
    wr = jnp.zeros((d, ROUTE_LANES), F32).at[:, :N_GROUPS].set(w_rg).at[:, N_GROUPS:used].set(w_re)
    wr_hi = wr.astype(BF16)
    wr_lo = (wr - wr_hi.astype(F32)).astype(BF16)
    wr2 = jnp.concatenate([wr_hi, wr_lo], axis=1)
    br = jnp.zeros((1, ROUTE_LANES), F32).at[0, :N_GROUPS].set(b_rg).at[0, N_GROUPS:used].set(b_re)
    nt = t // tm
    cur = lambda i: (jnp.minimum(i, nt - 1), 0)
    prev = lambda i: (jnp.clip(i - 1, 0, nt - 1), 0)
    prev2 = lambda i: (jnp.maximum(i - 2, 0), 0, 0)
    row = pl.BlockSpec((tm, d), cur)
    const = lambda a: pl.BlockSpec(a.shape, lambda i: (0, 0))
    in_specs = [row]
    in_specs += [pl.BlockSpec((tm, a.shape[1]), cur) for a in acts]
    in_specs += [const(w) for w in ws]
    in_specs += [pl.BlockSpec((1, d), lambda i: (0, 0)), const(wr2), const(br)]
    return pl.pallas_call(
        functools.partial(_outproj_router_kernel, n_act=len(acts)),
        grid=(nt + 2,),
        in_specs=in_specs,
        out_specs=[row, pl.BlockSpec((tm, d // 2), prev),
                   pl.BlockSpec((1, ROUTE_FIELDS, tm), prev2),
                   pl.BlockSpec((1, ROUTE_LANES), lambda i: (0, 0))],
        out_shape=[jax.ShapeDtypeStruct((t, d), F32), jax.ShapeDtypeStruct((t, d // 2), jnp.uint32),
                   jax.ShapeDtypeStruct((nt, ROUTE_FIELDS, tm), F32),
                   jax.ShapeDtypeStruct((1, ROUTE_LANES), F32)],
        scratch_shapes=[pltpu.VMEM((1, ROUTE_LANES), F32), pltpu.VMEM((2, tm, d), F32),
                        pltpu.VMEM((2, tm, ROUTE_LANES), F32)],
        compiler_params=_params("arbitrary"),
        name="outproj_router",
    )(x, *acts, *ws, g.reshape(1, d), wr2, br)


def _expert_changed(be_ref, b):
    return (b == 0) | (be_ref[b] != be_ref[jnp.maximum(b - 1, 0)])


def _moe_ffn_kernel(be_ref, nu_ref, ne_ref, *refs, layer):
    nslot = MOE_ROW_SLOTS
    first_refs, ahead_ref = refs[:nslot - 1], refs[nslot - 1]
    (h_hbm, wg_hbm, wu_hbm, wd_hbm, out_ref,
     xbuf, stage_g, stage_u, stage_d, wg_bf, wu_bf, wd_bf, xsem, wsem) = refs[nslot:]
    b = pl.program_id(0)
    nu = nu_ref[0]
    blk = xbuf.shape[1]

    def start_rows(idx_ref, dst_slot):
        for r in range(blk):
            pltpu.make_async_copy(h_hbm.at[pl.ds(idx_ref[0, 0, r], 1)],
                                  xbuf.at[dst_slot, pl.ds(r, 1)], xsem.at[dst_slot]).start(priority=0)

    def wait_rows(dst_slot):
        pltpu.make_async_copy(h_hbm.at[pl.ds(0, blk)], xbuf.at[dst_slot], xsem.at[dst_slot]).wait()

    def weight_copies(e):
        return (pltpu.make_async_copy(wg_hbm.at[layer, e], stage_g, wsem.at[0]),
                pltpu.make_async_copy(wu_hbm.at[layer, e], stage_u, wsem.at[1]),
                pltpu.make_async_copy(wd_hbm.at[layer, e], stage_d, wsem.at[2]))

    def start_weights(e):
        for cp in weight_copies(e):
            cp.start(priority=1)

    @pl.when(b == 0)
    def _():
        start_weights(be_ref[0])
        for slot, idx_ref in enumerate(first_refs):
            start_rows(idx_ref, slot)

    changed = _expert_changed(be_ref, b)

    def compute(slot, new_expert):
        if new_expert:
            for cp in weight_copies(be_ref[b]):
                cp.wait()
        wait_rows(slot)
        x = _unpack_bf16_pairs(xbuf[slot])
        start_rows(ahead_ref, (slot + nslot - 1) % nslot)
        if new_expert:
            wg_bf[...] = stage_g[...].astype(BF16)
        hg = jnp.dot(x, wg_bf[...], preferred_element_type=F32)
        if new_expert:
            wu_bf[...] = stage_u[...].astype(BF16)
        hu = jnp.dot(x, wu_bf[...], preferred_element_type=F32)
        if new_expert:
            wd_bf[...] = stage_d[...].astype(BF16)
        hid = (jax.nn.silu(hg) * hu).astype(BF16)
        out_ref[...] = jnp.dot(hid, wd_bf[...], preferred_element_type=F32)

        if new_expert:
            @pl.when(ne_ref[b] >= 0)
            def _():
                start_weights(ne_ref[b])

        @pl.when(b == nu - 1)
        def _():
            for k in range(1, nslot):
                wait_rows((slot + k) % nslot)

    for slot in range(nslot):
        here = (b < nu) & (b % nslot == slot)
        pl.when(here & changed)(functools.partial(compute, slot, True))
        pl.when(here & jnp.logical_not(changed))(functools.partial(compute, slot, False))

    @pl.when(b >= nu)
    def _():
        out_ref[...] = jnp.zeros_like(out_ref)


def _combine_kernel(*refs, final):
    if final:
        pos0_ref, pos1_ref, ahead_ref, x_ref, route_ref, eo_hbm, g_ref, o_ref, ybuf, sem = refs
    else:
        pos0_ref, pos1_ref, ahead_ref, x_ref, route_ref, eo_hbm, o_ref, ybuf, sem = refs
    i = pl.program_id(0)
    last = pl.num_programs(0) - 1
    tm = x_ref.shape[0]
    nslot = COMBINE_SLOTS

    def start_rows(idx_ref, dst_slot):
        for r in range(tm):
            for k in range(TOP_K):
                pltpu.make_async_copy(eo_hbm.at[pl.ds(idx_ref[0, 0, k * tm + r], 1)],
                                      ybuf.at[dst_slot, k, pl.ds(r, 1)], sem.at[dst_slot]).start(priority=k)

    def wait_rows(dst_slot):
        for k in range(TOP_K):
            pltpu.make_async_copy(eo_hbm.at[pl.ds(0, tm)], ybuf.at[dst_slot, k], sem.at[dst_slot]).wait()

    @pl.when(i == 0)
    def _():
        start_rows(pos0_ref, 0)
        start_rows(pos1_ref, 1)

    def step(slot):
        wait_rows(slot)
        start_rows(ahead_ref, (slot + 2) % nslot)
        route = route_ref[0].T
        y = ybuf[slot, 0] * route[:, TOP_K:TOP_K + 1] + ybuf[slot, 1] * route[:, TOP_K + 1:TOP_K + 2]
        x = x_ref[...] + y
        o_ref[...] = _rms(x, g_ref[...]) if final else x

        @pl.when(i == last)
        def _():
            wait_rows((slot + 1) % nslot)
            wait_rows((slot + 2) % nslot)

    for slot in range(nslot):
        pl.when(i % nslot == slot)(functools.partial(step, slot))


def _moe_dispatch(route, counts_slab, t):
    m = t * TOP_K
    nt, _, tm = route.shape
    expert = route[:, :TOP_K, :].astype(jnp.int32)
    rank = route[:, 2 * TOP_K:3 * TOP_K, :].astype(jnp.int32)
    counts = counts_slab[0, N_GROUPS:N_GROUPS + N_EXPERTS].astype(jnp.int32)
    padded = (counts + MOE_BLOCK - 1) // MOE_BLOCK * MOE_BLOCK
    padded_ends = jnp.cumsum(padded)
    padded_starts = padded_ends - padded
    ids = jnp.arange(N_EXPERTS, dtype=jnp.int32)
    start_of = jnp.sum(jnp.where(expert[..., None] == ids, padded_starts, 0), axis=-1)
    dest = start_of + rank
    n_blocks = -(-(m + N_EXPERTS * (MOE_BLOCK - 1)) // MOE_BLOCK)
    cap = n_blocks * MOE_BLOCK
    token = (jnp.arange(nt, dtype=jnp.int32)[:, None, None] * tm
             + jnp.arange(tm, dtype=jnp.int32)[None, None, :])
    token = jnp.broadcast_to(token, dest.shape)
    assert cap == m + N_EXPERTS * MOE_BLOCK
    pad_i = jnp.arange(MOE_BLOCK, dtype=jnp.int32)[None, :]
    pad_row = jnp.where(pad_i < (padded - counts)[:, None],
                        (padded_starts + counts)[:, None] + pad_i,
                        cap + ids[:, None] * MOE_BLOCK + pad_i)
    rows = jnp.concatenate([dest.reshape(m), pad_row.reshape(-1)])
    toks = jnp.concatenate([token.reshape(m), jnp.zeros((N_EXPERTS * MOE_BLOCK,), jnp.int32)])
    _, buf_src = lax.sort_key_val(rows, toks)
    block_start = jnp.arange(n_blocks, dtype=jnp.int32) * MOE_BLOCK
    block_expert = jnp.minimum(
        jnp.sum((padded_ends[None, :] <= block_start[:, None]).astype(jnp.int32), axis=1), N_EXPERTS - 1)
    n_used = (padded_ends[-1] // MOE_BLOCK).astype(jnp.int32).reshape(1)
    ids = jnp.arange(N_EXPERTS, dtype=jnp.int32)
    later = (ids[None, :] > ids[:, None]) & (counts[None, :] > 0)
    next_active = jnp.min(jnp.where(later, ids[None, :], N_EXPERTS), axis=1)
    next_active = jnp.where(next_active < N_EXPERTS, next_active, -1)
    block_next = next_active[block_expert]
    return block_expert, n_used, block_next, buf_src.reshape(n_blocks, 1, MOE_BLOCK), dest


def _moe_experts(h_packed, block_expert, n_used, block_next, buf_src, w_gate, w_up, w_down, layer):
    d, ff = w_gate.shape[-2:]
    n_blocks = buf_src.shape[0]
    cap = n_blocks * MOE_BLOCK
    idx_block = (1, 1, MOE_BLOCK)
    hbm = pl.BlockSpec(memory_space=pl.ANY)

    look = MOE_ROW_SLOTS - 1

    def fixed(k, b, *_):
        return (min(k, n_blocks - 1), 0, 0)

    def ahead(b, *_):
        return (jnp.minimum(b + look, n_blocks - 1), 0, 0)

    idx_specs = [pl.BlockSpec(idx_block, functools.partial(fixed, k), memory_space=pltpu.SMEM)
                 for k in range(look)]
    idx_specs.append(pl.BlockSpec(idx_block, ahead, memory_space=pltpu.SMEM))
    return pl.pallas_call(
        functools.partial(_moe_ffn_kernel, layer=layer),
        grid_spec=pltpu.PrefetchScalarGridSpec(
            num_scalar_prefetch=3,
            grid=(n_blocks,),
            in_specs=idx_specs + [hbm, hbm, hbm, hbm],
            out_specs=pl.BlockSpec((MOE_BLOCK, d), lambda b, *_: (b, 0)),
            scratch_shapes=[pltpu.VMEM((MOE_ROW_SLOTS, MOE_BLOCK, d // 2), jnp.uint32),
                            pltpu.VMEM((d, ff), F32), pltpu.VMEM((d, ff), F32), pltpu.VMEM((ff, d), F32),
                            pltpu.VMEM((d, ff), BF16), pltpu.VMEM((d, ff), BF16), pltpu.VMEM((ff, d), BF16),
                            pltpu.SemaphoreType.DMA((MOE_ROW_SLOTS,)), pltpu.SemaphoreType.DMA((3,))],
        ),
        out_shape=jax.ShapeDtypeStruct((cap, d), F32),
        compiler_params=pltpu.CompilerParams(dimension_semantics=("arbitrary",),
                                             vmem_limit_bytes=MOE_VMEM_LIMIT),
        name="moe_ffn",
    )(block_expert, n_used, block_next, *([buf_src] * MOE_ROW_SLOTS), h_packed, w_gate, w_up, w_down)


def _combine(x, route, dest, expert_out, final_gain):
    t, d = x.shape
    nt, _, tm = route.shape
    final = final_gain is not None
    pos = dest.reshape(nt, 1, TOP_K * tm)
    idx_block = (1, 1, TOP_K * tm)
    row = pl.BlockSpec((tm, d), lambda i: (i, 0))
    in_specs = [pl.BlockSpec(idx_block, lambda i: (0, 0, 0), memory_space=pltpu.SMEM),
                pl.BlockSpec(idx_block, lambda i: (min(1, nt - 1), 0, 0), memory_space=pltpu.SMEM),
                pl.BlockSpec(idx_block, lambda i: (jnp.minimum(i + 2, nt - 1), 0, 0),
                             memory_space=pltpu.SMEM),
                row, pl.BlockSpec((1, ROUTE_FIELDS, tm), lambda i: (i, 0, 0)),
                pl.BlockSpec(memory_space=pl.ANY)]
    args = [pos, pos, pos, x, route, expert_out]
    if final:
        in_specs.append(pl.BlockSpec((1, d), lambda i: (0, 0)))
        args.append(final_gain.reshape(1, d))
    return pl.pallas_call(
        functools.partial(_combine_kernel, final=final),
        grid=(nt,),
        in_specs=in_specs, out_specs=row,
        out_shape=jax.ShapeDtypeStruct((t, d), F32),
        scratch_shapes=[pltpu.VMEM((COMBINE_SLOTS, TOP_K, tm, d), F32),
                        pltpu.SemaphoreType.DMA((COMBINE_SLOTS,))],
        compiler_params=_params("arbitrary"),
        name="moe_combine",
    )(*args)


def kernel(x, attn_norm, ffn_norm, final_norm, w_in_ab, ret_norm, gmlp_norm, gmlp_ws, gmlp_bs, w_out_ab, w_in_c, lb_params, hgrn_norm, w_out_c, router_w_group, router_b_group, router_w_expert, router_b_expert, w_gate, w_up, w_down):
    b, s, d = x.shape
    assert b == 1, "the sequence mixers carry state along the flattened token axis"
    depth = attn_norm.shape[0]
    lb_soft = jax.nn.softmax(lb_params.astype(F32), axis=0)
    lower_bounds = jnp.cumsum(lb_soft, axis=0) - lb_soft[0]
    xt = x.reshape(b * s, d)
    t = b * s
    for layer in range(depth):
        i = layer // 2
        if layer % 2 == 0:
            proj = _norm_matmul(xt, attn_norm[layer], w_in_ab[i].astype(BF16))
            ret = _retention(proj, ret_norm[i])
            gm = _gmlp(proj, gmlp_norm[i], gmlp_ws[i], gmlp_bs[i])
            w_out = w_out_ab[i].astype(BF16)
            nr = ret.shape[1]
            acts, w_outs = [ret, gm], [w_out[:nr], w_out[nr:]]
        else:
            proj = _norm_matmul(xt, attn_norm[layer], w_in_c[i].astype(BF16))
            acts = [_hgrn2(proj, lower_bounds[layer], hgrn_norm[i])]
            w_outs = [w_out_c[i].astype(BF16)]
        xt, h_packed, route, counts = _outproj_router(
            xt, acts, w_outs, ffn_norm[layer], router_w_group[layer], router_b_group[layer],
            router_w_expert[layer], router_b_expert[layer])
        block_expert, n_used, block_next, buf_src, dest = _moe_dispatch(route, counts, t)
        expert_out = _moe_experts(h_packed, block_expert, n_used, block_next, buf_src,
                                  w_gate, w_up, w_down, layer)
        xt = _combine(xt, route, dest, expert_out, final_norm if layer == depth - 1 else None)
    return xt.reshape(b, s, d)
```

```python
import functools

import jax
import jax.numpy as jnp
from jax import lax
from jax.experimental import pallas as pl
from jax.experimental.pallas import tpu as pltpu

F32 = jnp.float32
BF16 = jnp.bfloat16
EPS = 1e-6

RET_HEADS = 4
RET_CHUNK = 128
ROPE_BASE = 10000.0
GMLP_GROUPS = 4
GMLP_CHUNK = 128
HGRN_DK = 128
HGRN_CHUNK = 32
N_GROUPS = 4
EXPERTS_PER_GROUP = 8
N_EXPERTS = N_GROUPS * EXPERTS_PER_GROUP
TOP_K = 2
MOE_BLOCK = 128
MOE_ROW_SLOTS = 4
COMBINE_SLOTS = 3
ROUTE_LANES = 128
NORM_SLAB = 256
ROUTE_FIELDS = 8
TOKEN_TILE = 256

VMEM_LIMIT = 48 * 1024 * 1024
MOE_VMEM_LIMIT = 56 * 1024 * 1024


def _params(*sem):
    return pltpu.CompilerParams(dimension_semantics=sem, vmem_limit_bytes=VMEM_LIMIT)


def _rms(x, g):
    return x * lax.rsqrt(jnp.mean(x * x, axis=-1, keepdims=True) + EPS) * g


def _norm_matmul_kernel(x_ref, g_ref, w_ref, proj_ref, xn_ref):
    @pl.when(pl.program_id(1) == 0)
    def _():
        for r0 in range(0, x_ref.shape[0], NORM_SLAB):
            rows = pl.ds(r0, NORM_SLAB)
            xn_ref[rows, :] = _rms(x_ref[rows, :], g_ref[...]).astype(BF16)

    proj_ref[...] = jnp.dot(xn_ref[...], w_ref[...],
                            preferred_element_type=F32).astype(proj_ref.dtype)


def _norm_matmul(x, g, w, *, tm=1024, tn=1024):
    t, d = x.shape
    n = w.shape[1]
    return pl.pallas_call(
        _norm_matmul_kernel,
        grid=(t // tm, n // tn),
        in_specs=[pl.BlockSpec((tm, d), lambda i, j: (i, 0)),
                  pl.BlockSpec((1, d), lambda i, j: (0, 0)),
                  pl.BlockSpec((d, tn), lambda i, j: (0, j))],
        out_specs=pl.BlockSpec((tm, tn), lambda i, j: (i, j)),
        out_shape=jax.ShapeDtypeStruct((t, n), BF16),
        scratch_shapes=[pltpu.VMEM((tm, d), BF16)],
        compiler_params=_params("arbitrary", "arbitrary"),
        name="norm_matmul",
    )(x, g.reshape(1, d), w)


def _rope(x, cos, sin):
    half = x.shape[-1] // 2
    x1, x2 = x[:, :half], x[:, half:]
    return jnp.concatenate([x1 * cos - x2 * sin, x2 * cos + x1 * sin], axis=-1)


def _retention_kernel(cd_ref, q_ref, k_ref, v_ref, g_ref, cos_ref, sin_ref, dint_ref, qd_ref, kd_ref,
                      gain_ref, o_ref, state_ref, *, chunk, nchunk):
    @pl.when(pl.program_id(1) == 0)
    def _():
        state_ref[...] = jnp.zeros_like(state_ref)

    dk = q_ref.shape[-1]
    dint = dint_ref[0]
    qd = qd_ref[0]
    kd = kd_ref[0]
    cd = cd_ref[pl.program_id(0)]
    gain = gain_ref[...]
    for c in range(nchunk):
        rows = pl.ds(c * chunk, chunk)
        cos = cos_ref[rows, :]
        sin = sin_ref[rows, :]
        q = _rope(q_ref[rows, :].astype(F32), cos, sin)
        k = _rope(k_ref[rows, :].astype(F32), cos, sin) * (dk ** -0.5)
        v = v_ref[rows, :]
        scores = lax.dot_general(q.astype(BF16), k.astype(BF16), (((1,), (1,)), ((), ())),
                                 preferred_element_type=F32) * dint
        state = state_ref[...]
        o = (jnp.dot(scores.astype(BF16), v, preferred_element_type=F32)
             + jnp.dot((q * qd).astype(BF16), state.astype(BF16), preferred_element_type=F32))
        state_ref[...] = cd * state + lax.dot_general(
            (k * kd).astype(BF16), v, (((0,), (0,)), ((), ())), preferred_element_type=F32)
        o = o - jnp.mean(o, axis=-1, keepdims=True)
        o = o * lax.rsqrt(jnp.mean(o * o, axis=-1, keepdims=True) + EPS) * gain
        o_ref[rows, :] = (jax.nn.silu(g_ref[rows, :].astype(F32)) * o).astype(o_ref.dtype)


def _retention(proj, gain, *, rows=1024):
    t = proj.shape[0]
    h = RET_HEADS
    dk = proj.shape[1] // 6 // h
    dv = dk
    c = RET_CHUNK
    f32 = F32
    inv = ROPE_BASE ** (-jnp.arange(0, dk, 2, dtype=f32) / dk)
    ang = jnp.arange(t, dtype=f32)[:, None] * inv[None, :]
    cos, sin = jnp.cos(ang), jnp.sin(ang)
    log_gamma = jnp.log(1.0 - jnp.exp2(-5.0 - jnp.arange(h, dtype=f32)))
    idx = jnp.arange(c, dtype=f32)
    diff = idx[:, None] - idx[None, :]
    d_intra = jnp.where(diff >= 0, jnp.exp(jnp.maximum(diff, 0.0) * log_gamma[:, None, None]), 0.0)
    q_decay = jnp.broadcast_to(jnp.exp((idx + 1.0)[None, :] * log_gamma[:, None])[..., None], (h, c, dk))
    k_decay = jnp.broadcast_to(jnp.exp((c - 1.0 - idx)[None, :] * log_gamma[:, None])[..., None], (h, c, dk))
    chunk_decay = jnp.exp(c * log_gamma)

    def col(off):
        return pl.BlockSpec((rows, dk), lambda hh, i: (i, off + hh))

    tab = pl.BlockSpec((rows, dk // 2), lambda hh, i: (i, 0))
    per_head = lambda shp: pl.BlockSpec((1,) + shp, lambda hh, i: (hh, 0, 0))
    return pl.pallas_call(
        functools.partial(_retention_kernel, chunk=c, nchunk=rows // c),
        grid=(h, t // rows),
        in_specs=[pl.BlockSpec(memory_space=pltpu.SMEM),
                  col(0), col(h), col(2 * h), col(3 * h), tab, tab,
                  per_head((c, c)), per_head((c, dk)), per_head((c, dk)),
                  pl.BlockSpec((1, dv), lambda hh, i: (0, hh))],
        out_specs=pl.BlockSpec((rows, dv), lambda hh, i: (i, hh)),
        out_shape=jax.ShapeDtypeStruct((t, h * dv), BF16),
        scratch_shapes=[pltpu.VMEM((dk, dv), F32)],
        compiler_params=_params("arbitrary", "arbitrary"),
        name="retention",
    )(chunk_decay, proj, proj, proj, proj, cos, sin, d_intra, q_decay, k_decay, gain.reshape(1, h * dv))


def _gmlp_kernel(u_ref, vs_ref, gain_ref, ws_ref, bs_ref, o_ref, *, chunk, nchunk):
    r = lax.broadcasted_iota(jnp.int32, (chunk, chunk), 0)
    s = lax.broadcasted_iota(jnp.int32, (chunk, chunk), 1)
    w = jnp.where(r >= s, ws_ref[0], 0.0).astype(BF16)
    gain = gain_ref[...]
    bs = bs_ref[0]
    for c in range(nchunk):
        rows = pl.ds(c * chunk, chunk)
        v = jax.nn.gelu(vs_ref[rows, :].astype(F32))
        v = v - jnp.mean(v, axis=-1, keepdims=True)
        v = v * lax.rsqrt(jnp.mean(v * v, axis=-1, keepdims=True) + EPS) * gain
        mixed = jnp.dot(w, v.astype(BF16), preferred_element_type=F32) + bs
        o_ref[rows, :] = (jax.nn.gelu(u_ref[rows, :].astype(F32)) * mixed).astype(o_ref.dtype)


def _gmlp(proj, gain, ws, bs, *, rows=1024):
    t = proj.shape[0]
    g = GMLP_GROUPS
    dim = proj.shape[1] // 6 // g
    c = GMLP_CHUNK
    return pl.pallas_call(
        functools.partial(_gmlp_kernel, chunk=c, nchunk=rows // c),
        grid=(g, t // rows),
        in_specs=[pl.BlockSpec((rows, dim), lambda gg, i: (i, 4 * g + gg)),
                  pl.BlockSpec((rows, dim), lambda gg, i: (i, 5 * g + gg)),
                  pl.BlockSpec((1, dim), lambda gg, i: (0, gg)),
                  pl.BlockSpec((1, c, c), lambda gg, i: (gg, 0, 0)),
                  pl.BlockSpec((1, c, 1), lambda gg, i: (gg, 0, 0))],
        out_specs=pl.BlockSpec((rows, dim), lambda gg, i: (i, gg)),
        out_shape=jax.ShapeDtypeStruct((t, g * dim), BF16),
        compiler_params=_params("arbitrary", "arbitrary"),
        name="gmlp",
    )(proj, proj, gain.reshape(1, g * dim), ws, bs.reshape(g, c, 1))


def _hgrn_kernel(zq_ref, zf_ref, zi_ref, zg_ref, lb_ref, gain_ref, tri_ref, keep_ref, o_ref,
                 state_ref, qd_ref, kd_ref, ko_ref, dec_ref, *, chunk, nchunk, heads, dk):
    i = pl.program_id(1)
    rows = chunk * nchunk
    hw = heads * dk

    @pl.when(i == 0)
    def _():
        state_ref[...] = jnp.zeros_like(state_ref)
        for ref in (qd_ref, kd_ref, ko_ref, dec_ref):
            ref[1] = jnp.zeros(ref.shape[1:], ref.dtype)

    nt = (((1,), (1,)), ((), ()))
    head_cols = [slice(hd * dk, (hd + 1) * dk) for hd in range(heads)]

    def matmul_stage(prev):
        keep = keep_ref[...] > 0.0
        chunk_of_row = lax.broadcasted_iota(jnp.int32, (rows, dk), 0) // chunk
        q_dec, k_dec, k_out, dec = qd_ref[prev], kd_ref[prev], ko_ref[prev], dec_ref[prev]
        v = zi_ref[...]
        gate = jax.nn.silu(zg_ref[...].astype(F32))
        scores = [lax.dot_general(q_dec[:, cs], k_dec[:, cs], nt, preferred_element_type=F32)
                  for cs in head_cols]
        scores = [jnp.where(keep, s, 0.0).astype(BF16) for s in scores]
        o_intra = [jnp.dot(s, v[:, cs], preferred_element_type=F32) for s, cs in zip(scores, head_cols)]
        v_t = [v[:, cs].astype(F32).T.astype(BF16) for cs in head_cols]
        zero = jnp.zeros((rows, dk), BF16)
        k_blocks = [jnp.concatenate([jnp.where(chunk_of_row == c, k_out[:, cs], zero)
                                     for c in range(nchunk)], axis=1) for cs in head_cols]
        incr = [jnp.dot(v_t[hd], k_blocks[hd], preferred_element_type=F32) for hd in range(heads)]
        return q_dec, dec, gate, o_intra, incr

    def recurrence_stage(q_dec, dec, gate, o_intra, incr):
        states = [state_ref[hd] for hd in range(heads)]
        inter = [[] for _ in range(heads)]
        for c in range(nchunk):
            sl = slice(c * chunk, (c + 1) * chunk)
            for hd, cs in enumerate(head_cols):
                inter[hd].append(lax.dot_general(q_dec[sl, cs], states[hd].astype(BF16), nt,
                                                 preferred_element_type=F32))
                states[hd] = dec[c:c + 1, cs] * states[hd] + incr[hd][:, c * dk:(c + 1) * dk]
        for hd, cs in enumerate(head_cols):
            state_ref[hd] = states[hd]
            o = o_intra[hd] + jnp.concatenate(inter[hd], axis=0)
            o = o * lax.rsqrt(jnp.mean(o * o, axis=-1, keepdims=True) + EPS) * gain_ref[:, cs]
            o_ref[:, cs] = (o * gate[:, cs]).astype(o_ref.dtype)

    def elementwise_stage(cur):
        tri = tri_ref[...]
        lb = lb_ref[...]
        zf = zf_ref[...].astype(F32)
        f = lb + (1.0 - lb) * jax.nn.sigmoid(zf)
        kk = (1.0 - lb) * jax.nn.sigmoid(-zf)
        log_f = jnp.log(f)
        p0 = log_f.astype(BF16)
        r0 = log_f - p0.astype(F32)
        p1 = r0.astype(BF16)
        p2 = (r0 - p1.astype(F32)).astype(BF16)
        cum = (jnp.dot(tri, p0, preferred_element_type=F32)
               + jnp.dot(tri, p1, preferred_element_type=F32)
               + jnp.dot(tri, p2, preferred_element_type=F32))
        lasts = [cum[(c + 1) * chunk - 1:(c + 1) * chunk, :] for c in range(nchunk)]
        last_rows = jnp.concatenate([jnp.broadcast_to(l, (chunk, hw)) for l in lasts], axis=0)
        dec_ref[cur] = jnp.exp(jnp.concatenate(lasts, axis=0))
        qd_ref[cur] = (jax.nn.silu(zq_ref[...].astype(F32)) * jnp.exp(cum)).astype(BF16)
        kd_ref[cur] = (kk * jnp.exp(-cum)).astype(BF16)
        ko_ref[cur] = (kk * jnp.exp(last_rows - cum)).astype(BF16)

    def step(slot):
        carried = matmul_stage(1 - slot)
        elementwise_stage(slot)
        recurrence_stage(*carried)

    for slot in range(2):
        pl.when(i % 2 == slot)(functools.partial(step, slot))


def _hgrn2(proj, lb, gain, *, rows=256, heads=4):
    t = proj.shape[0]
    width = proj.shape[1] // 4
    dk = HGRN_DK
    h = width // dk
    c = HGRN_CHUNK
    nchunk = rows // c
    hw = heads * dk
    groups = h // heads
    nb = t // rows
    r = jnp.arange(rows, dtype=jnp.int32)
    tri = ((r[:, None] >= r[None, :]) & ((r[:, None] // c) == (r[None, :] // c))).astype(F32)

    def cur(off):
        return pl.BlockSpec((rows, hw), lambda hh, i: (jnp.minimum(i, nb - 1), off + hh))

    def prev(off):
        return pl.BlockSpec((rows, hw), lambda hh, i: (jnp.maximum(i - 1, 0), off + hh))

    vec = pl.BlockSpec((1, hw), lambda hh, i: (0, hh))
    mask = pl.BlockSpec((rows, rows), lambda hh, i: (0, 0))
    return pl.pallas_call(
        functools.partial(_hgrn_kernel, chunk=c, nchunk=nchunk, heads=heads, dk=dk),
        grid=(groups, nb + 1),
        in_specs=[cur(0), cur(groups), prev(2 * groups), prev(3 * groups), vec, vec, mask, mask],
        out_specs=prev(0),
        out_shape=jax.ShapeDtypeStruct((t, width), BF16),
        scratch_shapes=[pltpu.VMEM((heads, dk, dk), F32),
                        pltpu.VMEM((2, rows, hw), BF16), pltpu.VMEM((2, rows, hw), BF16),
                        pltpu.VMEM((2, rows, hw), BF16), pltpu.VMEM((2, nchunk, hw), F32)],
        compiler_params=_params("arbitrary", "arbitrary"),
        name="hgrn2",
    )(proj, proj, proj, proj, lb.reshape(1, width), gain.reshape(1, width), tri.astype(BF16), tri)


def _pack_bf16_pairs(h):
    half = h.shape[1] // 2
    bits = pltpu.bitcast(h.astype(BF16).astype(F32), jnp.uint32)
    return (bits[:, :half] >> 16) | bits[:, half:]


def _unpack_bf16_pairs(words):
    lo = pltpu.bitcast(words << 16, F32)
    hi = pltpu.bitcast(words & jnp.uint32(0xFFFF0000), F32)
    return jnp.concatenate([lo, hi], axis=1).astype(BF16)


def _route(logits, run):
    lane = lax.broadcasted_iota(jnp.int32, logits.shape, 1)
    lane_f = lane.astype(F32)
    neg = -jnp.inf
    big = float(ROUTE_LANES)
    lg = jnp.where(lane < N_GROUPS, logits, neg)
    mg = jnp.max(lg, axis=-1, keepdims=True)
    eg = jnp.exp(lg - mg)
    pg = eg / jnp.sum(eg, axis=-1, keepdims=True)
    p_sel = jnp.max(pg, axis=-1, keepdims=True)
    g_sel = jnp.min(jnp.where(lg == mg, lane_f, big), axis=-1, keepdims=True)
    e_grp = ((lane - N_GROUPS) // EXPERTS_PER_GROUP).astype(F32)
    in_grp = jnp.where(lane >= N_GROUPS, e_grp, -1.0) == g_sel
    le = jnp.where(in_grp, logits, neg)
    t1 = jnp.max(le, axis=-1, keepdims=True)
    i1 = jnp.min(jnp.where(le == t1, lane_f, big), axis=-1, keepdims=True)
    le2 = jnp.where(lane_f == i1, neg, le)
    t2 = jnp.max(le2, axis=-1, keepdims=True)
    i2 = jnp.min(jnp.where(le2 == t2, lane_f, big), axis=-1, keepdims=True)
    e2 = jnp.exp(t2 - t1)
    den = 1.0 + e2
    gate1 = p_sel * (1.0 / den)
    gate2 = p_sel * (e2 / den)
    tm = logits.shape[0]
    hit1 = jnp.where(lane_f == i1, 1.0, 0.0)
    hit2 = jnp.where(lane_f == i2, 1.0, 0.0)
    hits = hit1 + hit2
    r = lax.broadcasted_iota(jnp.int32, (tm, tm), 0)
    s = lax.broadcasted_iota(jnp.int32, (tm, tm), 1)
    before = jnp.where(r > s, 1.0, 0.0).astype(BF16)
    prefix = jnp.dot(before, hits.astype(BF16), preferred_element_type=F32) + run
    rank1 = jnp.sum(prefix * hit1, axis=-1, keepdims=True)
    rank2 = jnp.sum(prefix * hit2, axis=-1, keepdims=True)
    vals = (i1 - N_GROUPS, i2 - N_GROUPS, gate1, gate2, rank1, rank2)
    slab = jnp.zeros_like(logits)
    for pos, val in enumerate(vals):
        slab = jnp.where(lane == pos, val, slab)
    return slab, run + jnp.sum(hits, axis=0, keepdims=True)


def _outproj_router_kernel(*refs, n_act):
    x_ref = refs[0]
    a_refs = refs[1:1 + n_act]
    w_refs = refs[1 + n_act:1 + 2 * n_act]
    (g_ref, wr_ref, br_ref, xo_ref, hp_ref, route_ref, count_ref,
     run_ref, xs_ref, lg_ref) = refs[1 + 2 * n_act:]
    i = pl.program_id(0)

    @pl.when(i == 0)
    def _():
        run_ref[...] = jnp.zeros_like(run_ref)
        xs_ref[1] = jnp.zeros(xs_ref.shape[1:], xs_ref.dtype)
        lg_ref[1] = jnp.zeros(lg_ref.shape[1:], lg_ref.dtype)

    def step(slot):
        x = x_ref[...]
        for a_ref, w_ref in zip(a_refs, w_refs):
            x = x + jnp.dot(a_ref[...], w_ref[...], preferred_element_type=F32)
        xo_ref[...] = x
        xs_ref[slot] = x

        run = run_ref[...]
        slab, run_next = _route(lg_ref[1 - slot], run)
        route_ref[0] = slab.T[:ROUTE_FIELDS, :]
        run = jnp.where(i >= 2, run_next, run)
        run_ref[...] = run
        count_ref[...] = run

        h = _rms(xs_ref[1 - slot], g_ref[...])
        hp_ref[...] = _pack_bf16_pairs(h)
        h_hi = h.astype(BF16)
        h_lo = (h - h_hi.astype(F32)).astype(BF16)
        p = (jnp.dot(h_hi, wr_ref[...], preferred_element_type=F32)
             + jnp.dot(h_lo, wr_ref[...], preferred_element_type=F32))
        lg_ref[slot] = p[:, :ROUTE_LANES] + p[:, ROUTE_LANES:] + br_ref[...]

    for slot in range(2):
        pl.when(i % 2 == slot)(functools.partial(step, slot))


def _outproj_router(x, acts, ws, g, w_rg, b_rg, w_re, b_re, *, tm=TOKEN_TILE):
    t, d = x.shape
    used = N_GROUPS + N_EXPERTS
    wr = jnp.zeros((d, ROUTE_LANES), F32).at[:, :N_GROUPS].set(w_rg).at[:, N_GROUPS:used].set(w_re)
    wr_hi = wr.astype(BF16)
    wr_lo = (wr - wr_hi.astype(F32)).astype(BF16)
    wr2 = jnp.concatenate([wr_hi, wr_lo], axis=1)
    br = jnp.zeros((1, ROUTE_LANES), F32).at[0, :N_GROUPS].set(b_rg).at[0, N_GROUPS:used].set(b_re)
    nt = t // tm
    cur = lambda i: (jnp.minimum(i, nt - 1), 0)
    prev = lambda i: (jnp.clip(i - 1, 0, nt - 1), 0)
    prev2 = lambda i: (jnp.maximum(i - 2, 0), 0, 0)
    row = pl.BlockSpec((tm, d), cur)
    const = lambda a: pl.BlockSpec(a.shape, lambda i: (0, 0))
    in_specs = [row]
    in_specs += [pl.BlockSpec((tm, a.shape[1]), cur) for a in acts]
    in_specs += [const(w) for w in ws]
    in_specs += [pl.BlockSpec((1, d), lambda i: (0, 0)), const(wr2), const(br)]
    return pl.pallas_call(
        functools.partial(_outproj_router_kernel, n_act=len(acts)),
        grid=(nt + 2,),
        in_specs=in_specs,
        out_specs=[row, pl.BlockSpec((tm, d // 2), prev),
                   pl.BlockSpec((1, ROUTE_FIELDS, tm), prev2),
                   pl.BlockSpec((1, ROUTE_LANES), lambda i: (0, 0))],
        out_shape=[jax.ShapeDtypeStruct((t, d), F32), jax.ShapeDtypeStruct((t, d // 2), jnp.uint32),
                   jax.ShapeDtypeStruct((nt, ROUTE_FIELDS, tm), F32),
                   jax.ShapeDtypeStruct((1, ROUTE_LANES), F32)],
        scratch_shapes=[pltpu.VMEM((1, ROUTE_LANES), F32), pltpu.VMEM((2, tm, d), F32),
                        pltpu.VMEM((2, tm, ROUTE_LANES), F32)],
        compiler_params=_params("arbitrary"),
        name="outproj_router",
    )(x, *acts, *ws, g.reshape(1, d), wr2, br)


def _expert_changed(be_ref, b):
    return (b == 0) | (be_ref[b] != be_ref[jnp.maximum(b - 1, 0)])


def _moe_ffn_kernel(be_ref, nu_ref, ne_ref, *refs, layer):
    nslot = MOE_ROW_SLOTS
    first_refs, ahead_ref = refs[:nslot - 1], refs[nslot - 1]
    (h_hbm, wg_hbm, wu_hbm, wd_hbm, out_ref,
     xbuf, stage_g, stage_u, stage_d, wg_bf, wu_bf, wd_bf, xsem, wsem) = refs[nslot:]
    b = pl.program_id(0)
    nu = nu_ref[0]
    blk = xbuf.shape[1]

    def start_rows(idx_ref, dst_slot):
        for r in range(blk):
            pltpu.make_async_copy(h_hbm.at[pl.ds(idx_ref[0, 0, r], 1)],
                                  xbuf.at[dst_slot, pl.ds(r, 1)], xsem.at[dst_slot]).start(priority=0)

    def wait_rows(dst_slot):
        pltpu.make_async_copy(h_hbm.at[pl.ds(0, blk)], xbuf.at[dst_slot], xsem.at[dst_slot]).wait()

    def weight_copies(e):
        return (pltpu.make_async_copy(wg_hbm.at[layer, e], stage_g, wsem.at[0]),
                pltpu.make_async_copy(wu_hbm.at[layer, e], stage_u, wsem.at[1]),
                pltpu.make_async_copy(wd_hbm.at[layer, e], stage_d, wsem.at[2]))

    def start_weights(e):
        for cp in weight_copies(e):
            cp.start(priority=1)

    @pl.when(b == 0)
    def _():
        start_weights(be_ref[0])
        for slot, idx_ref in enumerate(first_refs):
            start_rows(idx_ref, slot)

    changed = _expert_changed(be_ref, b)

    def compute(slot, new_expert):
        if new_expert:
            for cp in weight_copies(be_ref[b]):
                cp.wait()
        wait_rows(slot)
        x = _unpack_bf16_pairs(xbuf[slot])
        start_rows(ahead_ref, (slot + nslot - 1) % nslot)
        if new_expert:
            wg_bf[...] = stage_g[...].astype(BF16)
        hg = jnp.dot(x, wg_bf[...], preferred_element_type=F32)
        if new_expert:
            wu_bf[...] = stage_u[...].astype(BF16)
        hu = jnp.dot(x, wu_bf[...], preferred_element_type=F32)
        if new_expert:
            wd_bf[...] = stage_d[...].astype(BF16)
        hid = (jax.nn.silu(hg) * hu).astype(BF16)
        out_ref[...] = jnp.dot(hid, wd_bf[...], preferred_element_type=F32)

        if new_expert:
            @pl.when(ne_ref[b] >= 0)
            def _():
                start_weights(ne_ref[b])

        @pl.when(b == nu - 1)
        def _():
            for k in range(1, nslot):
                wait_rows((slot + k) % nslot)

    for slot in range(nslot):
        here = (b < nu) & (b % nslot == slot)
        pl.when(here & changed)(functools.partial(compute, slot, True))
        pl.when(here & jnp.logical_not(changed))(functools.partial(compute, slot, False))

    @pl.when(b >= nu)
    def _():
        out_ref[...] = jnp.zeros_like(out_ref)


def _combine_kernel(*refs, final):
    if final:
        pos0_ref, pos1_ref, ahead_ref, x_ref, route_ref, eo_hbm, g_ref, o_ref, ybuf, sem = refs
    else:
        pos0_ref, pos1_ref, ahead_ref, x_ref, route_ref, eo_hbm, o_ref, ybuf, sem = refs
    i = pl.program_id(0)
    last = pl.num_programs(0) - 1
    tm = x_ref.shape[0]
    nslot = COMBINE_SLOTS

    def start_rows(idx_ref, dst_slot):
        for r in range(tm):
            for k in range(TOP_K):
                pltpu.make_async_copy(eo_hbm.at[pl.ds(idx_ref[0, 0, k * tm + r], 1)],
                                      ybuf.at[dst_slot, k, pl.ds(r, 1)], sem.at[dst_slot]).start(priority=k)

    def wait_rows(dst_slot):
        for k in range(TOP_K):
            pltpu.make_async_copy(eo_hbm.at[pl.ds(0, tm)], ybuf.at[dst_slot, k], sem.at[dst_slot]).wait()

    @pl.when(i == 0)
    def _():
        start_rows(pos0_ref, 0)
        start_rows(pos1_ref, 1)

    def step(slot):
        wait_rows(slot)
        start_rows(ahead_ref, (slot + 2) % nslot)
        route = route_ref[0].T
        y = ybuf[slot, 0] * route[:, TOP_K:TOP_K + 1] + ybuf[slot, 1] * route[:, TOP_K + 1:TOP_K + 2]
        x = x_ref[...] + y
        o_ref[...] = _rms(x, g_ref[...]) if final else x

        @pl.when(i == last)
        def _():
            wait_rows((slot + 1) % nslot)
            wait_rows((slot + 2) % nslot)

    for slot in range(nslot):
        pl.when(i % nslot == slot)(functools.partial(step, slot))


def _moe_dispatch(route, counts_slab, t):
    m = t * TOP_K
    nt, _, tm = route.shape
    expert = route[:, :TOP_K, :].astype(jnp.int32)
    rank = route[:, 2 * TOP_K:3 * TOP_K, :].astype(jnp.int32)
    counts = counts_slab[0, N_GROUPS:N_GROUPS + N_EXPERTS].astype(jnp.int32)
    padded = (counts + MOE_BLOCK - 1) // MOE_BLOCK * MOE_BLOCK
    padded_ends = jnp.cumsum(padded)
    padded_starts = padded_ends - padded
    ids = jnp.arange(N_EXPERTS, dtype=jnp.int32)
    start_of = jnp.sum(jnp.where(expert[..., None] == ids, padded_starts, 0), axis=-1)
    dest = start_of + rank
    n_blocks = -(-(m + N_EXPERTS * (MOE_BLOCK - 1)) // MOE_BLOCK)
    cap = n_blocks * MOE_BLOCK
    token = (jnp.arange(nt, dtype=jnp.int32)[:, None, None] * tm
             + jnp.arange(tm, dtype=jnp.int32)[None, None, :])
    token = jnp.broadcast_to(token, dest.shape)
    assert cap == m + N_EXPERTS * MOE_BLOCK
    pad_i = jnp.arange(MOE_BLOCK, dtype=jnp.int32)[None, :]
    pad_row = jnp.where(pad_i < (padded - counts)[:, None],
                        (padded_starts + counts)[:, None] + pad_i,
                        cap + ids[:, None] * MOE_BLOCK + pad_i)
    rows = jnp.concatenate([dest.reshape(m), pad_row.reshape(-1)])
    toks = jnp.concatenate([token.reshape(m), jnp.zeros((N_EXPERTS * MOE_BLOCK,), jnp.int32)])
    _, buf_src = lax.sort_key_val(rows, toks)
    block_start = jnp.arange(n_blocks, dtype=jnp.int32) * MOE_BLOCK
    block_expert = jnp.minimum(
        jnp.sum((padded_ends[None, :] <= block_start[:, None]).astype(jnp.int32), axis=1), N_EXPERTS - 1)
    n_used = (padded_ends[-1] // MOE_BLOCK).astype(jnp.int32).reshape(1)
    ids = jnp.arange(N_EXPERTS, dtype=jnp.int32)
    later = (ids[None, :] > ids[:, None]) & (counts[None, :] > 0)
    next_active = jnp.min(jnp.where(later, ids[None, :], N_EXPERTS), axis=1)
    next_active = jnp.where(next_active < N_EXPERTS, next_active, -1)
    block_next = next_active[block_expert]
    return block_expert, n_used, block_next, buf_src.reshape(n_blocks, 1, MOE_BLOCK), dest


def _moe_experts(h_packed, block_expert, n_used, block_next, buf_src, w_gate, w_up, w_down, layer):
    d, ff = w_gate.shape[-2:]
    n_blocks = buf_src.shape[0]
    cap = n_blocks * MOE_BLOCK
    idx_block = (1, 1, MOE_BLOCK)
    hbm = pl.BlockSpec(memory_space=pl.ANY)

    look = MOE_ROW_SLOTS - 1

    def fixed(k, b, *_):
        return (min(k, n_blocks - 1), 0, 0)

    def ahead(b, *_):
        return (jnp.minimum(b + look, n_blocks - 1), 0, 0)

    idx_specs = [pl.BlockSpec(idx_block, functools.partial(fixed, k), memory_space=pltpu.SMEM)
                 for k in range(look)]
    idx_specs.append(pl.BlockSpec(idx_block, ahead, memory_space=pltpu.SMEM))
    return pl.pallas_call(
        functools.partial(_moe_ffn_kernel, layer=layer),
        grid_spec=pltpu.PrefetchScalarGridSpec(
            num_scalar_prefetch=3,
            grid=(n_blocks,),
            in_specs=idx_specs + [hbm, hbm, hbm, hbm],
            out_specs=pl.BlockSpec((MOE_BLOCK, d), lambda b, *_: (b, 0)),
            scratch_shapes=[pltpu.VMEM((MOE_ROW_SLOTS, MOE_BLOCK, d // 2), jnp.uint32),
                            pltpu.VMEM((d, ff), F32), pltpu.VMEM((d, ff), F32), pltpu.VMEM((ff, d), F32),
                            pltpu.VMEM((d, ff), BF16), pltpu.VMEM((d, ff), BF16), pltpu.VMEM((ff, d), BF16),
                            pltpu.SemaphoreType.DMA((MOE_ROW_SLOTS,)), pltpu.SemaphoreType.DMA((3,))],
        ),
        out_shape=jax.ShapeDtypeStruct((cap, d), F32),
        compiler_params=pltpu.CompilerParams(dimension_semantics=("arbitrary",),
                                             vmem_limit_bytes=MOE_VMEM_LIMIT),
        name="moe_ffn",
    )(block_expert, n_used, block_next, *([buf_src] * MOE_ROW_SLOTS), h_packed, w_gate, w_up, w_down)


def _combine(x, route, dest, expert_out, final_gain):
    t, d = x.shape
    nt, _, tm = route.shape
    final = final_gain is not None
    pos = dest.reshape(nt, 1, TOP_K * tm)
    idx_block = (1, 1, TOP_K * tm)
    row = pl.BlockSpec((tm, d), lambda i: (i, 0))
    in_specs = [pl.BlockSpec(idx_block, lambda i: (0, 0, 0), memory_space=pltpu.SMEM),
                pl.BlockSpec(idx_block, lambda i: (min(1, nt - 1), 0, 0), memory_space=pltpu.SMEM),
                pl.BlockSpec(idx_block, lambda i: (jnp.minimum(i + 2, nt - 1), 0, 0),
                             memory_space=pltpu.SMEM),
                row, pl.BlockSpec((1, ROUTE_FIELDS, tm), lambda i: (i, 0, 0)),
                pl.BlockSpec(memory_space=pl.ANY)]
    args = [pos, pos, pos, x, route, expert_out]
    if final:
        in_specs.append(pl.BlockSpec((1, d), lambda i: (0, 0)))
        args.append(final_gain.reshape(1, d))
    return pl.pallas_call(
        functools.partial(_combine_kernel, final=final),
        grid=(nt,),
        in_specs=in_specs, out_specs=row,
        out_shape=jax.ShapeDtypeStruct((t, d), F32),
        scratch_shapes=[pltpu.VMEM((COMBINE_SLOTS, TOP_K, tm, d), F32),
                        pltpu.SemaphoreType.DMA((COMBINE_SLOTS,))],
        compiler_params=_params("arbitrary"),
        name="moe_combine",
    )(*args)


def kernel(x, attn_norm, ffn_norm, final_norm, w_in_ab, ret_norm, gmlp_norm, gmlp_ws, gmlp_bs, w_out_ab, w_in_c, lb_params, hgrn_norm, w_out_c, router_w_group, router_b_group, router_w_expert, router_b_expert, w_gate, w_up, w_down):
    b, s, d = x.shape
    assert b == 1, "the sequence mixers carry state along the flattened token axis"
    depth = attn_norm.shape[0]
    lb_soft = jax.nn.softmax(lb_params.astype(F32), axis=0)
    lower_bounds = jnp.cumsum(lb_soft, axis=0) - lb_soft[0]
    xt = x.reshape(b * s, d)
    t = b * s
    for layer in range(depth):
        i = layer // 2
        if layer % 2 == 0:
            proj = _norm_matmul(xt, attn_norm[layer], w_in_ab[i].astype(BF16))
            ret = _retention(proj, ret_norm[i])
            gm = _gmlp(proj, gmlp_norm[i], gmlp_ws[i], gmlp_bs[i])
            w_out = w_out_ab[i].astype(BF16)
            nr = ret.shape[1]
            acts, w_outs = [ret, gm], [w_out[:nr], w_out[nr:]]
        else:
            proj = _norm_matmul(xt, attn_norm[layer], w_in_c[i].astype(BF16))
            acts = [_hgrn2(proj, lower_bounds[layer], hgrn_norm[i])]
            w_outs = [w_out_c[i].astype(BF16)]
        xt, h_packed, route, counts = _outproj_router(
            xt, acts, w_outs, ffn_norm[layer], router_w_group[layer], router_b_group[layer],
            router_w_expert[layer], router_b_expert[layer])
        block_expert, n_used, block_next, buf_src, dest = _moe_dispatch(route, counts, t)
        expert_out = _moe_experts(h_packed, block_expert, n_used, block_next, buf_src,
                                  w_gate, w_up, w_down, layer)
        xt = _combine(xt, route, dest, expert_out, final_norm if layer == depth - 1 else None)
    return xt.reshape(b, s, d)
```

```python
import functools

import jax
import jax.numpy as jnp
from jax import lax
from jax.experimental import pallas as pl
from jax.experimental.pallas import tpu as pltpu

F32 = jnp.float32
BF16 = jnp.bfloat16
EPS = 1e-6

RET_HEADS = 4
RET_CHUNK = 128
ROPE_BASE = 10000.0
GMLP_GROUPS = 4
GMLP_CHUNK = 128
HGRN_DK = 128
HGRN_CHUNK = 32
N_GROUPS = 4
EXPERTS_PER_GROUP = 8
N_EXPERTS = N_GROUPS * EXPERTS_PER_GROUP
TOP_K = 2
MOE_BLOCK = 128
MOE_ROW_SLOTS = 4
COMBINE_SLOTS = 3
ROUTE_LANES = 128
NORM_SLAB = 256
ROUTE_FIELDS = 8
TOKEN_TILE = 256

VMEM_LIMIT = 48 * 1024 * 1024
MOE_VMEM_LIMIT = 56 * 1024 * 1024


def _params(*sem):
    return pltpu.CompilerParams(dimension_semantics=sem, vmem_limit_bytes=VMEM_LIMIT)


def _rms(x, g):
    return x * lax.rsqrt(jnp.mean(x * x, axis=-1, keepdims=True) + EPS) * g


def _norm_matmul_kernel(x_ref, g_ref, w_ref, proj_ref, xn_ref):
    @pl.when(pl.program_id(1) == 0)
    def _():
        for r0 in range(0, x_ref.shape[0], NORM_SLAB):
            rows = pl.ds(r0, NORM_SLAB)
            xn_ref[rows, :] = _rms(x_ref[rows, :], g_ref[...]).astype(BF16)

    proj_ref[...] = jnp.dot(xn_ref[...], w_ref[...],
                            preferred_element_type=F32).astype(proj_ref.dtype)


def _norm_matmul(x, g, w, *, tm=1024, tn=1024):
    t, d = x.shape
    n = w.shape[1]
    return pl.pallas_call(
        _norm_matmul_kernel,
        grid=(t // tm, n // tn),
        in_specs=[pl.BlockSpec((tm, d), lambda i, j: (i, 0)),
                  pl.BlockSpec((1, d), lambda i, j: (0, 0)),
                  pl.BlockSpec((d, tn), lambda i, j: (0, j))],
        out_specs=pl.BlockSpec((tm, tn), lambda i, j: (i, j)),
        out_shape=jax.ShapeDtypeStruct((t, n), BF16),
        scratch_shapes=[pltpu.VMEM((tm, d), BF16)],
        compiler_params=_params("arbitrary", "arbitrary"),
        name="norm_matmul",
    )(x, g.reshape(1, d), w)


def _rope(x, cos, sin):
    half = x.shape[-1] // 2
    x1, x2 = x[:, :half], x[:, half:]
    return jnp.concatenate([x1 * cos - x2 * sin, x2 * cos + x1 * sin], axis=-1)


def _retention_kernel(cd_ref, q_ref, k_ref, v_ref, g_ref, cos_ref, sin_ref, dint_ref, qd_ref, kd_ref,
                      gain_ref, o_ref, state_ref, *, chunk, nchunk):
    @pl.when(pl.program_id(1) == 0)
    def _():
        state_ref[...] = jnp.zeros_like(state_ref)

    dk = q_ref.shape[-1]
    dint = dint_ref[0]
    qd = qd_ref[0]
    kd = kd_ref[0]
    cd = cd_ref[pl.program_id(0)]
    gain = gain_ref[...]
    for c in range(nchunk):
        rows = pl.ds(c * chunk, chunk)
        cos = cos_ref[rows, :]
        sin = sin_ref[rows, :]
        q = _rope(q_ref[rows, :].astype(F32), cos, sin)
        k = _rope(k_ref[rows, :].astype(F32), cos, sin) * (dk ** -0.5)
        v = v_ref[rows, :]
        scores = lax.dot_general(q.astype(BF16), k.astype(BF16), (((1,), (1,)), ((), ())),
                                 preferred_element_type=F32) * dint
        state = state_ref[...]
        o = (jnp.dot(scores.astype(BF16), v, preferred_element_type=F32)
             + jnp.dot((q * qd).astype(BF16), state.astype(BF16), preferred_element_type=F32))
        state_ref[...] = cd * state + lax.dot_general(
            (k * kd).astype(BF16), v, (((0,), (0,)), ((), ())), preferred_element_type=F32)
        o = o - jnp.mean(o, axis=-1, keepdims=True)
        o = o * lax.rsqrt(jnp.mean(o * o, axis=-1, keepdims=True) + EPS) * gain
        o_ref[rows, :] = (jax.nn.silu(g_ref[rows, :].astype(F32)) * o).astype(o_ref.dtype)


def _retention(proj, gain, *, rows=1024):
    t = proj.shape[0]
    h = RET_HEADS
    dk = proj.shape[1] // 6 // h
    dv = dk
    c = RET_CHUNK
    f32 = F32
    inv = ROPE_BASE ** (-jnp.arange(0, dk, 2, dtype=f32) / dk)
    ang = jnp.arange(t, dtype=f32)[:, None] * inv[None, :]
    cos, sin = jnp.cos(ang), jnp.sin(ang)
    log_gamma = jnp.log(1.0 - jnp.exp2(-5.0 - jnp.arange(h, dtype=f32)))
    idx = jnp.arange(c, dtype=f32)
    diff = idx[:, None] - idx[None, :]
    d_intra = jnp.where(diff >= 0, jnp.exp(jnp.maximum(diff, 0.0) * log_gamma[:, None, None]), 0.0)
    q_decay = jnp.broadcast_to(jnp.exp((idx + 1.0)[None, :] * log_gamma[:, None])[..., None], (h, c, dk))
    k_decay = jnp.broadcast_to(jnp.exp((c - 1.0 - idx)[None, :] * log_gamma[:, None])[..., None], (h, c, dk))
    chunk_decay = jnp.exp(c * log_gamma)

    def col(off):
        return pl.BlockSpec((rows, dk), lambda hh, i: (i, off + hh))

    tab = pl.BlockSpec((rows, dk // 2), lambda hh, i: (i, 0))
    per_head = lambda shp: pl.BlockSpec((1,) + shp, lambda hh, i: (hh, 0, 0))
    return pl.pallas_call(
        functools.partial(_retention_kernel, chunk=c, nchunk=rows // c),
        grid=(h, t // rows),
        in_specs=[pl.BlockSpec(memory_space=pltpu.SMEM),
                  col(0), col(h), col(2 * h), col(3 * h), tab, tab,
                  per_head((c, c)), per_head((c, dk)), per_head((c, dk)),
                  pl.BlockSpec((1, dv), lambda hh, i: (0, hh))],
        out_specs=pl.BlockSpec((rows, dv), lambda hh, i: (i, hh)),
        out_shape=jax.ShapeDtypeStruct((t, h * dv), BF16),
        scratch_shapes=[pltpu.VMEM((dk, dv), F32)],
        compiler_params=_params("arbitrary", "arbitrary"),
        name="retention",
    )(chunk_decay, proj, proj, proj, proj, cos, sin, d_intra, q_decay, k_decay, gain.reshape(1, h * dv))


def _gmlp_kernel(u_ref, vs_ref, gain_ref, ws_ref, bs_ref, o_ref, *, chunk, nchunk):
    r = lax.broadcasted_iota(jnp.int32, (chunk, chunk), 0)
    s = lax.broadcasted_iota(jnp.int32, (chunk, chunk), 1)
    w = jnp.where(r >= s, ws_ref[0], 0.0).astype(BF16)
    gain = gain_ref[...]
    bs = bs_ref[0]
    for c in range(nchunk):
        rows = pl.ds(c * chunk, chunk)
        v = jax.nn.gelu(vs_ref[rows, :].astype(F32))
        v = v - jnp.mean(v, axis=-1, keepdims=True)
        v = v * lax.rsqrt(jnp.mean(v * v, axis=-1, keepdims=True) + EPS) * gain
        mixed = jnp.dot(w, v.astype(BF16), preferred_element_type=F32) + bs
        o_ref[rows, :] = (jax.nn.gelu(u_ref[rows, :].astype(F32)) * mixed).astype(o_ref.dtype)


def _gmlp(proj, gain, ws, bs, *, rows=1024):
    t = proj.shape[0]
    g = GMLP_GROUPS
    dim = proj.shape[1] // 6 // g
    c = GMLP_CHUNK
    return pl.pallas_call(
        functools.partial(_gmlp_kernel, chunk=c, nchunk=rows // c),
        grid=(g, t // rows),
        in_specs=[pl.BlockSpec((rows, dim), lambda gg, i: (i, 4 * g + gg)),
                  pl.BlockSpec((rows, dim), lambda gg, i: (i, 5 * g + gg)),
                  pl.BlockSpec((1, dim), lambda gg, i: (0, gg)),
                  pl.BlockSpec((1, c, c), lambda gg, i: (gg, 0, 0)),
                  pl.BlockSpec((1, c, 1), lambda gg, i: (gg, 0, 0))],
        out_specs=pl.BlockSpec((rows, dim), lambda gg, i: (i, gg)),
        out_shape=jax.ShapeDtypeStruct((t, g * dim), BF16),
        compiler_params=_params("arbitrary", "arbitrary"),
        name="gmlp",
    )(proj, proj, gain.reshape(1, g * dim), ws, bs.reshape(g, c, 1))


def _hgrn_kernel(zq_ref, zf_ref, zi_ref, zg_ref, lb_ref, gain_ref, tri_ref, keep_ref, o_ref,
                 state_ref, qd_ref, kd_ref, ko_ref, dec_ref, *, chunk, nchunk, heads, dk):
    i = pl.program_id(1)
    rows = chunk * nchunk
    hw = heads * dk

    @pl.when(i == 0)
    def _():
        state_ref[...] = jnp.zeros_like(state_ref)
        for ref in (qd_ref, kd_ref, ko_ref, dec_ref):
            ref[1] = jnp.zeros(ref.shape[1:], ref.dtype)

    nt = (((1,), (1,)), ((), ()))
    head_cols = [slice(hd * dk, (hd + 1) * dk) for hd in range(heads)]

    def matmul_stage(prev):
        keep = keep_ref[...] > 0.0
        chunk_of_row = lax.broadcasted_iota(jnp.int32, (rows, dk), 0) // chunk
        q_dec, k_dec, k_out, dec = qd_ref[prev], kd_ref[prev], ko_ref[prev], dec_ref[prev]
        v = zi_ref[...]
        gate = jax.nn.silu(zg_ref[...].astype(F32))
        scores = [lax.dot_general(q_dec[:, cs], k_dec[:, cs], nt, preferred_element_type=F32)
                  for cs in head_cols]
        scores = [jnp.where(keep, s, 0.0).astype(BF16) for s in scores]
        o_intra = [jnp.dot(s, v[:, cs], preferred_element_type=F32) for s, cs in zip(scores, head_cols)]
        v_t = [v[:, cs].astype(F32).T.astype(BF16) for cs in head_cols]
        zero = jnp.zeros((rows, dk), BF16)
        k_blocks = [jnp.concatenate([jnp.where(chunk_of_row == c, k_out[:, cs], zero)
                                     for c in range(nchunk)], axis=1) for cs in head_cols]
        incr = [jnp.dot(v_t[hd], k_blocks[hd], preferred_element_type=F32) for hd in range(heads)]
        return q_dec, dec, gate, o_intra, incr

    def recurrence_stage(q_dec, dec, gate, o_intra, incr):
        states = [state_ref[hd] for hd in range(heads)]
        inter = [[] for _ in range(heads)]
        for c in range(nchunk):
            sl = slice(c * chunk, (c + 1) * chunk)
            for hd, cs in enumerate(head_cols):
                inter[hd].append(lax.dot_general(q_dec[sl, cs], states[hd].astype(BF16), nt,
                                                 preferred_element_type=F32))
                states[hd] = dec[c:c + 1, cs] * states[hd] + incr[hd][:, c * dk:(c + 1) * dk]
        for hd, cs in enumerate(head_cols):
            state_ref[hd] = states[hd]
            o = o_intra[hd] + jnp.concatenate(inter[hd], axis=0)
            o = o * lax.rsqrt(jnp.mean(o * o, axis=-1, keepdims=True) + EPS) * gain_ref[:, cs]
            o_ref[:, cs] = (o * gate[:, cs]).astype(o_ref.dtype)

    def elementwise_stage(cur):
        tri = tri_ref[...]
        lb = lb_ref[...]
        zf = zf_ref[...].astype(F32)
        f = lb + (1.0 - lb) * jax.nn.sigmoid(zf)
        kk = (1.0 - lb) * jax.nn.sigmoid(-zf)
        log_f = jnp.log(f)
        p0 = log_f.astype(BF16)
        r0 = log_f - p0.astype(F32)
        p1 = r0.astype(BF16)
        p2 = (r0 - p1.astype(F32)).astype(BF16)
        cum = (jnp.dot(tri, p0, preferred_element_type=F32)
               + jnp.dot(tri, p1, preferred_element_type=F32)
               + jnp.dot(tri, p2, preferred_element_type=F32))
        lasts = [cum[(c + 1) * chunk - 1:(c + 1) * chunk, :] for c in range(nchunk)]
        last_rows = jnp.concatenate([jnp.broadcast_to(l, (chunk, hw)) for l in lasts], axis=0)
        dec_ref[cur] = jnp.exp(jnp.concatenate(lasts, axis=0))
        qd_ref[cur] = (jax.nn.silu(zq_ref[...].astype(F32)) * jnp.exp(cum)).astype(BF16)
        kd_ref[cur] = (kk * jnp.exp(-cum)).astype(BF16)
        ko_ref[cur] = (kk * jnp.exp(last_rows - cum)).astype(BF16)

    def step(slot):
        carried = matmul_stage(1 - slot)
        elementwise_stage(slot)
        recurrence_stage(*carried)

    for slot in range(2):
        pl.when(i % 2 == slot)(functools.partial(step, slot))


def _hgrn2(proj, lb, gain, *, rows=256, heads=4):
    t = proj.shape[0]
    width = proj.shape[1] // 4
    dk = HGRN_DK
    h = width // dk
    c = HGRN_CHUNK
    nchunk = rows // c
    hw = heads * dk
    groups = h // heads
    nb = t // rows
    r = jnp.arange(rows, dtype=jnp.int32)
    tri = ((r[:, None] >= r[None, :]) & ((r[:, None] // c) == (r[None, :] // c))).astype(F32)

    def cur(off):
        return pl.BlockSpec((rows, hw), lambda hh, i: (jnp.minimum(i, nb - 1), off + hh))

    def prev(off):
        return pl.BlockSpec((rows, hw), lambda hh, i: (jnp.maximum(i - 1, 0), off + hh))

    vec = pl.BlockSpec((1, hw), lambda hh, i: (0, hh))
    mask = pl.BlockSpec((rows, rows), lambda hh, i: (0, 0))
    return pl.pallas_call(
        functools.partial(_hgrn_kernel, chunk=c, nchunk=nchunk, heads=heads, dk=dk),
        grid=(groups, nb + 1),
        in_specs=[cur(0), cur(groups), prev(2 * groups), prev(3 * groups), vec, vec, mask, mask],
        out_specs=prev(0),
        out_shape=jax.ShapeDtypeStruct((t, width), BF16),
        scratch_shapes=[pltpu.VMEM((heads, dk, dk), F32),
                        pltpu.VMEM((2, rows, hw), BF16), pltpu.VMEM((2, rows, hw), BF16),
                        pltpu.VMEM((2, rows, hw), BF16), pltpu.VMEM((2, nchunk, hw), F32)],
        compiler_params=_params("arbitrary", "arbitrary"),
        name="hgrn2",
    )(proj, proj, proj, proj, lb.reshape(1, width), gain.reshape(1, width), tri.astype(BF16), tri)


def _pack_bf16_pairs(h):
    half = h.shape[1] // 2
    bits = pltpu.bitcast(h.astype(BF16).astype(F32), jnp.uint32)
    return (bits[:, :half] >> 16) | bits[:, half:]


def _unpack_bf16_pairs(words):
    lo = pltpu.bitcast(words << 16, F32)
    hi = pltpu.bitcast(words & jnp.uint32(0xFFFF0000), F32)
    return jnp.concatenate([lo, hi], axis=1).astype(BF16)


def _route(logits, run):
    lane = lax.broadcasted_iota(jnp.int32, logits.shape, 1)
    lane_f = lane.astype(F32)
    neg = -jnp.inf
    big = float(ROUTE_LANES)
    lg = jnp.where(lane < N_GROUPS, logits, neg)
    mg = jnp.max(lg, axis=-1, keepdims=True)
    eg = jnp.exp(lg - mg)
    pg = eg / jnp.sum(eg, axis=-1, keepdims=True)
    p_sel = jnp.max(pg, axis=-1, keepdims=True)
    g_sel = jnp.min(jnp.where(lg == mg, lane_f, big), axis=-1, keepdims=True)
    e_grp = ((lane - N_GROUPS) // EXPERTS_PER_GROUP).astype(F32)
    in_grp = jnp.where(lane >= N_GROUPS, e_grp, -1.0) == g_sel
    le = jnp.where(in_grp, logits, neg)
    t1 = jnp.max(le, axis=-1, keepdims=True)
    i1 = jnp.min(jnp.where(le == t1, lane_f, big), axis=-1, keepdims=True)
    le2 = jnp.where(lane_f == i1, neg, le)
    t2 = jnp.max(le2, axis=-1, keepdims=True)
    i2 = jnp.min(jnp.where(le2 == t2, lane_f, big), axis=-1, keepdims=True)
    e2 = jnp.exp(t2 - t1)
    den = 1.0 + e2
    gate1 = p_sel * (1.0 / den)
    gate2 = p_sel * (e2 / den)
    tm = logits.shape[0]
    hit1 = jnp.where(lane_f == i1, 1.0, 0.0)
    hit2 = jnp.where(lane_f == i2, 1.0, 0.0)
    hits = hit1 + hit2
    r = lax.broadcasted_iota(jnp.int32, (tm, tm), 0)
    s = lax.broadcasted_iota(jnp.int32, (tm, tm), 1)
    before = jnp.where(r > s, 1.0, 0.0).astype(BF16)
    prefix = jnp.dot(before, hits.astype(BF16), preferred_element_type=F32) + run
    rank1 = jnp.sum(prefix * hit1, axis=-1, keepdims=True)
    rank2 = jnp.sum(prefix * hit2, axis=-1, keepdims=True)
    vals = (i1 - N_GROUPS, i2 - N_GROUPS, gate1, gate2, rank1, rank2)
    slab = jnp.zeros_like(logits)
    for pos, val in enumerate(vals):
        slab = jnp.where(lane == pos, val, slab)
    return slab, run + jnp.sum(hits, axis=0, keepdims=True)


def _outproj_router_kernel(*refs, n_act):
    x_ref = refs[0]
    a_refs = refs[1:1 + n_act]
    w_refs = refs[1 + n_act:1 + 2 * n_act]
    (g_ref, wr_ref, br_ref, xo_ref, hp_ref, route_ref, count_ref,
     run_ref, xs_ref, lg_ref) = refs[1 + 2 * n_act:]
    i = pl.program_id(0)

    @pl.when(i == 0)
    def _():
        run_ref[...] = jnp.zeros_like(run_ref)
        xs_ref[1] = jnp.zeros(xs_ref.shape[1:], xs_ref.dtype)
        lg_ref[1] = jnp.zeros(lg_ref.shape[1:], lg_ref.dtype)

    def step(slot):
        x = x_ref[...]
        for a_ref, w_ref in zip(a_refs, w_refs):
            x = x + jnp.dot(a_ref[...], w_ref[...], preferred_element_type=F32)
        xo_ref[...] = x
        xs_ref[slot] = x

        run = run_ref[...]
        slab, run_next = _route(lg_ref[1 - slot], run)
        route_ref[0] = slab.T[:ROUTE_FIELDS, :]
        run = jnp.where(i >= 2, run_next, run)
        run_ref[...] = run
        count_ref[...] = run

        h = _rms(xs_ref[1 - slot], g_ref[...])
        hp_ref[...] = _pack_bf16_pairs(h)
        h_hi = h.astype(BF16)
        h_lo = (h - h_hi.astype(F32)).astype(BF16)
        p = (jnp.dot(h_hi, wr_ref[...], preferred_element_type=F32)
             + jnp.dot(h_lo, wr_ref[...], preferred_element_type=F32))
        lg_ref[slot] = p[:, :ROUTE_LANES] + p[:, ROUTE_LANES:] + br_ref[...]

    for slot in range(2):
        pl.when(i % 2 == slot)(functools.partial(step, slot))


def _outproj_router(x, acts, ws, g, w_rg, b_rg, w_re, b_re, *, tm=TOKEN_TILE):
    t, d = x.shape
    used = N_GROUPS + N_EXPERTS
    wr = jnp.zeros((d, ROUTE_LANES), F32).at[:, :N_GROUPS].set(w_rg).at[:, N_GROUPS:used].set(w_re)
    wr_hi = wr.astype(BF16)
    wr_lo = (wr - wr_hi.astype(F32)).astype(BF16)
    wr2 = jnp.concatenate([wr_hi, wr_lo], axis=1)
    br = jnp.zeros((1, ROUTE_LANES), F32).at[0, :N_GROUPS].set(b_rg).at[0, N_GROUPS:used].set(b_re)
    nt = t // tm
    cur = lambda i: (jnp.minimum(i, nt - 1), 0)
    prev = lambda i: (jnp.clip(i - 1, 0, nt - 1), 0)
    prev2 = lambda i: (jnp.maximum(i - 2, 0), 0, 0)
    row = pl.BlockSpec((tm, d), cur)
    const = lambda a: pl.BlockSpec(a.shape, lambda i: (0, 0))
    in_specs = [row]
    in_specs += [pl.BlockSpec((tm, a.shape[1]), cur) for a in acts]
    in_specs += [const(w) for w in ws]
    in_specs += [pl.BlockSpec((1, d), lambda i: (0, 0)), const(wr2), const(br)]
    return pl.pallas_call(
        functools.partial(_outproj_router_kernel, n_act=len(acts)),
        grid=(nt + 2,),
        in_specs=in_specs,
        out_specs=[row, pl.BlockSpec((tm, d // 2), prev),
                   pl.BlockSpec((1, ROUTE_FIELDS, tm), prev2),
                   pl.BlockSpec((1, ROUTE_LANES), lambda i: (0, 0))],
        out_shape=[jax.ShapeDtypeStruct((t, d), F32), jax.ShapeDtypeStruct((t, d // 2), jnp.uint32),
                   jax.ShapeDtypeStruct((nt, ROUTE_FIELDS, tm), F32),
                   jax.ShapeDtypeStruct((1, ROUTE_LANES), F32)],
        scratch_shapes=[pltpu.VMEM((1, ROUTE_LANES), F32), pltpu.VMEM((2, tm, d), F32),
                        pltpu.VMEM((2, tm, ROUTE_LANES), F32)],
        compiler_params=_params("arbitrary"),
        name="outproj_router",
    )(x, *acts, *ws, g.reshape(1, d), wr2, br)


def _expert_changed(be_ref, b):
    return (b == 0) | (be_ref[b] != be_ref[jnp.maximum(b - 1, 0)])


def _moe_ffn_kernel(be_ref, nu_ref, ne_ref, *refs, layer):
    nslot = MOE_ROW_SLOTS
    first_refs, ahead_ref = refs[:nslot - 1], refs[nslot - 1]
    (h_hbm, wg_hbm, wu_hbm, wd_hbm, out_ref,
     xbuf, stage_g, stage_u, stage_d, wg_bf, wu_bf, wd_bf, xsem, wsem) = refs[nslot:]
    b = pl.program_id(0)
    nu = nu_ref[0]
    blk = xbuf.shape[1]

    def start_rows(idx_ref, dst_slot):
        for r in range(blk):
            pltpu.make_async_copy(h_hbm.at[pl.ds(idx_ref[0, 0, r], 1)],
                                  xbuf.at[dst_slot, pl.ds(r, 1)], xsem.at[dst_slot]).start(priority=0)

    def wait_rows(dst_slot):
        pltpu.make_async_copy(h_hbm.at[pl.ds(0, blk)], xbuf.at[dst_slot], xsem.at[dst_slot]).wait()

    def weight_copies(e):
        return (pltpu.make_async_copy(wg_hbm.at[layer, e], stage_g, wsem.at[0]),
                pltpu.make_async_copy(wu_hbm.at[layer, e], stage_u, wsem.at[1]),
                pltpu.make_async_copy(wd_hbm.at[layer, e], stage_d, wsem.at[2]))

    def start_weights(e):
        for cp in weight_copies(e):
            cp.start(priority=1)

    @pl.when(b == 0)
    def _():
        start_weights(be_ref[0])
        for slot, idx_ref in enumerate(first_refs):
            start_rows(idx_ref, slot)

    changed = _expert_changed(be_ref, b)

    def compute(slot, new_expert):
        if new_expert:
            for cp in weight_copies(be_ref[b]):
                cp.wait()
        wait_rows(slot)
        x = _unpack_bf16_pairs(xbuf[slot])
        start_rows(ahead_ref, (slot + nslot - 1) % nslot)
        refill = weight_copies(ne_ref[b]) if new_expert else None
        if new_expert:
            wg_bf[...] = stage_g[...].astype(BF16)
            refill[0].start(priority=1)
        hg = jnp.dot(x, wg_bf[...], preferred_element_type=F32)
        if new_expert:
            wu_bf[...] = stage_u[...].astype(BF16)
            refill[1].start(priority=1)
        hu = jnp.dot(x, wu_bf[...], preferred_element_type=F32)
        if new_expert:
            wd_bf[...] = stage_d[...].astype(BF16)
            refill[2].start(priority=1)
        hid = (jax.nn.silu(hg) * hu).astype(BF16)
        out_ref[...] = jnp.dot(hid, wd_bf[...], preferred_element_type=F32)

        @pl.when(b == nu - 1)
        def _():
            for k in range(1, nslot):
                wait_rows((slot + k) % nslot)
            for cp in weight_copies(be_ref[b]):
                cp.wait()

    for slot in range(nslot):
        here = (b < nu) & (b % nslot == slot)
        pl.when(here & changed)(functools.partial(compute, slot, True))
        pl.when(here & jnp.logical_not(changed))(functools.partial(compute, slot, False))

    @pl.when(b >= nu)
    def _():
        out_ref[...] = jnp.zeros_like(out_ref)


def _combine_kernel(*refs, final):
    if final:
        pos0_ref, pos1_ref, ahead_ref, x_ref, route_ref, eo_hbm, g_ref, o_ref, ybuf, sem = refs
    else:
        pos0_ref, pos1_ref, ahead_ref, x_ref, route_ref, eo_hbm, o_ref, ybuf, sem = refs
    i = pl.program_id(0)
    last = pl.num_programs(0) - 1
    tm = x_ref.shape[0]
    nslot = COMBINE_SLOTS

    def start_rows(idx_ref, dst_slot):
        for r in range(tm):
            for k in range(TOP_K):
                pltpu.make_async_copy(eo_hbm.at[pl.ds(idx_ref[0, 0, k * tm + r], 1)],
                                      ybuf.at[dst_slot, k, pl.ds(r, 1)], sem.at[dst_slot]).start(priority=k)

    def wait_rows(dst_slot):
        for k in range(TOP_K):
            pltpu.make_async_copy(eo_hbm.at[pl.ds(0, tm)], ybuf.at[dst_slot, k], sem.at[dst_slot]).wait()

    @pl.when(i == 0)
    def _():
        start_rows(pos0_ref, 0)
        start_rows(pos1_ref, 1)

    def step(slot):
        wait_rows(slot)
        start_rows(ahead_ref, (slot + 2) % nslot)
        route = route_ref[0].T
        y = ybuf[slot, 0] * route[:, TOP_K:TOP_K + 1] + ybuf[slot, 1] * route[:, TOP_K + 1:TOP_K + 2]
        x = x_ref[...] + y
        o_ref[...] = _rms(x, g_ref[...]) if final else x

        @pl.when(i == last)
        def _():
            wait_rows((slot + 1) % nslot)
            wait_rows((slot + 2) % nslot)

    for slot in range(nslot):
        pl.when(i % nslot == slot)(functools.partial(step, slot))


def _moe_dispatch(route, counts_slab, t):
    m = t * TOP_K
    nt, _, tm = route.shape
    expert = route[:, :TOP_K, :].astype(jnp.int32)
    rank = route[:, 2 * TOP_K:3 * TOP_K, :].astype(jnp.int32)
    counts = counts_slab[0, N_GROUPS:N_GROUPS + N_EXPERTS].astype(jnp.int32)
    padded = (counts + MOE_BLOCK - 1) // MOE_BLOCK * MOE_BLOCK
    padded_ends = jnp.cumsum(padded)
    padded_starts = padded_ends - padded
    ids = jnp.arange(N_EXPERTS, dtype=jnp.int32)
    start_of = jnp.sum(jnp.where(expert[..., None] == ids, padded_starts, 0), axis=-1)
    dest = start_of + rank
    n_blocks = -(-(m + N_EXPERTS * (MOE_BLOCK - 1)) // MOE_BLOCK)
    cap = n_blocks * MOE_BLOCK
    token = (jnp.arange(nt, dtype=jnp.int32)[:, None, None] * tm
             + jnp.arange(tm, dtype=jnp.int32)[None, None, :])
    token = jnp.broadcast_to(token, dest.shape)
    assert cap == m + N_EXPERTS * MOE_BLOCK
    pad_i = jnp.arange(MOE_BLOCK, dtype=jnp.int32)[None, :]
    pad_row = jnp.where(pad_i < (padded - counts)[:, None],
                        (padded_starts + counts)[:, None] + pad_i,
                        cap + ids[:, None] * MOE_BLOCK + pad_i)
    rows = jnp.concatenate([dest.reshape(m), pad_row.reshape(-1)])
    toks = jnp.concatenate([token.reshape(m), jnp.zeros((N_EXPERTS * MOE_BLOCK,), jnp.int32)])
    _, buf_src = lax.sort_key_val(rows, toks)
    block_start = jnp.arange(n_blocks, dtype=jnp.int32) * MOE_BLOCK
    block_expert = jnp.minimum(
        jnp.sum((padded_ends[None, :] <= block_start[:, None]).astype(jnp.int32), axis=1), N_EXPERTS - 1)
    n_used = (padded_ends[-1] // MOE_BLOCK).astype(jnp.int32).reshape(1)
    ids = jnp.arange(N_EXPERTS, dtype=jnp.int32)
    later = (ids[None, :] > ids[:, None]) & (counts[None, :] > 0)
    next_active = jnp.min(jnp.where(later, ids[None, :], N_EXPERTS), axis=1)
    next_active = jnp.where(next_active < N_EXPERTS, next_active, ids)
    block_next = next_active[block_expert]
    return block_expert, n_used, block_next, buf_src.reshape(n_blocks, 1, MOE_BLOCK), dest


def _moe_experts(h_packed, block_expert, n_used, block_next, buf_src, w_gate, w_up, w_down, layer):
    d, ff = w_gate.shape[-2:]
    n_blocks = buf_src.shape[0]
    cap = n_blocks * MOE_BLOCK
    idx_block = (1, 1, MOE_BLOCK)
    hbm = pl.BlockSpec(memory_space=pl.ANY)

    look = MOE_ROW_SLOTS - 1

    def fixed(k, b, *_):
        return (min(k, n_blocks - 1), 0, 0)

    def ahead(b, *_):
        return (jnp.minimum(b + look, n_blocks - 1), 0, 0)

    idx_specs = [pl.BlockSpec(idx_block, functools.partial(fixed, k), memory_space=pltpu.SMEM)
                 for k in range(look)]
    idx_specs.append(pl.BlockSpec(idx_block, ahead, memory_space=pltpu.SMEM))
    return pl.pallas_call(
        functools.partial(_moe_ffn_kernel, layer=layer),
        grid_spec=pltpu.PrefetchScalarGridSpec(
            num_scalar_prefetch=3,
            grid=(n_blocks,),
            in_specs=idx_specs + [hbm, hbm, hbm, hbm],
            out_specs=pl.BlockSpec((MOE_BLOCK, d), lambda b, *_: (b, 0)),
            scratch_shapes=[pltpu.VMEM((MOE_ROW_SLOTS, MOE_BLOCK, d // 2), jnp.uint32),
                            pltpu.VMEM((d, ff), F32), pltpu.VMEM((d, ff), F32), pltpu.VMEM((ff, d), F32),
                            pltpu.VMEM((d, ff), BF16), pltpu.VMEM((d, ff), BF16), pltpu.VMEM((ff, d), BF16),
                            pltpu.SemaphoreType.DMA((MOE_ROW_SLOTS,)), pltpu.SemaphoreType.DMA((3,))],
        ),
        out_shape=jax.ShapeDtypeStruct((cap, d), F32),
        compiler_params=pltpu.CompilerParams(dimension_semantics=("arbitrary",),
                                             vmem_limit_bytes=MOE_VMEM_LIMIT),
        name="moe_ffn",
    )(block_expert, n_used, block_next, *([buf_src] * MOE_ROW_SLOTS), h_packed, w_gate, w_up, w_down)


def _combine(x, route, dest, expert_out, final_gain):
    t, d = x.shape
    nt, _, tm = route.shape
    final = final_gain is not None
    pos = dest.reshape(nt, 1, TOP_K * tm)
    idx_block = (1, 1, TOP_K * tm)
    row = pl.BlockSpec((tm, d), lambda i: (i, 0))
    in_specs = [pl.BlockSpec(idx_block, lambda i: (0, 0, 0), memory_space=pltpu.SMEM),
                pl.BlockSpec(idx_block, lambda i: (min(1, nt - 1), 0, 0), memory_space=pltpu.SMEM),
                pl.BlockSpec(idx_block, lambda i: (jnp.minimum(i + 2, nt - 1), 0, 0),
                             memory_space=pltpu.SMEM),
                row, pl.BlockSpec((1, ROUTE_FIELDS, tm), lambda i: (i, 0, 0)),
                pl.BlockSpec(memory_space=pl.ANY)]
    args = [pos, pos, pos, x, route, expert_out]
    if final:
        in_specs.append(pl.BlockSpec((1, d), lambda i: (0, 0)))
        args.append(final_gain.reshape(1, d))
    return pl.pallas_call(
        functools.partial(_combine_kernel, final=final),
        grid=(nt,),
        in_specs=in_specs, out_specs=row,
        out_shape=jax.ShapeDtypeStruct((t, d), F32),
        scratch_shapes=[pltpu.VMEM((COMBINE_SLOTS, TOP_K, tm, d), F32),
                        pltpu.SemaphoreType.DMA((COMBINE_SLOTS,))],
        compiler_params=_params("arbitrary"),
        name="moe_combine",
    )(*args)


def kernel(x, attn_norm, ffn_norm, final_norm, w_in_ab, ret_norm, gmlp_norm, gmlp_ws, gmlp_bs, w_out_ab, w_in_c, lb_params, hgrn_norm, w_out_c, router_w_group, router_b_group, router_w_expert, router_b_expert, w_gate, w_up, w_down):
    b, s, d = x.shape
    assert b == 1, "the sequence mixers carry state along the flattened token axis"
    depth = attn_norm.shape[0]
    lb_soft = jax.nn.softmax(lb_params.astype(F32), axis=0)
    lower_bounds = jnp.cumsum(lb_soft, axis=0) - lb_soft[0]
    xt = x.reshape(b * s, d)
    t = b * s
    for layer in range(depth):
        i = layer // 2
        if layer % 2 == 0:
            proj = _norm_matmul(xt, attn_norm[layer], w_in_ab[i].astype(BF16))
            ret = _retention(proj, ret_norm[i])
            gm = _gmlp(proj, gmlp_norm[i], gmlp_ws[i], gmlp_bs[i])
            w_out = w_out_ab[i].astype(BF16)
            nr = ret.shape[1]
            acts, w_outs = [ret, gm], [w_out[:nr], w_out[nr:]]
        else:
            proj = _norm_matmul(xt, attn_norm[layer], w_in_c[i].astype(BF16))
            acts = [_hgrn2(proj, lower_bounds[layer], hgrn_norm[i])]
            w_outs = [w_out_c[i].astype(BF16)]
        xt, h_packed, route, counts = _outproj_router(
            xt, acts, w_outs, ffn_norm[layer], router_w_group[layer], router_b_group[layer],
            router_w_expert[layer], router_b_expert[layer])
        block_expert, n_used, block_next, buf_src, dest = _moe_dispatch(route, counts, t)
        expert_out = _moe_experts(h_packed, block_expert, n_used, block_next, buf_src,
                                  w_gate, w_up, w_down, layer)
        xt = _combine(xt, route, dest, expert_out, final_norm if layer == depth - 1 else None)
    return xt.reshape(b, s, d)
```

```python
import functools

import jax
import jax.numpy as jnp
from jax import lax
from jax.experimental import pallas as pl
from jax.experimental.pallas import tpu as pltpu

F32 = jnp.float32
BF16 = jnp.bfloat16
EPS = 1e-6

RET_HEADS = 4
RET_CHUNK = 128
ROPE_BASE = 10000.0
GMLP_GROUPS = 4
GMLP_CHUNK = 128
HGRN_DK = 128
HGRN_CHUNK = 32
N_GROUPS = 4
EXPERTS_PER_GROUP = 8
N_EXPERTS = N_GROUPS * EXPERTS_PER_GROUP
TOP_K = 2
MOE_BLOCK = 128
MOE_ROW_SLOTS = 4
COMBINE_SLOTS = 3
ROUTE_LANES = 128
NORM_SLAB = 256
ROUTE_FIELDS = 8
TOKEN_TILE = 256

VMEM_LIMIT = 48 * 1024 * 1024
PROJ_VMEM_LIMIT = 56 * 1024 * 1024
MOE_VMEM_LIMIT = 56 * 1024 * 1024


def _params(*sem):
    return pltpu.CompilerParams(dimension_semantics=sem, vmem_limit_bytes=VMEM_LIMIT)


def _rms(x, g):
    return x * lax.rsqrt(jnp.mean(x * x, axis=-1, keepdims=True) + EPS) * g


def _norm_matmul_kernel(x_ref, g_ref, w_ref, proj_ref, xn_ref):
    @pl.when(pl.program_id(1) == 0)
    def _():
        for r0 in range(0, x_ref.shape[0], NORM_SLAB):
            rows = pl.ds(r0, NORM_SLAB)
            xn_ref[rows, :] = _rms(x_ref[rows, :], g_ref[...]).astype(BF16)

    proj_ref[...] = jnp.dot(xn_ref[...], w_ref[...],
                            preferred_element_type=F32).astype(proj_ref.dtype)


def _norm_matmul(x, g, w, *, tm=1024, tn=2048):
    t, d = x.shape
    n = w.shape[1]
    assert 2 * tm * d * 4 + 2 * d * tn * 2 + 2 * tm * tn * 2 + tm * d * 2 <= PROJ_VMEM_LIMIT
    return pl.pallas_call(
        _norm_matmul_kernel,
        grid=(t // tm, n // tn),
        in_specs=[pl.BlockSpec((tm, d), lambda i, j: (i, 0)),
                  pl.BlockSpec((1, d), lambda i, j: (0, 0)),
                  pl.BlockSpec((d, tn), lambda i, j: (0, j))],
        out_specs=pl.BlockSpec((tm, tn), lambda i, j: (i, j)),
        out_shape=jax.ShapeDtypeStruct((t, n), BF16),
        scratch_shapes=[pltpu.VMEM((tm, d), BF16)],
        compiler_params=pltpu.CompilerParams(dimension_semantics=("arbitrary", "arbitrary"),
                                             vmem_limit_bytes=PROJ_VMEM_LIMIT),
        name="norm_matmul",
    )(x, g.reshape(1, d), w)


def _rope(x, cos, sin):
    half = x.shape[-1] // 2
    x1, x2 = x[:, :half], x[:, half:]
    return jnp.concatenate([x1 * cos - x2 * sin, x2 * cos + x1 * sin], axis=-1)


def _retention_kernel(cd_ref, q_ref, k_ref, v_ref, g_ref, cos_ref, sin_ref, dint_ref, qd_ref, kd_ref,
                      gain_ref, o_ref, state_ref, *, chunk, nchunk):
    @pl.when(pl.program_id(1) == 0)
    def _():
        state_ref[...] = jnp.zeros_like(state_ref)

    dk = q_ref.shape[-1]
    dint = dint_ref[0]
    qd = qd_ref[0]
    kd = kd_ref[0]
    cd = cd_ref[pl.program_id(0)]
    gain = gain_ref[...]
    for c in range(nchunk):
        rows = pl.ds(c * chunk, chunk)
        cos = cos_ref[rows, :]
        sin = sin_ref[rows, :]
        q = _rope(q_ref[rows, :].astype(F32), cos, sin)
        k = _rope(k_ref[rows, :].astype(F32), cos, sin) * (dk ** -0.5)
        v = v_ref[rows, :]
        scores = lax.dot_general(q.astype(BF16), k.astype(BF16), (((1,), (1,)), ((), ())),
                                 preferred_element_type=F32) * dint
        state = state_ref[...]
        o = (jnp.dot(scores.astype(BF16), v, preferred_element_type=F32)
             + jnp.dot((q * qd).astype(BF16), state.astype(BF16), preferred_element_type=F32))
        state_ref[...] = cd * state + lax.dot_general(
            (k * kd).astype(BF16), v, (((0,), (0,)), ((), ())), preferred_element_type=F32)
        o = o - jnp.mean(o, axis=-1, keepdims=True)
        o = o * lax.rsqrt(jnp.mean(o * o, axis=-1, keepdims=True) + EPS) * gain
        o_ref[rows, :] = (jax.nn.silu(g_ref[rows, :].astype(F32)) * o).astype(o_ref.dtype)


def _retention(proj, gain, *, rows=1024):
    t = proj.shape[0]
    h = RET_HEADS
    dk = proj.shape[1] // 6 // h
    dv = dk
    c = RET_CHUNK
    f32 = F32
    inv = ROPE_BASE ** (-jnp.arange(0, dk, 2, dtype=f32) / dk)
    ang = jnp.arange(t, dtype=f32)[:, None] * inv[None, :]
    cos, sin = jnp.cos(ang), jnp.sin(ang)
    log_gamma = jnp.log(1.0 - jnp.exp2(-5.0 - jnp.arange(h, dtype=f32)))
    idx = jnp.arange(c, dtype=f32)
    diff = idx[:, None] - idx[None, :]
    d_intra = jnp.where(diff >= 0, jnp.exp(jnp.maximum(diff, 0.0) * log_gamma[:, None, None]), 0.0)
    q_decay = jnp.broadcast_to(jnp.exp((idx + 1.0)[None, :] * log_gamma[:, None])[..., None], (h, c, dk))
    k_decay = jnp.broadcast_to(jnp.exp((c - 1.0 - idx)[None, :] * log_gamma[:, None])[..., None], (h, c, dk))
    chunk_decay = jnp.exp(c * log_gamma)

    def col(off):
        return pl.BlockSpec((rows, dk), lambda hh, i: (i, off + hh))

    tab = pl.BlockSpec((rows, dk // 2), lambda hh, i: (i, 0))
    per_head = lambda shp: pl.BlockSpec((1,) + shp, lambda hh, i: (hh, 0, 0))
    return pl.pallas_call(
        functools.partial(_retention_kernel, chunk=c, nchunk=rows // c),
        grid=(h, t // rows),
        in_specs=[pl.BlockSpec(memory_space=pltpu.SMEM),
                  col(0), col(h), col(2 * h), col(3 * h), tab, tab,
                  per_head((c, c)), per_head((c, dk)), per_head((c, dk)),
                  pl.BlockSpec((1, dv), lambda hh, i: (0, hh))],
        out_specs=pl.BlockSpec((rows, dv), lambda hh, i: (i, hh)),
        out_shape=jax.ShapeDtypeStruct((t, h * dv), BF16),
        scratch_shapes=[pltpu.VMEM((dk, dv), F32)],
        compiler_params=_params("arbitrary", "arbitrary"),
        name="retention",
    )(chunk_decay, proj, proj, proj, proj, cos, sin, d_intra, q_decay, k_decay, gain.reshape(1, h * dv))


def _gmlp_kernel(u_ref, vs_ref, gain_ref, ws_ref, bs_ref, o_ref, *, chunk, nchunk):
    r = lax.broadcasted_iota(jnp.int32, (chunk, chunk), 0)
    s = lax.broadcasted_iota(jnp.int32, (chunk, chunk), 1)
    w = jnp.where(r >= s, ws_ref[0], 0.0).astype(BF16)
    gain = gain_ref[...]
    bs = bs_ref[0]
    for c in range(nchunk):
        rows = pl.ds(c * chunk, chunk)
        v = jax.nn.gelu(vs_ref[rows, :].astype(F32))
        v = v - jnp.mean(v, axis=-1, keepdims=True)
        v = v * lax.rsqrt(jnp.mean(v * v, axis=-1, keepdims=True) + EPS) * gain
        mixed = jnp.dot(w, v.astype(BF16), preferred_element_type=F32) + bs
        o_ref[rows, :] = (jax.nn.gelu(u_ref[rows, :].astype(F32)) * mixed).astype(o_ref.dtype)


def _gmlp(proj, gain, ws, bs, *, rows=1024):
    t = proj.shape[0]
    g = GMLP_GROUPS
    dim = proj.shape[1] // 6 // g
    c = GMLP_CHUNK
    return pl.pallas_call(
        functools.partial(_gmlp_kernel, chunk=c, nchunk=rows // c),
        grid=(g, t // rows),
        in_specs=[pl.BlockSpec((rows, dim), lambda gg, i: (i, 4 * g + gg)),
                  pl.BlockSpec((rows, dim), lambda gg, i: (i, 5 * g + gg)),
                  pl.BlockSpec((1, dim), lambda gg, i: (0, gg)),
                  pl.BlockSpec((1, c, c), lambda gg, i: (gg, 0, 0)),
                  pl.BlockSpec((1, c, 1), lambda gg, i: (gg, 0, 0))],
        out_specs=pl.BlockSpec((rows, dim), lambda gg, i: (i, gg)),
        out_shape=jax.ShapeDtypeStruct((t, g * dim), BF16),
        compiler_params=_params("arbitrary", "arbitrary"),
        name="gmlp",
    )(proj, proj, gain.reshape(1, g * dim), ws, bs.reshape(g, c, 1))


def _hgrn_kernel(zq_ref, zf_ref, zi_ref, zg_ref, lb_ref, gain_ref, tri_ref, keep_ref, o_ref,
                 state_ref, qd_ref, kd_ref, ko_ref, dec_ref, *, chunk, nchunk, heads, dk):
    i = pl.program_id(1)
    rows = chunk * nchunk
    hw = heads * dk

    @pl.when(i == 0)
    def _():
        state_ref[...] = jnp.zeros_like(state_ref)
        for ref in (qd_ref, kd_ref, ko_ref, dec_ref):
            ref[1] = jnp.zeros(ref.shape[1:], ref.dtype)

    nt = (((1,), (1,)), ((), ()))
    head_cols = [slice(hd * dk, (hd + 1) * dk) for hd in range(heads)]

    def matmul_stage(prev):
        keep = keep_ref[...] > 0.0
        chunk_of_row = lax.broadcasted_iota(jnp.int32, (rows, dk), 0) // chunk
        q_dec, k_dec, k_out, dec = qd_ref[prev], kd_ref[prev], ko_ref[prev], dec_ref[prev]
        v = zi_ref[...]
        gate = jax.nn.silu(zg_ref[...].astype(F32))
        scores = [lax.dot_general(q_dec[:, cs], k_dec[:, cs], nt, preferred_element_type=F32)
                  for cs in head_cols]
        scores = [jnp.where(keep, s, 0.0).astype(BF16) for s in scores]
        o_intra = [jnp.dot(s, v[:, cs], preferred_element_type=F32) for s, cs in zip(scores, head_cols)]
        v_t = [v[:, cs].astype(F32).T.astype(BF16) for cs in head_cols]
        zero = jnp.zeros((rows, dk), BF16)
        k_blocks = [jnp.concatenate([jnp.where(chunk_of_row == c, k_out[:, cs], zero)
                                     for c in range(nchunk)], axis=1) for cs in head_cols]
        incr = [jnp.dot(v_t[hd], k_blocks[hd], preferred_element_type=F32) for hd in range(heads)]
        return q_dec, dec, gate, o_intra, incr

    def recurrence_stage(q_dec, dec, gate, o_intra, incr):
        states = [state_ref[hd] for hd in range(heads)]
        inter = [[] for _ in range(heads)]
        for c in range(nchunk):
            sl = slice(c * chunk, (c + 1) * chunk)
            for hd, cs in enumerate(head_cols):
                inter[hd].append(lax.dot_general(q_dec[sl, cs], states[hd].astype(BF16), nt,
                                                 preferred_element_type=F32))
                states[hd] = dec[c:c + 1, cs] * states[hd] + incr[hd][:, c * dk:(c + 1) * dk]
        for hd, cs in enumerate(head_cols):
            state_ref[hd] = states[hd]
            o = o_intra[hd] + jnp.concatenate(inter[hd], axis=0)
            o = o * lax.rsqrt(jnp.mean(o * o, axis=-1, keepdims=True) + EPS) * gain_ref[:, cs]
            o_ref[:, cs] = (o * gate[:, cs]).astype(o_ref.dtype)

    def elementwise_stage(cur):
        tri = tri_ref[...]
        lb = lb_ref[...]
        zf = zf_ref[...].astype(F32)
        f = lb + (1.0 - lb) * jax.nn.sigmoid(zf)
        kk = (1.0 - lb) * jax.nn.sigmoid(-zf)
        log_f = jnp.log(f)
        p0 = log_f.astype(BF16)
        r0 = log_f - p0.astype(F32)
        p1 = r0.astype(BF16)
        p2 = (r0 - p1.astype(F32)).astype(BF16)
        cum = (jnp.dot(tri, p0, preferred_element_type=F32)
               + jnp.dot(tri, p1, preferred_element_type=F32)
               + jnp.dot(tri, p2, preferred_element_type=F32))
        lasts = [cum[(c + 1) * chunk - 1:(c + 1) * chunk, :] for c in range(nchunk)]
        last_rows = jnp.concatenate([jnp.broadcast_to(l, (chunk, hw)) for l in lasts], axis=0)
        dec_ref[cur] = jnp.exp(jnp.concatenate(lasts, axis=0))
        qd_ref[cur] = (jax.nn.silu(zq_ref[...].astype(F32)) * jnp.exp(cum)).astype(BF16)
        kd_ref[cur] = (kk * jnp.exp(-cum)).astype(BF16)
        ko_ref[cur] = (kk * jnp.exp(last_rows - cum)).astype(BF16)

    def step(slot):
        carried = matmul_stage(1 - slot)
        elementwise_stage(slot)
        recurrence_stage(*carried)

    for slot in range(2):
        pl.when(i % 2 == slot)(functools.partial(step, slot))


def _hgrn2(proj, lb, gain, *, rows=256, heads=4):
    t = proj.shape[0]
    width = proj.shape[1] // 4
    dk = HGRN_DK
    h = width // dk
    c = HGRN_CHUNK
    nchunk = rows // c
    hw = heads * dk
    groups = h // heads
    nb = t // rows
    r = jnp.arange(rows, dtype=jnp.int32)
    tri = ((r[:, None] >= r[None, :]) & ((r[:, None] // c) == (r[None, :] // c))).astype(F32)

    def cur(off):
        return pl.BlockSpec((rows, hw), lambda hh, i: (jnp.minimum(i, nb - 1), off + hh))

    def prev(off):
        return pl.BlockSpec((rows, hw), lambda hh, i: (jnp.maximum(i - 1, 0), off + hh))

    vec = pl.BlockSpec((1, hw), lambda hh, i: (0, hh))
    mask = pl.BlockSpec((rows, rows), lambda hh, i: (0, 0))
    return pl.pallas_call(
        functools.partial(_hgrn_kernel, chunk=c, nchunk=nchunk, heads=heads, dk=dk),
        grid=(groups, nb + 1),
        in_specs=[cur(0), cur(groups), prev(2 * groups), prev(3 * groups), vec, vec, mask, mask],
        out_specs=prev(0),
        out_shape=jax.ShapeDtypeStruct((t, width), BF16),
        scratch_shapes=[pltpu.VMEM((heads, dk, dk), F32),
                        pltpu.VMEM((2, rows, hw), BF16), pltpu.VMEM((2, rows, hw), BF16),
                        pltpu.VMEM((2, rows, hw), BF16), pltpu.VMEM((2, nchunk, hw), F32)],
        compiler_params=_params("arbitrary", "arbitrary"),
        name="hgrn2",
    )(proj, proj, proj, proj, lb.reshape(1, width), gain.reshape(1, width), tri.astype(BF16), tri)


def _pack_bf16_pairs(h):
    half = h.shape[1] // 2
    bits = pltpu.bitcast(h.astype(BF16).astype(F32), jnp.uint32)
    return (bits[:, :half] >> 16) | bits[:, half:]


def _unpack_bf16_pairs(words):
    lo = pltpu.bitcast(words << 16, F32)
    hi = pltpu.bitcast(words & jnp.uint32(0xFFFF0000), F32)
    return jnp.concatenate([lo, hi], axis=1).astype(BF16)


def _route(logits, run):
    lane = lax.broadcasted_iota(jnp.int32, logits.shape, 1)
    lane_f = lane.astype(F32)
    neg = -jnp.inf
    big = float(ROUTE_LANES)
    lg = jnp.where(lane < N_GROUPS, logits, neg)
    mg = jnp.max(lg, axis=-1, keepdims=True)
    eg = jnp.exp(lg - mg)
    pg = eg / jnp.sum(eg, axis=-1, keepdims=True)
    p_sel = jnp.max(pg, axis=-1, keepdims=True)
    g_sel = jnp.min(jnp.where(lg == mg, lane_f, big), axis=-1, keepdims=True)
    e_grp = ((lane - N_GROUPS) // EXPERTS_PER_GROUP).astype(F32)
    in_grp = jnp.where(lane >= N_GROUPS, e_grp, -1.0) == g_sel
    le = jnp.where(in_grp, logits, neg)
    t1 = jnp.max(le, axis=-1, keepdims=True)
    i1 = jnp.min(jnp.where(le == t1, lane_f, big), axis=-1, keepdims=True)
    le2 = jnp.where(lane_f == i1, neg, le)
    t2 = jnp.max(le2, axis=-1, keepdims=True)
    i2 = jnp.min(jnp.where(le2 == t2, lane_f, big), axis=-1, keepdims=True)
    e2 = jnp.exp(t2 - t1)
    den = 1.0 + e2
    gate1 = p_sel * (1.0 / den)
    gate2 = p_sel * (e2 / den)
    tm = logits.shape[0]
    hit1 = jnp.where(lane_f == i1, 1.0, 0.0)
    hit2 = jnp.where(lane_f == i2, 1.0, 0.0)
    hits = hit1 + hit2
    r = lax.broadcasted_iota(jnp.int32, (tm, tm), 0)
    s = lax.broadcasted_iota(jnp.int32, (tm, tm), 1)
    before = jnp.where(r > s, 1.0, 0.0).astype(BF16)
    prefix = jnp.dot(before, hits.astype(BF16), preferred_element_type=F32) + run
    rank1 = jnp.sum(prefix * hit1, axis=-1, keepdims=True)
    rank2 = jnp.sum(prefix * hit2, axis=-1, keepdims=True)
    vals = (i1 - N_GROUPS, i2 - N_GROUPS, gate1, gate2, rank1, rank2)
    slab = jnp.zeros_like(logits)
    for pos, val in enumerate(vals):
        slab = jnp.where(lane == pos, val, slab)
    return slab, run + jnp.sum(hits, axis=0, keepdims=True)


def _outproj_router_kernel(*refs, n_act):
    x_ref = refs[0]
    a_refs = refs[1:1 + n_act]
    w_refs = refs[1 + n_act:1 + 2 * n_act]
    (g_ref, wr_ref, br_ref, xo_ref, hp_ref, route_ref, count_ref,
     run_ref, xs_ref, lg_ref) = refs[1 + 2 * n_act:]
    i = pl.program_id(0)

    @pl.when(i == 0)
    def _():
        run_ref[...] = jnp.zeros_like(run_ref)
        xs_ref[1] = jnp.zeros(xs_ref.shape[1:], xs_ref.dtype)
        lg_ref[1] = jnp.zeros(lg_ref.shape[1:], lg_ref.dtype)

    def step(slot):
        x = x_ref[...]
        for a_ref, w_ref in zip(a_refs, w_refs):
            x = x + jnp.dot(a_ref[...], w_ref[...], preferred_element_type=F32)
        xo_ref[...] = x
        xs_ref[slot] = x

        run = run_ref[...]
        slab, run_next = _route(lg_ref[1 - slot], run)
        route_ref[0] = slab.T[:ROUTE_FIELDS, :]
        run = jnp.where(i >= 2, run_next, run)
        run_ref[...] = run
        count_ref[...] = run

        h = _rms(xs_ref[1 - slot], g_ref[...])
        hp_ref[...] = _pack_bf16_pairs(h)
        h_hi = h.astype(BF16)
        h_lo = (h - h_hi.astype(F32)).astype(BF16)
        p = (jnp.dot(h_hi, wr_ref[...], preferred_element_type=F32)
             + jnp.dot(h_lo, wr_ref[...], preferred_element_type=F32))
        lg_ref[slot] = p[:, :ROUTE_LANES] + p[:, ROUTE_LANES:] + br_ref[...]

    for slot in range(2):
        pl.when(i % 2 == slot)(functools.partial(step, slot))


def _outproj_router(x, acts, ws, g, w_rg, b_rg, w_re, b_re, *, tm=TOKEN_TILE):
    t, d = x.shape
    used = N_GROUPS + N_EXPERTS
    wr = jnp.zeros((d, ROUTE_LANES), F32).at[:, :N_GROUPS].set(w_rg).at[:, N_GROUPS:used].set(w_re)
    wr_hi = wr.astype(BF16)
    wr_lo = (wr - wr_hi.astype(F32)).astype(BF16)
    wr2 = jnp.concatenate([wr_hi, wr_lo], axis=1)
    br = jnp.zeros((1, ROUTE_LANES), F32).at[0, :N_GROUPS].set(b_rg).at[0, N_GROUPS:used].set(b_re)
    nt = t // tm
    cur = lambda i: (jnp.minimum(i, nt - 1), 0)
    prev = lambda i: (jnp.clip(i - 1, 0, nt - 1), 0)
    prev2 = lambda i: (jnp.maximum(i - 2, 0), 0, 0)
    row = pl.BlockSpec((tm, d), cur)
    const = lambda a: pl.BlockSpec(a.shape, lambda i: (0, 0))
    in_specs = [row]
    in_specs += [pl.BlockSpec((tm, a.shape[1]), cur) for a in acts]
    in_specs += [const(w) for w in ws]
    in_specs += [pl.BlockSpec((1, d), lambda i: (0, 0)), const(wr2), const(br)]
    return pl.pallas_call(
        functools.partial(_outproj_router_kernel, n_act=len(acts)),
        grid=(nt + 2,),
        in_specs=in_specs,
        out_specs=[row, pl.BlockSpec((tm, d // 2), prev),
                   pl.BlockSpec((1, ROUTE_FIELDS, tm), prev2),
                   pl.BlockSpec((1, ROUTE_LANES), lambda i: (0, 0))],
        out_shape=[jax.ShapeDtypeStruct((t, d), F32), jax.ShapeDtypeStruct((t, d // 2), jnp.uint32),
                   jax.ShapeDtypeStruct((nt, ROUTE_FIELDS, tm), F32),
                   jax.ShapeDtypeStruct((1, ROUTE_LANES), F32)],
        scratch_shapes=[pltpu.VMEM((1, ROUTE_LANES), F32), pltpu.VMEM((2, tm, d), F32),
                        pltpu.VMEM((2, tm, ROUTE_LANES), F32)],
        compiler_params=_params("arbitrary"),
        name="outproj_router",
    )(x, *acts, *ws, g.reshape(1, d), wr2, br)


def _expert_changed(be_ref, b):
    return (b == 0) | (be_ref[b] != be_ref[jnp.maximum(b - 1, 0)])


def _moe_ffn_kernel(be_ref, nu_ref, ne_ref, *refs, layer):
    nslot = MOE_ROW_SLOTS
    first_refs, ahead_ref = refs[:nslot - 1], refs[nslot - 1]
    (h_hbm, wg_hbm, wu_hbm, wd_hbm, out_ref,
     xbuf, stage_g, stage_u, stage_d, wg_bf, wu_bf, wd_bf, xsem, wsem) = refs[nslot:]
    b = pl.program_id(0)
    nu = nu_ref[0]
    blk = xbuf.shape[1]

    def start_rows(idx_ref, dst_slot):
        for r in range(blk):
            pltpu.make_async_copy(h_hbm.at[pl.ds(idx_ref[0, 0, r], 1)],
                                  xbuf.at[dst_slot, pl.ds(r, 1)], xsem.at[dst_slot]).start(priority=0)

    def wait_rows(dst_slot):
        pltpu.make_async_copy(h_hbm.at[pl.ds(0, blk)], xbuf.at[dst_slot], xsem.at[dst_slot]).wait()

    def weight_copies(e):
        return (pltpu.make_async_copy(wg_hbm.at[layer, e], stage_g, wsem.at[0]),
                pltpu.make_async_copy(wu_hbm.at[layer, e], stage_u, wsem.at[1]),
                pltpu.make_async_copy(wd_hbm.at[layer, e], stage_d, wsem.at[2]))

    def start_weights(e):
        for cp in weight_copies(e):
            cp.start(priority=1)

    @pl.when(b == 0)
    def _():
        start_weights(be_ref[0])
        for slot, idx_ref in enumerate(first_refs):
            start_rows(idx_ref, slot)

    changed = _expert_changed(be_ref, b)

    def compute(slot, new_expert):
        if new_expert:
            for cp in weight_copies(be_ref[b]):
                cp.wait()
        wait_rows(slot)
        x = _unpack_bf16_pairs(xbuf[slot])
        start_rows(ahead_ref, (slot + nslot - 1) % nslot)
        refill = weight_copies(ne_ref[b]) if new_expert else None
        if new_expert:
            wg_bf[...] = stage_g[...].astype(BF16)
            refill[0].start(priority=1)
        hg = jnp.dot(x, wg_bf[...], preferred_element_type=F32)
        if new_expert:
            wu_bf[...] = stage_u[...].astype(BF16)
            refill[1].start(priority=1)
        hu = jnp.dot(x, wu_bf[...], preferred_element_type=F32)
        if new_expert:
            wd_bf[...] = stage_d[...].astype(BF16)
            refill[2].start(priority=1)
        hid = (jax.nn.silu(hg) * hu).astype(BF16)
        out_ref[...] = jnp.dot(hid, wd_bf[...], preferred_element_type=F32)

        @pl.when(b == nu - 1)
        def _():
            for k in range(1, nslot):
                wait_rows((slot + k) % nslot)
            for cp in weight_copies(be_ref[b]):
                cp.wait()

    for slot in range(nslot):
        here = (b < nu) & (b % nslot == slot)
        pl.when(here & changed)(functools.partial(compute, slot, True))
        pl.when(here & jnp.logical_not(changed))(functools.partial(compute, slot, False))

    @pl.when(b >= nu)
    def _():
        out_ref[...] = jnp.zeros_like(out_ref)


def _combine_kernel(*refs, final):
    if final:
        pos0_ref, pos1_ref, ahead_ref, x_ref, route_ref, eo_hbm, g_ref, o_ref, ybuf, sem = refs
    else:
        pos0_ref, pos1_ref, ahead_ref, x_ref, route_ref, eo_hbm, o_ref, ybuf, sem = refs
    i = pl.program_id(0)
    last = pl.num_programs(0) - 1
    tm = x_ref.shape[0]
    nslot = COMBINE_SLOTS

    def start_rows(idx_ref, dst_slot):
        for r in range(tm):
            for k in range(TOP_K):
                pltpu.make_async_copy(eo_hbm.at[pl.ds(idx_ref[0, 0, k * tm + r], 1)],
                                      ybuf.at[dst_slot, k, pl.ds(r, 1)], sem.at[dst_slot]).start(priority=k)

    def wait_rows(dst_slot):
        for k in range(TOP_K):
            pltpu.make_async_copy(eo_hbm.at[pl.ds(0, tm)], ybuf.at[dst_slot, k], sem.at[dst_slot]).wait()

    @pl.when(i == 0)
    def _():
        start_rows(pos0_ref, 0)
        start_rows(pos1_ref, 1)

    def step(slot):
        wait_rows(slot)
        start_rows(ahead_ref, (slot + 2) % nslot)
        route = route_ref[0].T
        y = ybuf[slot, 0] * route[:, TOP_K:TOP_K + 1] + ybuf[slot, 1] * route[:, TOP_K + 1:TOP_K + 2]
        x = x_ref[...] + y
        o_ref[...] = _rms(x, g_ref[...]) if final else x

        @pl.when(i == last)
        def _():
            wait_rows((slot + 1) % nslot)
            wait_rows((slot + 2) % nslot)

    for slot in range(nslot):
        pl.when(i % nslot == slot)(functools.partial(step, slot))


def _moe_dispatch(route, counts_slab, t):
    m = t * TOP_K
    nt, _, tm = route.shape
    expert = route[:, :TOP_K, :].astype(jnp.int32)
    rank = route[:, 2 * TOP_K:3 * TOP_K, :].astype(jnp.int32)
    counts = counts_slab[0, N_GROUPS:N_GROUPS + N_EXPERTS].astype(jnp.int32)
    padded = (counts + MOE_BLOCK - 1) // MOE_BLOCK * MOE_BLOCK
    padded_ends = jnp.cumsum(padded)
    padded_starts = padded_ends - padded
    ids = jnp.arange(N_EXPERTS, dtype=jnp.int32)
    start_of = jnp.sum(jnp.where(expert[..., None] == ids, padded_starts, 0), axis=-1)
    dest = start_of + rank
    n_blocks = -(-(m + N_EXPERTS * (MOE_BLOCK - 1)) // MOE_BLOCK)
    cap = n_blocks * MOE_BLOCK
    token = (jnp.arange(nt, dtype=jnp.int32)[:, None, None] * tm
             + jnp.arange(tm, dtype=jnp.int32)[None, None, :])
    token = jnp.broadcast_to(token, dest.shape)
    assert cap == m + N_EXPERTS * MOE_BLOCK
    pad_i = jnp.arange(MOE_BLOCK, dtype=jnp.int32)[None, :]
    pad_row = jnp.where(pad_i < (padded - counts)[:, None],
                        (padded_starts + counts)[:, None] + pad_i,
                        cap + ids[:, None] * MOE_BLOCK + pad_i)
    rows = jnp.concatenate([dest.reshape(m), pad_row.reshape(-1)])
    toks = jnp.concatenate([token.reshape(m), jnp.zeros((N_EXPERTS * MOE_BLOCK,), jnp.int32)])
    _, buf_src = lax.sort_key_val(rows, toks)
    block_start = jnp.arange(n_blocks, dtype=jnp.int32) * MOE_BLOCK
    block_expert = jnp.minimum(
        jnp.sum((padded_ends[None, :] <= block_start[:, None]).astype(jnp.int32), axis=1), N_EXPERTS - 1)
    n_used = (padded_ends[-1] // MOE_BLOCK).astype(jnp.int32).reshape(1)
    ids = jnp.arange(N_EXPERTS, dtype=jnp.int32)
    later = (ids[None, :] > ids[:, None]) & (counts[None, :] > 0)
    next_active = jnp.min(jnp.where(later, ids[None, :], N_EXPERTS), axis=1)
    next_active = jnp.where(next_active < N_EXPERTS, next_active, ids)
    block_next = next_active[block_expert]
    return block_expert, n_used, block_next, buf_src.reshape(n_blocks, 1, MOE_BLOCK), dest


def _moe_experts(h_packed, block_expert, n_used, block_next, buf_src, w_gate, w_up, w_down, layer):
    d, ff = w_gate.shape[-2:]
    n_blocks = buf_src.shape[0]
    cap = n_blocks * MOE_BLOCK
    idx_block = (1, 1, MOE_BLOCK)
    hbm = pl.BlockSpec(memory_space=pl.ANY)

    look = MOE_ROW_SLOTS - 1

    def fixed(k, b, *_):
        return (min(k, n_blocks - 1), 0, 0)

    def ahead(b, *_):
        return (jnp.minimum(b + look, n_blocks - 1), 0, 0)

    idx_specs = [pl.BlockSpec(idx_block, functools.partial(fixed, k), memory_space=pltpu.SMEM)
                 for k in range(look)]
    idx_specs.append(pl.BlockSpec(idx_block, ahead, memory_space=pltpu.SMEM))
    return pl.pallas_call(
        functools.partial(_moe_ffn_kernel, layer=layer),
        grid_spec=pltpu.PrefetchScalarGridSpec(
            num_scalar_prefetch=3,
            grid=(n_blocks,),
            in_specs=idx_specs + [hbm, hbm, hbm, hbm],
            out_specs=pl.BlockSpec((MOE_BLOCK, d), lambda b, *_: (b, 0)),
            scratch_shapes=[pltpu.VMEM((MOE_ROW_SLOTS, MOE_BLOCK, d // 2), jnp.uint32),
                            pltpu.VMEM((d, ff), F32), pltpu.VMEM((d, ff), F32), pltpu.VMEM((ff, d), F32),
                            pltpu.VMEM((d, ff), BF16), pltpu.VMEM((d, ff), BF16), pltpu.VMEM((ff, d), BF16),
                            pltpu.SemaphoreType.DMA((MOE_ROW_SLOTS,)), pltpu.SemaphoreType.DMA((3,))],
        ),
        out_shape=jax.ShapeDtypeStruct((cap, d), F32),
        compiler_params=pltpu.CompilerParams(dimension_semantics=("arbitrary",),
                                             vmem_limit_bytes=MOE_VMEM_LIMIT),
        name="moe_ffn",
    )(block_expert, n_used, block_next, *([buf_src] * MOE_ROW_SLOTS), h_packed, w_gate, w_up, w_down)


def _combine(x, route, dest, expert_out, final_gain):
    t, d = x.shape
    nt, _, tm = route.shape
    final = final_gain is not None
    pos = dest.reshape(nt, 1, TOP_K * tm)
    idx_block = (1, 1, TOP_K * tm)
    row = pl.BlockSpec((tm, d), lambda i: (i, 0))
    in_specs = [pl.BlockSpec(idx_block, lambda i: (0, 0, 0), memory_space=pltpu.SMEM),
                pl.BlockSpec(idx_block, lambda i: (min(1, nt - 1), 0, 0), memory_space=pltpu.SMEM),
                pl.BlockSpec(idx_block, lambda i: (jnp.minimum(i + 2, nt - 1), 0, 0),
                             memory_space=pltpu.SMEM),
                row, pl.BlockSpec((1, ROUTE_FIELDS, tm), lambda i: (i, 0, 0)),
                pl.BlockSpec(memory_space=pl.ANY)]
    args = [pos, pos, pos, x, route, expert_out]
    if final:
        in_specs.append(pl.BlockSpec((1, d), lambda i: (0, 0)))
        args.append(final_gain.reshape(1, d))
    return pl.pallas_call(
        functools.partial(_combine_kernel, final=final),
        grid=(nt,),
        in_specs=in_specs, out_specs=row,
        out_shape=jax.ShapeDtypeStruct((t, d), F32),
        scratch_shapes=[pltpu.VMEM((COMBINE_SLOTS, TOP_K, tm, d), F32),
                        pltpu.SemaphoreType.DMA((COMBINE_SLOTS,))],
        compiler_params=_params("arbitrary"),
        name="moe_combine",
    )(*args)


def kernel(x, attn_norm, ffn_norm, final_norm, w_in_ab, ret_norm, gmlp_norm, gmlp_ws, gmlp_bs, w_out_ab, w_in_c, lb_params, hgrn_norm, w_out_c, router_w_group, router_b_group, router_w_expert, router_b_expert, w_gate, w_up, w_down):
    b, s, d = x.shape
    assert b == 1, "the sequence mixers carry state along the flattened token axis"
    depth = attn_norm.shape[0]
    lb_soft = jax.nn.softmax(lb_params.astype(F32), axis=0)
    lower_bounds = jnp.cumsum(lb_soft, axis=0) - lb_soft[0]
    xt = x.reshape(b * s, d)
    t = b * s
    for layer in range(depth):
        i = layer // 2
        if layer % 2 == 0:
            proj = _norm_matmul(xt, attn_norm[layer], w_in_ab[i].astype(BF16))
            ret = _retention(proj, ret_norm[i])
            gm = _gmlp(proj, gmlp_norm[i], gmlp_ws[i], gmlp_bs[i])
            w_out = w_out_ab[i].astype(BF16)
            nr = ret.shape[1]
            acts, w_outs = [ret, gm], [w_out[:nr], w_out[nr:]]
        else:
            proj = _norm_matmul(xt, attn_norm[layer], w_in_c[i].astype(BF16))
            acts = [_hgrn2(proj, lower_bounds[layer], hgrn_norm[i])]
            w_outs = [w_out_c[i].astype(BF16)]
        xt, h_packed, route, counts = _outproj_router(
            xt, acts, w_outs, ffn_norm[layer], router_w_group[layer], router_b_group[layer],
            router_w_expert[layer], router_b_expert[layer])
        block_expert, n_used, block_next, buf_src, dest = _moe_dispatch(route, counts, t)
        expert_out = _moe_experts(h_packed, block_expert, n_used, block_next, buf_src,
                                  w_gate, w_up, w_down, layer)
        xt = _combine(xt, route, dest, expert_out, final_norm if layer == depth - 1 else None)
    return xt.reshape(b, s, d)
```
